```python
import math, functools
import jax, jax.numpy as jnp
from jax import lax
import numpy as np

D_MODEL = 1024
BATCH = 1
SEQ = 16384
DEPTH = 1
DEC_BATCH = 32
DEC_SEQ = 4
PAST_LEN = 16384
PAGE_SIZE = 128

N_HEADS_A = 8
HEAD_DIM_A = 64
IDX_HEADS = 4
IDX_DIM = 64
TOPK_MAX = 256
Q_BLOCK = 128
N_HEADS_R = 4
DK_R = 128
DV_R = 256
RET_CHUNK = 128
N_HEADS_M = 4
HEAD_DIM_M = 128
N_MEM = 256
D_FF = 2816
CONV_W = 3

ROPE_THETA = 10000.0
EPS = 1e-6

kernel_name = "hybrid_dsa_retention_memory_convffn_step"


def _in_widths():
    wa = N_HEADS_A * HEAD_DIM_A
    return (wa, wa, wa, IDX_HEADS * IDX_DIM, IDX_DIM, IDX_HEADS,
            N_HEADS_R * DK_R, N_HEADS_R * DK_R, N_HEADS_R * DV_R, N_HEADS_R * DV_R,
            N_HEADS_M * HEAD_DIM_M, 3 * D_MODEL)


def rms_norm(x, g):
    x32 = x.astype(jnp.float32)
    y = x32 * lax.rsqrt(jnp.mean(x32 * x32, axis=-1, keepdims=True) + EPS)
    return (y * g.astype(jnp.float32)).astype(x.dtype)


def group_norm(o):
    o32 = o.astype(jnp.float32)
    mu = jnp.mean(o32, axis=-1, keepdims=True)
    var = jnp.mean(jnp.square(o32 - mu), axis=-1, keepdims=True)
    return (o32 - mu) * lax.rsqrt(var + EPS)


def rope(x, pos):
    d = x.shape[-1]
    half = d // 2
    inv = 1.0 / (ROPE_THETA ** (jnp.arange(half, dtype=jnp.float32) * 2.0 / d))
    ang = pos.astype(jnp.float32)[:, None] * inv[None, :]
    cos = jnp.cos(ang)[:, None, :]
    sin = jnp.sin(ang)[:, None, :]
    x32 = x.astype(jnp.float32)
    x1, x2 = x32[..., :half], x32[..., half:]
    return jnp.concatenate([x1 * cos - x2 * sin, x2 * cos + x1 * sin], axis=-1).astype(x.dtype)


def index_scores(qi, wi, kidx):
    s = jax.nn.relu(jnp.einsum('bthd,bsd->bths', qi.astype(jnp.float32), kidx.astype(jnp.float32)))
    return jnp.einsum('bth,bths->bts', wi.astype(jnp.float32), s)


def sparse_attend(q, kg, vg, valid):
    s = jnp.einsum('bthd,btkhd->bhtk', q, kg).astype(jnp.float32) * (q.shape[-1] ** -0.5)
    s = jnp.where(valid[:, None], s, -jnp.inf)
    p = jax.nn.softmax(s, axis=-1).astype(vg.dtype)
    return jnp.einsum('bhtk,btkhd->bthd', p, vg)


def dsa_prompt(q, k, v, qi, wi, kidx):
    B, S = q.shape[:2]
    qb = math.gcd(S, Q_BLOCK)
    nb = S // qb
    topk = min(TOPK_MAX, S // 4)
    kpos = jnp.arange(S)
    bi = jnp.arange(B)[:, None, None]

    def blocks(a):
        return a.reshape(B, nb, qb, *a.shape[2:]).swapaxes(0, 1)

    def one(args):
        q_b, qi_b, wi_b, p_b = args
        I = index_scores(qi_b, wi_b, kidx)
        I = jnp.where(kpos[None, None, :] <= p_b[None, :, None], I, -jnp.inf)
        _, idx = lax.top_k(I, topk)
        valid = idx <= p_b[None, :, None]
        return sparse_attend(q_b, k[bi, idx], v[bi, idx], valid)

    o = lax.map(one, (blocks(q), blocks(qi), blocks(wi), kpos.reshape(nb, qb)))
    return o.swapaxes(0, 1).reshape(B, S, *q.shape[2:])


def dsa_sample(q, k_new, v_new, qi, wi, kidx_new, cache_k, cache_v, cache_kidx, page_table):
    DB, T = q.shape[:2]
    past = page_table.shape[1] * PAGE_SIZE
    L = past + T
    topk = min(TOPK_MAX, L // 4)
    kidx_past = cache_kidx[page_table].reshape(DB, past, -1)
    kidx_all = jnp.concatenate([kidx_past, kidx_new.astype(kidx_past.dtype)], axis=1)
    qpos = past + jnp.arange(T)
    I = index_scores(qi, wi, kidx_all)
    I = jnp.where(jnp.arange(L)[None, None, :] <= qpos[None, :, None], I, -jnp.inf)
    _, idx = lax.top_k(I, topk)
    valid = idx <= qpos[None, :, None]
    bi = jnp.arange(DB)[:, None, None]
    pidx = jnp.clip(idx, 0, past - 1)
    phys = page_table[bi, pidx // PAGE_SIZE]
    off = pidx % PAGE_SIZE
    nidx = jnp.clip(idx - past, 0, T - 1)
    is_new = (idx >= past)[..., None, None]
    kg = jnp.where(is_new, k_new[bi, nidx], cache_k[phys, off].astype(k_new.dtype))
    vg = jnp.where(is_new, v_new[bi, nidx], cache_v[phys, off].astype(v_new.dtype))
    return sparse_attend(q, kg, vg, valid)


def retention(q, k, v, s0):
    B, L, H, dk = q.shape
    dv = v.shape[-1]
    c = math.gcd(L, RET_CHUNK)
    n = L // c
    log_g = jnp.log1p(-jnp.exp2(-5.0 - jnp.arange(H, dtype=jnp.float32)))
    i = jnp.arange(c, dtype=jnp.float32)
    diff = i[:, None] - i[None, :]
    inner = jnp.where(diff[None] >= 0, jnp.exp(jnp.maximum(diff, 0.0)[None] * log_g[:, None, None]), 0.0)
    q_dec = jnp.exp((i + 1.0)[:, None] * log_g[None, :])
    k_dec = jnp.exp((c - 1.0 - i)[:, None] * log_g[None, :])
    c_dec = jnp.exp(c * log_g)

    def chunks(a):
        return a.astype(jnp.float32).reshape(B, n, c, H, a.shape[-1]).swapaxes(0, 1)

    def step(S, inp):
        qc, kc, vc = inp
        a = jnp.einsum('bihd,bjhd->bhij', qc, kc) * inner
        o = jnp.einsum('bhij,bjhe->bihe', a, vc) + \
            jnp.einsum('bihd,bhde->bihe', qc, S) * q_dec[None, :, :, None]
        S = S * c_dec[None, :, None, None] + jnp.einsum('bjhd,bjhe->bhde', kc * k_dec[None, :, :, None], vc)
        return S, o

    S, o = lax.scan(step, s0.astype(jnp.float32), (chunks(q), chunks(k), chunks(v)))
    return o.swapaxes(0, 1).reshape(B, L, H, dv), S


def cross_attend(q, mk, mv):
    s = jnp.einsum('bthd,bmhd->bhtm', q, mk.astype(q.dtype)).astype(jnp.float32) * (q.shape[-1] ** -0.5)
    p = jax.nn.softmax(s, axis=-1).astype(q.dtype)
    return jnp.einsum('bhtm,bmhd->bthd', p, mv.astype(q.dtype))


def memory_kv(mem, norm_mem, w_mem_kv):
    B = mem.shape[0]
    kv = rms_norm(mem, norm_mem) @ w_mem_kv
    mk, mv = jnp.split(kv, 2, axis=-1)
    return mk.reshape(B, -1, N_HEADS_M, HEAD_DIM_M), mv.reshape(B, -1, N_HEADS_M, HEAD_DIM_M)


def conv_ffn(h, conv_s0, w_up, conv_w, conv_b, w_down):
    T = h.shape[1]
    u, g = jnp.split(h @ w_up, 2, axis=-1)
    ext = jnp.concatenate([conv_s0.astype(u.dtype), u], axis=1)
    c = conv_b
    for j in range(CONV_W):
        c = c + ext[:, j:j + T] * conv_w[j]
    out = (jax.nn.gelu(c) * g) @ w_down
    return out, ext[:, T:]


def layer(x, pos, attend_a, mem_k, mem_v, ret_s0, conv_s0,
          norm_pre_mix, norm_post_mix, norm_pre_ffn, norm_post_ffn,
          w_in, w_proj_a, w_proj_b, w_proj_c, w_out, w_up, conv_w, conv_b, w_down):
    B, T, _ = x.shape
    xn = rms_norm(x, norm_pre_mix)
    widths = _in_widths()
    offs = [int(o) for o in np.cumsum(widths)[:-1]]
    q_a, k_a, v_a, q_i, k_i, w_i, q_r, k_r, v_r, g_r, q_m, gates = jnp.split(xn @ w_in, offs, axis=-1)

    def heads(t, h):
        return t.reshape(B, T, h, -1)

    q_a = rope(heads(q_a, N_HEADS_A), pos)
    k_a = rope(heads(k_a, N_HEADS_A), pos)
    v_a = heads(v_a, N_HEADS_A)
    q_i = rope(heads(q_i, IDX_HEADS), pos)
    k_i = rope(k_i[:, :, None, :], pos)[:, :, 0]
    o_a = attend_a(q_a, k_a, v_a, q_i, w_i, k_i).reshape(B, T, -1)
    q_r = rope(heads(q_r, N_HEADS_R), pos)
    k_r = rope(heads(k_r, N_HEADS_R), pos) * (DK_R ** -0.5)
    o_r, ret_new = retention(q_r, k_r, heads(v_r, N_HEADS_R), ret_s0)
    o_r = group_norm(o_r).astype(x.dtype).reshape(B, T, -1) * jax.nn.silu(g_r)
    o_m = cross_attend(heads(q_m, N_HEADS_M), mem_k, mem_v).reshape(B, T, -1)
    g_a, g_b, g_c = jnp.split(gates, 3, axis=-1)
    mixed = (jax.nn.sigmoid(g_a) * (o_a @ w_proj_a) + jax.nn.sigmoid(g_b) * (o_r @ w_proj_b)
             + jax.nn.sigmoid(g_c) * (o_m @ w_proj_c))
    h = x + rms_norm(mixed @ w_out, norm_post_mix)
    ff, conv_new = conv_ffn(rms_norm(h, norm_pre_ffn), conv_s0, w_up, conv_w, conv_b, w_down)
    y = h + rms_norm(ff, norm_post_ffn)
    return y, k_a, v_a, k_i, ret_new, conv_new


def setup_inputs(seed: int = 0) -> dict:
    key = jax.random.key(seed)
    ks = iter(jax.random.split(key, 40))
    nrm = lambda shape, scale=1.0: jax.random.normal(next(ks), shape, jnp.float32) * scale
    n_pages = PAST_LEN // PAGE_SIZE
    used = DEC_BATCH * n_pages
    n_pool = used + max(1, used // 4)
    n_in = sum(_in_widths())
    wa = N_HEADS_A * HEAD_DIM_A
    wr = N_HEADS_R * DV_R
    wm = N_HEADS_M * HEAD_DIM_M
    gain = lambda: 1.0 + nrm((DEPTH, D_MODEL), 0.02)
    perm = jax.random.permutation(next(ks), n_pool)[:used]
    return {
        "x_prompt": nrm((BATCH, SEQ, D_MODEL)),
        "x_sample": nrm((DEC_BATCH, DEC_SEQ, D_MODEL)),
        "cache_k": nrm((DEPTH, n_pool, PAGE_SIZE, N_HEADS_A, HEAD_DIM_A)),
        "cache_v": nrm((DEPTH, n_pool, PAGE_SIZE, N_HEADS_A, HEAD_DIM_A)),
        "cache_kidx": nrm((DEPTH, n_pool, PAGE_SIZE, IDX_DIM)),
        "cache_mem_k": nrm((DEPTH, DEC_BATCH, N_MEM, N_HEADS_M, HEAD_DIM_M)),
        "cache_mem_v": nrm((DEPTH, DEC_BATCH, N_MEM, N_HEADS_M, HEAD_DIM_M)),
        "state_ret": nrm((DEPTH, DEC_BATCH, N_HEADS_R, DK_R, DV_R), 0.5),
        "state_conv": nrm((DEPTH, DEC_BATCH, CONV_W - 1, D_FF)),
        "page_table": perm.reshape(DEC_BATCH, n_pages).astype(jnp.int32),
        "mem_prompt": nrm((BATCH, N_MEM, D_MODEL)),
        "norm_pre_mix": gain(),
        "norm_post_mix": gain(),
        "norm_pre_ffn": gain(),
        "norm_post_ffn": gain(),
        "norm_mem": gain(),
        "w_in": nrm((DEPTH, D_MODEL, n_in), D_MODEL ** -0.5),
        "w_mem_kv": nrm((DEPTH, D_MODEL, 2 * wm), D_MODEL ** -0.5),
        "w_proj_a": nrm((DEPTH, wa, D_MODEL), wa ** -0.5),
        "w_proj_b": nrm((DEPTH, wr, D_MODEL), wr ** -0.5),
        "w_proj_c": nrm((DEPTH, wm, D_MODEL), wm ** -0.5),
        "w_out": nrm((DEPTH, D_MODEL, D_MODEL), D_MODEL ** -0.5),
        "w_up": nrm((DEPTH, D_MODEL, 2 * D_FF), D_MODEL ** -0.5),
        "conv_w": nrm((DEPTH, CONV_W, D_FF), CONV_W ** -0.5),
        "conv_b": nrm((DEPTH, D_FF), 0.01),
        "w_down": nrm((DEPTH, D_FF, D_MODEL), D_FF ** -0.5),
    }


def reference(x_prompt, x_sample, cache_k, cache_v, cache_kidx, cache_mem_k, cache_mem_v,
              state_ret, state_conv, page_table, mem_prompt,
              norm_pre_mix, norm_post_mix, norm_pre_ffn, norm_post_ffn, norm_mem,
              w_in, w_mem_kv, w_proj_a, w_proj_b, w_proj_c, w_out, w_up, conv_w, conv_b, w_down):
    B, S = x_prompt.shape[:2]
    T = x_sample.shape[1]
    past = page_table.shape[1] * PAGE_SIZE
    pos_p = jnp.arange(S)
    pos_s = past + jnp.arange(T)
    yp, ys = x_prompt, x_sample
    kp_l, vp_l, kip_l, rp_l, cp_l, mkp_l, mvp_l = [], [], [], [], [], [], []
    ks_l, vs_l, kis_l, rs_l, cs_l = [], [], [], [], []
    for l in range(DEPTH):
        lw = (norm_pre_mix[l], norm_post_mix[l], norm_pre_ffn[l], norm_post_ffn[l],
              w_in[l], w_proj_a[l], w_proj_b[l], w_proj_c[l], w_out[l], w_up[l], conv_w[l], conv_b[l], w_down[l])
        mk_p, mv_p = memory_kv(mem_prompt, norm_mem[l], w_mem_kv[l])
        ret0 = jnp.zeros((B, N_HEADS_R, DK_R, DV_R), jnp.float32)
        conv0 = jnp.zeros((B, CONV_W - 1, D_FF), yp.dtype)
        yp, kp, vp, kip, rp, cp = layer(yp, pos_p, dsa_prompt, mk_p, mv_p, ret0, conv0, *lw)
        attend_s = functools.partial(dsa_sample, cache_k=cache_k[l], cache_v=cache_v[l],
                                     cache_kidx=cache_kidx[l], page_table=page_table)
        ys, ks_, vs_, kis, rs, cs = layer(ys, pos_s, attend_s, cache_mem_k[l], cache_mem_v[l],
                                          state_ret[l], state_conv[l], *lw)
        kp_l.append(kp); vp_l.append(vp); kip_l.append(kip); rp_l.append(rp); cp_l.append(cp)
        mkp_l.append(mk_p); mvp_l.append(mv_p)
        ks_l.append(ks_); vs_l.append(vs_); kis_l.append(kis); rs_l.append(rs); cs_l.append(cs)
    k_prompt = jnp.stack(kp_l)
    v_prompt = jnp.stack(vp_l)
    kidx_prompt = jnp.stack(kip_l)
    ret_prompt = jnp.stack(rp_l)
    conv_prompt = jnp.stack(cp_l)
    mem_k_prompt = jnp.stack(mkp_l)
    mem_v_prompt = jnp.stack(mvp_l)
    k_sample = jnp.stack(ks_l)
    v_sample = jnp.stack(vs_l)
    kidx_sample = jnp.stack(kis_l)
    ret_sample = jnp.stack(rs_l)
    conv_sample = jnp.stack(cs_l)
    return (yp, ys, k_prompt, v_prompt, kidx_prompt, ret_prompt, conv_prompt, mem_k_prompt, mem_v_prompt,
            k_sample, v_sample, kidx_sample, ret_sample, conv_sample)
```

```python
import functools
import math

import numpy as np
import jax
import jax.numpy as jnp
from jax import lax
from jax.experimental import pallas as pl
from jax.experimental.pallas import tpu as pltpu

F32 = jnp.float32
I32 = jnp.int32
MXU_DTYPE = jnp.bfloat16

N_HEADS_A, HEAD_DIM_A = 8, 64
IDX_HEADS, IDX_DIM = 4, 64
TOPK_MAX = 256
N_HEADS_R, DK_R, DV_R = 4, 128, 256
RET_CHUNK = 128
N_HEADS_M, HEAD_DIM_M = 4, 128
CONV_W = 3
ROPE_THETA = 10000.0
EPS = 1e-6

LANES = 128
SUBLANES = 8
BF16_ROWS = 16
VMEM_LIMIT = 56 * 1024 * 1024
NEG_BIG = -1e30
F32_LOWEST = float(np.finfo(np.float32).min)
T_PAD = 8


def _cparams(sem):
    return pltpu.CompilerParams(dimension_semantics=sem, vmem_limit_bytes=VMEM_LIMIT)


def _dot(a, b):
    return jnp.dot(a, b, preferred_element_type=F32)


def _dot_nt(a, b):
    return lax.dot_general(a, b, (((1,), (1,)), ((), ())), preferred_element_type=F32)


def _mx(a):
    return a.astype(MXU_DTYPE)


def _rms(x, g):
    return x * lax.rsqrt(jnp.mean(x * x, axis=-1, keepdims=True) + EPS) * g


def _sigmoid(x):
    return 1.0 / (1.0 + jnp.exp(-x))


def _rope_tables(pos, d):
    half = d // 2
    inv = 1.0 / (ROPE_THETA ** (jnp.arange(half, dtype=F32) * 2.0 / d))
    ang = pos.astype(F32)[:, None] * inv[None, :]
    cos, sin = jnp.cos(ang), jnp.sin(ang)
    reps = LANES // d
    cos_t = jnp.tile(jnp.concatenate([cos, cos], axis=1), (1, reps))
    sin_t = jnp.tile(jnp.concatenate([-sin, sin], axis=1), (1, reps))
    return cos_t, sin_t


def _rope(y, cos, sin, d):
    w = y.shape[1]
    half = d // 2
    reps = w // LANES
    c = jnp.concatenate([cos] * reps, axis=1) if reps > 1 else cos
    s = jnp.concatenate([sin] * reps, axis=1) if reps > 1 else sin
    lane = lax.broadcasted_iota(I32, y.shape, 1)
    first = (lane & (d - 1)) < half
    rot = jnp.where(first, pltpu.roll(y, w - half, 1), pltpu.roll(y, half, 1))
    return y * c + rot * s


def _proj_a_kernel(x_ref, g_ref, cos_ref, sin_ref, wa_ref, wv_ref,
                   qa_hm, ka_f, ka_hm, qi_hm, ki_f, ki_b, va_f, va_hm, wi_f):
    xn = _mx(_rms(x_ref[...], g_ref[...]))
    wa = N_HEADS_A * HEAD_DIM_A
    y = _rope(_dot(xn, wa_ref[...]), cos_ref[...], sin_ref[...], HEAD_DIM_A)
    q = y[:, :wa] * (HEAD_DIM_A ** -0.5)
    k = y[:, wa:2 * wa]
    ka_f[...] = k
    for h in range(N_HEADS_A):
        sl = slice(h * HEAD_DIM_A, (h + 1) * HEAD_DIM_A)
        qa_hm[h] = _mx(q[:, sl])
        ka_hm[h] = _mx(k[:, sl])
    qi = y[:, 2 * wa:2 * wa + IDX_HEADS * IDX_DIM]
    for h in range(IDX_HEADS):
        qi_hm[h] = _mx(qi[:, h * IDX_DIM:(h + 1) * IDX_DIM])
    ki = y[:, 2 * wa + IDX_HEADS * IDX_DIM:2 * wa + IDX_HEADS * IDX_DIM + IDX_DIM]
    ki_f[...] = ki
    ki_b[...] = _mx(ki)
    z = _dot(xn, wv_ref[...])
    v = z[:, :wa]
    va_f[...] = v
    for h in range(N_HEADS_A):
        va_hm[h] = _mx(v[:, h * HEAD_DIM_A:(h + 1) * HEAD_DIM_A])
    wi_f[...] = z[:, wa:wa + LANES]


def _proj_b_kernel(x_ref, g_ref, cos_ref, sin_ref, wr_ref, wvg_ref, qr_f, kr_f, vr_b, gr_f):
    xn = _mx(_rms(x_ref[...], g_ref[...]))
    wr = N_HEADS_R * DK_R
    y = _rope(_dot(xn, wr_ref[...]), cos_ref[...], sin_ref[...], DK_R)
    qr_f[...] = y[:, :wr]
    kr_f[...] = y[:, wr:] * (DK_R ** -0.5)
    z = _dot(xn, wvg_ref[...])
    wv = N_HEADS_R * DV_R
    vr_b[...] = _mx(z[:, :wv])
    gr_f[...] = z[:, wv:]


def _proj_c_kernel(x_ref, g_ref, wc_ref, qm_b, gates_f):
    xn = _mx(_rms(x_ref[...], g_ref[...]))
    z = _dot(xn, wc_ref[...])
    wm = N_HEADS_M * HEAD_DIM_M
    qm_b[...] = _mx(z[:, :wm])
    gates_f[...] = z[:, wm:]


def _row_spec(tm, w):
    return pl.BlockSpec((tm, w), lambda i: (i, 0))


def _full_spec(shape):
    nd = len(shape)
    return pl.BlockSpec(shape, lambda i: (0,) * nd)


def _hm_spec(nh, tm, d):
    return pl.BlockSpec((nh, tm, d), lambda i: (0, i, 0))


def _projections(x, gain, pos, wts, tm):
    rows, d = x.shape
    grid = (rows // tm,)
    wa = N_HEADS_A * HEAD_DIM_A
    cos64, sin64 = _rope_tables(pos, HEAD_DIM_A)
    cos128, sin128 = _rope_tables(pos, DK_R)
    g2 = gain.reshape(1, d)
    sds = jax.ShapeDtypeStruct
    outs_a = pl.pallas_call(
        _proj_a_kernel,
        grid=grid,
        in_specs=[_row_spec(tm, d), _full_spec((1, d)), _row_spec(tm, LANES), _row_spec(tm, LANES),
                  _full_spec(wts["wa"].shape), _full_spec(wts["wv"].shape)],
        out_specs=[_hm_spec(N_HEADS_A, tm, HEAD_DIM_A), _row_spec(tm, wa), _hm_spec(N_HEADS_A, tm, HEAD_DIM_A),
                   _hm_spec(IDX_HEADS, tm, IDX_DIM), _row_spec(tm, IDX_DIM), _row_spec(tm, IDX_DIM),
                   _row_spec(tm, wa), _hm_spec(N_HEADS_A, tm, HEAD_DIM_A), _row_spec(tm, LANES)],
        out_shape=[sds((N_HEADS_A, rows, HEAD_DIM_A), MXU_DTYPE), sds((rows, wa), F32),
                   sds((N_HEADS_A, rows, HEAD_DIM_A), MXU_DTYPE), sds((IDX_HEADS, rows, IDX_DIM), MXU_DTYPE),
                   sds((rows, IDX_DIM), F32), sds((rows, IDX_DIM), MXU_DTYPE),
                   sds((rows, wa), F32), sds((N_HEADS_A, rows, HEAD_DIM_A), MXU_DTYPE), sds((rows, LANES), F32)],
        compiler_params=_cparams(("parallel",)),
        name="proj_a",
    )(x, g2, cos64, sin64, wts["wa"], wts["wv"])
    names_a = ("qa_hm", "ka_f", "ka_hm", "qi_hm", "ki_f", "ki_b", "va_f", "va_hm", "wi_f")
    wr, wv = N_HEADS_R * DK_R, N_HEADS_R * DV_R
    outs_b = pl.pallas_call(
        _proj_b_kernel,
        grid=grid,
        in_specs=[_row_spec(tm, d), _full_spec((1, d)), _row_spec(tm, LANES), _row_spec(tm, LANES),
                  _full_spec(wts["wr"].shape), _full_spec(wts["wvg"].shape)],
        out_specs=[_row_spec(tm, wr), _row_spec(tm, wr), _row_spec(tm, wv), _row_spec(tm, wv)],
        out_shape=[sds((rows, wr), F32), sds((rows, wr), F32), sds((rows, wv), MXU_DTYPE), sds((rows, wv), F32)],
        compiler_params=_cparams(("parallel",)),
        name="proj_b",
    )(x, g2, cos128, sin128, wts["wr"], wts["wvg"])
    names_b = ("qr_f", "kr_f", "vr_b", "gr_f")
    wm = N_HEADS_M * HEAD_DIM_M
    outs_c = pl.pallas_call(
        _proj_c_kernel,
        grid=grid,
        in_specs=[_row_spec(tm, d), _full_spec((1, d)), _full_spec(wts["wc"].shape)],
        out_specs=[_row_spec(tm, wm), _row_spec(tm, 3 * d)],
        out_shape=[sds((rows, wm), MXU_DTYPE), sds((rows, 3 * d), F32)],
        compiler_params=_cparams(("parallel",)),
        name="proj_c",
    )(x, g2, wts["wc"])
    names_c = ("qm_b", "gates_f")
    out = dict(zip(names_a, outs_a))
    out.update(zip(names_b, outs_b))
    out.update(zip(names_c, outs_c))
    return out


def _prep_in_weights(w_in):
    d = w_in.shape[0]
    wa = N_HEADS_A * HEAD_DIM_A
    widths = (wa, wa, wa, IDX_HEADS * IDX_DIM, IDX_DIM, IDX_HEADS,
              N_HEADS_R * DK_R, N_HEADS_R * DK_R, N_HEADS_R * DV_R, N_HEADS_R * DV_R,
              N_HEADS_M * HEAD_DIM_M, 3 * d)
    offs = np.concatenate([[0], np.cumsum(widths)])
    seg = [w_in[:, int(offs[i]):int(offs[i + 1])] for i in range(len(widths))]
    q_a, k_a, v_a, q_i, k_i, w_i, q_r, k_r, v_r, g_r, q_m, gates = seg
    zpad = lambda n: jnp.zeros((d, n), w_in.dtype)
    return {
        "wa": _mx(jnp.concatenate([q_a, k_a, q_i, k_i, zpad(LANES - IDX_DIM)], axis=1)),
        "wv": _mx(jnp.concatenate([v_a, w_i, zpad(LANES - IDX_HEADS)], axis=1)),
        "wr": _mx(jnp.concatenate([q_r, k_r], axis=1)),
        "wvg": _mx(jnp.concatenate([v_r, g_r], axis=1)),
        "wc": _mx(jnp.concatenate([q_m, gates], axis=1)),
    }


SEL_ROWS = 64
SEL_WIDE = 512


def _key_to_f32(key):
    bits = jnp.where(key >= 0, key, key ^ jnp.int32(0x7FFFFFFF))
    return pltpu.bitcast(bits, F32)


def _count_rows(segments, rows, pred):
    outs = []
    for g in range(rows // SEL_ROWS):
        rs = slice(g * SEL_ROWS, (g + 1) * SEL_ROWS)
        acc = jnp.zeros((SEL_ROWS, SEL_WIDE), F32)
        base = 0
        for ref, n_wide in segments:
            def body(c, a, ref=ref, base=base):
                off = pl.multiple_of(c * SEL_WIDE, SEL_WIDE)
                x = ref[rs, pl.ds(off, SEL_WIDE)]
                idx = base + off + lax.broadcasted_iota(I32, x.shape, 1)
                return a + jnp.where(pred(x, idx, rs), 1.0, 0.0)
            acc = lax.fori_loop(0, n_wide, body, acc)
            base = base + n_wide * SEL_WIDE
        outs.append(jnp.sum(acc, axis=1, keepdims=True))
    return jnp.concatenate(outs, axis=0) if len(outs) > 1 else outs[0]


def _select_threshold(segments, rows, topk):
    kf = float(topk)

    def count_ge(thr):
        return _count_rows(segments, rows, lambda x, idx, rs: x >= thr[rs])

    lo0 = jnp.full((rows, 1), np.int32(-2139095041), I32)
    hi0 = jnp.full((rows, 1), np.int32(0x7F800000), I32)
    c0 = jnp.full((rows, 1), 2.0 ** 30, F32)

    def cond(carry):
        it, go = carry[0], carry[1]
        return jnp.logical_and(it < 34, go > 0)

    def body(carry):
        it, _, lo, hi, c_lo = carry
        mid = (lo >> 1) + (hi >> 1) + (lo & hi & 1)
        active = jnp.logical_and(mid != lo, c_lo != kf)
        cnt = count_ge(_key_to_f32(mid))
        ge = cnt >= kf
        up = jnp.logical_and(active, ge)
        dn = jnp.logical_and(active, jnp.logical_not(ge))
        lo = jnp.where(up, mid, lo)
        c_lo = jnp.where(up, cnt, c_lo)
        hi = jnp.where(dn, mid, hi)
        mid2 = (lo >> 1) + (hi >> 1) + (lo & hi & 1)
        still = jnp.logical_and(mid2 != lo, c_lo != kf)
        go = jnp.max(jnp.where(still, 1, 0).astype(I32))
        return it + 1, go, lo, hi, c_lo

    _, _, lo, _, c_lo = lax.while_loop(cond, body, (jnp.int32(0), jnp.int32(1), lo0, hi0, c0))
    tau = jnp.maximum(_key_to_f32(lo), F32_LOWEST)
    need = jnp.logical_and(c_lo > kf, _key_to_f32(lo) > F32_LOWEST)
    big = jnp.full((rows, 1), np.int32(2 ** 31 - 1), I32)

    def fix_ties(_):
        c_gt = _count_rows(segments, rows, lambda x, idx, rs: x > tau[rs])
        r = kf - c_gt
        total = sum(n for _, n in segments) * SEL_WIDE
        lo_j = jnp.full((rows, 1), -1, I32)
        hi_j = jnp.zeros((rows, 1), I32) + (total - 1)

        def jbody(_, c):
            lo_j, hi_j = c
            mid = (lo_j + hi_j) >> 1
            cnt = _count_rows(segments, rows,
                              lambda x, idx, rs: jnp.logical_and(x == tau[rs], idx <= mid[rs]))
            ok = cnt >= r
            return jnp.where(ok, lo_j, mid), jnp.where(ok, mid, hi_j)

        _, hi_j = lax.fori_loop(0, 32, jbody, (lo_j, hi_j))
        return jnp.where(need, hi_j, big)

    any_need = jnp.max(jnp.where(need, 1, 0).astype(I32))
    jstar = lax.cond(any_need > 0, fix_ties, lambda _: big, 0)
    return tau, jstar


def _write_mask(segments_out, rows, tau, jstar):
    base = 0
    for src, dst, n_wide, n_total in segments_out:
        for g in range(rows // SEL_ROWS):
            rs = slice(g * SEL_ROWS, (g + 1) * SEL_ROWS)
            t, js = tau[rs], jstar[rs]

            def body(c, carry, src=src, dst=dst, base=base, rs=rs, t=t, js=js):
                off = pl.multiple_of(c * SEL_WIDE, SEL_WIDE)
                x = src[rs, pl.ds(off, SEL_WIDE)]
                idx = base + off + lax.broadcasted_iota(I32, x.shape, 1)
                sel = jnp.where(x > t, 1.0, jnp.where(jnp.logical_and(x == t, idx <= js), 1.0, 0.0))
                dst[rs, pl.ds(off, SEL_WIDE)] = sel.astype(dst.dtype)
                return carry

            lax.fori_loop(0, n_wide, body, 0)

            def zbody(c, carry, dst=dst, rs=rs):
                off = pl.multiple_of(c * SEL_WIDE, SEL_WIDE)
                dst[rs, pl.ds(off, SEL_WIDE)] = jnp.zeros((SEL_ROWS, SEL_WIDE), dst.dtype)
                return carry

            lax.fori_loop(n_wide, n_total, zbody, 0)
        base = base + n_wide * SEL_WIDE


def _index_scores(qi_ref, w, kb):
    acc = None
    for h in range(IDX_HEADS):
        s = _dot_nt(_mx(qi_ref[h]), kb)
        t = w[:, h:h + 1] * jnp.maximum(s, 0.0)
        acc = t if acc is None else acc + t
    return acc


def _prompt_select_kernel(qi_ref, wi_ref, kidx_ref, mask_ref, i_scr, *, tq, tk, topk):
    i, j = pl.program_id(0), pl.program_id(1)
    nk = pl.num_programs(1)
    q_lo = i * tq
    n_wide = (q_lo + tq - 1) // tk + 1

    @pl.when(j < n_wide)
    def _():
        acc = _index_scores(qi_ref, wi_ref[...], kidx_ref[...])
        qpos = q_lo + lax.broadcasted_iota(I32, acc.shape, 0)
        kpos = j * tk + lax.broadcasted_iota(I32, acc.shape, 1)
        i_scr[:, pl.ds(pl.multiple_of(j * tk, tk), tk)] = jnp.where(kpos <= qpos, acc, -jnp.inf)

    @pl.when(j == nk - 1)
    def _():
        seg = [(i_scr, n_wide)]
        tau, jstar = _select_threshold(seg, tq, topk)
        _write_mask([(i_scr, mask_ref, n_wide, nk)], tq, tau, jstar)


def _prompt_select(qi_hm, wi_f, ki_b, topk, tq, tk):
    s = ki_b.shape[0]
    assert tk == SEL_WIDE and s % tk == 0 and s % tq == 0 and tq % SEL_ROWS == 0
    nq, nk = s // tq, s // tk
    kmap = lambda i, j: (jnp.minimum(j, (i * tq + tq - 1) // tk), 0)
    return pl.pallas_call(
        functools.partial(_prompt_select_kernel, tq=tq, tk=tk, topk=topk),
        grid=(nq, nk),
        in_specs=[pl.BlockSpec((IDX_HEADS, tq, IDX_DIM), lambda i, j: (0, i, 0)),
                  pl.BlockSpec((tq, LANES), lambda i, j: (i, 0)),
                  pl.BlockSpec((tk, IDX_DIM), kmap)],
        out_specs=pl.BlockSpec((tq, s), lambda i, j: (i, 0)),
        out_shape=jax.ShapeDtypeStruct((s, s), MXU_DTYPE),
        scratch_shapes=[pltpu.VMEM((tq, s), F32)],
        compiler_params=_cparams(("parallel", "arbitrary")),
        name="prompt_select",
    )(qi_hm, wi_f, ki_b)


def _flash_update(h, s, v, m_scr, l_scr, acc_scr):
    m_prev = m_scr[h]
    m_new = jnp.maximum(m_prev, jnp.max(s, axis=1, keepdims=True))
    alpha = jnp.exp(m_prev - m_new)
    p = jnp.exp(s - m_new)
    l_scr[h] = alpha * l_scr[h] + jnp.sum(p, axis=1, keepdims=True)
    acc_scr[h] = alpha * acc_scr[h] + _dot(_mx(p), v)
    m_scr[h] = m_new


def _prompt_attend_kernel(q_ref, k_ref, v_ref, mask_ref, o_ref, m_scr, l_scr, acc_scr, *, tq, tk):
    i, j = pl.program_id(0), pl.program_id(1)
    nk = pl.num_programs(1)

    @pl.when(j == 0)
    def _():
        m_scr[...] = jnp.full(m_scr.shape, NEG_BIG, F32)
        l_scr[...] = jnp.zeros(l_scr.shape, F32)
        acc_scr[...] = jnp.zeros(acc_scr.shape, F32)

    @pl.when(j * tk <= i * tq + tq - 1)
    def _():
        keep = mask_ref[...] > 0
        for h in range(N_HEADS_A):
            s = jnp.where(keep, _dot_nt(q_ref[h], k_ref[h]), NEG_BIG)
            _flash_update(h, s, v_ref[h], m_scr, l_scr, acc_scr)

    @pl.when(j == nk - 1)
    def _():
        for h in range(N_HEADS_A):
            o = acc_scr[h] / l_scr[h]
            o_ref[:, h * HEAD_DIM_A:(h + 1) * HEAD_DIM_A] = o.astype(o_ref.dtype)


def _prompt_attend(qa_hm, ka_hm, va_hm, mask, tq, tk):
    nh, s, dh = qa_hm.shape
    nq, nk = s // tq, s // tk
    diag = lambda i, j: jnp.minimum(j, (i * tq + tq - 1) // tk)
    return pl.pallas_call(
        functools.partial(_prompt_attend_kernel, tq=tq, tk=tk),
        grid=(nq, nk),
        in_specs=[pl.BlockSpec((nh, tq, dh), lambda i, j: (0, i, 0)),
                  pl.BlockSpec((nh, tk, dh), lambda i, j: (0, diag(i, j), 0)),
                  pl.BlockSpec((nh, tk, dh), lambda i, j: (0, diag(i, j), 0)),
                  pl.BlockSpec((tq, tk), lambda i, j: (i, diag(i, j)))],
        out_specs=pl.BlockSpec((tq, nh * dh), lambda i, j: (i, 0)),
        out_shape=jax.ShapeDtypeStruct((s, nh * dh), MXU_DTYPE),
        scratch_shapes=[pltpu.VMEM((nh, tq, 1), F32), pltpu.VMEM((nh, tq, 1), F32),
                        pltpu.VMEM((nh, tq, dh), F32)],
        compiler_params=_cparams(("parallel", "arbitrary")),
        name="prompt_attend",
    )(qa_hm, ka_hm, va_hm, mask)


PAGES_PER_STEP = 8


def _sample_index_kernel(pt_ref, qi_ref, wi_ref, *refs, page):
    del pt_ref
    pages, out_ref = refs[:-1], refs[-1]
    w = wi_ref[0]
    for p, kref in enumerate(pages):
        out_ref[0, :, p * page:(p + 1) * page] = _index_scores(qi_ref, w, _mx(kref[0]))


def _sample_index(page_table, qi_hm, wi_f, cache_kidx):
    db, n_pages = page_table.shape
    _, page, idim = cache_kidx.shape
    pps = math.gcd(PAGES_PER_STEP, n_pages)
    nsteps = n_pages // pps
    pt = page_table.reshape(-1).astype(I32)

    def kspec(p):
        return pl.BlockSpec((1, page, idim), lambda b, j, pt: (pt[b * n_pages + j * pps + p], 0, 0))

    grid_spec = pltpu.PrefetchScalarGridSpec(
        num_scalar_prefetch=1,
        grid=(db, nsteps),
        in_specs=[pl.BlockSpec((IDX_HEADS, T_PAD, idim), lambda b, j, pt: (0, b, 0)),
                  pl.BlockSpec((1, T_PAD, LANES), lambda b, j, pt: (b, 0, 0))]
                 + [kspec(p) for p in range(pps)],
        out_specs=pl.BlockSpec((1, T_PAD, pps * page), lambda b, j, pt: (b, 0, j)),
    )
    return pl.pallas_call(
        functools.partial(_sample_index_kernel, page=page),
        grid_spec=grid_spec,
        out_shape=jax.ShapeDtypeStruct((db, T_PAD, n_pages * page), F32),
        compiler_params=_cparams(("parallel", "arbitrary")),
        name="sample_index",
    )(pt, qi_hm, wi_f.reshape(db, T_PAD, LANES), *([cache_kidx] * pps))


def _sample_select_kernel(ipast_ref, qi_ref, wi_ref, kin_ref, mpast_ref, mnew_ref, inew_scr, *, t_real, topk):
    rows = ipast_ref.shape[0]
    acc = _index_scores(qi_ref, wi_ref[...], kin_ref[...])
    r = lax.broadcasted_iota(I32, acc.shape, 0)
    c = lax.broadcasted_iota(I32, acc.shape, 1)
    same = (r // T_PAD) == (c // T_PAD)
    tq, tc = r % T_PAD, c % T_PAD
    ok = jnp.logical_and(same, jnp.logical_and(tc <= tq, tc < t_real))
    inew_scr[...] = jnp.full(inew_scr.shape, -jnp.inf, F32)
    inew_scr[:, :rows] = jnp.where(ok, acc, -jnp.inf)
    n_past = ipast_ref.shape[1] // SEL_WIDE
    n_new = inew_scr.shape[1] // SEL_WIDE
    tau, jstar = _select_threshold([(ipast_ref, n_past), (inew_scr, n_new)], rows, topk)
    _write_mask([(ipast_ref, mpast_ref, n_past, n_past), (inew_scr, mnew_ref, n_new, n_new)], rows, tau, jstar)


def _sample_select(i_past, qi_hm, wi_f, ki_b, t_real, topk):
    rows, past = i_past.shape
    assert past % SEL_WIDE == 0 and rows % SEL_ROWS == 0
    wnew = -(-rows // SEL_WIDE) * SEL_WIDE
    return pl.pallas_call(
        functools.partial(_sample_select_kernel, t_real=t_real, topk=topk),
        out_shape=[jax.ShapeDtypeStruct((rows, past), F32), jax.ShapeDtypeStruct((rows, wnew), F32)],
        scratch_shapes=[pltpu.VMEM((rows, wnew), F32)],
        compiler_params=pltpu.CompilerParams(vmem_limit_bytes=VMEM_LIMIT),
        name="sample_select",
    )(i_past, qi_hm, wi_f, ki_b)


def _sample_attend_kernel(pt_ref, q_ref, mp_ref, mn_ref, kn_ref, vn_ref, *refs, page, pps):
    del pt_ref
    kpages, vpages = refs[:pps], refs[pps:2 * pps]
    o_ref, m_scr, l_scr, acc_scr = refs[2 * pps:]
    j = pl.program_id(1)
    nj = pl.num_programs(1)
    q = q_ref[0]
    nrep = q.shape[0] // T_PAD

    def update(m_t, k, v):
        keep = jnp.concatenate([m_t] * nrep, axis=0) > 0
        s = jnp.where(keep, _dot_nt(q, k), NEG_BIG)
        _flash_update(0, s, v, m_scr, l_scr, acc_scr)

    @pl.when(j == 0)
    def _():
        m_scr[...] = jnp.full(m_scr.shape, NEG_BIG, F32)
        l_scr[...] = jnp.zeros(l_scr.shape, F32)
        acc_scr[...] = jnp.zeros(acc_scr.shape, F32)

    for p in range(pps):
        update(mp_ref[0, :, p * page:(p + 1) * page], _mx(kpages[p][0]), _mx(vpages[p][0]))

    @pl.when(j == nj - 1)
    def _():
        update(mn_ref[0], kn_ref[0], vn_ref[0])
        full = acc_scr[0] / l_scr[0]
        lane = lax.broadcasted_iota(I32, (T_PAD, full.shape[1]), 1)
        out = jnp.zeros((T_PAD, full.shape[1]), F32)
        for h in range(nrep):
            blk = full[h * T_PAD:(h + 1) * T_PAD]
            out = out + jnp.where((lane // HEAD_DIM_A) == h, blk, 0.0)
        o_ref[0] = out.astype(o_ref.dtype)


def _sample_attend(page_table, q_bd, m_past, m_new, k_new, v_new, cache_k, cache_v):
    db, n_pages = page_table.shape
    _, page, hd = cache_k.shape
    pps = math.gcd(PAGES_PER_STEP, n_pages)
    nsteps = n_pages // pps
    nnew = k_new.shape[1]
    nq = q_bd.shape[1]
    pt = page_table.reshape(-1).astype(I32)

    def pspec(p):
        return pl.BlockSpec((1, page, hd), lambda b, j, pt: (pt[b * n_pages + j * pps + p], 0, 0))

    bspec = lambda shape: pl.BlockSpec((1,) + shape, lambda b, j, pt: (b, 0, 0))
    grid_spec = pltpu.PrefetchScalarGridSpec(
        num_scalar_prefetch=1,
        grid=(db, nsteps),
        in_specs=[bspec((nq, hd)),
                  pl.BlockSpec((1, T_PAD, pps * page), lambda b, j, pt: (b, 0, j)),
                  bspec((T_PAD, nnew)), bspec((nnew, hd)), bspec((nnew, hd))]
                 + [pspec(p) for p in range(pps)] * 2,
        out_specs=bspec((T_PAD, hd)),
        scratch_shapes=[pltpu.VMEM((1, nq, 1), F32), pltpu.VMEM((1, nq, 1), F32), pltpu.VMEM((1, nq, hd), F32)],
    )
    return pl.pallas_call(
        functools.partial(_sample_attend_kernel, page=page, pps=pps),
        grid_spec=grid_spec,
        out_shape=jax.ShapeDtypeStruct((db, T_PAD, hd), F32),
        compiler_params=_cparams(("parallel", "arbitrary")),
        name="sample_attend",
    )(pt, q_bd, m_past, m_new, k_new, v_new, *([cache_k] * pps), *([cache_v] * pps))


def _retention_tables(c_real, c_pad):
    h = np.arange(N_HEADS_R, dtype=np.float64)
    log_g = np.log1p(-np.exp2(-5.0 - h))
    i = np.arange(c_pad, dtype=np.float64)
    diff = i[:, None] - i[None, :]
    live = (diff >= 0) & (i[:, None] < c_real) & (i[None, :] < c_real)
    inner = np.where(live[None], np.exp(np.maximum(diff, 0.0)[None] * log_g[:, None, None]), 0.0)
    q_dec = np.exp((i + 1.0)[None, :] * log_g[:, None])
    k_dec = np.where(i[None, :] < c_real, np.exp((c_real - 1.0 - i)[None, :] * log_g[:, None]), 0.0)
    c_dec = np.exp(c_real * log_g)
    f = lambda a: jnp.asarray(a, F32)
    return f(inner), f(q_dec[:, :, None]), f(k_dec[:, :, None]), [float(v) for v in c_dec]


def _retention_kernel(q_ref, k_ref, v_ref, g_ref, s0_ref, inner_ref, qdec_ref, kdec_ref,
                      o_ref, s_out_ref, s_scr, *, c_dec):
    j = pl.program_id(1)
    nj = pl.num_programs(1)

    @pl.when(j == 0)
    def _():
        s_scr[...] = s0_ref[0]

    for h in range(N_HEADS_R):
        q = q_ref[:, h * DK_R:(h + 1) * DK_R]
        k = k_ref[:, h * DK_R:(h + 1) * DK_R]
        v = v_ref[:, h * DV_R:(h + 1) * DV_R]
        s_prev = s_scr[h]
        a = _dot_nt(_mx(q), _mx(k)) * inner_ref[h]
        o = _dot(_mx(a), v) + _dot(_mx(q), _mx(s_prev)) * qdec_ref[h]
        kd = k * kdec_ref[h]
        s_scr[h] = s_prev * c_dec[h] + _dot(_mx(kd.T), v)
        mu = jnp.mean(o, axis=-1, keepdims=True)
        var = jnp.mean(jnp.square(o - mu), axis=-1, keepdims=True)
        gn = (o - mu) * lax.rsqrt(var + EPS)
        g = g_ref[:, h * DV_R:(h + 1) * DV_R]
        o_ref[:, h * DV_R:(h + 1) * DV_R] = (gn * (g * _sigmoid(g))).astype(o_ref.dtype)

    @pl.when(j == nj - 1)
    def _():
        s_out_ref[0] = s_scr[...]


def _retention(qr, kr, vr, gr, s0, c_real):
    b = s0.shape[0]
    c = RET_CHUNK
    n = qr.shape[0] // (b * c)
    inner, qdec, kdec, c_dec = _retention_tables(c_real, c)
    wr, wv = N_HEADS_R * DK_R, N_HEADS_R * DV_R
    rmap = lambda bi, j: (bi * n + j, 0)
    full3 = lambda shape: pl.BlockSpec(shape, lambda bi, j: (0, 0, 0))
    return pl.pallas_call(
        functools.partial(_retention_kernel, c_dec=c_dec),
        grid=(b, n),
        in_specs=[pl.BlockSpec((c, wr), rmap), pl.BlockSpec((c, wr), rmap), pl.BlockSpec((c, wv), rmap),
                  pl.BlockSpec((c, wv), rmap),
                  pl.BlockSpec((1, N_HEADS_R, DK_R, DV_R), lambda bi, j: (bi, 0, 0, 0)),
                  full3(inner.shape), full3(qdec.shape), full3(kdec.shape)],
        out_specs=[pl.BlockSpec((c, wv), rmap),
                   pl.BlockSpec((1, N_HEADS_R, DK_R, DV_R), lambda bi, j: (bi, 0, 0, 0))],
        out_shape=[jax.ShapeDtypeStruct((b * n * c, wv), MXU_DTYPE),
                   jax.ShapeDtypeStruct((b, N_HEADS_R, DK_R, DV_R), F32)],
        scratch_shapes=[pltpu.VMEM((N_HEADS_R, DK_R, DV_R), F32)],
        compiler_params=_cparams(("parallel", "arbitrary")),
        name="retention",
    )(qr, kr, vr, gr, s0, inner, qdec, kdec)


def _cross_kernel(q_ref, mk_ref, mv_ref, o_ref):
    scale = HEAD_DIM_M ** -0.5
    for h in range(N_HEADS_M):
        sl = slice(h * HEAD_DIM_M, (h + 1) * HEAD_DIM_M)
        s = _dot_nt(_mx(q_ref[:, sl]), _mx(mk_ref[0, :, sl])) * scale
        p = jnp.exp(s - jnp.max(s, axis=1, keepdims=True))
        p = p / jnp.sum(p, axis=1, keepdims=True)
        o_ref[:, sl] = _dot(_mx(p), _mx(mv_ref[0, :, sl])).astype(o_ref.dtype)


def _cross_attend(qm, mk, mv, tm, out_dtype):
    b, n_mem, hd = mk.shape
    nt = qm.shape[0] // (b * tm)
    return pl.pallas_call(
        _cross_kernel,
        grid=(b, nt),
        in_specs=[pl.BlockSpec((tm, hd), lambda bi, i: (bi * nt + i, 0)),
                  pl.BlockSpec((1, n_mem, hd), lambda bi, i: (bi, 0, 0)),
                  pl.BlockSpec((1, n_mem, hd), lambda bi, i: (bi, 0, 0))],
        out_specs=pl.BlockSpec((tm, hd), lambda bi, i: (bi * nt + i, 0)),
        out_shape=jax.ShapeDtypeStruct(qm.shape, out_dtype),
        compiler_params=_cparams(("parallel", "parallel")),
        name="cross_attend",
    )(qm, mk, mv)


def _memkv_kernel(x_ref, g_ref, w_ref, o_ref):
    o_ref[...] = _dot(_mx(_rms(x_ref[...], g_ref[...])), w_ref[...])


def _memory_kv(mem, gain, w):
    rows, d = mem.shape
    return pl.pallas_call(
        _memkv_kernel,
        out_shape=jax.ShapeDtypeStruct((rows, w.shape[1]), F32),
        compiler_params=pltpu.CompilerParams(vmem_limit_bytes=VMEM_LIMIT),
        name="memory_kv",
    )(mem, gain.reshape(1, d), _mx(w))


def _merge_kernel(x_ref, oa_ref, or_ref, om_ref, gates_ref, wpa_ref, wpb_ref, wpc_ref, wo_ref, g_ref, h_ref):
    d = x_ref.shape[1]
    gt = gates_ref[...]
    mixed = (_sigmoid(gt[:, :d]) * _dot(_mx(oa_ref[...]), wpa_ref[...])
             + _sigmoid(gt[:, d:2 * d]) * _dot(_mx(or_ref[...]), wpb_ref[...])
             + _sigmoid(gt[:, 2 * d:]) * _dot(_mx(om_ref[...]), wpc_ref[...]))
    z = _dot(_mx(mixed), wo_ref[...])
    h_ref[...] = x_ref[...] + _rms(z, g_ref[...])


def _merge(x, oa, o_r, om, gates, wpa, wpb, wpc, wo, gain, tm):
    rows, d = x.shape
    return pl.pallas_call(
        _merge_kernel,
        grid=(rows // tm,),
        in_specs=[_row_spec(tm, d), _row_spec(tm, oa.shape[1]), _row_spec(tm, o_r.shape[1]),
                  _row_spec(tm, om.shape[1]), _row_spec(tm, 3 * d),
                  _full_spec(wpa.shape), _full_spec(wpb.shape), _full_spec(wpc.shape), _full_spec(wo.shape),
                  _full_spec((1, d))],
        out_specs=_row_spec(tm, d),
        out_shape=jax.ShapeDtypeStruct((rows, d), F32),
        compiler_params=_cparams(("parallel",)),
        name="merge",
    )(x, oa, o_r, om, gates, wpa, wpb, wpc, wo, gain.reshape(1, d))


HALO = BF16_ROWS


def _ffn_kernel(h_ref, halo_ref, s0_ref, s1_ref, g1_ref, g2_ref, wu_ref, wg_ref, cw_ref, cb_ref, wd_ref,
                y_ref, utail_ref, x_scr, u_scr, *, tm, seq, keep):
    i = pl.program_id(0)
    h = h_ref[...]
    hn = _rms(h, g1_ref[...])
    x_scr[HALO:, :] = _mx(hn)
    x_scr[:HALO, :] = _mx(_rms(halo_ref[...], g1_ref[...]))
    xc = x_scr[...]
    u_scr[...] = _dot(xc, wu_ref[...])
    gate = _dot(xc[HALO:], wg_ref[...])
    cur = u_scr[HALO:, :]
    prev1 = u_scr[HALO - 1:HALO - 1 + tm, :]
    prev2 = u_scr[HALO - 2:HALO - 2 + tm, :]
    seq_loc = min(seq, tm)
    t = lax.rem(lax.broadcasted_iota(I32, (tm, 1), 0), seq_loc)
    t = jnp.where(lax.rem(i * tm, seq) == 0, t, CONV_W)
    st0, st1 = s0_ref[...], s1_ref[...]
    if st0.shape[0] != tm:
        st0, st1 = st0[0:1], st1[0:1]
    prev1 = jnp.where(t == 0, st1, prev1)
    prev2 = jnp.where(t == 0, st0, jnp.where(t == 1, st1, prev2))
    c = cb_ref[...] + prev2 * cw_ref[0:1, :] + prev1 * cw_ref[1:2, :] + cur * cw_ref[2:3, :]
    act = jax.nn.gelu(c, approximate=True) * gate
    ff = _dot(_mx(act), wd_ref[...])
    y_ref[...] = h + _rms(ff, g2_ref[...])
    utail_ref[...] = u_scr[HALO + tm - keep:, :]


def _conv_ffn(h, s0e, s1e, g1, g2, wu, wg, cw, cb, wd, tm, seq, keep):
    rows, d = h.shape
    f = wu.shape[1]
    nt = rows // tm
    hb = tm // HALO
    sr = s0e.shape[0]
    return pl.pallas_call(
        functools.partial(_ffn_kernel, tm=tm, seq=seq, keep=keep),
        grid=(nt,),
        in_specs=[_row_spec(tm, d),
                  pl.BlockSpec((HALO, d), lambda i: (jnp.maximum(i * hb - 1, 0), 0)),
                  _full_spec((sr, f)), _full_spec((sr, f)),
                  _full_spec((1, d)), _full_spec((1, d)),
                  _full_spec(wu.shape), _full_spec(wg.shape), _full_spec(cw.shape), _full_spec((1, f)),
                  _full_spec(wd.shape)],
        out_specs=[_row_spec(tm, d), _row_spec(keep, f)],
        out_shape=[jax.ShapeDtypeStruct((rows, d), F32), jax.ShapeDtypeStruct((nt * keep, f), F32)],
        scratch_shapes=[pltpu.VMEM((tm + HALO, d), MXU_DTYPE), pltpu.VMEM((tm + HALO, f), F32)],
        compiler_params=_cparams(("parallel",)),
        name="conv_ffn",
    )(h, h, s0e, s1e, g1.reshape(1, d), g2.reshape(1, d), wu, wg, cw, cb.reshape(1, f), wd)


def _layer_weights(l, w_in, w_proj_a, w_proj_b, w_proj_c, w_out, w_up, w_down):
    wts = _prep_in_weights(w_in[l])
    f = w_down.shape[1]
    wts.update(wpa=_mx(w_proj_a[l]), wpb=_mx(w_proj_b[l]), wpc=_mx(w_proj_c[l]), wo=_mx(w_out[l]),
               wu=_mx(w_up[l][:, :f]), wg=_mx(w_up[l][:, f:]), wd=_mx(w_down[l]))
    return wts


def _prompt_layer(x, mem, wts, norms, conv_w, conv_b, w_mem_kv, tiles):
    s, d = x.shape
    f = wts["wd"].shape[0]
    pos = jnp.arange(s)
    pr = _projections(x, norms["pre_mix"], pos, wts, tiles["proj"])
    topk = min(TOPK_MAX, s // 4)
    mask = _prompt_select(pr["qi_hm"], pr["wi_f"], pr["ki_b"], topk, tiles["sel_q"], SEL_WIDE)
    o_a = _prompt_attend(pr["qa_hm"], pr["ka_hm"], pr["va_hm"], mask, tiles["att_q"], tiles["att_k"])
    s0 = jnp.zeros((1, N_HEADS_R, DK_R, DV_R), F32)
    o_r, ret_new = _retention(pr["qr_f"], pr["kr_f"], pr["vr_b"], pr["gr_f"], s0, RET_CHUNK)
    kv = _memory_kv(mem, norms["mem"], w_mem_kv)
    wm = N_HEADS_M * HEAD_DIM_M
    mk, mv = kv[:, :wm], kv[:, wm:]
    o_m = _cross_attend(pr["qm_b"], mk[None], mv[None], tiles["cross"], MXU_DTYPE)
    h = _merge(x, o_a, o_r, o_m, pr["gates_f"], wts["wpa"], wts["wpb"], wts["wpc"], wts["wo"],
               norms["post_mix"], tiles["merge"])
    zst = jnp.zeros((SUBLANES, f), F32)
    y, utail = _conv_ffn(h, zst, zst, norms["pre_ffn"], norms["post_ffn"], wts["wu"], wts["wg"], conv_w, conv_b,
                         wts["wd"], tiles["ffn"], s, SUBLANES)
    conv_new = utail[-(CONV_W - 1):]
    return y, pr["ka_f"], pr["va_f"], pr["ki_f"], ret_new, conv_new, mk, mv


def _sample_layer(x, wts, norms, conv_w, conv_b, cache_k, cache_v, cache_kidx, mem_k, mem_v,
                  state_ret, state_conv, page_table):
    db, t, d = x.shape
    f = wts["wd"].shape[0]
    n_pages = page_table.shape[1]
    page = cache_k.shape[1]
    past = n_pages * page
    rows = db * T_PAD
    xp = jnp.pad(x, ((0, 0), (0, T_PAD - t), (0, 0))).reshape(rows, d)
    pos = jnp.tile(past + jnp.arange(T_PAD), db)
    pr = _projections(xp, norms["pre_mix"], pos, wts, rows)
    hd = N_HEADS_A * HEAD_DIM_A

    topk = min(TOPK_MAX, (past + t) // 4)
    i_past = _sample_index(page_table, pr["qi_hm"].astype(F32), pr["wi_f"], cache_kidx).reshape(rows, past)
    m_past, m_new = _sample_select(i_past, pr["qi_hm"], pr["wi_f"], pr["ki_b"], t, topk)
    nnew = m_new.shape[1]
    own = m_new[:, :rows].reshape(db, T_PAD, db, T_PAD)[jnp.arange(db), :, jnp.arange(db), :]
    m_new_own = jnp.pad(own, ((0, 0), (0, 0), (0, LANES - T_PAD)))
    q_rows = pr["qa_hm"].reshape(N_HEADS_A, db, T_PAD, HEAD_DIM_A)
    eye = jnp.eye(N_HEADS_A, dtype=MXU_DTYPE)
    q_bd = jnp.einsum("hbtd,hg->bhtgd", q_rows, eye).reshape(db, N_HEADS_A * T_PAD, hd)
    k_new = jnp.pad(_mx(pr["ka_f"]).reshape(db, T_PAD, hd), ((0, 0), (0, LANES - T_PAD), (0, 0)))
    v_new = jnp.pad(_mx(pr["va_f"]).reshape(db, T_PAD, hd), ((0, 0), (0, LANES - T_PAD), (0, 0)))
    o_a = _sample_attend(page_table, q_bd, m_past.reshape(db, T_PAD, past), m_new_own, k_new, v_new,
                         cache_k.reshape(-1, page, hd), cache_v.reshape(-1, page, hd)).reshape(rows, hd)

    padc = lambda a: jnp.pad(a.reshape(db, T_PAD, -1), ((0, 0), (0, RET_CHUNK - T_PAD), (0, 0))).reshape(db * RET_CHUNK, -1)
    o_r, ret_new = _retention(padc(pr["qr_f"]), padc(pr["kr_f"]), padc(pr["vr_b"]), padc(pr["gr_f"]), state_ret, t)
    o_r = o_r.reshape(db, RET_CHUNK, -1)[:, :T_PAD].reshape(rows, -1)

    wm = N_HEADS_M * HEAD_DIM_M
    o_m = _cross_attend(pr["qm_b"].astype(F32), mem_k.reshape(db, -1, wm), mem_v.reshape(db, -1, wm), T_PAD, F32)

    h = _merge(xp, o_a, o_r, o_m, pr["gates_f"], wts["wpa"], wts["wpb"], wts["wpc"], wts["wo"],
               norms["post_mix"], rows)
    s0e = jnp.repeat(state_conv[:, 0], T_PAD, axis=0)
    s1e = jnp.repeat(state_conv[:, 1], T_PAD, axis=0)
    y, u_all = _conv_ffn(h, s0e, s1e, norms["pre_ffn"], norms["post_ffn"], wts["wu"], wts["wg"], conv_w, conv_b,
                         wts["wd"], rows, T_PAD, rows)
    ext = jnp.concatenate([state_conv.astype(F32), u_all.reshape(db, T_PAD, f)[:, :t]], axis=1)
    conv_new = ext[:, t:]
    unpad = lambda a: a.reshape(db, T_PAD, -1)[:, :t]
    return (unpad(y), unpad(pr["ka_f"]), unpad(pr["va_f"]), unpad(pr["ki_f"]), ret_new, conv_new)


PROMPT_TILES = dict(proj=512, sel_q=256, att_q=256, att_k=512, cross=512, merge=512, ffn=256)


def kernel(x_prompt, x_sample, cache_k, cache_v, cache_kidx, cache_mem_k, cache_mem_v, state_ret, state_conv,
           page_table, mem_prompt, norm_pre_mix, norm_post_mix, norm_pre_ffn, norm_post_ffn, norm_mem,
           w_in, w_mem_kv, w_proj_a, w_proj_b, w_proj_c, w_out, w_up, conv_w, conv_b, w_down):
    bp, s, d = x_prompt.shape
    db, t, _ = x_sample.shape
    depth = w_in.shape[0]
    assert bp == 1 and t <= T_PAD and CONV_W - 1 <= t
    tiles = {k: min(v, s) for k, v in PROMPT_TILES.items()}
    yp, ys = x_prompt[0], x_sample
    outs = [[] for _ in range(12)]
    for l in range(depth):
        wts = _layer_weights(l, w_in, w_proj_a, w_proj_b, w_proj_c, w_out, w_up, w_down)
        norms = dict(pre_mix=norm_pre_mix[l], post_mix=norm_post_mix[l], pre_ffn=norm_pre_ffn[l],
                     post_ffn=norm_post_ffn[l], mem=norm_mem[l])
        yp, kp, vp, kip, rp, cp, mk, mv = _prompt_layer(yp, mem_prompt[0], wts, norms, conv_w[l], conv_b[l],
                                                        w_mem_kv[l], tiles)
        ys, ks, vs, kis, rs, cs = _sample_layer(ys, wts, norms, conv_w[l], conv_b[l], cache_k[l], cache_v[l],
                                                cache_kidx[l], cache_mem_k[l], cache_mem_v[l], state_ret[l],
                                                state_conv[l], page_table)
        n_mem = mk.shape[0]
        vals = (kp.reshape(1, s, N_HEADS_A, HEAD_DIM_A), vp.reshape(1, s, N_HEADS_A, HEAD_DIM_A),
                kip.reshape(1, s, IDX_DIM), rp, cp[None],
                mk.reshape(1, n_mem, N_HEADS_M, HEAD_DIM_M), mv.reshape(1, n_mem, N_HEADS_M, HEAD_DIM_M),
                ks.reshape(db, t, N_HEADS_A, HEAD_DIM_A), vs.reshape(db, t, N_HEADS_A, HEAD_DIM_A),
                kis, rs, cs)
        for o, v in zip(outs, vals):
            o.append(v)
    stacked = [jnp.stack(o) for o in outs]
    return (yp[None], ys, *stacked)
```

```python
import functools
import math

import numpy as np
import jax
import jax.numpy as jnp
from jax import lax
from jax.experimental import pallas as pl
from jax.experimental.pallas import tpu as pltpu

F32 = jnp.float32
I32 = jnp.int32
MXU_DTYPE = jnp.bfloat16

N_HEADS_A, HEAD_DIM_A = 8, 64
IDX_HEADS, IDX_DIM = 4, 64
TOPK_MAX = 256
N_HEADS_R, DK_R, DV_R = 4, 128, 256
RET_CHUNK = 128
N_HEADS_M, HEAD_DIM_M = 4, 128
CONV_W = 3
ROPE_THETA = 10000.0
EPS = 1e-6

LANES = 128
SUBLANES = 8
BF16_ROWS = 16
VMEM_LIMIT = 56 * 1024 * 1024
NEG_BIG = -1e30
F32_LOWEST = float(np.finfo(np.float32).min)
T_PAD = 8


def _cparams(sem):
    return pltpu.CompilerParams(dimension_semantics=sem, vmem_limit_bytes=VMEM_LIMIT)


def _dot(a, b):
    return jnp.dot(a, b, preferred_element_type=F32)


def _dot_nt(a, b):
    return lax.dot_general(a, b, (((1,), (1,)), ((), ())), preferred_element_type=F32)


def _mx(a):
    return a.astype(MXU_DTYPE)


def _rms(x, g):
    return x * lax.rsqrt(jnp.mean(x * x, axis=-1, keepdims=True) + EPS) * g


def _sigmoid(x):
    return 1.0 / (1.0 + jnp.exp(-x))


def _rope_tables(pos, d):
    half = d // 2
    inv = 1.0 / (ROPE_THETA ** (jnp.arange(half, dtype=F32) * 2.0 / d))
    ang = pos.astype(F32)[:, None] * inv[None, :]
    cos, sin = jnp.cos(ang), jnp.sin(ang)
    reps = LANES // d
    cos_t = jnp.tile(jnp.concatenate([cos, cos], axis=1), (1, reps))
    sin_t = jnp.tile(jnp.concatenate([-sin, sin], axis=1), (1, reps))
    return cos_t, sin_t


def _rope(y, cos, sin, d):
    w = y.shape[1]
    half = d // 2
    reps = w // LANES
    c = jnp.concatenate([cos] * reps, axis=1) if reps > 1 else cos
    s = jnp.concatenate([sin] * reps, axis=1) if reps > 1 else sin
    lane = lax.broadcasted_iota(I32, y.shape, 1)
    first = (lane & (d - 1)) < half
    rot = jnp.where(first, pltpu.roll(y, w - half, 1), pltpu.roll(y, half, 1))
    return y * c + rot * s


def _proj_a_kernel(x_ref, g_ref, cos_ref, sin_ref, wa_ref, wv_ref,
                   qa_hm, ka_f, ka_hm, qi_hm, ki_f, ki_b, va_f, va_hm, wi_f):
    xn = _mx(_rms(x_ref[...], g_ref[...]))
    wa = N_HEADS_A * HEAD_DIM_A
    y = _rope(_dot(xn, wa_ref[...]), cos_ref[...], sin_ref[...], HEAD_DIM_A)
    q = y[:, :wa] * (HEAD_DIM_A ** -0.5)
    k = y[:, wa:2 * wa]
    ka_f[...] = k
    for h in range(N_HEADS_A):
        sl = slice(h * HEAD_DIM_A, (h + 1) * HEAD_DIM_A)
        qa_hm[h] = _mx(q[:, sl])
        ka_hm[h] = _mx(k[:, sl])
    qi = y[:, 2 * wa:2 * wa + IDX_HEADS * IDX_DIM]
    for h in range(IDX_HEADS):
        qi_hm[h] = _mx(qi[:, h * IDX_DIM:(h + 1) * IDX_DIM])
    ki = y[:, 2 * wa + IDX_HEADS * IDX_DIM:2 * wa + IDX_HEADS * IDX_DIM + IDX_DIM]
    ki_f[...] = ki
    ki_b[...] = _mx(ki)
    z = _dot(xn, wv_ref[...])
    v = z[:, :wa]
    va_f[...] = v
    for h in range(N_HEADS_A):
        va_hm[h] = _mx(v[:, h * HEAD_DIM_A:(h + 1) * HEAD_DIM_A])
    wi_f[...] = z[:, wa:wa + LANES]


def _proj_b_kernel(x_ref, g_ref, cos_ref, sin_ref, wr_ref, wvg_ref, qr_f, kr_f, vr_b, gr_f):
    xn = _mx(_rms(x_ref[...], g_ref[...]))
    wr = N_HEADS_R * DK_R
    y = _rope(_dot(xn, wr_ref[...]), cos_ref[...], sin_ref[...], DK_R)
    qr_f[...] = y[:, :wr]
    kr_f[...] = y[:, wr:] * (DK_R ** -0.5)
    z = _dot(xn, wvg_ref[...])
    wv = N_HEADS_R * DV_R
    vr_b[...] = _mx(z[:, :wv])
    gr_f[...] = z[:, wv:]


def _proj_c_kernel(x_ref, g_ref, wc_ref, qm_b, gates_f):
    xn = _mx(_rms(x_ref[...], g_ref[...]))
    z = _dot(xn, wc_ref[...])
    wm = N_HEADS_M * HEAD_DIM_M
    qm_b[...] = _mx(z[:, :wm])
    gates_f[...] = z[:, wm:]


def _row_spec(tm, w):
    return pl.BlockSpec((tm, w), lambda i: (i, 0))


def _full_spec(shape):
    nd = len(shape)
    return pl.BlockSpec(shape, lambda i: (0,) * nd)


def _hm_spec(nh, tm, d):
    return pl.BlockSpec((nh, tm, d), lambda i: (0, i, 0))


def _projections(x, gain, pos, wts, tm):
    rows, d = x.shape
    grid = (rows // tm,)
    wa = N_HEADS_A * HEAD_DIM_A
    cos64, sin64 = _rope_tables(pos, HEAD_DIM_A)
    cos128, sin128 = _rope_tables(pos, DK_R)
    g2 = gain.reshape(1, d)
    sds = jax.ShapeDtypeStruct
    outs_a = pl.pallas_call(
        _proj_a_kernel,
        grid=grid,
        in_specs=[_row_spec(tm, d), _full_spec((1, d)), _row_spec(tm, LANES), _row_spec(tm, LANES),
                  _full_spec(wts["wa"].shape), _full_spec(wts["wv"].shape)],
        out_specs=[_hm_spec(N_HEADS_A, tm, HEAD_DIM_A), _row_spec(tm, wa), _hm_spec(N_HEADS_A, tm, HEAD_DIM_A),
                   _hm_spec(IDX_HEADS, tm, IDX_DIM), _row_spec(tm, IDX_DIM), _row_spec(tm, IDX_DIM),
                   _row_spec(tm, wa), _hm_spec(N_HEADS_A, tm, HEAD_DIM_A), _row_spec(tm, LANES)],
        out_shape=[sds((N_HEADS_A, rows, HEAD_DIM_A), MXU_DTYPE), sds((rows, wa), F32),
                   sds((N_HEADS_A, rows, HEAD_DIM_A), MXU_DTYPE), sds((IDX_HEADS, rows, IDX_DIM), MXU_DTYPE),
                   sds((rows, IDX_DIM), F32), sds((rows, IDX_DIM), MXU_DTYPE),
                   sds((rows, wa), F32), sds((N_HEADS_A, rows, HEAD_DIM_A), MXU_DTYPE), sds((rows, LANES), F32)],
        compiler_params=_cparams(("parallel",)),
        name="proj_a",
    )(x, g2, cos64, sin64, wts["wa"], wts["wv"])
    names_a = ("qa_hm", "ka_f", "ka_hm", "qi_hm", "ki_f", "ki_b", "va_f", "va_hm", "wi_f")
    wr, wv = N_HEADS_R * DK_R, N_HEADS_R * DV_R
    outs_b = pl.pallas_call(
        _proj_b_kernel,
        grid=grid,
        in_specs=[_row_spec(tm, d), _full_spec((1, d)), _row_spec(tm, LANES), _row_spec(tm, LANES),
                  _full_spec(wts["wr"].shape), _full_spec(wts["wvg"].shape)],
        out_specs=[_row_spec(tm, wr), _row_spec(tm, wr), _row_spec(tm, wv), _row_spec(tm, wv)],
        out_shape=[sds((rows, wr), F32), sds((rows, wr), F32), sds((rows, wv), MXU_DTYPE), sds((rows, wv), F32)],
        compiler_params=_cparams(("parallel",)),
        name="proj_b",
    )(x, g2, cos128, sin128, wts["wr"], wts["wvg"])
    names_b = ("qr_f", "kr_f", "vr_b", "gr_f")
    wm = N_HEADS_M * HEAD_DIM_M
    outs_c = pl.pallas_call(
        _proj_c_kernel,
        grid=grid,
        in_specs=[_row_spec(tm, d), _full_spec((1, d)), _full_spec(wts["wc"].shape)],
        out_specs=[_row_spec(tm, wm), _row_spec(tm, 3 * d)],
        out_shape=[sds((rows, wm), MXU_DTYPE), sds((rows, 3 * d), F32)],
        compiler_params=_cparams(("parallel",)),
        name="proj_c",
    )(x, g2, wts["wc"])
    names_c = ("qm_b", "gates_f")
    out = dict(zip(names_a, outs_a))
    out.update(zip(names_b, outs_b))
    out.update(zip(names_c, outs_c))
    return out


def _prep_in_weights(w_in):
    d = w_in.shape[0]
    wa = N_HEADS_A * HEAD_DIM_A
    widths = (wa, wa, wa, IDX_HEADS * IDX_DIM, IDX_DIM, IDX_HEADS,
              N_HEADS_R * DK_R, N_HEADS_R * DK_R, N_HEADS_R * DV_R, N_HEADS_R * DV_R,
              N_HEADS_M * HEAD_DIM_M, 3 * d)
    offs = np.concatenate([[0], np.cumsum(widths)])
    seg = [w_in[:, int(offs[i]):int(offs[i + 1])] for i in range(len(widths))]
    q_a, k_a, v_a, q_i, k_i, w_i, q_r, k_r, v_r, g_r, q_m, gates = seg
    zpad = lambda n: jnp.zeros((d, n), w_in.dtype)
    return {
        "wa": _mx(jnp.concatenate([q_a, k_a, q_i, k_i, zpad(LANES - IDX_DIM)], axis=1)),
        "wv": _mx(jnp.concatenate([v_a, w_i, zpad(LANES - IDX_HEADS)], axis=1)),
        "wr": _mx(jnp.concatenate([q_r, k_r], axis=1)),
        "wvg": _mx(jnp.concatenate([v_r, g_r], axis=1)),
        "wc": _mx(jnp.concatenate([q_m, gates], axis=1)),
    }


SEL_ROWS = 64
SEL_WIDE = 512
SEL_GROUPS = 2 * LANES
INT_MAX = np.int32(2 ** 31 - 1)
MIN_NORMAL_KEY = 0x00800000
SEARCH_PERIOD = 4
SEARCH_CAP = SEARCH_PERIOD * 33


def _key_to_f32(key):
    bits = jnp.where(key >= 0, key, key ^ jnp.int32(0x7FFFFFFF))
    return pltpu.bitcast(bits, F32)


def _f32_to_key(f):
    bits = pltpu.bitcast(f, I32)
    return jnp.where(f == 0.0, 0, jnp.where(bits >= 0, bits, bits ^ jnp.int32(0x7FFFFFFF)))


def _sweep(segments, rs, init, fn):
    acc = init
    base = 0
    for ref, n_wide in segments:
        def body(c, a, ref=ref, base=base):
            off = c * SEL_WIDE
            for k in range(SEL_WIDE // LANES):
                x = ref[rs, pl.ds(pl.multiple_of(off + k * LANES, LANES), LANES)]
                a = fn(a, x, base + off + k * LANES, k)
            return a
        acc = lax.fori_loop(0, n_wide, body, acc)
        base = base + n_wide * SEL_WIDE
    return acc


def _count(segments, rows, make_pred):
    outs = []
    for g in range(rows // SEL_ROWS):
        rs = slice(g * SEL_ROWS, (g + 1) * SEL_ROWS)
        pred = make_pred(rs)
        acc = _sweep(segments, rs, jnp.zeros((SEL_ROWS, LANES), F32),
                     lambda a, x, idx0, k: a + jnp.where(pred(x, idx0), 1.0, 0.0))
        outs.append(jnp.sum(acc, axis=1, keepdims=True))
    return jnp.concatenate(outs, axis=0) if len(outs) > 1 else outs[0]


def _lanes(a, n=LANES):
    return jnp.broadcast_to(a, (a.shape[0], n))


def _search(count_fn, lo, hi, c_lo, c_hi, target, alive):
    def unfinished(lo, hi, c_lo):
        return jnp.logical_and(alive, jnp.logical_and(c_lo > target, lo + 1 < hi))

    def any_row(flag):
        return jnp.max(jnp.where(flag, 1, 0).astype(I32))

    def cond(carry):
        return jnp.logical_and(carry[0] < SEARCH_CAP, carry[1] > 0)

    def body(carry):
        it, _, phase, lo, hi, c_lo, c_hi, w_lo, w_hi, last = carry
        act = unfinished(lo, hi, c_lo)
        bis = (lo >> 1) + (hi >> 1) + (lo & hi & 1)
        width = hi - lo
        a = (c_lo - target + 0.5) * w_lo
        b = (target - 0.5 - c_hi) * w_hi
        frac = a / jnp.maximum(a + b, 1e-6)
        step = (frac * width.astype(F32)).astype(I32)
        itp = lo + jnp.clip(step, 1, jnp.maximum(width - 1, 1))
        use_itp = jnp.logical_and((lo ^ hi) >= 0, (jnp.zeros_like(lo) + phase) != SEARCH_PERIOD - 1)
        v = jnp.where(use_itp, itp, bis)
        cnt = count_fn(v)
        up = jnp.logical_and(act, cnt >= target)
        dn = jnp.logical_and(act, cnt < target)
        lo = jnp.where(up, v, lo)
        c_lo = jnp.where(up, cnt, c_lo)
        hi = jnp.where(dn, v, hi)
        c_hi = jnp.where(dn, cnt, c_hi)
        w_hi = jnp.where(up, jnp.where(last == 1, w_hi * 0.5, 1.0), jnp.where(dn, 1.0, w_hi))
        w_lo = jnp.where(dn, jnp.where(last == -1, w_lo * 0.5, 1.0), jnp.where(up, 1.0, w_lo))
        last = jnp.where(up, 1, jnp.where(dn, -1, last))
        phase = jnp.where(phase == SEARCH_PERIOD - 1, 0, phase + 1)
        return it + 1, any_row(unfinished(lo, hi, c_lo)), phase, lo, hi, c_lo, c_hi, w_lo, w_hi, last

    go = any_row(unfinished(lo, hi, c_lo))
    one = jnp.ones(lo.shape, F32)
    out = lax.while_loop(cond, body, (jnp.int32(0), go, jnp.int32(0), lo, hi, c_lo, c_hi,
                                      one, one, jnp.zeros(lo.shape, I32)))
    return out[3], out[5], out[6]


def _select_threshold(segments, rows, topk, alive=None):
    assert topk <= SEL_GROUPS
    kf = jnp.full((rows, 1), float(topk), F32)
    if alive is None:
        alive = jnp.full((rows, 1), True)
    total = sum(n for _, n in segments) * SEL_WIDE

    los, his = [], []
    for g in range(rows // SEL_ROWS):
        rs = slice(g * SEL_ROWS, (g + 1) * SEL_ROWS)
        ninf = jnp.full((SEL_ROWS, LANES), -jnp.inf, F32)
        ga, gb = _sweep(segments, rs, (ninf, ninf),
                        lambda a, x, idx0, k: ((jnp.maximum(a[0], x), a[1]) if k % 2 == 0
                                               else (a[0], jnp.maximum(a[1], x))))
        los.append(jnp.min(jnp.minimum(ga, gb), axis=1, keepdims=True))
        his.append(jnp.max(jnp.maximum(ga, gb), axis=1, keepdims=True))
    cat = lambda xs: jnp.concatenate(xs, axis=0) if len(xs) > 1 else xs[0]
    lo = _f32_to_key(jnp.maximum(cat(los), F32_LOWEST))
    hi = _f32_to_key(cat(his)) + 1

    def count_ge_f(thr):
        def make_pred(rs):
            t = _lanes(thr[rs])
            return lambda x, idx0: x >= t
        return _count(segments, rows, make_pred)

    count_ge = lambda v: count_ge_f(_key_to_f32(v))
    c_lo = count_ge(lo)
    c_hi = jnp.zeros((rows, 1), F32)
    for probe in (0, MIN_NORMAL_KEY):
        v = jnp.full((rows, 1), probe, I32)
        cnt = count_ge_f(jnp.full((rows, 1), np.int32(probe).view(np.float32), F32))
        inside = jnp.logical_and(lo < v, v < hi)
        up = jnp.logical_and(inside, cnt >= kf)
        dn = jnp.logical_and(inside, cnt < kf)
        lo, c_lo = jnp.where(up, v, lo), jnp.where(up, cnt, c_lo)
        hi, c_hi = jnp.where(dn, v, hi), jnp.where(dn, cnt, c_hi)
    at_zero = jnp.logical_and(lo == 0, hi == MIN_NORMAL_KEY)
    lo, c_lo, c_hi = _search(count_ge, lo, hi, c_lo, c_hi, kf, jnp.logical_and(alive, jnp.logical_not(at_zero)))
    tau = _key_to_f32(lo)

    need = jnp.logical_and(alive, c_lo > kf)
    drop = c_lo - kf

    def count_tail(v):
        def make_pred(rs):
            t, vv = _lanes(tau[rs]), _lanes(v[rs])
            lane = lax.broadcasted_iota(I32, (SEL_ROWS, LANES), 1)
            return lambda x, idx0: jnp.logical_and(x == t, lane >= vv - idx0)
        return _count(segments, rows, make_pred)

    zero = jnp.zeros((rows, 1), I32)
    cut, _, _ = _search(count_tail, zero, zero + total, c_lo - c_hi, jnp.zeros((rows, 1), F32), drop, need)
    return tau, jnp.where(need, cut, INT_MAX)


def _write_mask(segments_out, rows, tau, cut):
    base = 0
    for src, dst, n_wide, n_total in segments_out:
        for g in range(rows // SEL_ROWS):
            rs = slice(g * SEL_ROWS, (g + 1) * SEL_ROWS)
            t = _spread(_lanes(tau[rs]), SEL_WIDE)
            ct = _spread(_lanes(cut[rs]), SEL_WIDE)
            lane = lax.broadcasted_iota(I32, (SEL_ROWS, SEL_WIDE), 1)

            def body(c, carry, src=src, dst=dst, base=base, rs=rs, t=t, ct=ct, lane=lane):
                off = pl.multiple_of(c * SEL_WIDE, SEL_WIDE)
                x = src[rs, pl.ds(off, SEL_WIDE)]
                tie = jnp.logical_and(x == t, lane < ct - (base + off))
                sel = jnp.where(x > t, 1.0, jnp.where(tie, 1.0, 0.0))
                dst[rs, pl.ds(off, SEL_WIDE)] = sel.astype(dst.dtype)
                return carry

            lax.fori_loop(0, n_wide, body, 0)

            def zbody(c, carry, dst=dst, rs=rs):
                off = pl.multiple_of(c * SEL_WIDE, SEL_WIDE)
                dst[rs, pl.ds(off, SEL_WIDE)] = jnp.zeros((SEL_ROWS, SEL_WIDE), dst.dtype)
                return carry

            lax.fori_loop(n_wide, n_total, zbody, 0)
        base = base + n_wide * SEL_WIDE


def _index_scores(qi_ref, w, kb):
    acc = None
    for h in range(IDX_HEADS):
        s = _dot_nt(_mx(qi_ref[h]), kb)
        t = w[:, h:h + 1] * jnp.maximum(s, 0.0)
        acc = t if acc is None else acc + t
    return acc


def _prompt_select_kernel(qi_ref, wi_ref, kidx_ref, mask_ref, i_scr, *, tq, tk, topk):
    i, j = pl.program_id(0), pl.program_id(1)
    nk = pl.num_programs(1)
    q_lo = i * tq
    n_wide = (q_lo + tq - 1) // tk + 1

    @pl.when(j < n_wide)
    def _():
        acc = _index_scores(qi_ref, wi_ref[...], kidx_ref[...])
        qpos = q_lo + lax.broadcasted_iota(I32, acc.shape, 0)
        kpos = j * tk + lax.broadcasted_iota(I32, acc.shape, 1)
        i_scr[:, pl.ds(pl.multiple_of(j * tk, tk), tk)] = jnp.where(kpos <= qpos, acc, -jnp.inf)

    @pl.when(j == nk - 1)
    def _():
        seg = [(i_scr, n_wide)]
        tau, cut = _select_threshold(seg, tq, topk)
        _write_mask([(i_scr, mask_ref, n_wide, nk)], tq, tau, cut)


def _prompt_select(qi_hm, wi_f, ki_b, topk, tq, tk):
    s = ki_b.shape[0]
    assert tk == SEL_WIDE and s % tk == 0 and s % tq == 0 and tq % SEL_ROWS == 0
    nq, nk = s // tq, s // tk
    kmap = lambda i, j: (jnp.minimum(j, (i * tq + tq - 1) // tk), 0)
    return pl.pallas_call(
        functools.partial(_prompt_select_kernel, tq=tq, tk=tk, topk=topk),
        grid=(nq, nk),
        in_specs=[pl.BlockSpec((IDX_HEADS, tq, IDX_DIM), lambda i, j: (0, i, 0)),
                  pl.BlockSpec((tq, LANES), lambda i, j: (i, 0)),
                  pl.BlockSpec((tk, IDX_DIM), kmap)],
        out_specs=pl.BlockSpec((tq, s), lambda i, j: (i, 0)),
        out_shape=jax.ShapeDtypeStruct((s, s), MXU_DTYPE),
        scratch_shapes=[pltpu.VMEM((tq, s), F32)],
        compiler_params=_cparams(("parallel", "arbitrary")),
        name="prompt_select",
    )(qi_hm, wi_f, ki_b)


def _spread(a, n):
    if n <= LANES:
        return a[:, :n]
    return jnp.concatenate([a] * (n // LANES), axis=1)


def _flash_update(s, v, m_prev, l_prev, acc_prev):
    m_new = jnp.maximum(m_prev, jnp.max(s, axis=1, keepdims=True))
    alpha = jnp.exp(m_prev - m_new)
    p = jnp.exp(s - _spread(m_new, s.shape[1]))
    l_new = alpha * l_prev + jnp.sum(p, axis=1, keepdims=True)
    acc_new = _spread(alpha, acc_prev.shape[1]) * acc_prev + _dot(_mx(p), v)
    return m_new, l_new, acc_new


ATT_ROWS = 256


def _prompt_attend_kernel(q_ref, k_ref, v_ref, mask_ref, o_ref, m_scr, l_scr, acc_scr, *, tq, tk):
    i, j = pl.program_id(0), pl.program_id(1)
    nk = pl.num_programs(1)

    @pl.when(j == 0)
    def _():
        m_scr[...] = jnp.full(m_scr.shape, NEG_BIG, F32)
        l_scr[...] = jnp.zeros(l_scr.shape, F32)
        acc_scr[...] = jnp.zeros(acc_scr.shape, F32)

    @pl.when(j * tk <= i * tq + tq - 1)
    def _():
        for r in range(tq // ATT_ROWS):
            rs = slice(r * ATT_ROWS, (r + 1) * ATT_ROWS)
            keep = mask_ref[rs, :] > 0
            for h in range(N_HEADS_A):
                s = jnp.where(keep, _dot_nt(q_ref[h, rs, :], k_ref[h]), NEG_BIG)
                m_scr[h, rs, :], l_scr[h, rs, :], acc_scr[h, rs, :] = _flash_update(
                    s, v_ref[h], m_scr[h, rs, :], l_scr[h, rs, :], acc_scr[h, rs, :])

    @pl.when(j == nk - 1)
    def _():
        for h in range(N_HEADS_A):
            o = acc_scr[h] / l_scr[h][:, :HEAD_DIM_A]
            o_ref[:, h * HEAD_DIM_A:(h + 1) * HEAD_DIM_A] = o.astype(o_ref.dtype)


def _prompt_attend(qa_hm, ka_hm, va_hm, mask, tq, tk):
    nh, s, dh = qa_hm.shape
    nq, nk = s // tq, s // tk
    diag = lambda i, j: jnp.minimum(j, (i * tq + tq - 1) // tk)
    return pl.pallas_call(
        functools.partial(_prompt_attend_kernel, tq=tq, tk=tk),
        grid=(nq, nk),
        in_specs=[pl.BlockSpec((nh, tq, dh), lambda i, j: (0, i, 0)),
                  pl.BlockSpec((nh, tk, dh), lambda i, j: (0, diag(i, j), 0)),
                  pl.BlockSpec((nh, tk, dh), lambda i, j: (0, diag(i, j), 0)),
                  pl.BlockSpec((tq, tk), lambda i, j: (i, diag(i, j)))],
        out_specs=pl.BlockSpec((tq, nh * dh), lambda i, j: (i, 0)),
        out_shape=jax.ShapeDtypeStruct((s, nh * dh), MXU_DTYPE),
        scratch_shapes=[pltpu.VMEM((nh, tq, LANES), F32), pltpu.VMEM((nh, tq, LANES), F32),
                        pltpu.VMEM((nh, tq, dh), F32)],
        compiler_params=_cparams(("parallel", "arbitrary")),
        name="prompt_attend",
    )(qa_hm, ka_hm, va_hm, mask)


PAGES_PER_STEP = 8


def _sample_index_kernel(pt_ref, qi_ref, wi_ref, *refs, page):
    del pt_ref
    pages, out_ref = refs[:-1], refs[-1]
    w = wi_ref[0]
    for p, kref in enumerate(pages):
        out_ref[0, :, p * page:(p + 1) * page] = _index_scores(qi_ref, w, _mx(kref[0]))


def _sample_index(page_table, qi_hm, wi_f, cache_kidx):
    db, n_pages = page_table.shape
    _, page, idim = cache_kidx.shape
    pps = math.gcd(PAGES_PER_STEP, n_pages)
    nsteps = n_pages // pps
    pt = page_table.reshape(-1).astype(I32)

    def kspec(p):
        return pl.BlockSpec((1, page, idim), lambda b, j, pt: (pt[b * n_pages + j * pps + p], 0, 0))

    grid_spec = pltpu.PrefetchScalarGridSpec(
        num_scalar_prefetch=1,
        grid=(db, nsteps),
        in_specs=[pl.BlockSpec((IDX_HEADS, T_PAD, idim), lambda b, j, pt: (0, b, 0)),
                  pl.BlockSpec((1, T_PAD, LANES), lambda b, j, pt: (b, 0, 0))]
                 + [kspec(p) for p in range(pps)],
        out_specs=pl.BlockSpec((1, T_PAD, pps * page), lambda b, j, pt: (b, 0, j)),
    )
    return pl.pallas_call(
        functools.partial(_sample_index_kernel, page=page),
        grid_spec=grid_spec,
        out_shape=jax.ShapeDtypeStruct((db, T_PAD, n_pages * page), F32),
        compiler_params=_cparams(("parallel", "arbitrary")),
        name="sample_index",
    )(pt, qi_hm, wi_f.reshape(db, T_PAD, LANES), *([cache_kidx] * pps))


def _sample_select_kernel(ipast_ref, qi_ref, wi_ref, kin_ref, mpast_ref, mnew_ref, inew_scr, *, t_real, topk):
    rows = ipast_ref.shape[0]
    acc = _index_scores(qi_ref, wi_ref[...], kin_ref[...])
    r = lax.broadcasted_iota(I32, acc.shape, 0)
    c = lax.broadcasted_iota(I32, acc.shape, 1)
    same = (r // T_PAD) == (c // T_PAD)
    tq, tc = r % T_PAD, c % T_PAD
    ok = jnp.logical_and(same, jnp.logical_and(tc <= tq, tc < t_real))
    inew_scr[...] = jnp.full(inew_scr.shape, -jnp.inf, F32)
    inew_scr[:, :rows] = jnp.where(ok, acc, -jnp.inf)
    n_past = ipast_ref.shape[1] // SEL_WIDE
    n_new = inew_scr.shape[1] // SEL_WIDE
    alive = lax.rem(lax.broadcasted_iota(I32, (rows, 1), 0), T_PAD) < t_real
    tau, cut = _select_threshold([(ipast_ref, n_past), (inew_scr, n_new)], rows, topk, alive)
    _write_mask([(ipast_ref, mpast_ref, n_past, n_past), (inew_scr, mnew_ref, n_new, n_new)], rows, tau, cut)


def _sample_select(i_past, qi_hm, wi_f, ki_b, t_real, topk):
    rows, past = i_past.shape
    assert past % SEL_WIDE == 0 and rows % SEL_ROWS == 0
    wnew = -(-rows // SEL_WIDE) * SEL_WIDE
    return pl.pallas_call(
        functools.partial(_sample_select_kernel, t_real=t_real, topk=topk),
        out_shape=[jax.ShapeDtypeStruct((rows, past), F32), jax.ShapeDtypeStruct((rows, wnew), F32)],
        scratch_shapes=[pltpu.VMEM((rows, wnew), F32)],
        compiler_params=pltpu.CompilerParams(vmem_limit_bytes=VMEM_LIMIT),
        name="sample_select",
    )(i_past, qi_hm, wi_f, ki_b)


def _sample_attend_kernel(pt_ref, q_ref, mp_ref, mn_ref, kn_ref, vn_ref, *refs, page, pps):
    del pt_ref
    kpages, vpages = refs[:pps], refs[pps:2 * pps]
    o_ref, m_scr, l_scr, acc_scr = refs[2 * pps:]
    j = pl.program_id(1)
    nj = pl.num_programs(1)
    q = q_ref[0]
    nrep = q.shape[0] // T_PAD

    def update(m_t, k, v):
        keep = jnp.concatenate([m_t] * nrep, axis=0) > 0
        s = jnp.where(keep, _dot_nt(q, k), NEG_BIG)
        m_scr[...], l_scr[...], acc_scr[...] = _flash_update(s, v, m_scr[...], l_scr[...], acc_scr[...])

    @pl.when(j == 0)
    def _():
        m_scr[...] = jnp.full(m_scr.shape, NEG_BIG, F32)
        l_scr[...] = jnp.zeros(l_scr.shape, F32)
        acc_scr[...] = jnp.zeros(acc_scr.shape, F32)

    for p in range(pps):
        update(mp_ref[0, :, p * page:(p + 1) * page], _mx(kpages[p][0]), _mx(vpages[p][0]))

    @pl.when(j == nj - 1)
    def _():
        update(mn_ref[0], kn_ref[0], vn_ref[0])
        full = acc_scr[...] / _spread(l_scr[...], acc_scr.shape[1])
        lane = lax.broadcasted_iota(I32, (T_PAD, full.shape[1]), 1)
        out = jnp.zeros((T_PAD, full.shape[1]), F32)
        for h in range(nrep):
            blk = full[h * T_PAD:(h + 1) * T_PAD]
            out = out + jnp.where((lane // HEAD_DIM_A) == h, blk, 0.0)
        o_ref[0] = out.astype(o_ref.dtype)


def _sample_attend(page_table, q_bd, m_past, m_new, k_new, v_new, cache_k, cache_v):
    db, n_pages = page_table.shape
    _, page, hd = cache_k.shape
    pps = math.gcd(PAGES_PER_STEP, n_pages)
    nsteps = n_pages // pps
    nnew = k_new.shape[1]
    nq = q_bd.shape[1]
    pt = page_table.reshape(-1).astype(I32)

    def pspec(p):
        return pl.BlockSpec((1, page, hd), lambda b, j, pt: (pt[b * n_pages + j * pps + p], 0, 0))

    bspec = lambda shape: pl.BlockSpec((1,) + shape, lambda b, j, pt: (b, 0, 0))
    grid_spec = pltpu.PrefetchScalarGridSpec(
        num_scalar_prefetch=1,
        grid=(db, nsteps),
        in_specs=[bspec((nq, hd)),
                  pl.BlockSpec((1, T_PAD, pps * page), lambda b, j, pt: (b, 0, j)),
                  bspec((T_PAD, nnew)), bspec((nnew, hd)), bspec((nnew, hd))]
                 + [pspec(p) for p in range(pps)] * 2,
        out_specs=bspec((T_PAD, hd)),
        scratch_shapes=[pltpu.VMEM((nq, LANES), F32), pltpu.VMEM((nq, LANES), F32), pltpu.VMEM((nq, hd), F32)],
    )
    return pl.pallas_call(
        functools.partial(_sample_attend_kernel, page=page, pps=pps),
        grid_spec=grid_spec,
        out_shape=jax.ShapeDtypeStruct((db, T_PAD, hd), F32),
        compiler_params=_cparams(("parallel", "arbitrary")),
        name="sample_attend",
    )(pt, q_bd, m_past, m_new, k_new, v_new, *([cache_k] * pps), *([cache_v] * pps))


def _retention_tables(c_real, c_pad):
    h = np.arange(N_HEADS_R, dtype=np.float64)
    log_g = np.log1p(-np.exp2(-5.0 - h))
    i = np.arange(c_pad, dtype=np.float64)
    diff = i[:, None] - i[None, :]
    live = (diff >= 0) & (i[:, None] < c_real) & (i[None, :] < c_real)
    inner = np.where(live[None], np.exp(np.maximum(diff, 0.0)[None] * log_g[:, None, None]), 0.0)
    q_dec = np.exp((i + 1.0)[None, :] * log_g[:, None])
    k_dec = np.where(i[None, :] < c_real, np.exp((c_real - 1.0 - i)[None, :] * log_g[:, None]), 0.0)
    c_dec = np.exp(c_real * log_g)
    f = lambda a: jnp.asarray(a, F32)
    return f(inner), f(q_dec[:, :, None]), f(k_dec[:, :, None]), [float(v) for v in c_dec]


def _retention_kernel(q_ref, k_ref, v_ref, g_ref, s0_ref, inner_ref, qdec_ref, kdec_ref,
                      o_ref, s_out_ref, s_scr, *, c_dec):
    j = pl.program_id(1)
    nj = pl.num_programs(1)

    @pl.when(j == 0)
    def _():
        s_scr[...] = s0_ref[0]

    for h in range(N_HEADS_R):
        q = q_ref[:, h * DK_R:(h + 1) * DK_R]
        k = k_ref[:, h * DK_R:(h + 1) * DK_R]
        v = v_ref[:, h * DV_R:(h + 1) * DV_R]
        s_prev = s_scr[h]
        a = _dot_nt(_mx(q), _mx(k)) * inner_ref[h]
        o = _dot(_mx(a), v) + _dot(_mx(q), _mx(s_prev)) * qdec_ref[h]
        kd = k * kdec_ref[h]
        s_scr[h] = s_prev * c_dec[h] + _dot(_mx(kd.T), v)
        mu = jnp.mean(o, axis=-1, keepdims=True)
        var = jnp.mean(jnp.square(o - mu), axis=-1, keepdims=True)
        gn = (o - mu) * lax.rsqrt(var + EPS)
        g = g_ref[:, h * DV_R:(h + 1) * DV_R]
        o_ref[:, h * DV_R:(h + 1) * DV_R] = (gn * (g * _sigmoid(g))).astype(o_ref.dtype)

    @pl.when(j == nj - 1)
    def _():
        s_out_ref[0] = s_scr[...]


def _retention(qr, kr, vr, gr, s0, c_real):
    b = s0.shape[0]
    c = RET_CHUNK
    n = qr.shape[0] // (b * c)
    inner, qdec, kdec, c_dec = _retention_tables(c_real, c)
    wr, wv = N_HEADS_R * DK_R, N_HEADS_R * DV_R
    rmap = lambda bi, j: (bi * n + j, 0)
    full3 = lambda shape: pl.BlockSpec(shape, lambda bi, j: (0, 0, 0))
    return pl.pallas_call(
        functools.partial(_retention_kernel, c_dec=c_dec),
        grid=(b, n),
        in_specs=[pl.BlockSpec((c, wr), rmap), pl.BlockSpec((c, wr), rmap), pl.BlockSpec((c, wv), rmap),
                  pl.BlockSpec((c, wv), rmap),
                  pl.BlockSpec((1, N_HEADS_R, DK_R, DV_R), lambda bi, j: (bi, 0, 0, 0)),
                  full3(inner.shape), full3(qdec.shape), full3(kdec.shape)],
        out_specs=[pl.BlockSpec((c, wv), rmap),
                   pl.BlockSpec((1, N_HEADS_R, DK_R, DV_R), lambda bi, j: (bi, 0, 0, 0))],
        out_shape=[jax.ShapeDtypeStruct((b * n * c, wv), MXU_DTYPE),
                   jax.ShapeDtypeStruct((b, N_HEADS_R, DK_R, DV_R), F32)],
        scratch_shapes=[pltpu.VMEM((N_HEADS_R, DK_R, DV_R), F32)],
        compiler_params=_cparams(("parallel", "arbitrary")),
        name="retention",
    )(qr, kr, vr, gr, s0, inner, qdec, kdec)


def _cross_kernel(q_ref, mk_ref, mv_ref, o_ref):
    scale = HEAD_DIM_M ** -0.5
    for h in range(N_HEADS_M):
        sl = slice(h * HEAD_DIM_M, (h + 1) * HEAD_DIM_M)
        s = _dot_nt(_mx(q_ref[:, sl]), _mx(mk_ref[0, :, sl])) * scale
        p = jnp.exp(s - jnp.max(s, axis=1, keepdims=True))
        p = p / jnp.sum(p, axis=1, keepdims=True)
        o_ref[:, sl] = _dot(_mx(p), _mx(mv_ref[0, :, sl])).astype(o_ref.dtype)


def _cross_attend(qm, mk, mv, tm, out_dtype):
    b, n_mem, hd = mk.shape
    nt = qm.shape[0] // (b * tm)
    return pl.pallas_call(
        _cross_kernel,
        grid=(b, nt),
        in_specs=[pl.BlockSpec((tm, hd), lambda bi, i: (bi * nt + i, 0)),
                  pl.BlockSpec((1, n_mem, hd), lambda bi, i: (bi, 0, 0)),
                  pl.BlockSpec((1, n_mem, hd), lambda bi, i: (bi, 0, 0))],
        out_specs=pl.BlockSpec((tm, hd), lambda bi, i: (bi * nt + i, 0)),
        out_shape=jax.ShapeDtypeStruct(qm.shape, out_dtype),
        compiler_params=_cparams(("parallel", "parallel")),
        name="cross_attend",
    )(qm, mk, mv)


def _memkv_kernel(x_ref, g_ref, w_ref, o_ref):
    o_ref[...] = _dot(_mx(_rms(x_ref[...], g_ref[...])), w_ref[...])


def _memory_kv(mem, gain, w):
    rows, d = mem.shape
    return pl.pallas_call(
        _memkv_kernel,
        out_shape=jax.ShapeDtypeStruct((rows, w.shape[1]), F32),
        compiler_params=pltpu.CompilerParams(vmem_limit_bytes=VMEM_LIMIT),
        name="memory_kv",
    )(mem, gain.reshape(1, d), _mx(w))


def _merge_kernel(x_ref, oa_ref, or_ref, om_ref, gates_ref, wpa_ref, wpb_ref, wpc_ref, wo_ref, g_ref, h_ref):
    d = x_ref.shape[1]
    gt = gates_ref[...]
    mixed = (_sigmoid(gt[:, :d]) * _dot(_mx(oa_ref[...]), wpa_ref[...])
             + _sigmoid(gt[:, d:2 * d]) * _dot(_mx(or_ref[...]), wpb_ref[...])
             + _sigmoid(gt[:, 2 * d:]) * _dot(_mx(om_ref[...]), wpc_ref[...]))
    z = _dot(_mx(mixed), wo_ref[...])
    h_ref[...] = x_ref[...] + _rms(z, g_ref[...])


def _merge(x, oa, o_r, om, gates, wpa, wpb, wpc, wo, gain, tm):
    rows, d = x.shape
    return pl.pallas_call(
        _merge_kernel,
        grid=(rows // tm,),
        in_specs=[_row_spec(tm, d), _row_spec(tm, oa.shape[1]), _row_spec(tm, o_r.shape[1]),
                  _row_spec(tm, om.shape[1]), _row_spec(tm, 3 * d),
                  _full_spec(wpa.shape), _full_spec(wpb.shape), _full_spec(wpc.shape), _full_spec(wo.shape),
                  _full_spec((1, d))],
        out_specs=_row_spec(tm, d),
        out_shape=jax.ShapeDtypeStruct((rows, d), F32),
        compiler_params=_cparams(("parallel",)),
        name="merge",
    )(x, oa, o_r, om, gates, wpa, wpb, wpc, wo, gain.reshape(1, d))


HALO = BF16_ROWS


def _ffn_kernel(h_ref, halo_ref, s0_ref, s1_ref, g1_ref, g2_ref, wu_ref, wg_ref, cw_ref, cb_ref, wd_ref,
                y_ref, utail_ref, x_scr, u_scr, *, tm, seq, keep):
    i = pl.program_id(0)
    h = h_ref[...]
    hn = _rms(h, g1_ref[...])
    x_scr[HALO:, :] = _mx(hn)
    x_scr[:HALO, :] = _mx(_rms(halo_ref[...], g1_ref[...]))
    xc = x_scr[...]
    u_scr[...] = _dot(xc, wu_ref[...])
    gate = _dot(xc[HALO:], wg_ref[...])
    cur = u_scr[HALO:, :]
    prev1 = u_scr[HALO - 1:HALO - 1 + tm, :]
    prev2 = u_scr[HALO - 2:HALO - 2 + tm, :]
    seq_loc = min(seq, tm)
    t = lax.rem(lax.broadcasted_iota(I32, (tm, 1), 0), seq_loc)
    t = jnp.where(lax.rem(i * tm, seq) == 0, t, CONV_W)
    st0, st1 = s0_ref[...], s1_ref[...]
    if st0.shape[0] != tm:
        st0, st1 = st0[0:1], st1[0:1]
    prev1 = jnp.where(t == 0, st1, prev1)
    prev2 = jnp.where(t == 0, st0, jnp.where(t == 1, st1, prev2))
    c = cb_ref[...] + prev2 * cw_ref[0:1, :] + prev1 * cw_ref[1:2, :] + cur * cw_ref[2:3, :]
    act = jax.nn.gelu(c, approximate=True) * gate
    ff = _dot(_mx(act), wd_ref[...])
    y_ref[...] = h + _rms(ff, g2_ref[...])
    utail_ref[...] = u_scr[HALO + tm - keep:, :]


def _conv_ffn(h, s0e, s1e, g1, g2, wu, wg, cw, cb, wd, tm, seq, keep):
    rows, d = h.shape
    f = wu.shape[1]
    nt = rows // tm
    hb = tm // HALO
    sr = s0e.shape[0]
    return pl.pallas_call(
        functools.partial(_ffn_kernel, tm=tm, seq=seq, keep=keep),
        grid=(nt,),
        in_specs=[_row_spec(tm, d),
                  pl.BlockSpec((HALO, d), lambda i: (jnp.maximum(i * hb - 1, 0), 0)),
                  _full_spec((sr, f)), _full_spec((sr, f)),
                  _full_spec((1, d)), _full_spec((1, d)),
                  _full_spec(wu.shape), _full_spec(wg.shape), _full_spec(cw.shape), _full_spec((1, f)),
                  _full_spec(wd.shape)],
        out_specs=[_row_spec(tm, d), _row_spec(keep, f)],
        out_shape=[jax.ShapeDtypeStruct((rows, d), F32), jax.ShapeDtypeStruct((nt * keep, f), F32)],
        scratch_shapes=[pltpu.VMEM((tm + HALO, d), MXU_DTYPE), pltpu.VMEM((tm + HALO, f), F32)],
        compiler_params=_cparams(("parallel",)),
        name="conv_ffn",
    )(h, h, s0e, s1e, g1.reshape(1, d), g2.reshape(1, d), wu, wg, cw, cb.reshape(1, f), wd)


def _layer_weights(l, w_in, w_proj_a, w_proj_b, w_proj_c, w_out, w_up, w_down):
    wts = _prep_in_weights(w_in[l])
    f = w_down.shape[1]
    wts.update(wpa=_mx(w_proj_a[l]), wpb=_mx(w_proj_b[l]), wpc=_mx(w_proj_c[l]), wo=_mx(w_out[l]),
               wu=_mx(w_up[l][:, :f]), wg=_mx(w_up[l][:, f:]), wd=_mx(w_down[l]))
    return wts


def _prompt_layer(x, mem, wts, norms, conv_w, conv_b, w_mem_kv, tiles):
    s, d = x.shape
    f = wts["wd"].shape[0]
    pos = jnp.arange(s)
    pr = _projections(x, norms["pre_mix"], pos, wts, tiles["proj"])
    topk = min(TOPK_MAX, s // 4)
    mask = _prompt_select(pr["qi_hm"], pr["wi_f"], pr["ki_b"], topk, tiles["sel_q"], SEL_WIDE)
    o_a = _prompt_attend(pr["qa_hm"], pr["ka_hm"], pr["va_hm"], mask, tiles["att_q"], tiles["att_k"])
    s0 = jnp.zeros((1, N_HEADS_R, DK_R, DV_R), F32)
    o_r, ret_new = _retention(pr["qr_f"], pr["kr_f"], pr["vr_b"], pr["gr_f"], s0, RET_CHUNK)
    kv = _memory_kv(mem, norms["mem"], w_mem_kv)
    wm = N_HEADS_M * HEAD_DIM_M
    mk, mv = kv[:, :wm], kv[:, wm:]
    o_m = _cross_attend(pr["qm_b"], mk[None], mv[None], tiles["cross"], MXU_DTYPE)
    h = _merge(x, o_a, o_r, o_m, pr["gates_f"], wts["wpa"], wts["wpb"], wts["wpc"], wts["wo"],
               norms["post_mix"], tiles["merge"])
    zst = jnp.zeros((SUBLANES, f), F32)
    y, utail = _conv_ffn(h, zst, zst, norms["pre_ffn"], norms["post_ffn"], wts["wu"], wts["wg"], conv_w, conv_b,
                         wts["wd"], tiles["ffn"], s, SUBLANES)
    conv_new = utail[-(CONV_W - 1):]
    return y, pr["ka_f"], pr["va_f"], pr["ki_f"], ret_new, conv_new, mk, mv


def _sample_layer(x, wts, norms, conv_w, conv_b, cache_k, cache_v, cache_kidx, mem_k, mem_v,
                  state_ret, state_conv, page_table):
    db, t, d = x.shape
    f = wts["wd"].shape[0]
    n_pages = page_table.shape[1]
    page = cache_k.shape[1]
    past = n_pages * page
    rows = db * T_PAD
    xp = jnp.pad(x, ((0, 0), (0, T_PAD - t), (0, 0))).reshape(rows, d)
    pos = jnp.tile(past + jnp.arange(T_PAD), db)
    pr = _projections(xp, norms["pre_mix"], pos, wts, rows)
    hd = N_HEADS_A * HEAD_DIM_A

    topk = min(TOPK_MAX, (past + t) // 4)
    i_past = _sample_index(page_table, pr["qi_hm"].astype(F32), pr["wi_f"], cache_kidx).reshape(rows, past)
    m_past, m_new = _sample_select(i_past, pr["qi_hm"], pr["wi_f"], pr["ki_b"], t, topk)
    nnew = m_new.shape[1]
    own = m_new[:, :rows].reshape(db, T_PAD, db, T_PAD)[jnp.arange(db), :, jnp.arange(db), :]
    m_new_own = jnp.pad(own, ((0, 0), (0, 0), (0, LANES - T_PAD)))
    q_rows = pr["qa_hm"].reshape(N_HEADS_A, db, T_PAD, HEAD_DIM_A)
    eye = jnp.eye(N_HEADS_A, dtype=MXU_DTYPE)
    q_bd = jnp.einsum("hbtd,hg->bhtgd", q_rows, eye).reshape(db, N_HEADS_A * T_PAD, hd)
    k_new = jnp.pad(_mx(pr["ka_f"]).reshape(db, T_PAD, hd), ((0, 0), (0, LANES - T_PAD), (0, 0)))
    v_new = jnp.pad(_mx(pr["va_f"]).reshape(db, T_PAD, hd), ((0, 0), (0, LANES - T_PAD), (0, 0)))
    o_a = _sample_attend(page_table, q_bd, m_past.reshape(db, T_PAD, past), m_new_own, k_new, v_new,
                         _mx(cache_k).reshape(-1, page, hd), _mx(cache_v).reshape(-1, page, hd)).reshape(rows, hd)

    padc = lambda a: jnp.pad(a.reshape(db, T_PAD, -1), ((0, 0), (0, RET_CHUNK - T_PAD), (0, 0))).reshape(db * RET_CHUNK, -1)
    o_r, ret_new = _retention(padc(pr["qr_f"]), padc(pr["kr_f"]), padc(pr["vr_b"]), padc(pr["gr_f"]), state_ret, t)
    o_r = o_r.reshape(db, RET_CHUNK, -1)[:, :T_PAD].reshape(rows, -1)

    wm = N_HEADS_M * HEAD_DIM_M
    o_m = _cross_attend(pr["qm_b"].astype(F32), mem_k.reshape(db, -1, wm), mem_v.reshape(db, -1, wm), T_PAD, F32)

    h = _merge(xp, o_a, o_r, o_m, pr["gates_f"], wts["wpa"], wts["wpb"], wts["wpc"], wts["wo"],
               norms["post_mix"], rows)
    s0e = jnp.repeat(state_conv[:, 0], T_PAD, axis=0)
    s1e = jnp.repeat(state_conv[:, 1], T_PAD, axis=0)
    y, u_all = _conv_ffn(h, s0e, s1e, norms["pre_ffn"], norms["post_ffn"], wts["wu"], wts["wg"], conv_w, conv_b,
                         wts["wd"], rows, T_PAD, rows)
    ext = jnp.concatenate([state_conv.astype(F32), u_all.reshape(db, T_PAD, f)[:, :t]], axis=1)
    conv_new = ext[:, t:]
    unpad = lambda a: a.reshape(db, T_PAD, -1)[:, :t]
    return (unpad(y), unpad(pr["ka_f"]), unpad(pr["va_f"]), unpad(pr["ki_f"]), ret_new, conv_new)


PROMPT_TILES = dict(proj=512, sel_q=256, att_q=256, att_k=512, cross=512, merge=512, ffn=256)


def kernel(x_prompt, x_sample, cache_k, cache_v, cache_kidx, cache_mem_k, cache_mem_v, state_ret, state_conv,
           page_table, mem_prompt, norm_pre_mix, norm_post_mix, norm_pre_ffn, norm_post_ffn, norm_mem,
           w_in, w_mem_kv, w_proj_a, w_proj_b, w_proj_c, w_out, w_up, conv_w, conv_b, w_down):
    bp, s, d = x_prompt.shape
    db, t, _ = x_sample.shape
    depth = w_in.shape[0]
    assert bp == 1 and t <= T_PAD and CONV_W - 1 <= t
    tiles = {k: min(v, s) for k, v in PROMPT_TILES.items()}
    yp, ys = x_prompt[0], x_sample
    outs = [[] for _ in range(12)]
    for l in range(depth):
        wts = _layer_weights(l, w_in, w_proj_a, w_proj_b, w_proj_c, w_out, w_up, w_down)
        norms = dict(pre_mix=norm_pre_mix[l], post_mix=norm_post_mix[l], pre_ffn=norm_pre_ffn[l],
                     post_ffn=norm_post_ffn[l], mem=norm_mem[l])
        yp, kp, vp, kip, rp, cp, mk, mv = _prompt_layer(yp, mem_prompt[0], wts, norms, conv_w[l], conv_b[l],
                                                        w_mem_kv[l], tiles)
        ys, ks, vs, kis, rs, cs = _sample_layer(ys, wts, norms, conv_w[l], conv_b[l], cache_k[l], cache_v[l],
                                                cache_kidx[l], cache_mem_k[l], cache_mem_v[l], state_ret[l],
                                                state_conv[l], page_table)
        n_mem = mk.shape[0]
        vals = (kp.reshape(1, s, N_HEADS_A, HEAD_DIM_A), vp.reshape(1, s, N_HEADS_A, HEAD_DIM_A),
                kip.reshape(1, s, IDX_DIM), rp, cp[None],
                mk.reshape(1, n_mem, N_HEADS_M, HEAD_DIM_M), mv.reshape(1, n_mem, N_HEADS_M, HEAD_DIM_M),
                ks.reshape(db, t, N_HEADS_A, HEAD_DIM_A), vs.reshape(db, t, N_HEADS_A, HEAD_DIM_A),
                kis, rs, cs)
        for o, v in zip(outs, vals):
            o.append(v)
    stacked = [jnp.stack(o) for o in outs]
    return (yp[None], ys, *stacked)
```

```python
import functools
import math

import numpy as np
import jax
import jax.numpy as jnp
from jax import lax
from jax.experimental import pallas as pl
from jax.experimental.pallas import tpu as pltpu

F32 = jnp.float32
I32 = jnp.int32
MXU_DTYPE = jnp.bfloat16

N_HEADS_A, HEAD_DIM_A = 8, 64
IDX_HEADS, IDX_DIM = 4, 64
TOPK_MAX = 256
N_HEADS_R, DK_R, DV_R = 4, 128, 256
RET_CHUNK = 128
N_HEADS_M, HEAD_DIM_M = 4, 128
CONV_W = 3
ROPE_THETA = 10000.0
EPS = 1e-6

LANES = 128
SUBLANES = 8
BF16_ROWS = 16
VMEM_LIMIT = 56 * 1024 * 1024
NEG_BIG = -1e30
F32_LOWEST = float(np.finfo(np.float32).min)
T_PAD = 8


def _cparams(sem):
    return pltpu.CompilerParams(dimension_semantics=sem, vmem_limit_bytes=VMEM_LIMIT)


def _dot(a, b):
    return jnp.dot(a, b, preferred_element_type=F32)


def _dot_nt(a, b):
    return lax.dot_general(a, b, (((1,), (1,)), ((), ())), preferred_element_type=F32)


def _mx(a):
    return a.astype(MXU_DTYPE)


def _rms(x, g):
    return x * lax.rsqrt(jnp.mean(x * x, axis=-1, keepdims=True) + EPS) * g


def _sigmoid(x):
    return 1.0 / (1.0 + jnp.exp(-x))


def _rope_tables(pos, d):
    half = d // 2
    inv = 1.0 / (ROPE_THETA ** (jnp.arange(half, dtype=F32) * 2.0 / d))
    ang = pos.astype(F32)[:, None] * inv[None, :]
    cos, sin = jnp.cos(ang), jnp.sin(ang)
    reps = LANES // d
    cos_t = jnp.tile(jnp.concatenate([cos, cos], axis=1), (1, reps))
    sin_t = jnp.tile(jnp.concatenate([-sin, sin], axis=1), (1, reps))
    return cos_t, sin_t


def _rope(y, cos, sin, d):
    w = y.shape[1]
    half = d // 2
    reps = w // LANES
    c = jnp.concatenate([cos] * reps, axis=1) if reps > 1 else cos
    s = jnp.concatenate([sin] * reps, axis=1) if reps > 1 else sin
    lane = lax.broadcasted_iota(I32, y.shape, 1)
    first = (lane & (d - 1)) < half
    rot = jnp.where(first, pltpu.roll(y, w - half, 1), pltpu.roll(y, half, 1))
    return y * c + rot * s


def _proj_a_kernel(x_ref, g_ref, cos_ref, sin_ref, wa_ref, wv_ref,
                   qa_hm, ka_f, ka_hm, qi_hm, ki_f, ki_b, va_f, va_hm, wi_f):
    xn = _mx(_rms(x_ref[...], g_ref[...]))
    wa = N_HEADS_A * HEAD_DIM_A
    y = _rope(_dot(xn, wa_ref[...]), cos_ref[...], sin_ref[...], HEAD_DIM_A)
    q = y[:, :wa] * (HEAD_DIM_A ** -0.5)
    k = y[:, wa:2 * wa]
    ka_f[...] = k
    for h in range(N_HEADS_A):
        sl = slice(h * HEAD_DIM_A, (h + 1) * HEAD_DIM_A)
        qa_hm[h] = _mx(q[:, sl])
        ka_hm[h] = _mx(k[:, sl])
    qi = y[:, 2 * wa:2 * wa + IDX_HEADS * IDX_DIM]
    for h in range(IDX_HEADS):
        qi_hm[h] = _mx(qi[:, h * IDX_DIM:(h + 1) * IDX_DIM])
    ki = y[:, 2 * wa + IDX_HEADS * IDX_DIM:2 * wa + IDX_HEADS * IDX_DIM + IDX_DIM]
    ki_f[...] = ki
    ki_b[...] = _mx(ki)
    z = _dot(xn, wv_ref[...])
    v = z[:, :wa]
    va_f[...] = v
    for h in range(N_HEADS_A):
        va_hm[h] = _mx(v[:, h * HEAD_DIM_A:(h + 1) * HEAD_DIM_A])
    wi_f[...] = z[:, wa:wa + LANES]


def _proj_b_kernel(x_ref, g_ref, cos_ref, sin_ref, wr_ref, wvg_ref, qr_f, kr_f, vr_b, gr_f):
    xn = _mx(_rms(x_ref[...], g_ref[...]))
    wr = N_HEADS_R * DK_R
    y = _rope(_dot(xn, wr_ref[...]), cos_ref[...], sin_ref[...], DK_R)
    qr_f[...] = y[:, :wr]
    kr_f[...] = y[:, wr:] * (DK_R ** -0.5)
    z = _dot(xn, wvg_ref[...])
    wv = N_HEADS_R * DV_R
    vr_b[...] = _mx(z[:, :wv])
    gr_f[...] = z[:, wv:]


def _proj_c_kernel(x_ref, g_ref, wc_ref, qm_b, gates_f):
    xn = _mx(_rms(x_ref[...], g_ref[...]))
    z = _dot(xn, wc_ref[...])
    wm = N_HEADS_M * HEAD_DIM_M
    qm_b[...] = _mx(z[:, :wm])
    gates_f[...] = z[:, wm:]


def _row_spec(tm, w):
    return pl.BlockSpec((tm, w), lambda i: (i, 0))


def _full_spec(shape):
    nd = len(shape)
    return pl.BlockSpec(shape, lambda i: (0,) * nd)


def _hm_spec(nh, tm, d):
    return pl.BlockSpec((nh, tm, d), lambda i: (0, i, 0))


def _projections(x, gain, pos, wts, tm):
    rows, d = x.shape
    grid = (rows // tm,)
    wa = N_HEADS_A * HEAD_DIM_A
    cos64, sin64 = _rope_tables(pos, HEAD_DIM_A)
    cos128, sin128 = _rope_tables(pos, DK_R)
    g2 = gain.reshape(1, d)
    sds = jax.ShapeDtypeStruct
    outs_a = pl.pallas_call(
        _proj_a_kernel,
        grid=grid,
        in_specs=[_row_spec(tm, d), _full_spec((1, d)), _row_spec(tm, LANES), _row_spec(tm, LANES),
                  _full_spec(wts["wa"].shape), _full_spec(wts["wv"].shape)],
        out_specs=[_hm_spec(N_HEADS_A, tm, HEAD_DIM_A), _row_spec(tm, wa), _hm_spec(N_HEADS_A, tm, HEAD_DIM_A),
                   _hm_spec(IDX_HEADS, tm, IDX_DIM), _row_spec(tm, IDX_DIM), _row_spec(tm, IDX_DIM),
                   _row_spec(tm, wa), _hm_spec(N_HEADS_A, tm, HEAD_DIM_A), _row_spec(tm, LANES)],
        out_shape=[sds((N_HEADS_A, rows, HEAD_DIM_A), MXU_DTYPE), sds((rows, wa), F32),
                   sds((N_HEADS_A, rows, HEAD_DIM_A), MXU_DTYPE), sds((IDX_HEADS, rows, IDX_DIM), MXU_DTYPE),
                   sds((rows, IDX_DIM), F32), sds((rows, IDX_DIM), MXU_DTYPE),
                   sds((rows, wa), F32), sds((N_HEADS_A, rows, HEAD_DIM_A), MXU_DTYPE), sds((rows, LANES), F32)],
        compiler_params=_cparams(("parallel",)),
        name="proj_a",
    )(x, g2, cos64, sin64, wts["wa"], wts["wv"])
    names_a = ("qa_hm", "ka_f", "ka_hm", "qi_hm", "ki_f", "ki_b", "va_f", "va_hm", "wi_f")
    wr, wv = N_HEADS_R * DK_R, N_HEADS_R * DV_R
    outs_b = pl.pallas_call(
        _proj_b_kernel,
        grid=grid,
        in_specs=[_row_spec(tm, d), _full_spec((1, d)), _row_spec(tm, LANES), _row_spec(tm, LANES),
                  _full_spec(wts["wr"].shape), _full_spec(wts["wvg"].shape)],
        out_specs=[_row_spec(tm, wr), _row_spec(tm, wr), _row_spec(tm, wv), _row_spec(tm, wv)],
        out_shape=[sds((rows, wr), F32), sds((rows, wr), F32), sds((rows, wv), MXU_DTYPE), sds((rows, wv), F32)],
        compiler_params=_cparams(("parallel",)),
        name="proj_b",
    )(x, g2, cos128, sin128, wts["wr"], wts["wvg"])
    names_b = ("qr_f", "kr_f", "vr_b", "gr_f")
    wm = N_HEADS_M * HEAD_DIM_M
    outs_c = pl.pallas_call(
        _proj_c_kernel,
        grid=grid,
        in_specs=[_row_spec(tm, d), _full_spec((1, d)), _full_spec(wts["wc"].shape)],
        out_specs=[_row_spec(tm, wm), _row_spec(tm, 3 * d)],
        out_shape=[sds((rows, wm), MXU_DTYPE), sds((rows, 3 * d), F32)],
        compiler_params=_cparams(("parallel",)),
        name="proj_c",
    )(x, g2, wts["wc"])
    names_c = ("qm_b", "gates_f")
    out = dict(zip(names_a, outs_a))
    out.update(zip(names_b, outs_b))
    out.update(zip(names_c, outs_c))
    return out


def _prep_in_weights(w_in):
    d = w_in.shape[0]
    wa = N_HEADS_A * HEAD_DIM_A
    widths = (wa, wa, wa, IDX_HEADS * IDX_DIM, IDX_DIM, IDX_HEADS,
              N_HEADS_R * DK_R, N_HEADS_R * DK_R, N_HEADS_R * DV_R, N_HEADS_R * DV_R,
              N_HEADS_M * HEAD_DIM_M, 3 * d)
    offs = np.concatenate([[0], np.cumsum(widths)])
    seg = [w_in[:, int(offs[i]):int(offs[i + 1])] for i in range(len(widths))]
    q_a, k_a, v_a, q_i, k_i, w_i, q_r, k_r, v_r, g_r, q_m, gates = seg
    zpad = lambda n: jnp.zeros((d, n), w_in.dtype)
    return {
        "wa": _mx(jnp.concatenate([q_a, k_a, q_i, k_i, zpad(LANES - IDX_DIM)], axis=1)),
        "wv": _mx(jnp.concatenate([v_a, w_i, zpad(LANES - IDX_HEADS)], axis=1)),
        "wr": _mx(jnp.concatenate([q_r, k_r], axis=1)),
        "wvg": _mx(jnp.concatenate([v_r, g_r], axis=1)),
        "wc": _mx(jnp.concatenate([q_m, gates], axis=1)),
    }


SEL_ROWS = 64
SEL_WIDE = 512
SEL_GROUPS = 2 * LANES
INT_MAX = np.int32(2 ** 31 - 1)
MIN_NORMAL_KEY = 0x00800000
SEARCH_PERIOD = 4
SEARCH_CAP = SEARCH_PERIOD * 33


def _key_to_f32(key):
    bits = jnp.where(key >= 0, key, key ^ jnp.int32(0x7FFFFFFF))
    return pltpu.bitcast(bits, F32)


def _f32_to_key(f):
    bits = pltpu.bitcast(f, I32)
    return jnp.where(f == 0.0, 0, jnp.where(bits >= 0, bits, bits ^ jnp.int32(0x7FFFFFFF)))


def _sweep(segments, rs, init, fn):
    acc = init
    base = 0
    for ref, n_wide in segments:
        def body(c, a, ref=ref, base=base):
            off = c * SEL_WIDE
            for k in range(SEL_WIDE // LANES):
                x = ref[rs, pl.ds(pl.multiple_of(off + k * LANES, LANES), LANES)]
                a = fn(a, x, base + off + k * LANES, k)
            return a
        acc = lax.fori_loop(0, n_wide, body, acc)
        base = base + n_wide * SEL_WIDE
    return acc


def _count(segments, rows, make_pred):
    outs = []
    for g in range(rows // SEL_ROWS):
        rs = slice(g * SEL_ROWS, (g + 1) * SEL_ROWS)
        pred = make_pred(rs)
        acc = _sweep(segments, rs, jnp.zeros((SEL_ROWS, LANES), F32),
                     lambda a, x, idx0, k: a + jnp.where(pred(x, idx0), 1.0, 0.0))
        outs.append(jnp.sum(acc, axis=1, keepdims=True))
    return jnp.concatenate(outs, axis=0) if len(outs) > 1 else outs[0]


def _lanes(a, n=LANES):
    return jnp.broadcast_to(a, (a.shape[0], n))


def _search(count_fn, lo, hi, c_lo, c_hi, target, alive):
    def unfinished(lo, hi, c_lo):
        return jnp.logical_and(alive, jnp.logical_and(c_lo > target, lo + 1 < hi))

    def any_row(flag):
        return jnp.max(jnp.where(flag, 1, 0).astype(I32))

    def cond(carry):
        return jnp.logical_and(carry[0] < SEARCH_CAP, carry[1] > 0)

    def body(carry):
        it, _, phase, lo, hi, c_lo, c_hi, w_lo, w_hi, last = carry
        act = unfinished(lo, hi, c_lo)
        bis = (lo >> 1) + (hi >> 1) + (lo & hi & 1)
        width = hi - lo
        a = (c_lo - target + 0.5) * w_lo
        b = (target - 0.5 - c_hi) * w_hi
        frac = a / jnp.maximum(a + b, 1e-6)
        step = (frac * width.astype(F32)).astype(I32)
        itp = lo + jnp.clip(step, 1, jnp.maximum(width - 1, 1))
        use_itp = jnp.logical_and((lo ^ hi) >= 0, (jnp.zeros_like(lo) + phase) != SEARCH_PERIOD - 1)
        v = jnp.where(use_itp, itp, bis)
        cnt = count_fn(v)
        up = jnp.logical_and(act, cnt >= target)
        dn = jnp.logical_and(act, cnt < target)
        lo = jnp.where(up, v, lo)
        c_lo = jnp.where(up, cnt, c_lo)
        hi = jnp.where(dn, v, hi)
        c_hi = jnp.where(dn, cnt, c_hi)
        w_hi = jnp.where(up, jnp.where(last == 1, w_hi * 0.5, 1.0), jnp.where(dn, 1.0, w_hi))
        w_lo = jnp.where(dn, jnp.where(last == -1, w_lo * 0.5, 1.0), jnp.where(up, 1.0, w_lo))
        last = jnp.where(up, 1, jnp.where(dn, -1, last))
        phase = jnp.where(phase == SEARCH_PERIOD - 1, 0, phase + 1)
        return it + 1, any_row(unfinished(lo, hi, c_lo)), phase, lo, hi, c_lo, c_hi, w_lo, w_hi, last

    go = any_row(unfinished(lo, hi, c_lo))
    one = jnp.ones(lo.shape, F32)
    out = lax.while_loop(cond, body, (jnp.int32(0), go, jnp.int32(0), lo, hi, c_lo, c_hi,
                                      one, one, jnp.zeros(lo.shape, I32)))
    return out[3], out[5], out[6]


def _select_threshold(segments, rows, topk, alive=None):
    assert topk <= SEL_GROUPS
    kf = jnp.full((rows, 1), float(topk), F32)
    if alive is None:
        alive = jnp.full((rows, 1), True)
    total = sum(n for _, n in segments) * SEL_WIDE

    los, his = [], []
    for g in range(rows // SEL_ROWS):
        rs = slice(g * SEL_ROWS, (g + 1) * SEL_ROWS)
        ninf = jnp.full((SEL_ROWS, LANES), -jnp.inf, F32)
        ga, gb = _sweep(segments, rs, (ninf, ninf),
                        lambda a, x, idx0, k: ((jnp.maximum(a[0], x), a[1]) if k % 2 == 0
                                               else (a[0], jnp.maximum(a[1], x))))
        los.append(jnp.min(jnp.minimum(ga, gb), axis=1, keepdims=True))
        his.append(jnp.max(jnp.maximum(ga, gb), axis=1, keepdims=True))
    cat = lambda xs: jnp.concatenate(xs, axis=0) if len(xs) > 1 else xs[0]
    lo = _f32_to_key(jnp.maximum(cat(los), F32_LOWEST))
    hi = _f32_to_key(cat(his)) + 1

    def count_ge_f(thr):
        def make_pred(rs):
            t = _lanes(thr[rs])
            return lambda x, idx0: x >= t
        return _count(segments, rows, make_pred)

    count_ge = lambda v: count_ge_f(_key_to_f32(v))
    c_lo = count_ge(lo)
    c_hi = jnp.zeros((rows, 1), F32)
    for probe in (0, MIN_NORMAL_KEY):
        v = jnp.full((rows, 1), probe, I32)
        cnt = count_ge_f(jnp.full((rows, 1), np.int32(probe).view(np.float32), F32))
        inside = jnp.logical_and(lo < v, v < hi)
        up = jnp.logical_and(inside, cnt >= kf)
        dn = jnp.logical_and(inside, cnt < kf)
        lo, c_lo = jnp.where(up, v, lo), jnp.where(up, cnt, c_lo)
        hi, c_hi = jnp.where(dn, v, hi), jnp.where(dn, cnt, c_hi)
    at_zero = jnp.logical_and(lo == 0, hi == MIN_NORMAL_KEY)
    lo, c_lo, c_hi = _search(count_ge, lo, hi, c_lo, c_hi, kf, jnp.logical_and(alive, jnp.logical_not(at_zero)))
    tau = _key_to_f32(lo)

    need = jnp.logical_and(alive, c_lo > kf)
    drop = c_lo - kf

    def count_tail(v):
        def make_pred(rs):
            t, vv = _lanes(tau[rs]), _lanes(v[rs])
            lane = lax.broadcasted_iota(I32, (SEL_ROWS, LANES), 1)
            return lambda x, idx0: jnp.logical_and(x == t, lane >= vv - idx0)
        return _count(segments, rows, make_pred)

    zero = jnp.zeros((rows, 1), I32)
    cut, _, _ = _search(count_tail, zero, zero + total, c_lo - c_hi, jnp.zeros((rows, 1), F32), drop, need)
    return tau, jnp.where(need, cut, INT_MAX)


def _write_mask(segments_out, rows, tau, cut):
    base = 0
    for src, dst, n_wide, n_total in segments_out:
        for g in range(rows // SEL_ROWS):
            rs = slice(g * SEL_ROWS, (g + 1) * SEL_ROWS)
            t = _spread(_lanes(tau[rs]), SEL_WIDE)
            ct = _spread(_lanes(cut[rs]), SEL_WIDE)
            lane = lax.broadcasted_iota(I32, (SEL_ROWS, SEL_WIDE), 1)

            def body(c, carry, src=src, dst=dst, base=base, rs=rs, t=t, ct=ct, lane=lane):
                off = pl.multiple_of(c * SEL_WIDE, SEL_WIDE)
                x = src[rs, pl.ds(off, SEL_WIDE)]
                tie = jnp.logical_and(x == t, lane < ct - (base + off))
                sel = jnp.where(x > t, 1.0, jnp.where(tie, 1.0, 0.0))
                dst[rs, pl.ds(off, SEL_WIDE)] = sel.astype(dst.dtype)
                return carry

            lax.fori_loop(0, n_wide, body, 0)

            def zbody(c, carry, dst=dst, rs=rs):
                off = pl.multiple_of(c * SEL_WIDE, SEL_WIDE)
                dst[rs, pl.ds(off, SEL_WIDE)] = jnp.zeros((SEL_ROWS, SEL_WIDE), dst.dtype)
                return carry

            lax.fori_loop(n_wide, n_total, zbody, 0)
        base = base + n_wide * SEL_WIDE


def _index_scores(qi_ref, w, kb):
    acc = None
    for h in range(IDX_HEADS):
        s = _dot_nt(_mx(qi_ref[h]), kb)
        t = w[:, h:h + 1] * jnp.maximum(s, 0.0)
        acc = t if acc is None else acc + t
    return acc


def _prompt_select_kernel(qi_ref, wi_ref, kidx_ref, mask_ref, i_scr, *, tq, tk, topk):
    i, j = pl.program_id(0), pl.program_id(1)
    nk = pl.num_programs(1)
    q_lo = i * tq
    n_wide = (q_lo + tq - 1) // tk + 1

    @pl.when(j < n_wide)
    def _():
        acc = _index_scores(qi_ref, wi_ref[...], kidx_ref[...])
        qpos = q_lo + lax.broadcasted_iota(I32, acc.shape, 0)
        kpos = j * tk + lax.broadcasted_iota(I32, acc.shape, 1)
        i_scr[:, pl.ds(pl.multiple_of(j * tk, tk), tk)] = jnp.where(kpos <= qpos, acc, -jnp.inf)

    @pl.when(j == nk - 1)
    def _():
        seg = [(i_scr, n_wide)]
        tau, cut = _select_threshold(seg, tq, topk)
        _write_mask([(i_scr, mask_ref, n_wide, nk)], tq, tau, cut)


def _prompt_select(qi_hm, wi_f, ki_b, topk, tq, tk):
    s = ki_b.shape[0]
    assert tk == SEL_WIDE and s % tk == 0 and s % tq == 0 and tq % SEL_ROWS == 0
    nq, nk = s // tq, s // tk
    kmap = lambda i, j: (jnp.minimum(j, (i * tq + tq - 1) // tk), 0)
    return pl.pallas_call(
        functools.partial(_prompt_select_kernel, tq=tq, tk=tk, topk=topk),
        grid=(nq, nk),
        in_specs=[pl.BlockSpec((IDX_HEADS, tq, IDX_DIM), lambda i, j: (0, i, 0)),
                  pl.BlockSpec((tq, LANES), lambda i, j: (i, 0)),
                  pl.BlockSpec((tk, IDX_DIM), kmap)],
        out_specs=pl.BlockSpec((tq, s), lambda i, j: (i, 0)),
        out_shape=jax.ShapeDtypeStruct((s, s), MXU_DTYPE),
        scratch_shapes=[pltpu.VMEM((tq, s), F32)],
        compiler_params=_cparams(("parallel", "arbitrary")),
        name="prompt_select",
    )(qi_hm, wi_f, ki_b)


def _spread(a, n):
    if n <= LANES:
        return a[:, :n]
    return jnp.concatenate([a] * (n // LANES), axis=1)


def _flash_update(s, v, m_prev, l_prev, acc_prev):
    m_new = jnp.maximum(m_prev, jnp.max(s, axis=1, keepdims=True))
    alpha = jnp.exp(m_prev - m_new)
    p = jnp.exp(s - _spread(m_new, s.shape[1]))
    l_new = alpha * l_prev + jnp.sum(p, axis=1, keepdims=True)
    acc_new = _spread(alpha, acc_prev.shape[1]) * acc_prev + _dot(_mx(p), v)
    return m_new, l_new, acc_new


ATT_ROWS = 256


def _prompt_attend_kernel(q_ref, k_ref, v_ref, mask_ref, o_ref, m_scr, l_scr, acc_scr, *, tq, tk):
    i, j = pl.program_id(0), pl.program_id(1)
    nk = pl.num_programs(1)

    @pl.when(j == 0)
    def _():
        m_scr[...] = jnp.full(m_scr.shape, NEG_BIG, F32)
        l_scr[...] = jnp.zeros(l_scr.shape, F32)
        acc_scr[...] = jnp.zeros(acc_scr.shape, F32)

    @pl.when(j * tk <= i * tq + tq - 1)
    def _():
        for r in range(tq // ATT_ROWS):
            rs = slice(r * ATT_ROWS, (r + 1) * ATT_ROWS)
            keep = mask_ref[rs, :] > 0
            for h in range(N_HEADS_A):
                s = jnp.where(keep, _dot_nt(q_ref[h, rs, :], k_ref[h]), NEG_BIG)
                m_scr[h, rs, :], l_scr[h, rs, :], acc_scr[h, rs, :] = _flash_update(
                    s, v_ref[h], m_scr[h, rs, :], l_scr[h, rs, :], acc_scr[h, rs, :])

    @pl.when(j == nk - 1)
    def _():
        for h in range(N_HEADS_A):
            o = acc_scr[h] / l_scr[h][:, :HEAD_DIM_A]
            o_ref[:, h * HEAD_DIM_A:(h + 1) * HEAD_DIM_A] = o.astype(o_ref.dtype)


def _prompt_attend(qa_hm, ka_hm, va_hm, mask, tq, tk):
    nh, s, dh = qa_hm.shape
    nq, nk = s // tq, s // tk
    diag = lambda i, j: jnp.minimum(j, (i * tq + tq - 1) // tk)
    return pl.pallas_call(
        functools.partial(_prompt_attend_kernel, tq=tq, tk=tk),
        grid=(nq, nk),
        in_specs=[pl.BlockSpec((nh, tq, dh), lambda i, j: (0, i, 0)),
                  pl.BlockSpec((nh, tk, dh), lambda i, j: (0, diag(i, j), 0)),
                  pl.BlockSpec((nh, tk, dh), lambda i, j: (0, diag(i, j), 0)),
                  pl.BlockSpec((tq, tk), lambda i, j: (i, diag(i, j)))],
        out_specs=pl.BlockSpec((tq, nh * dh), lambda i, j: (i, 0)),
        out_shape=jax.ShapeDtypeStruct((s, nh * dh), MXU_DTYPE),
        scratch_shapes=[pltpu.VMEM((nh, tq, LANES), F32), pltpu.VMEM((nh, tq, LANES), F32),
                        pltpu.VMEM((nh, tq, dh), F32)],
        compiler_params=_cparams(("parallel", "arbitrary")),
        name="prompt_attend",
    )(qa_hm, ka_hm, va_hm, mask)


PAGES_PER_STEP = 8


def _sample_index_kernel(pt_ref, qi_ref, wi_ref, *refs, page):
    del pt_ref
    pages, out_ref = refs[:-1], refs[-1]
    w = wi_ref[0]
    for p, kref in enumerate(pages):
        out_ref[0, :, p * page:(p + 1) * page] = _index_scores(qi_ref, w, _mx(kref[0]))


def _sample_index(page_table, qi_hm, wi_f, cache_kidx):
    db, n_pages = page_table.shape
    _, page, idim = cache_kidx.shape
    pps = math.gcd(PAGES_PER_STEP, n_pages)
    nsteps = n_pages // pps
    pt = page_table.reshape(-1).astype(I32)

    def kspec(p):
        return pl.BlockSpec((1, page, idim), lambda b, j, pt: (pt[b * n_pages + j * pps + p], 0, 0))

    grid_spec = pltpu.PrefetchScalarGridSpec(
        num_scalar_prefetch=1,
        grid=(db, nsteps),
        in_specs=[pl.BlockSpec((IDX_HEADS, T_PAD, idim), lambda b, j, pt: (0, b, 0)),
                  pl.BlockSpec((1, T_PAD, LANES), lambda b, j, pt: (b, 0, 0))]
                 + [kspec(p) for p in range(pps)],
        out_specs=pl.BlockSpec((1, T_PAD, pps * page), lambda b, j, pt: (b, 0, j)),
    )
    return pl.pallas_call(
        functools.partial(_sample_index_kernel, page=page),
        grid_spec=grid_spec,
        out_shape=jax.ShapeDtypeStruct((db, T_PAD, n_pages * page), F32),
        compiler_params=_cparams(("parallel", "arbitrary")),
        name="sample_index",
    )(pt, qi_hm, wi_f.reshape(db, T_PAD, LANES), *([cache_kidx] * pps))


def _sample_select_kernel(ipast_ref, qi_ref, wi_ref, kin_ref, mpast_ref, mnew_ref, inew_scr, *, t_real, topk):
    rows = ipast_ref.shape[0]
    acc = _index_scores(qi_ref, wi_ref[...], kin_ref[...])
    r = lax.broadcasted_iota(I32, acc.shape, 0)
    c = lax.broadcasted_iota(I32, acc.shape, 1)
    same = (r // T_PAD) == (c // T_PAD)
    tq, tc = r % T_PAD, c % T_PAD
    ok = jnp.logical_and(same, jnp.logical_and(tc <= tq, tc < t_real))
    inew_scr[...] = jnp.full(inew_scr.shape, -jnp.inf, F32)
    inew_scr[:, :rows] = jnp.where(ok, acc, -jnp.inf)
    n_past = ipast_ref.shape[1] // SEL_WIDE
    n_new = inew_scr.shape[1] // SEL_WIDE
    alive = lax.rem(lax.broadcasted_iota(I32, (rows, 1), 0), T_PAD) < t_real
    tau, cut = _select_threshold([(ipast_ref, n_past), (inew_scr, n_new)], rows, topk, alive)
    _write_mask([(ipast_ref, mpast_ref, n_past, n_past), (inew_scr, mnew_ref, n_new, n_new)], rows, tau, cut)


def _sample_select(i_past, qi_hm, wi_f, ki_b, t_real, topk):
    rows, past = i_past.shape
    assert past % SEL_WIDE == 0 and rows % SEL_ROWS == 0
    wnew = -(-rows // SEL_WIDE) * SEL_WIDE
    return pl.pallas_call(
        functools.partial(_sample_select_kernel, t_real=t_real, topk=topk),
        out_shape=[jax.ShapeDtypeStruct((rows, past), F32), jax.ShapeDtypeStruct((rows, wnew), F32)],
        scratch_shapes=[pltpu.VMEM((rows, wnew), F32)],
        compiler_params=pltpu.CompilerParams(vmem_limit_bytes=VMEM_LIMIT),
        name="sample_select",
    )(i_past, qi_hm, wi_f, ki_b)


def _sample_attend_kernel(pt_ref, q_ref, mp_ref, mn_ref, kn_ref, vn_ref, *refs, page, pps):
    del pt_ref
    kpages, vpages = refs[:pps], refs[pps:2 * pps]
    o_ref, m_scr, l_scr, acc_scr = refs[2 * pps:]
    j = pl.program_id(1)
    nj = pl.num_programs(1)
    q = q_ref[0]
    nh = q.shape[0] // T_PAD

    def scores(m_t, k):
        n = k.shape[0]
        row_h = lax.broadcasted_iota(I32, (q.shape[0], n), 0) // T_PAD
        lane_h = lax.broadcasted_iota(I32, (q.shape[0], n), 1) % nh
        flags = jnp.where(row_h == lane_h, jnp.concatenate([m_t] * nh, axis=0), 0.0)
        return jnp.where(flags > 0, _dot_nt(q, k), NEG_BIG)

    def update(s_list, v_list):
        m_prev = m_scr[...]
        m_blk = functools.reduce(jnp.maximum, [jnp.max(s, axis=1, keepdims=True) for s in s_list])
        m_new = jnp.maximum(m_prev, m_blk)
        alpha = jnp.exp(m_prev - m_new)
        l_new = alpha * l_scr[...]
        acc = _spread(alpha, acc_scr.shape[1]) * acc_scr[...]
        for s, v in zip(s_list, v_list):
            p = jnp.exp(s - _spread(m_new, s.shape[1]))
            l_new = l_new + jnp.sum(p, axis=1, keepdims=True)
            acc = acc + _dot(_mx(p), v)
        m_scr[...], l_scr[...], acc_scr[...] = m_new, l_new, acc

    @pl.when(j == 0)
    def _():
        m_scr[...] = jnp.full(m_scr.shape, NEG_BIG, F32)
        l_scr[...] = jnp.zeros(l_scr.shape, F32)
        acc_scr[...] = jnp.zeros(acc_scr.shape, F32)

    rows_pp = page * nh
    s_list, v_list = [], []
    for p in range(pps):
        k = _mx(kpages[p][0].reshape(rows_pp, q.shape[1]))
        s_list.append(scores(mp_ref[0, :, p * rows_pp:(p + 1) * rows_pp], k))
        v_list.append(_mx(vpages[p][0].reshape(rows_pp, q.shape[1])))
    update(s_list, v_list)

    @pl.when(j == nj - 1)
    def _():
        update([scores(mn_ref[0], kn_ref[0])], [vn_ref[0]])
        o_ref[0] = acc_scr[...] / _spread(l_scr[...], acc_scr.shape[1])


def _sample_attend(page_table, q_ht, m_past, m_new, k_new, v_new, cache_k, cache_v):
    db, n_pages = page_table.shape
    _, page, nh, dh = cache_k.shape
    pps = math.gcd(PAGES_PER_STEP, n_pages)
    nsteps = n_pages // pps
    nnew = k_new.shape[1]
    nq = q_ht.shape[1]
    pt = page_table.reshape(-1).astype(I32)

    def pspec(p):
        return pl.BlockSpec((1, page, nh, dh), lambda b, j, pt: (pt[b * n_pages + j * pps + p], 0, 0, 0))

    bspec = lambda shape: pl.BlockSpec((1,) + shape, lambda b, j, pt: (b, 0, 0))
    grid_spec = pltpu.PrefetchScalarGridSpec(
        num_scalar_prefetch=1,
        grid=(db, nsteps),
        in_specs=[bspec((nq, dh)),
                  pl.BlockSpec((1, T_PAD, pps * page * nh), lambda b, j, pt: (b, 0, j)),
                  bspec((T_PAD, nnew)), bspec((nnew, dh)), bspec((nnew, dh))]
                 + [pspec(p) for p in range(pps)] * 2,
        out_specs=bspec((nq, dh)),
        scratch_shapes=[pltpu.VMEM((nq, LANES), F32), pltpu.VMEM((nq, LANES), F32), pltpu.VMEM((nq, dh), F32)],
    )
    return pl.pallas_call(
        functools.partial(_sample_attend_kernel, page=page, pps=pps),
        grid_spec=grid_spec,
        out_shape=jax.ShapeDtypeStruct((db, nq, dh), F32),
        compiler_params=_cparams(("parallel", "arbitrary")),
        name="sample_attend",
    )(pt, q_ht, m_past, m_new, k_new, v_new, *([cache_k] * pps), *([cache_v] * pps))


def _retention_tables(c_real, c_pad):
    h = np.arange(N_HEADS_R, dtype=np.float64)
    log_g = np.log1p(-np.exp2(-5.0 - h))
    i = np.arange(c_pad, dtype=np.float64)
    diff = i[:, None] - i[None, :]
    live = (diff >= 0) & (i[:, None] < c_real) & (i[None, :] < c_real)
    inner = np.where(live[None], np.exp(np.maximum(diff, 0.0)[None] * log_g[:, None, None]), 0.0)
    q_dec = np.exp((i + 1.0)[None, :] * log_g[:, None])
    k_dec = np.where(i[None, :] < c_real, np.exp((c_real - 1.0 - i)[None, :] * log_g[:, None]), 0.0)
    c_dec = np.exp(c_real * log_g)
    f = lambda a: jnp.asarray(a, F32)
    return f(inner), f(q_dec[:, :, None]), f(k_dec[:, :, None]), [float(v) for v in c_dec]


def _retention_kernel(q_ref, k_ref, v_ref, g_ref, s0_ref, inner_ref, qdec_ref, kdec_ref,
                      o_ref, s_out_ref, s_scr, *, c_dec):
    j = pl.program_id(1)
    nj = pl.num_programs(1)

    @pl.when(j == 0)
    def _():
        s_scr[...] = s0_ref[0]

    for h in range(N_HEADS_R):
        q = q_ref[:, h * DK_R:(h + 1) * DK_R]
        k = k_ref[:, h * DK_R:(h + 1) * DK_R]
        v = v_ref[:, h * DV_R:(h + 1) * DV_R]
        s_prev = s_scr[h]
        a = _dot_nt(_mx(q), _mx(k)) * inner_ref[h]
        o = _dot(_mx(a), v) + _dot(_mx(q), _mx(s_prev)) * qdec_ref[h]
        kd = k * kdec_ref[h]
        s_scr[h] = s_prev * c_dec[h] + _dot(_mx(kd.T), v)
        mu = jnp.mean(o, axis=-1, keepdims=True)
        var = jnp.mean(jnp.square(o - mu), axis=-1, keepdims=True)
        gn = (o - mu) * lax.rsqrt(var + EPS)
        g = g_ref[:, h * DV_R:(h + 1) * DV_R]
        o_ref[:, h * DV_R:(h + 1) * DV_R] = (gn * (g * _sigmoid(g))).astype(o_ref.dtype)

    @pl.when(j == nj - 1)
    def _():
        s_out_ref[0] = s_scr[...]


def _retention(qr, kr, vr, gr, s0, c_real):
    b = s0.shape[0]
    c = RET_CHUNK
    n = qr.shape[0] // (b * c)
    inner, qdec, kdec, c_dec = _retention_tables(c_real, c)
    wr, wv = N_HEADS_R * DK_R, N_HEADS_R * DV_R
    rmap = lambda bi, j: (bi * n + j, 0)
    full3 = lambda shape: pl.BlockSpec(shape, lambda bi, j: (0, 0, 0))
    return pl.pallas_call(
        functools.partial(_retention_kernel, c_dec=c_dec),
        grid=(b, n),
        in_specs=[pl.BlockSpec((c, wr), rmap), pl.BlockSpec((c, wr), rmap), pl.BlockSpec((c, wv), rmap),
                  pl.BlockSpec((c, wv), rmap),
                  pl.BlockSpec((1, N_HEADS_R, DK_R, DV_R), lambda bi, j: (bi, 0, 0, 0)),
                  full3(inner.shape), full3(qdec.shape), full3(kdec.shape)],
        out_specs=[pl.BlockSpec((c, wv), rmap),
                   pl.BlockSpec((1, N_HEADS_R, DK_R, DV_R), lambda bi, j: (bi, 0, 0, 0))],
        out_shape=[jax.ShapeDtypeStruct((b * n * c, wv), MXU_DTYPE),
                   jax.ShapeDtypeStruct((b, N_HEADS_R, DK_R, DV_R), F32)],
        scratch_shapes=[pltpu.VMEM((N_HEADS_R, DK_R, DV_R), F32)],
        compiler_params=_cparams(("parallel", "arbitrary")),
        name="retention",
    )(qr, kr, vr, gr, s0, inner, qdec, kdec)


def _cross_kernel(q_ref, mk_ref, mv_ref, o_ref):
    scale = HEAD_DIM_M ** -0.5
    for h in range(N_HEADS_M):
        sl = slice(h * HEAD_DIM_M, (h + 1) * HEAD_DIM_M)
        s = _dot_nt(_mx(q_ref[:, sl]), _mx(mk_ref[0, :, sl])) * scale
        p = jnp.exp(s - jnp.max(s, axis=1, keepdims=True))
        p = p / jnp.sum(p, axis=1, keepdims=True)
        o_ref[:, sl] = _dot(_mx(p), _mx(mv_ref[0, :, sl])).astype(o_ref.dtype)


def _cross_attend(qm, mk, mv, tm, out_dtype):
    b, n_mem, hd = mk.shape
    nt = qm.shape[0] // (b * tm)
    return pl.pallas_call(
        _cross_kernel,
        grid=(b, nt),
        in_specs=[pl.BlockSpec((tm, hd), lambda bi, i: (bi * nt + i, 0)),
                  pl.BlockSpec((1, n_mem, hd), lambda bi, i: (bi, 0, 0)),
                  pl.BlockSpec((1, n_mem, hd), lambda bi, i: (bi, 0, 0))],
        out_specs=pl.BlockSpec((tm, hd), lambda bi, i: (bi * nt + i, 0)),
        out_shape=jax.ShapeDtypeStruct(qm.shape, out_dtype),
        compiler_params=_cparams(("parallel", "parallel")),
        name="cross_attend",
    )(qm, mk, mv)


def _memkv_kernel(x_ref, g_ref, w_ref, o_ref):
    o_ref[...] = _dot(_mx(_rms(x_ref[...], g_ref[...])), w_ref[...])


def _memory_kv(mem, gain, w):
    rows, d = mem.shape
    return pl.pallas_call(
        _memkv_kernel,
        out_shape=jax.ShapeDtypeStruct((rows, w.shape[1]), F32),
        compiler_params=pltpu.CompilerParams(vmem_limit_bytes=VMEM_LIMIT),
        name="memory_kv",
    )(mem, gain.reshape(1, d), _mx(w))


def _merge_kernel(x_ref, oa_ref, or_ref, om_ref, gates_ref, wpa_ref, wpb_ref, wpc_ref, wo_ref, g_ref, h_ref):
    d = x_ref.shape[1]
    gt = gates_ref[...]
    mixed = (_sigmoid(gt[:, :d]) * _dot(_mx(oa_ref[...]), wpa_ref[...])
             + _sigmoid(gt[:, d:2 * d]) * _dot(_mx(or_ref[...]), wpb_ref[...])
             + _sigmoid(gt[:, 2 * d:]) * _dot(_mx(om_ref[...]), wpc_ref[...]))
    z = _dot(_mx(mixed), wo_ref[...])
    h_ref[...] = x_ref[...] + _rms(z, g_ref[...])


def _merge(x, oa, o_r, om, gates, wpa, wpb, wpc, wo, gain, tm):
    rows, d = x.shape
    return pl.pallas_call(
        _merge_kernel,
        grid=(rows // tm,),
        in_specs=[_row_spec(tm, d), _row_spec(tm, oa.shape[1]), _row_spec(tm, o_r.shape[1]),
                  _row_spec(tm, om.shape[1]), _row_spec(tm, 3 * d),
                  _full_spec(wpa.shape), _full_spec(wpb.shape), _full_spec(wpc.shape), _full_spec(wo.shape),
                  _full_spec((1, d))],
        out_specs=_row_spec(tm, d),
        out_shape=jax.ShapeDtypeStruct((rows, d), F32),
        compiler_params=_cparams(("parallel",)),
        name="merge",
    )(x, oa, o_r, om, gates, wpa, wpb, wpc, wo, gain.reshape(1, d))


HALO = BF16_ROWS


def _ffn_kernel(h_ref, halo_ref, s0_ref, s1_ref, g1_ref, g2_ref, wu_ref, wg_ref, cw_ref, cb_ref, wd_ref,
                y_ref, utail_ref, x_scr, u_scr, *, tm, seq, keep):
    i = pl.program_id(0)
    h = h_ref[...]
    hn = _rms(h, g1_ref[...])
    x_scr[HALO:, :] = _mx(hn)
    x_scr[:HALO, :] = _mx(_rms(halo_ref[...], g1_ref[...]))
    xc = x_scr[...]
    u_scr[...] = _dot(xc, wu_ref[...])
    gate = _dot(xc[HALO:], wg_ref[...])
    cur = u_scr[HALO:, :]
    prev1 = u_scr[HALO - 1:HALO - 1 + tm, :]
    prev2 = u_scr[HALO - 2:HALO - 2 + tm, :]
    seq_loc = min(seq, tm)
    t = lax.rem(lax.broadcasted_iota(I32, (tm, 1), 0), seq_loc)
    t = jnp.where(lax.rem(i * tm, seq) == 0, t, CONV_W)
    st0, st1 = s0_ref[...], s1_ref[...]
    if st0.shape[0] != tm:
        st0, st1 = st0[0:1], st1[0:1]
    prev1 = jnp.where(t == 0, st1, prev1)
    prev2 = jnp.where(t == 0, st0, jnp.where(t == 1, st1, prev2))
    c = cb_ref[...] + prev2 * cw_ref[0:1, :] + prev1 * cw_ref[1:2, :] + cur * cw_ref[2:3, :]
    act = jax.nn.gelu(c, approximate=True) * gate
    ff = _dot(_mx(act), wd_ref[...])
    y_ref[...] = h + _rms(ff, g2_ref[...])
    utail_ref[...] = u_scr[HALO + tm - keep:, :]


def _conv_ffn(h, s0e, s1e, g1, g2, wu, wg, cw, cb, wd, tm, seq, keep):
    rows, d = h.shape
    f = wu.shape[1]
    nt = rows // tm
    hb = tm // HALO
    sr = s0e.shape[0]
    return pl.pallas_call(
        functools.partial(_ffn_kernel, tm=tm, seq=seq, keep=keep),
        grid=(nt,),
        in_specs=[_row_spec(tm, d),
                  pl.BlockSpec((HALO, d), lambda i: (jnp.maximum(i * hb - 1, 0), 0)),
                  _full_spec((sr, f)), _full_spec((sr, f)),
                  _full_spec((1, d)), _full_spec((1, d)),
                  _full_spec(wu.shape), _full_spec(wg.shape), _full_spec(cw.shape), _full_spec((1, f)),
                  _full_spec(wd.shape)],
        out_specs=[_row_spec(tm, d), _row_spec(keep, f)],
        out_shape=[jax.ShapeDtypeStruct((rows, d), F32), jax.ShapeDtypeStruct((nt * keep, f), F32)],
        scratch_shapes=[pltpu.VMEM((tm + HALO, d), MXU_DTYPE), pltpu.VMEM((tm + HALO, f), F32)],
        compiler_params=_cparams(("parallel",)),
        name="conv_ffn",
    )(h, h, s0e, s1e, g1.reshape(1, d), g2.reshape(1, d), wu, wg, cw, cb.reshape(1, f), wd)


def _layer_weights(l, w_in, w_proj_a, w_proj_b, w_proj_c, w_out, w_up, w_down):
    wts = _prep_in_weights(w_in[l])
    f = w_down.shape[1]
    wts.update(wpa=_mx(w_proj_a[l]), wpb=_mx(w_proj_b[l]), wpc=_mx(w_proj_c[l]), wo=_mx(w_out[l]),
               wu=_mx(w_up[l][:, :f]), wg=_mx(w_up[l][:, f:]), wd=_mx(w_down[l]))
    return wts


def _prompt_layer(x, mem, wts, norms, conv_w, conv_b, w_mem_kv, tiles):
    s, d = x.shape
    f = wts["wd"].shape[0]
    pos = jnp.arange(s)
    pr = _projections(x, norms["pre_mix"], pos, wts, tiles["proj"])
    topk = min(TOPK_MAX, s // 4)
    mask = _prompt_select(pr["qi_hm"], pr["wi_f"], pr["ki_b"], topk, tiles["sel_q"], SEL_WIDE)
    o_a = _prompt_attend(pr["qa_hm"], pr["ka_hm"], pr["va_hm"], mask, tiles["att_q"], tiles["att_k"])
    s0 = jnp.zeros((1, N_HEADS_R, DK_R, DV_R), F32)
    o_r, ret_new = _retention(pr["qr_f"], pr["kr_f"], pr["vr_b"], pr["gr_f"], s0, RET_CHUNK)
    kv = _memory_kv(mem, norms["mem"], w_mem_kv)
    wm = N_HEADS_M * HEAD_DIM_M
    mk, mv = kv[:, :wm], kv[:, wm:]
    o_m = _cross_attend(pr["qm_b"], mk[None], mv[None], tiles["cross"], MXU_DTYPE)
    h = _merge(x, o_a, o_r, o_m, pr["gates_f"], wts["wpa"], wts["wpb"], wts["wpc"], wts["wo"],
               norms["post_mix"], tiles["merge"])
    zst = jnp.zeros((SUBLANES, f), F32)
    y, utail = _conv_ffn(h, zst, zst, norms["pre_ffn"], norms["post_ffn"], wts["wu"], wts["wg"], conv_w, conv_b,
                         wts["wd"], tiles["ffn"], s, SUBLANES)
    conv_new = utail[-(CONV_W - 1):]
    return y, pr["ka_f"], pr["va_f"], pr["ki_f"], ret_new, conv_new, mk, mv


def _sample_layer(x, wts, norms, conv_w, conv_b, cache_k, cache_v, cache_kidx, mem_k, mem_v,
                  state_ret, state_conv, page_table):
    db, t, d = x.shape
    f = wts["wd"].shape[0]
    n_pages = page_table.shape[1]
    page = cache_k.shape[1]
    past = n_pages * page
    rows = db * T_PAD
    xp = jnp.pad(x, ((0, 0), (0, T_PAD - t), (0, 0))).reshape(rows, d)
    pos = jnp.tile(past + jnp.arange(T_PAD), db)
    pr = _projections(xp, norms["pre_mix"], pos, wts, rows)
    hd = N_HEADS_A * HEAD_DIM_A

    topk = min(TOPK_MAX, (past + t) // 4)
    i_past = _sample_index(page_table, pr["qi_hm"].astype(F32), pr["wi_f"], cache_kidx).reshape(rows, past)
    m_past, m_new = _sample_select(i_past, pr["qi_hm"], pr["wi_f"], pr["ki_b"], t, topk)
    nnew = m_new.shape[1]
    own = m_new[:, :rows].reshape(db, T_PAD, db, T_PAD)[jnp.arange(db), :, jnp.arange(db), :]
    nh, dh = N_HEADS_A, HEAD_DIM_A
    per_head = lambda m: jnp.repeat(m, nh, axis=-1)
    m_new_own = jnp.pad(per_head(own), ((0, 0), (0, 0), (0, LANES - T_PAD * nh)))
    q_ht = pr["qa_hm"].reshape(nh, db, T_PAD, dh).transpose(1, 0, 2, 3).reshape(db, nh * T_PAD, dh)
    new_rows = lambda a: jnp.pad(_mx(a).reshape(db, T_PAD * nh, dh), ((0, 0), (0, LANES - T_PAD * nh), (0, 0)))
    o_ht = _sample_attend(page_table, q_ht, per_head(m_past.reshape(db, T_PAD, past)), m_new_own,
                          new_rows(pr["ka_f"]), new_rows(pr["va_f"]), cache_k, cache_v)
    o_a = o_ht.reshape(db, nh, T_PAD, dh).transpose(0, 2, 1, 3).reshape(rows, hd)

    padc = lambda a: jnp.pad(a.reshape(db, T_PAD, -1), ((0, 0), (0, RET_CHUNK - T_PAD), (0, 0))).reshape(db * RET_CHUNK, -1)
    o_r, ret_new = _retention(padc(pr["qr_f"]), padc(pr["kr_f"]), padc(pr["vr_b"]), padc(pr["gr_f"]), state_ret, t)
    o_r = o_r.reshape(db, RET_CHUNK, -1)[:, :T_PAD].reshape(rows, -1)

    wm = N_HEADS_M * HEAD_DIM_M
    o_m = _cross_attend(pr["qm_b"].astype(F32), mem_k.reshape(db, -1, wm), mem_v.reshape(db, -1, wm), T_PAD, F32)

    h = _merge(xp, o_a, o_r, o_m, pr["gates_f"], wts["wpa"], wts["wpb"], wts["wpc"], wts["wo"],
               norms["post_mix"], rows)
    s0e = jnp.repeat(state_conv[:, 0], T_PAD, axis=0)
    s1e = jnp.repeat(state_conv[:, 1], T_PAD, axis=0)
    y, u_all = _conv_ffn(h, s0e, s1e, norms["pre_ffn"], norms["post_ffn"], wts["wu"], wts["wg"], conv_w, conv_b,
                         wts["wd"], rows, T_PAD, rows)
    ext = jnp.concatenate([state_conv.astype(F32), u_all.reshape(db, T_PAD, f)[:, :t]], axis=1)
    conv_new = ext[:, t:]
    unpad = lambda a: a.reshape(db, T_PAD, -1)[:, :t]
    return (unpad(y), unpad(pr["ka_f"]), unpad(pr["va_f"]), unpad(pr["ki_f"]), ret_new, conv_new)


PROMPT_TILES = dict(proj=512, sel_q=256, att_q=256, att_k=512, cross=512, merge=512, ffn=256)


def kernel(x_prompt, x_sample, cache_k, cache_v, cache_kidx, cache_mem_k, cache_mem_v, state_ret, state_conv,
           page_table, mem_prompt, norm_pre_mix, norm_post_mix, norm_pre_ffn, norm_post_ffn, norm_mem,
           w_in, w_mem_kv, w_proj_a, w_proj_b, w_proj_c, w_out, w_up, conv_w, conv_b, w_down):
    bp, s, d = x_prompt.shape
    db, t, _ = x_sample.shape
    depth = w_in.shape[0]
    assert bp == 1 and t <= T_PAD and CONV_W - 1 <= t
    tiles = {k: min(v, s) for k, v in PROMPT_TILES.items()}
    yp, ys = x_prompt[0], x_sample
    outs = [[] for _ in range(12)]
    for l in range(depth):
        wts = _layer_weights(l, w_in, w_proj_a, w_proj_b, w_proj_c, w_out, w_up, w_down)
        norms = dict(pre_mix=norm_pre_mix[l], post_mix=norm_post_mix[l], pre_ffn=norm_pre_ffn[l],
                     post_ffn=norm_post_ffn[l], mem=norm_mem[l])
        yp, kp, vp, kip, rp, cp, mk, mv = _prompt_layer(yp, mem_prompt[0], wts, norms, conv_w[l], conv_b[l],
                                                        w_mem_kv[l], tiles)
        ys, ks, vs, kis, rs, cs = _sample_layer(ys, wts, norms, conv_w[l], conv_b[l], cache_k[l], cache_v[l],
                                                cache_kidx[l], cache_mem_k[l], cache_mem_v[l], state_ret[l],
                                                state_conv[l], page_table)
        n_mem = mk.shape[0]
        vals = (kp.reshape(1, s, N_HEADS_A, HEAD_DIM_A), vp.reshape(1, s, N_HEADS_A, HEAD_DIM_A),
                kip.reshape(1, s, IDX_DIM), rp, cp[None],
                mk.reshape(1, n_mem, N_HEADS_M, HEAD_DIM_M), mv.reshape(1, n_mem, N_HEADS_M, HEAD_DIM_M),
                ks.reshape(db, t, N_HEADS_A, HEAD_DIM_A), vs.reshape(db, t, N_HEADS_A, HEAD_DIM_A),
                kis, rs, cs)
        for o, v in zip(outs, vals):
            o.append(v)
    stacked = [jnp.stack(o) for o in outs]
    return (yp[None], ys, *stacked)
```

```python
import functools
import math

import numpy as np
import jax
import jax.numpy as jnp
from jax import lax
from jax.experimental import pallas as pl
from jax.experimental.pallas import tpu as pltpu

F32 = jnp.float32
I32 = jnp.int32
MXU_DTYPE = jnp.bfloat16

N_HEADS_A, HEAD_DIM_A = 8, 64
IDX_HEADS, IDX_DIM = 4, 64
TOPK_MAX = 256
N_HEADS_R, DK_R, DV_R = 4, 128, 256
RET_CHUNK = 128
N_HEADS_M, HEAD_DIM_M = 4, 128
CONV_W = 3
ROPE_THETA = 10000.0
EPS = 1e-6

LANES = 128
SUBLANES = 8
BF16_ROWS = 16
VMEM_LIMIT = 56 * 1024 * 1024
NEG_BIG = -1e30
F32_LOWEST = float(np.finfo(np.float32).min)
T_PAD = 8


def _cparams(sem):
    return pltpu.CompilerParams(dimension_semantics=sem, vmem_limit_bytes=VMEM_LIMIT)


def _dot(a, b):
    return jnp.dot(a, b, preferred_element_type=F32)


def _dot_nt(a, b):
    return lax.dot_general(a, b, (((1,), (1,)), ((), ())), preferred_element_type=F32)


def _mx(a):
    return a.astype(MXU_DTYPE)


def _rms(x, g):
    return x * lax.rsqrt(jnp.mean(x * x, axis=-1, keepdims=True) + EPS) * g


def _sigmoid(x):
    return 1.0 / (1.0 + jnp.exp(-x))


def _rope_tables(pos, d):
    half = d // 2
    inv = 1.0 / (ROPE_THETA ** (jnp.arange(half, dtype=F32) * 2.0 / d))
    ang = pos.astype(F32)[:, None] * inv[None, :]
    cos, sin = jnp.cos(ang), jnp.sin(ang)
    reps = LANES // d
    cos_t = jnp.tile(jnp.concatenate([cos, cos], axis=1), (1, reps))
    sin_t = jnp.tile(jnp.concatenate([-sin, sin], axis=1), (1, reps))
    return cos_t, sin_t


def _rope(y, cos, sin, d):
    w = y.shape[1]
    half = d // 2
    reps = w // LANES
    c = jnp.concatenate([cos] * reps, axis=1) if reps > 1 else cos
    s = jnp.concatenate([sin] * reps, axis=1) if reps > 1 else sin
    lane = lax.broadcasted_iota(I32, y.shape, 1)
    first = (lane & (d - 1)) < half
    rot = jnp.where(first, pltpu.roll(y, w - half, 1), pltpu.roll(y, half, 1))
    return y * c + rot * s


def _proj_a_kernel(x_ref, g_ref, cos_ref, sin_ref, wa_ref, wv_ref,
                   qa_hm, ka_f, ka_hm, qi_hm, ki_f, ki_b, va_f, va_hm, wi_f):
    xn = _mx(_rms(x_ref[...], g_ref[...]))
    wa = N_HEADS_A * HEAD_DIM_A
    y = _rope(_dot(xn, wa_ref[...]), cos_ref[...], sin_ref[...], HEAD_DIM_A)
    q = y[:, :wa] * (HEAD_DIM_A ** -0.5)
    k = y[:, wa:2 * wa]
    ka_f[...] = k
    for h in range(N_HEADS_A):
        sl = slice(h * HEAD_DIM_A, (h + 1) * HEAD_DIM_A)
        qa_hm[h] = _mx(q[:, sl])
        ka_hm[h] = _mx(k[:, sl])
    qi = y[:, 2 * wa:2 * wa + IDX_HEADS * IDX_DIM]
    for h in range(IDX_HEADS):
        qi_hm[h] = _mx(qi[:, h * IDX_DIM:(h + 1) * IDX_DIM])
    ki = y[:, 2 * wa + IDX_HEADS * IDX_DIM:2 * wa + IDX_HEADS * IDX_DIM + IDX_DIM]
    ki_f[...] = ki
    ki_b[...] = _mx(ki)
    z = _dot(xn, wv_ref[...])
    v = z[:, :wa]
    va_f[...] = v
    for h in range(N_HEADS_A):
        va_hm[h] = _mx(v[:, h * HEAD_DIM_A:(h + 1) * HEAD_DIM_A])
    wi_f[...] = z[:, wa:wa + LANES]


def _proj_b_kernel(x_ref, g_ref, cos_ref, sin_ref, wr_ref, wvg_ref, qr_f, kr_f, vr_b, gr_f):
    xn = _mx(_rms(x_ref[...], g_ref[...]))
    wr = N_HEADS_R * DK_R
    y = _rope(_dot(xn, wr_ref[...]), cos_ref[...], sin_ref[...], DK_R)
    qr_f[...] = y[:, :wr]
    kr_f[...] = y[:, wr:] * (DK_R ** -0.5)
    z = _dot(xn, wvg_ref[...])
    wv = N_HEADS_R * DV_R
    vr_b[...] = _mx(z[:, :wv])
    gr_f[...] = z[:, wv:]


def _proj_c_kernel(x_ref, g_ref, wc_ref, qm_b, gates_f):
    xn = _mx(_rms(x_ref[...], g_ref[...]))
    z = _dot(xn, wc_ref[...])
    wm = N_HEADS_M * HEAD_DIM_M
    qm_b[...] = _mx(z[:, :wm])
    gates_f[...] = z[:, wm:]


def _row_spec(tm, w):
    return pl.BlockSpec((tm, w), lambda i: (i, 0))


def _full_spec(shape):
    nd = len(shape)
    return pl.BlockSpec(shape, lambda i: (0,) * nd)


def _hm_spec(nh, tm, d):
    return pl.BlockSpec((nh, tm, d), lambda i: (0, i, 0))


def _projections(x, gain, pos, wts, tm):
    rows, d = x.shape
    grid = (rows // tm,)
    wa = N_HEADS_A * HEAD_DIM_A
    cos64, sin64 = _rope_tables(pos, HEAD_DIM_A)
    cos128, sin128 = _rope_tables(pos, DK_R)
    g2 = gain.reshape(1, d)
    sds = jax.ShapeDtypeStruct
    outs_a = pl.pallas_call(
        _proj_a_kernel,
        grid=grid,
        in_specs=[_row_spec(tm, d), _full_spec((1, d)), _row_spec(tm, LANES), _row_spec(tm, LANES),
                  _full_spec(wts["wa"].shape), _full_spec(wts["wv"].shape)],
        out_specs=[_hm_spec(N_HEADS_A, tm, HEAD_DIM_A), _row_spec(tm, wa), _hm_spec(N_HEADS_A, tm, HEAD_DIM_A),
                   _hm_spec(IDX_HEADS, tm, IDX_DIM), _row_spec(tm, IDX_DIM), _row_spec(tm, IDX_DIM),
                   _row_spec(tm, wa), _hm_spec(N_HEADS_A, tm, HEAD_DIM_A), _row_spec(tm, LANES)],
        out_shape=[sds((N_HEADS_A, rows, HEAD_DIM_A), MXU_DTYPE), sds((rows, wa), F32),
                   sds((N_HEADS_A, rows, HEAD_DIM_A), MXU_DTYPE), sds((IDX_HEADS, rows, IDX_DIM), MXU_DTYPE),
                   sds((rows, IDX_DIM), F32), sds((rows, IDX_DIM), MXU_DTYPE),
                   sds((rows, wa), F32), sds((N_HEADS_A, rows, HEAD_DIM_A), MXU_DTYPE), sds((rows, LANES), F32)],
        compiler_params=_cparams(("parallel",)),
        name="proj_a",
    )(x, g2, cos64, sin64, wts["wa"], wts["wv"])
    names_a = ("qa_hm", "ka_f", "ka_hm", "qi_hm", "ki_f", "ki_b", "va_f", "va_hm", "wi_f")
    wr, wv = N_HEADS_R * DK_R, N_HEADS_R * DV_R
    outs_b = pl.pallas_call(
        _proj_b_kernel,
        grid=grid,
        in_specs=[_row_spec(tm, d), _full_spec((1, d)), _row_spec(tm, LANES), _row_spec(tm, LANES),
                  _full_spec(wts["wr"].shape), _full_spec(wts["wvg"].shape)],
        out_specs=[_row_spec(tm, wr), _row_spec(tm, wr), _row_spec(tm, wv), _row_spec(tm, wv)],
        out_shape=[sds((rows, wr), F32), sds((rows, wr), F32), sds((rows, wv), MXU_DTYPE), sds((rows, wv), F32)],
        compiler_params=_cparams(("parallel",)),
        name="proj_b",
    )(x, g2, cos128, sin128, wts["wr"], wts["wvg"])
    names_b = ("qr_f", "kr_f", "vr_b", "gr_f")
    wm = N_HEADS_M * HEAD_DIM_M
    outs_c = pl.pallas_call(
        _proj_c_kernel,
        grid=grid,
        in_specs=[_row_spec(tm, d), _full_spec((1, d)), _full_spec(wts["wc"].shape)],
        out_specs=[_row_spec(tm, wm), _row_spec(tm, 3 * d)],
        out_shape=[sds((rows, wm), MXU_DTYPE), sds((rows, 3 * d), F32)],
        compiler_params=_cparams(("parallel",)),
        name="proj_c",
    )(x, g2, wts["wc"])
    names_c = ("qm_b", "gates_f")
    out = dict(zip(names_a, outs_a))
    out.update(zip(names_b, outs_b))
    out.update(zip(names_c, outs_c))
    return out


def _prep_in_weights(w_in):
    d = w_in.shape[0]
    wa = N_HEADS_A * HEAD_DIM_A
    widths = (wa, wa, wa, IDX_HEADS * IDX_DIM, IDX_DIM, IDX_HEADS,
              N_HEADS_R * DK_R, N_HEADS_R * DK_R, N_HEADS_R * DV_R, N_HEADS_R * DV_R,
              N_HEADS_M * HEAD_DIM_M, 3 * d)
    offs = np.concatenate([[0], np.cumsum(widths)])
    seg = [w_in[:, int(offs[i]):int(offs[i + 1])] for i in range(len(widths))]
    q_a, k_a, v_a, q_i, k_i, w_i, q_r, k_r, v_r, g_r, q_m, gates = seg
    zpad = lambda n: jnp.zeros((d, n), w_in.dtype)
    return {
        "wa": _mx(jnp.concatenate([q_a, k_a, q_i, k_i, zpad(LANES - IDX_DIM)], axis=1)),
        "wv": _mx(jnp.concatenate([v_a, w_i, zpad(LANES - IDX_HEADS)], axis=1)),
        "wr": _mx(jnp.concatenate([q_r, k_r], axis=1)),
        "wvg": _mx(jnp.concatenate([v_r, g_r], axis=1)),
        "wc": _mx(jnp.concatenate([q_m, gates], axis=1)),
    }


SEL_ROWS = 64
SEL_WIDE = 512
SEL_GROUPS = 2 * LANES
INT_MAX = np.int32(2 ** 31 - 1)
MIN_NORMAL_KEY = 0x00800000
SEARCH_PERIOD = 4
SEARCH_CAP = SEARCH_PERIOD * 33


def _key_to_f32(key):
    bits = jnp.where(key >= 0, key, key ^ jnp.int32(0x7FFFFFFF))
    return pltpu.bitcast(bits, F32)


def _f32_to_key(f):
    bits = pltpu.bitcast(f, I32)
    return jnp.where(f == 0.0, 0, jnp.where(bits >= 0, bits, bits ^ jnp.int32(0x7FFFFFFF)))


def _sweep(segments, rs, init, fn):
    acc = init
    base = 0
    for ref, n_wide in segments:
        def body(c, a, ref=ref, base=base):
            off = c * SEL_WIDE
            for k in range(SEL_WIDE // LANES):
                x = ref[rs, pl.ds(pl.multiple_of(off + k * LANES, LANES), LANES)]
                a = fn(a, x, base + off + k * LANES, k)
            return a
        acc = lax.fori_loop(0, n_wide, body, acc)
        base = base + n_wide * SEL_WIDE
    return acc


def _count(segments, rows, make_pred):
    outs = []
    for g in range(rows // SEL_ROWS):
        rs = slice(g * SEL_ROWS, (g + 1) * SEL_ROWS)
        pred = make_pred(rs)
        acc = _sweep(segments, rs, jnp.zeros((SEL_ROWS, LANES), F32),
                     lambda a, x, idx0, k: a + jnp.where(pred(x, idx0), 1.0, 0.0))
        outs.append(jnp.sum(acc, axis=1, keepdims=True))
    return jnp.concatenate(outs, axis=0) if len(outs) > 1 else outs[0]


def _lanes(a, n=LANES):
    return jnp.broadcast_to(a, (a.shape[0], n))


def _search(count_fn, lo, hi, c_lo, c_hi, target, alive):
    def unfinished(lo, hi, c_lo):
        return jnp.logical_and(alive, jnp.logical_and(c_lo > target, lo + 1 < hi))

    def any_row(flag):
        return jnp.max(jnp.where(flag, 1, 0).astype(I32))

    def cond(carry):
        return jnp.logical_and(carry[0] < SEARCH_CAP, carry[1] > 0)

    def body(carry):
        it, _, phase, lo, hi, c_lo, c_hi, w_lo, w_hi, last = carry
        act = unfinished(lo, hi, c_lo)
        bis = (lo >> 1) + (hi >> 1) + (lo & hi & 1)
        width = hi - lo
        a = (c_lo - target + 0.5) * w_lo
        b = (target - 0.5 - c_hi) * w_hi
        frac = a / jnp.maximum(a + b, 1e-6)
        step = (frac * width.astype(F32)).astype(I32)
        itp = lo + jnp.clip(step, 1, jnp.maximum(width - 1, 1))
        use_itp = jnp.logical_and((lo ^ hi) >= 0, (jnp.zeros_like(lo) + phase) != SEARCH_PERIOD - 1)
        v = jnp.where(use_itp, itp, bis)
        cnt = count_fn(v)
        up = jnp.logical_and(act, cnt >= target)
        dn = jnp.logical_and(act, cnt < target)
        lo = jnp.where(up, v, lo)
        c_lo = jnp.where(up, cnt, c_lo)
        hi = jnp.where(dn, v, hi)
        c_hi = jnp.where(dn, cnt, c_hi)
        w_hi = jnp.where(up, jnp.where(last == 1, w_hi * 0.5, 1.0), jnp.where(dn, 1.0, w_hi))
        w_lo = jnp.where(dn, jnp.where(last == -1, w_lo * 0.5, 1.0), jnp.where(up, 1.0, w_lo))
        last = jnp.where(up, 1, jnp.where(dn, -1, last))
        phase = jnp.where(phase == SEARCH_PERIOD - 1, 0, phase + 1)
        return it + 1, any_row(unfinished(lo, hi, c_lo)), phase, lo, hi, c_lo, c_hi, w_lo, w_hi, last

    go = any_row(unfinished(lo, hi, c_lo))
    one = jnp.ones(lo.shape, F32)
    out = lax.while_loop(cond, body, (jnp.int32(0), go, jnp.int32(0), lo, hi, c_lo, c_hi,
                                      one, one, jnp.zeros(lo.shape, I32)))
    return out[3], out[5], out[6]


def _select_threshold(segments, rows, topk, alive=None):
    assert topk <= SEL_GROUPS
    kf = jnp.full((rows, 1), float(topk), F32)
    if alive is None:
        alive = jnp.full((rows, 1), True)
    total = sum(n for _, n in segments) * SEL_WIDE

    los, his = [], []
    for g in range(rows // SEL_ROWS):
        rs = slice(g * SEL_ROWS, (g + 1) * SEL_ROWS)
        ninf = jnp.full((SEL_ROWS, LANES), -jnp.inf, F32)
        ga, gb = _sweep(segments, rs, (ninf, ninf),
                        lambda a, x, idx0, k: ((jnp.maximum(a[0], x), a[1]) if k % 2 == 0
                                               else (a[0], jnp.maximum(a[1], x))))
        los.append(jnp.min(jnp.minimum(ga, gb), axis=1, keepdims=True))
        his.append(jnp.max(jnp.maximum(ga, gb), axis=1, keepdims=True))
    cat = lambda xs: jnp.concatenate(xs, axis=0) if len(xs) > 1 else xs[0]
    lo = _f32_to_key(jnp.maximum(cat(los), F32_LOWEST))
    hi = _f32_to_key(cat(his)) + 1

    def count_ge_f(thr):
        def make_pred(rs):
            t = _lanes(thr[rs])
            return lambda x, idx0: x >= t
        return _count(segments, rows, make_pred)

    count_ge = lambda v: count_ge_f(_key_to_f32(v))
    c_lo = count_ge(lo)
    c_hi = jnp.zeros((rows, 1), F32)
    for probe in (0, MIN_NORMAL_KEY):
        v = jnp.full((rows, 1), probe, I32)
        cnt = count_ge_f(jnp.full((rows, 1), np.int32(probe).view(np.float32), F32))
        inside = jnp.logical_and(lo < v, v < hi)
        up = jnp.logical_and(inside, cnt >= kf)
        dn = jnp.logical_and(inside, cnt < kf)
        lo, c_lo = jnp.where(up, v, lo), jnp.where(up, cnt, c_lo)
        hi, c_hi = jnp.where(dn, v, hi), jnp.where(dn, cnt, c_hi)
    at_zero = jnp.logical_and(lo == 0, hi == MIN_NORMAL_KEY)
    lo, c_lo, c_hi = _search(count_ge, lo, hi, c_lo, c_hi, kf, jnp.logical_and(alive, jnp.logical_not(at_zero)))
    tau = _key_to_f32(lo)

    need = jnp.logical_and(alive, c_lo > kf)
    drop = c_lo - kf

    def count_tail(v):
        def make_pred(rs):
            t, vv = _lanes(tau[rs]), _lanes(v[rs])
            lane = lax.broadcasted_iota(I32, (SEL_ROWS, LANES), 1)
            return lambda x, idx0: jnp.logical_and(x == t, lane >= vv - idx0)
        return _count(segments, rows, make_pred)

    zero = jnp.zeros((rows, 1), I32)
    cut, _, _ = _search(count_tail, zero, zero + total, c_lo - c_hi, jnp.zeros((rows, 1), F32), drop, need)
    return tau, jnp.where(need, cut, INT_MAX)


def _write_mask(segments_out, rows, tau, cut):
    base = 0
    for src, dst, n_wide, n_total in segments_out:
        for g in range(rows // SEL_ROWS):
            rs = slice(g * SEL_ROWS, (g + 1) * SEL_ROWS)
            t = _spread(_lanes(tau[rs]), SEL_WIDE)
            ct = _spread(_lanes(cut[rs]), SEL_WIDE)
            lane = lax.broadcasted_iota(I32, (SEL_ROWS, SEL_WIDE), 1)

            def body(c, carry, src=src, dst=dst, base=base, rs=rs, t=t, ct=ct, lane=lane):
                off = pl.multiple_of(c * SEL_WIDE, SEL_WIDE)
                x = src[rs, pl.ds(off, SEL_WIDE)]
                tie = jnp.logical_and(x == t, lane < ct - (base + off))
                sel = jnp.where(x > t, 1.0, jnp.where(tie, 1.0, 0.0))
                dst[rs, pl.ds(off, SEL_WIDE)] = sel.astype(dst.dtype)
                return carry

            lax.fori_loop(0, n_wide, body, 0)

            def zbody(c, carry, dst=dst, rs=rs):
                off = pl.multiple_of(c * SEL_WIDE, SEL_WIDE)
                dst[rs, pl.ds(off, SEL_WIDE)] = jnp.zeros((SEL_ROWS, SEL_WIDE), dst.dtype)
                return carry

            lax.fori_loop(n_wide, n_total, zbody, 0)
        base = base + n_wide * SEL_WIDE


def _index_scores(qi_ref, w, kb, transposed_keys=False):
    acc = None
    for h in range(IDX_HEADS):
        s = _dot(_mx(qi_ref[h]), kb) if transposed_keys else _dot_nt(_mx(qi_ref[h]), kb)
        t = w[:, h:h + 1] * jnp.maximum(s, 0.0)
        acc = t if acc is None else acc + t
    return acc


def _prompt_select_kernel(qi_ref, wi_ref, kidx_ref, mask_ref, i_scr, *, tq, tk, topk):
    i, j = pl.program_id(0), pl.program_id(1)
    nk = pl.num_programs(1)
    q_lo = i * tq
    n_wide = (q_lo + tq - 1) // tk + 1

    @pl.when(j < n_wide)
    def _():
        acc = _index_scores(qi_ref, wi_ref[...], kidx_ref[...])
        qpos = q_lo + lax.broadcasted_iota(I32, acc.shape, 0)
        kpos = j * tk + lax.broadcasted_iota(I32, acc.shape, 1)
        i_scr[:, pl.ds(pl.multiple_of(j * tk, tk), tk)] = jnp.where(kpos <= qpos, acc, -jnp.inf)

    @pl.when(j == nk - 1)
    def _():
        seg = [(i_scr, n_wide)]
        tau, cut = _select_threshold(seg, tq, topk)
        _write_mask([(i_scr, mask_ref, n_wide, nk)], tq, tau, cut)


def _prompt_select(qi_hm, wi_f, ki_b, topk, tq, tk):
    s = ki_b.shape[0]
    assert tk == SEL_WIDE and s % tk == 0 and s % tq == 0 and tq % SEL_ROWS == 0
    nq, nk = s // tq, s // tk
    kmap = lambda i, j: (jnp.minimum(j, (i * tq + tq - 1) // tk), 0)
    return pl.pallas_call(
        functools.partial(_prompt_select_kernel, tq=tq, tk=tk, topk=topk),
        grid=(nq, nk),
        in_specs=[pl.BlockSpec((IDX_HEADS, tq, IDX_DIM), lambda i, j: (0, i, 0)),
                  pl.BlockSpec((tq, LANES), lambda i, j: (i, 0)),
                  pl.BlockSpec((tk, IDX_DIM), kmap)],
        out_specs=pl.BlockSpec((tq, s), lambda i, j: (i, 0)),
        out_shape=jax.ShapeDtypeStruct((s, s), MXU_DTYPE),
        scratch_shapes=[pltpu.VMEM((tq, s), F32)],
        compiler_params=_cparams(("parallel", "arbitrary")),
        name="prompt_select",
    )(qi_hm, wi_f, ki_b)


def _spread(a, n):
    if n <= LANES:
        return a[:, :n]
    return jnp.concatenate([a] * (n // LANES), axis=1)


def _flash_update(s, v, m_prev, l_prev, acc_prev):
    m_new = jnp.maximum(m_prev, jnp.max(s, axis=1, keepdims=True))
    alpha = jnp.exp(m_prev - m_new)
    p = jnp.exp(s - _spread(m_new, s.shape[1]))
    l_new = alpha * l_prev + jnp.sum(p, axis=1, keepdims=True)
    acc_new = _spread(alpha, acc_prev.shape[1]) * acc_prev + _dot(_mx(p), v)
    return m_new, l_new, acc_new


ATT_ROWS = 256


def _prompt_attend_kernel(q_ref, k_ref, v_ref, mask_ref, o_ref, m_scr, l_scr, acc_scr, *, tq, tk):
    i, j = pl.program_id(0), pl.program_id(1)
    nk = pl.num_programs(1)

    @pl.when(j == 0)
    def _():
        m_scr[...] = jnp.full(m_scr.shape, NEG_BIG, F32)
        l_scr[...] = jnp.zeros(l_scr.shape, F32)
        acc_scr[...] = jnp.zeros(acc_scr.shape, F32)

    @pl.when(j * tk <= i * tq + tq - 1)
    def _():
        for r in range(tq // ATT_ROWS):
            rs = slice(r * ATT_ROWS, (r + 1) * ATT_ROWS)
            keep = mask_ref[rs, :] > 0
            for h in range(N_HEADS_A):
                s = jnp.where(keep, _dot_nt(q_ref[h, rs, :], k_ref[h]), NEG_BIG)
                m_scr[h, rs, :], l_scr[h, rs, :], acc_scr[h, rs, :] = _flash_update(
                    s, v_ref[h], m_scr[h, rs, :], l_scr[h, rs, :], acc_scr[h, rs, :])

    @pl.when(j == nk - 1)
    def _():
        for h in range(N_HEADS_A):
            o = acc_scr[h] / l_scr[h][:, :HEAD_DIM_A]
            o_ref[:, h * HEAD_DIM_A:(h + 1) * HEAD_DIM_A] = o.astype(o_ref.dtype)


def _prompt_attend(qa_hm, ka_hm, va_hm, mask, tq, tk):
    nh, s, dh = qa_hm.shape
    nq, nk = s // tq, s // tk
    diag = lambda i, j: jnp.minimum(j, (i * tq + tq - 1) // tk)
    return pl.pallas_call(
        functools.partial(_prompt_attend_kernel, tq=tq, tk=tk),
        grid=(nq, nk),
        in_specs=[pl.BlockSpec((nh, tq, dh), lambda i, j: (0, i, 0)),
                  pl.BlockSpec((nh, tk, dh), lambda i, j: (0, diag(i, j), 0)),
                  pl.BlockSpec((nh, tk, dh), lambda i, j: (0, diag(i, j), 0)),
                  pl.BlockSpec((tq, tk), lambda i, j: (i, diag(i, j)))],
        out_specs=pl.BlockSpec((tq, nh * dh), lambda i, j: (i, 0)),
        out_shape=jax.ShapeDtypeStruct((s, nh * dh), MXU_DTYPE),
        scratch_shapes=[pltpu.VMEM((nh, tq, LANES), F32), pltpu.VMEM((nh, tq, LANES), F32),
                        pltpu.VMEM((nh, tq, dh), F32)],
        compiler_params=_cparams(("parallel", "arbitrary")),
        name="prompt_attend",
    )(qa_hm, ka_hm, va_hm, mask)


PAGES_PER_STEP = 8


def _sample_index_kernel(pt_ref, qi_ref, wi_ref, *refs, page):
    del pt_ref
    pages, out_ref = refs[:-1], refs[-1]
    w = wi_ref[0]
    for p, kref in enumerate(pages):
        out_ref[0, :, p * page:(p + 1) * page] = _index_scores(qi_ref, w, _mx(kref[0]), transposed_keys=True)


def _sample_index(page_table, qi_hm, wi_f, cache_kidx_t):
    db, n_pages = page_table.shape
    _, idim, page = cache_kidx_t.shape
    pps = math.gcd(PAGES_PER_STEP, n_pages)
    nsteps = n_pages // pps
    pt = page_table.reshape(-1).astype(I32)

    def kspec(p):
        return pl.BlockSpec((1, idim, page), lambda b, j, pt: (pt[b * n_pages + j * pps + p], 0, 0))

    grid_spec = pltpu.PrefetchScalarGridSpec(
        num_scalar_prefetch=1,
        grid=(db, nsteps),
        in_specs=[pl.BlockSpec((IDX_HEADS, T_PAD, idim), lambda b, j, pt: (0, b, 0)),
                  pl.BlockSpec((1, T_PAD, LANES), lambda b, j, pt: (b, 0, 0))]
                 + [kspec(p) for p in range(pps)],
        out_specs=pl.BlockSpec((1, T_PAD, pps * page), lambda b, j, pt: (b, 0, j)),
    )
    return pl.pallas_call(
        functools.partial(_sample_index_kernel, page=page),
        grid_spec=grid_spec,
        out_shape=jax.ShapeDtypeStruct((db, T_PAD, n_pages * page), F32),
        compiler_params=_cparams(("parallel", "arbitrary")),
        name="sample_index",
    )(pt, qi_hm, wi_f.reshape(db, T_PAD, LANES), *([cache_kidx_t] * pps))


def _sample_select_kernel(ipast_ref, qi_ref, wi_ref, kin_ref, mpast_ref, mnew_ref, inew_scr, *, t_real, topk):
    rows = ipast_ref.shape[0]
    acc = _index_scores(qi_ref, wi_ref[...], kin_ref[...])
    r = lax.broadcasted_iota(I32, acc.shape, 0)
    c = lax.broadcasted_iota(I32, acc.shape, 1)
    same = (r // T_PAD) == (c // T_PAD)
    tq, tc = r % T_PAD, c % T_PAD
    ok = jnp.logical_and(same, jnp.logical_and(tc <= tq, tc < t_real))
    inew_scr[...] = jnp.full(inew_scr.shape, -jnp.inf, F32)
    inew_scr[:, :rows] = jnp.where(ok, acc, -jnp.inf)
    n_past = ipast_ref.shape[1] // SEL_WIDE
    n_new = inew_scr.shape[1] // SEL_WIDE
    alive = lax.rem(lax.broadcasted_iota(I32, (rows, 1), 0), T_PAD) < t_real
    tau, cut = _select_threshold([(ipast_ref, n_past), (inew_scr, n_new)], rows, topk, alive)
    _write_mask([(ipast_ref, mpast_ref, n_past, n_past), (inew_scr, mnew_ref, n_new, n_new)], rows, tau, cut)


def _sample_select(i_past, qi_hm, wi_f, ki_b, t_real, topk):
    rows, past = i_past.shape
    assert past % SEL_WIDE == 0 and rows % SEL_ROWS == 0
    wnew = -(-rows // SEL_WIDE) * SEL_WIDE
    return pl.pallas_call(
        functools.partial(_sample_select_kernel, t_real=t_real, topk=topk),
        out_shape=[jax.ShapeDtypeStruct((rows, past), F32), jax.ShapeDtypeStruct((rows, wnew), F32)],
        scratch_shapes=[pltpu.VMEM((rows, wnew), F32)],
        compiler_params=pltpu.CompilerParams(vmem_limit_bytes=VMEM_LIMIT),
        name="sample_select",
    )(i_past, qi_hm, wi_f, ki_b)


def _sample_attend_kernel(pt_ref, q_ref, mp_ref, mn_ref, kn_ref, vn_ref, *refs, page, pps):
    del pt_ref
    kpages, vpages = refs[:pps], refs[pps:2 * pps]
    o_ref, m_scr, l_scr, acc_scr = refs[2 * pps:]
    j = pl.program_id(1)
    nj = pl.num_programs(1)
    q = q_ref[0]
    nh = q.shape[0] // T_PAD

    def scores(m_t, kt):
        keep = jnp.concatenate([m_t] * nh, axis=0) > 0
        return jnp.where(keep, _dot(q, kt), NEG_BIG)

    def update(s_list, vt_list):
        m_prev = m_scr[...]
        m_blk = functools.reduce(jnp.maximum, [jnp.max(s, axis=1, keepdims=True) for s in s_list])
        m_new = jnp.maximum(m_prev, m_blk)
        alpha = jnp.exp(m_prev - m_new)
        l_new = alpha * l_scr[...]
        acc = _spread(alpha, acc_scr.shape[1]) * acc_scr[...]
        for s, vt in zip(s_list, vt_list):
            p = jnp.exp(s - _spread(m_new, s.shape[1]))
            l_new = l_new + jnp.sum(p, axis=1, keepdims=True)
            acc = acc + _dot_nt(_mx(p), vt)
        m_scr[...], l_scr[...], acc_scr[...] = m_new, l_new, acc

    @pl.when(j == 0)
    def _():
        m_scr[...] = jnp.full(m_scr.shape, NEG_BIG, F32)
        l_scr[...] = jnp.zeros(l_scr.shape, F32)
        acc_scr[...] = jnp.zeros(acc_scr.shape, F32)

    update([scores(mp_ref[0, :, p * page:(p + 1) * page], _mx(kpages[p][0])) for p in range(pps)],
           [_mx(vpages[p][0]) for p in range(pps)])

    @pl.when(j == nj - 1)
    def _():
        update([scores(mn_ref[0], kn_ref[0])], [vn_ref[0]])
        full = acc_scr[...] / _spread(l_scr[...], acc_scr.shape[1])
        lane = lax.broadcasted_iota(I32, (T_PAD, full.shape[1]), 1)
        out = jnp.zeros((T_PAD, full.shape[1]), F32)
        for h in range(nh):
            out = out + jnp.where((lane // HEAD_DIM_A) == h, full[h * T_PAD:(h + 1) * T_PAD], 0.0)
        o_ref[0] = out


def _sample_attend(page_table, q_bd, m_past, m_new, kt_new, vt_new, cache_kt, cache_vt):
    db, n_pages = page_table.shape
    _, hd, page = cache_kt.shape
    pps = math.gcd(PAGES_PER_STEP, n_pages)
    nsteps = n_pages // pps
    nnew = kt_new.shape[2]
    nq = q_bd.shape[1]
    pt = page_table.reshape(-1).astype(I32)

    def pspec(p):
        return pl.BlockSpec((1, hd, page), lambda b, j, pt: (pt[b * n_pages + j * pps + p], 0, 0))

    bspec = lambda shape: pl.BlockSpec((1,) + shape, lambda b, j, pt: (b, 0, 0))
    grid_spec = pltpu.PrefetchScalarGridSpec(
        num_scalar_prefetch=1,
        grid=(db, nsteps),
        in_specs=[bspec((nq, hd)),
                  pl.BlockSpec((1, T_PAD, pps * page), lambda b, j, pt: (b, 0, j)),
                  bspec((T_PAD, nnew)), bspec((hd, nnew)), bspec((hd, nnew))]
                 + [pspec(p) for p in range(pps)] * 2,
        out_specs=bspec((T_PAD, hd)),
        scratch_shapes=[pltpu.VMEM((nq, LANES), F32), pltpu.VMEM((nq, LANES), F32), pltpu.VMEM((nq, hd), F32)],
    )
    return pl.pallas_call(
        functools.partial(_sample_attend_kernel, page=page, pps=pps),
        grid_spec=grid_spec,
        out_shape=jax.ShapeDtypeStruct((db, T_PAD, hd), F32),
        compiler_params=_cparams(("parallel", "arbitrary")),
        name="sample_attend",
    )(pt, q_bd, m_past, m_new, kt_new, vt_new, *([cache_kt] * pps), *([cache_vt] * pps))


def _retention_tables(c_real, c_pad):
    h = np.arange(N_HEADS_R, dtype=np.float64)
    log_g = np.log1p(-np.exp2(-5.0 - h))
    i = np.arange(c_pad, dtype=np.float64)
    diff = i[:, None] - i[None, :]
    live = (diff >= 0) & (i[:, None] < c_real) & (i[None, :] < c_real)
    inner = np.where(live[None], np.exp(np.maximum(diff, 0.0)[None] * log_g[:, None, None]), 0.0)
    q_dec = np.exp((i + 1.0)[None, :] * log_g[:, None])
    k_dec = np.where(i[None, :] < c_real, np.exp((c_real - 1.0 - i)[None, :] * log_g[:, None]), 0.0)
    c_dec = np.exp(c_real * log_g)
    f = lambda a: jnp.asarray(a, F32)
    return f(inner), f(q_dec[:, :, None]), f(k_dec[:, :, None]), [float(v) for v in c_dec]


def _retention_kernel(q_ref, k_ref, v_ref, g_ref, s0_ref, inner_ref, qdec_ref, kdec_ref,
                      o_ref, s_out_ref, s_scr, *, c_dec):
    j = pl.program_id(1)
    nj = pl.num_programs(1)

    @pl.when(j == 0)
    def _():
        s_scr[...] = s0_ref[0]

    for h in range(N_HEADS_R):
        q = q_ref[:, h * DK_R:(h + 1) * DK_R]
        k = k_ref[:, h * DK_R:(h + 1) * DK_R]
        v = v_ref[:, h * DV_R:(h + 1) * DV_R]
        s_prev = s_scr[h]
        a = _dot_nt(_mx(q), _mx(k)) * inner_ref[h]
        o = _dot(_mx(a), v) + _dot(_mx(q), _mx(s_prev)) * qdec_ref[h]
        kd = k * kdec_ref[h]
        s_scr[h] = s_prev * c_dec[h] + _dot(_mx(kd.T), v)
        mu = jnp.mean(o, axis=-1, keepdims=True)
        var = jnp.mean(jnp.square(o - mu), axis=-1, keepdims=True)
        gn = (o - mu) * lax.rsqrt(var + EPS)
        g = g_ref[:, h * DV_R:(h + 1) * DV_R]
        o_ref[:, h * DV_R:(h + 1) * DV_R] = (gn * (g * _sigmoid(g))).astype(o_ref.dtype)

    @pl.when(j == nj - 1)
    def _():
        s_out_ref[0] = s_scr[...]


def _retention(qr, kr, vr, gr, s0, c_real):
    b = s0.shape[0]
    c = RET_CHUNK
    n = qr.shape[0] // (b * c)
    inner, qdec, kdec, c_dec = _retention_tables(c_real, c)
    wr, wv = N_HEADS_R * DK_R, N_HEADS_R * DV_R
    rmap = lambda bi, j: (bi * n + j, 0)
    full3 = lambda shape: pl.BlockSpec(shape, lambda bi, j: (0, 0, 0))
    return pl.pallas_call(
        functools.partial(_retention_kernel, c_dec=c_dec),
        grid=(b, n),
        in_specs=[pl.BlockSpec((c, wr), rmap), pl.BlockSpec((c, wr), rmap), pl.BlockSpec((c, wv), rmap),
                  pl.BlockSpec((c, wv), rmap),
                  pl.BlockSpec((1, N_HEADS_R, DK_R, DV_R), lambda bi, j: (bi, 0, 0, 0)),
                  full3(inner.shape), full3(qdec.shape), full3(kdec.shape)],
        out_specs=[pl.BlockSpec((c, wv), rmap),
                   pl.BlockSpec((1, N_HEADS_R, DK_R, DV_R), lambda bi, j: (bi, 0, 0, 0))],
        out_shape=[jax.ShapeDtypeStruct((b * n * c, wv), MXU_DTYPE),
                   jax.ShapeDtypeStruct((b, N_HEADS_R, DK_R, DV_R), F32)],
        scratch_shapes=[pltpu.VMEM((N_HEADS_R, DK_R, DV_R), F32)],
        compiler_params=_cparams(("parallel", "arbitrary")),
        name="retention",
    )(qr, kr, vr, gr, s0, inner, qdec, kdec)


def _cross_kernel(q_ref, mk_ref, mv_ref, o_ref):
    scale = HEAD_DIM_M ** -0.5
    for h in range(N_HEADS_M):
        sl = slice(h * HEAD_DIM_M, (h + 1) * HEAD_DIM_M)
        s = _dot_nt(_mx(q_ref[:, sl]), _mx(mk_ref[0, :, sl])) * scale
        p = jnp.exp(s - jnp.max(s, axis=1, keepdims=True))
        p = p / jnp.sum(p, axis=1, keepdims=True)
        o_ref[:, sl] = _dot(_mx(p), _mx(mv_ref[0, :, sl])).astype(o_ref.dtype)


def _cross_attend(qm, mk, mv, tm, out_dtype):
    b, n_mem, hd = mk.shape
    nt = qm.shape[0] // (b * tm)
    return pl.pallas_call(
        _cross_kernel,
        grid=(b, nt),
        in_specs=[pl.BlockSpec((tm, hd), lambda bi, i: (bi * nt + i, 0)),
                  pl.BlockSpec((1, n_mem, hd), lambda bi, i: (bi, 0, 0)),
                  pl.BlockSpec((1, n_mem, hd), lambda bi, i: (bi, 0, 0))],
        out_specs=pl.BlockSpec((tm, hd), lambda bi, i: (bi * nt + i, 0)),
        out_shape=jax.ShapeDtypeStruct(qm.shape, out_dtype),
        compiler_params=_cparams(("parallel", "parallel")),
        name="cross_attend",
    )(qm, mk, mv)


def _memkv_kernel(x_ref, g_ref, w_ref, o_ref):
    o_ref[...] = _dot(_mx(_rms(x_ref[...], g_ref[...])), w_ref[...])


def _memory_kv(mem, gain, w):
    rows, d = mem.shape
    return pl.pallas_call(
        _memkv_kernel,
        out_shape=jax.ShapeDtypeStruct((rows, w.shape[1]), F32),
        compiler_params=pltpu.CompilerParams(vmem_limit_bytes=VMEM_LIMIT),
        name="memory_kv",
    )(mem, gain.reshape(1, d), _mx(w))


def _merge_kernel(x_ref, oa_ref, or_ref, om_ref, gates_ref, wpa_ref, wpb_ref, wpc_ref, wo_ref, g_ref, h_ref):
    d = x_ref.shape[1]
    gt = gates_ref[...]
    mixed = (_sigmoid(gt[:, :d]) * _dot(_mx(oa_ref[...]), wpa_ref[...])
             + _sigmoid(gt[:, d:2 * d]) * _dot(_mx(or_ref[...]), wpb_ref[...])
             + _sigmoid(gt[:, 2 * d:]) * _dot(_mx(om_ref[...]), wpc_ref[...]))
    z = _dot(_mx(mixed), wo_ref[...])
    h_ref[...] = x_ref[...] + _rms(z, g_ref[...])


def _merge(x, oa, o_r, om, gates, wpa, wpb, wpc, wo, gain, tm):
    rows, d = x.shape
    return pl.pallas_call(
        _merge_kernel,
        grid=(rows // tm,),
        in_specs=[_row_spec(tm, d), _row_spec(tm, oa.shape[1]), _row_spec(tm, o_r.shape[1]),
                  _row_spec(tm, om.shape[1]), _row_spec(tm, 3 * d),
                  _full_spec(wpa.shape), _full_spec(wpb.shape), _full_spec(wpc.shape), _full_spec(wo.shape),
                  _full_spec((1, d))],
        out_specs=_row_spec(tm, d),
        out_shape=jax.ShapeDtypeStruct((rows, d), F32),
        compiler_params=_cparams(("parallel",)),
        name="merge",
    )(x, oa, o_r, om, gates, wpa, wpb, wpc, wo, gain.reshape(1, d))


HALO = BF16_ROWS


def _ffn_kernel(h_ref, halo_ref, s0_ref, s1_ref, g1_ref, g2_ref, wu_ref, wg_ref, cw_ref, cb_ref, wd_ref,
                y_ref, utail_ref, x_scr, u_scr, *, tm, seq, keep):
    i = pl.program_id(0)
    h = h_ref[...]
    hn = _rms(h, g1_ref[...])
    x_scr[HALO:, :] = _mx(hn)
    x_scr[:HALO, :] = _mx(_rms(halo_ref[...], g1_ref[...]))
    xc = x_scr[...]
    u_scr[...] = _dot(xc, wu_ref[...])
    gate = _dot(xc[HALO:], wg_ref[...])
    cur = u_scr[HALO:, :]
    prev1 = u_scr[HALO - 1:HALO - 1 + tm, :]
    prev2 = u_scr[HALO - 2:HALO - 2 + tm, :]
    seq_loc = min(seq, tm)
    t = lax.rem(lax.broadcasted_iota(I32, (tm, 1), 0), seq_loc)
    t = jnp.where(lax.rem(i * tm, seq) == 0, t, CONV_W)
    st0, st1 = s0_ref[...], s1_ref[...]
    if st0.shape[0] != tm:
        st0, st1 = st0[0:1], st1[0:1]
    prev1 = jnp.where(t == 0, st1, prev1)
    prev2 = jnp.where(t == 0, st0, jnp.where(t == 1, st1, prev2))
    c = cb_ref[...] + prev2 * cw_ref[0:1, :] + prev1 * cw_ref[1:2, :] + cur * cw_ref[2:3, :]
    act = jax.nn.gelu(c, approximate=True) * gate
    ff = _dot(_mx(act), wd_ref[...])
    y_ref[...] = h + _rms(ff, g2_ref[...])
    utail_ref[...] = u_scr[HALO + tm - keep:, :]


def _conv_ffn(h, s0e, s1e, g1, g2, wu, wg, cw, cb, wd, tm, seq, keep):
    rows, d = h.shape
    f = wu.shape[1]
    nt = rows // tm
    hb = tm // HALO
    sr = s0e.shape[0]
    return pl.pallas_call(
        functools.partial(_ffn_kernel, tm=tm, seq=seq, keep=keep),
        grid=(nt,),
        in_specs=[_row_spec(tm, d),
                  pl.BlockSpec((HALO, d), lambda i: (jnp.maximum(i * hb - 1, 0), 0)),
                  _full_spec((sr, f)), _full_spec((sr, f)),
                  _full_spec((1, d)), _full_spec((1, d)),
                  _full_spec(wu.shape), _full_spec(wg.shape), _full_spec(cw.shape), _full_spec((1, f)),
                  _full_spec(wd.shape)],
        out_specs=[_row_spec(tm, d), _row_spec(keep, f)],
        out_shape=[jax.ShapeDtypeStruct((rows, d), F32), jax.ShapeDtypeStruct((nt * keep, f), F32)],
        scratch_shapes=[pltpu.VMEM((tm + HALO, d), MXU_DTYPE), pltpu.VMEM((tm + HALO, f), F32)],
        compiler_params=_cparams(("parallel",)),
        name="conv_ffn",
    )(h, h, s0e, s1e, g1.reshape(1, d), g2.reshape(1, d), wu, wg, cw, cb.reshape(1, f), wd)


def _layer_weights(l, w_in, w_proj_a, w_proj_b, w_proj_c, w_out, w_up, w_down):
    wts = _prep_in_weights(w_in[l])
    f = w_down.shape[1]
    wts.update(wpa=_mx(w_proj_a[l]), wpb=_mx(w_proj_b[l]), wpc=_mx(w_proj_c[l]), wo=_mx(w_out[l]),
               wu=_mx(w_up[l][:, :f]), wg=_mx(w_up[l][:, f:]), wd=_mx(w_down[l]))
    return wts


def _prompt_layer(x, mem, wts, norms, conv_w, conv_b, w_mem_kv, tiles):
    s, d = x.shape
    f = wts["wd"].shape[0]
    pos = jnp.arange(s)
    pr = _projections(x, norms["pre_mix"], pos, wts, tiles["proj"])
    topk = min(TOPK_MAX, s // 4)
    mask = _prompt_select(pr["qi_hm"], pr["wi_f"], pr["ki_b"], topk, tiles["sel_q"], SEL_WIDE)
    o_a = _prompt_attend(pr["qa_hm"], pr["ka_hm"], pr["va_hm"], mask, tiles["att_q"], tiles["att_k"])
    s0 = jnp.zeros((1, N_HEADS_R, DK_R, DV_R), F32)
    o_r, ret_new = _retention(pr["qr_f"], pr["kr_f"], pr["vr_b"], pr["gr_f"], s0, RET_CHUNK)
    kv = _memory_kv(mem, norms["mem"], w_mem_kv)
    wm = N_HEADS_M * HEAD_DIM_M
    mk, mv = kv[:, :wm], kv[:, wm:]
    o_m = _cross_attend(pr["qm_b"], mk[None], mv[None], tiles["cross"], MXU_DTYPE)
    h = _merge(x, o_a, o_r, o_m, pr["gates_f"], wts["wpa"], wts["wpb"], wts["wpc"], wts["wo"],
               norms["post_mix"], tiles["merge"])
    zst = jnp.zeros((SUBLANES, f), F32)
    y, utail = _conv_ffn(h, zst, zst, norms["pre_ffn"], norms["post_ffn"], wts["wu"], wts["wg"], conv_w, conv_b,
                         wts["wd"], tiles["ffn"], s, SUBLANES)
    conv_new = utail[-(CONV_W - 1):]
    return y, pr["ka_f"], pr["va_f"], pr["ki_f"], ret_new, conv_new, mk, mv


def _sample_layer(x, wts, norms, conv_w, conv_b, cache_k, cache_v, cache_kidx, mem_k, mem_v,
                  state_ret, state_conv, page_table):
    db, t, d = x.shape
    f = wts["wd"].shape[0]
    n_pages = page_table.shape[1]
    page = cache_k.shape[1]
    past = n_pages * page
    rows = db * T_PAD
    xp = jnp.pad(x, ((0, 0), (0, T_PAD - t), (0, 0))).reshape(rows, d)
    pos = jnp.tile(past + jnp.arange(T_PAD), db)
    pr = _projections(xp, norms["pre_mix"], pos, wts, rows)
    hd = N_HEADS_A * HEAD_DIM_A

    topk = min(TOPK_MAX, (past + t) // 4)
    i_past = _sample_index(page_table, pr["qi_hm"].astype(F32), pr["wi_f"], cache_kidx.transpose(0, 2, 1)).reshape(rows, past)
    m_past, m_new = _sample_select(i_past, pr["qi_hm"], pr["wi_f"], pr["ki_b"], t, topk)
    nnew = m_new.shape[1]
    own = m_new[:, :rows].reshape(db, T_PAD, db, T_PAD)[jnp.arange(db), :, jnp.arange(db), :]
    m_new_own = jnp.pad(own, ((0, 0), (0, 0), (0, LANES - T_PAD)))
    q_rows = pr["qa_hm"].reshape(N_HEADS_A, db, T_PAD, HEAD_DIM_A)
    eye = jnp.eye(N_HEADS_A, dtype=MXU_DTYPE)
    q_bd = jnp.einsum("hbtd,hg->bhtgd", q_rows, eye).reshape(db, N_HEADS_A * T_PAD, hd)
    new_t = lambda a: jnp.pad(_mx(a).reshape(db, T_PAD, hd).transpose(0, 2, 1), ((0, 0), (0, 0), (0, LANES - T_PAD)))
    paged_t = lambda c: c.transpose(0, 2, 3, 1).reshape(c.shape[0], hd, page)
    o_a = _sample_attend(page_table, q_bd, m_past.reshape(db, T_PAD, past), m_new_own,
                         new_t(pr["ka_f"]), new_t(pr["va_f"]), paged_t(cache_k), paged_t(cache_v)).reshape(rows, hd)

    padc = lambda a: jnp.pad(a.reshape(db, T_PAD, -1), ((0, 0), (0, RET_CHUNK - T_PAD), (0, 0))).reshape(db * RET_CHUNK, -1)
    o_r, ret_new = _retention(padc(pr["qr_f"]), padc(pr["kr_f"]), padc(pr["vr_b"]), padc(pr["gr_f"]), state_ret, t)
    o_r = o_r.reshape(db, RET_CHUNK, -1)[:, :T_PAD].reshape(rows, -1)

    wm = N_HEADS_M * HEAD_DIM_M
    o_m = _cross_attend(pr["qm_b"].astype(F32), mem_k.reshape(db, -1, wm), mem_v.reshape(db, -1, wm), T_PAD, F32)

    h = _merge(xp, o_a, o_r, o_m, pr["gates_f"], wts["wpa"], wts["wpb"], wts["wpc"], wts["wo"],
               norms["post_mix"], rows)
    s0e = jnp.repeat(state_conv[:, 0], T_PAD, axis=0)
    s1e = jnp.repeat(state_conv[:, 1], T_PAD, axis=0)
    y, u_all = _conv_ffn(h, s0e, s1e, norms["pre_ffn"], norms["post_ffn"], wts["wu"], wts["wg"], conv_w, conv_b,
                         wts["wd"], rows, T_PAD, rows)
    ext = jnp.concatenate([state_conv.astype(F32), u_all.reshape(db, T_PAD, f)[:, :t]], axis=1)
    conv_new = ext[:, t:]
    unpad = lambda a: a.reshape(db, T_PAD, -1)[:, :t]
    return (unpad(y), unpad(pr["ka_f"]), unpad(pr["va_f"]), unpad(pr["ki_f"]), ret_new, conv_new)


PROMPT_TILES = dict(proj=512, sel_q=256, att_q=256, att_k=512, cross=512, merge=512, ffn=256)


def kernel(x_prompt, x_sample, cache_k, cache_v, cache_kidx, cache_mem_k, cache_mem_v, state_ret, state_conv,
           page_table, mem_prompt, norm_pre_mix, norm_post_mix, norm_pre_ffn, norm_post_ffn, norm_mem,
           w_in, w_mem_kv, w_proj_a, w_proj_b, w_proj_c, w_out, w_up, conv_w, conv_b, w_down):
    bp, s, d = x_prompt.shape
    db, t, _ = x_sample.shape
    depth = w_in.shape[0]
    assert bp == 1 and t <= T_PAD and CONV_W - 1 <= t
    tiles = {k: min(v, s) for k, v in PROMPT_TILES.items()}
    yp, ys = x_prompt[0], x_sample
    outs = [[] for _ in range(12)]
    for l in range(depth):
        wts = _layer_weights(l, w_in, w_proj_a, w_proj_b, w_proj_c, w_out, w_up, w_down)
        norms = dict(pre_mix=norm_pre_mix[l], post_mix=norm_post_mix[l], pre_ffn=norm_pre_ffn[l],
                     post_ffn=norm_post_ffn[l], mem=norm_mem[l])
        yp, kp, vp, kip, rp, cp, mk, mv = _prompt_layer(yp, mem_prompt[0], wts, norms, conv_w[l], conv_b[l],
                                                        w_mem_kv[l], tiles)
        ys, ks, vs, kis, rs, cs = _sample_layer(ys, wts, norms, conv_w[l], conv_b[l], cache_k[l], cache_v[l],
                                                cache_kidx[l], cache_mem_k[l], cache_mem_v[l], state_ret[l],
                                                state_conv[l], page_table)
        n_mem = mk.shape[0]
        vals = (kp.reshape(1, s, N_HEADS_A, HEAD_DIM_A), vp.reshape(1, s, N_HEADS_A, HEAD_DIM_A),
                kip.reshape(1, s, IDX_DIM), rp, cp[None],
                mk.reshape(1, n_mem, N_HEADS_M, HEAD_DIM_M), mv.reshape(1, n_mem, N_HEADS_M, HEAD_DIM_M),
                ks.reshape(db, t, N_HEADS_A, HEAD_DIM_A), vs.reshape(db, t, N_HEADS_A, HEAD_DIM_A),
                kis, rs, cs)
        for o, v in zip(outs, vals):
            o.append(v)
    stacked = [jnp.stack(o) for o in outs]
    return (yp[None], ys, *stacked)
```

```python
import functools
import math

import numpy as np
import jax
import jax.numpy as jnp
from jax import lax
from jax.experimental import pallas as pl
from jax.experimental.pallas import tpu as pltpu

F32 = jnp.float32
I32 = jnp.int32
MXU_DTYPE = jnp.bfloat16

N_HEADS_A, HEAD_DIM_A = 8, 64
IDX_HEADS, IDX_DIM = 4, 64
TOPK_MAX = 256
N_HEADS_R, DK_R, DV_R = 4, 128, 256
RET_CHUNK = 128
N_HEADS_M, HEAD_DIM_M = 4, 128
CONV_W = 3
ROPE_THETA = 10000.0
EPS = 1e-6

LANES = 128
SUBLANES = 8
BF16_ROWS = 16
VMEM_LIMIT = 56 * 1024 * 1024
NEG_BIG = -1e30
F32_LOWEST = float(np.finfo(np.float32).min)
T_PAD = 8


def _cparams(sem):
    return pltpu.CompilerParams(dimension_semantics=sem, vmem_limit_bytes=VMEM_LIMIT)


def _dot(a, b):
    return jnp.dot(a, b, preferred_element_type=F32)


def _dot_nt(a, b):
    return lax.dot_general(a, b, (((1,), (1,)), ((), ())), preferred_element_type=F32)


def _mx(a):
    return a.astype(MXU_DTYPE)


def _rms(x, g):
    return x * lax.rsqrt(jnp.mean(x * x, axis=-1, keepdims=True) + EPS) * g


def _sigmoid(x):
    return 1.0 / (1.0 + jnp.exp(-x))


def _rope_tables(pos, d):
    half = d // 2
    inv = 1.0 / (ROPE_THETA ** (jnp.arange(half, dtype=F32) * 2.0 / d))
    ang = pos.astype(F32)[:, None] * inv[None, :]
    cos, sin = jnp.cos(ang), jnp.sin(ang)
    reps = LANES // d
    cos_t = jnp.tile(jnp.concatenate([cos, cos], axis=1), (1, reps))
    sin_t = jnp.tile(jnp.concatenate([-sin, sin], axis=1), (1, reps))
    return cos_t, sin_t


def _rope(y, cos, sin, d):
    w = y.shape[1]
    half = d // 2
    reps = w // LANES
    c = jnp.concatenate([cos] * reps, axis=1) if reps > 1 else cos
    s = jnp.concatenate([sin] * reps, axis=1) if reps > 1 else sin
    lane = lax.broadcasted_iota(I32, y.shape, 1)
    first = (lane & (d - 1)) < half
    rot = jnp.where(first, pltpu.roll(y, w - half, 1), pltpu.roll(y, half, 1))
    return y * c + rot * s


def _proj_a_kernel(x_ref, g_ref, cos_ref, sin_ref, wa_ref, wv_ref,
                   qa_hm, qa_t, ka_f, ka_hm, qi_hm, ki_f, ki_b, va_f, va_t, wi_f):
    xn = _mx(_rms(x_ref[...], g_ref[...]))
    wa = N_HEADS_A * HEAD_DIM_A
    y = _rope(_dot(xn, wa_ref[...]), cos_ref[...], sin_ref[...], HEAD_DIM_A)
    q = y[:, :wa] * (HEAD_DIM_A ** -0.5)
    k = y[:, wa:2 * wa]
    ka_f[...] = k
    qt = q.T
    for h in range(N_HEADS_A):
        sl = slice(h * HEAD_DIM_A, (h + 1) * HEAD_DIM_A)
        qa_hm[h] = _mx(q[:, sl])
        qa_t[h] = _mx(qt[sl, :])
        ka_hm[h] = _mx(k[:, sl])
    qi = y[:, 2 * wa:2 * wa + IDX_HEADS * IDX_DIM]
    for h in range(IDX_HEADS):
        qi_hm[h] = _mx(qi[:, h * IDX_DIM:(h + 1) * IDX_DIM])
    ki = y[:, 2 * wa + IDX_HEADS * IDX_DIM:2 * wa + IDX_HEADS * IDX_DIM + IDX_DIM]
    ki_f[...] = ki
    ki_b[...] = _mx(ki)
    z = _dot(xn, wv_ref[...])
    v = z[:, :wa]
    va_f[...] = v
    vt = v.T
    for h in range(N_HEADS_A):
        va_t[h] = _mx(vt[h * HEAD_DIM_A:(h + 1) * HEAD_DIM_A, :])
    wi_f[...] = z[:, wa:wa + LANES]


def _proj_b_kernel(x_ref, g_ref, cos_ref, sin_ref, wr_ref, wvg_ref, qr_f, kr_f, vr_b, gr_f):
    xn = _mx(_rms(x_ref[...], g_ref[...]))
    wr = N_HEADS_R * DK_R
    y = _rope(_dot(xn, wr_ref[...]), cos_ref[...], sin_ref[...], DK_R)
    qr_f[...] = y[:, :wr]
    kr_f[...] = y[:, wr:] * (DK_R ** -0.5)
    z = _dot(xn, wvg_ref[...])
    wv = N_HEADS_R * DV_R
    vr_b[...] = _mx(z[:, :wv])
    gr_f[...] = z[:, wv:]


def _proj_c_kernel(x_ref, g_ref, wc_ref, qm_b, gates_f):
    xn = _mx(_rms(x_ref[...], g_ref[...]))
    z = _dot(xn, wc_ref[...])
    wm = N_HEADS_M * HEAD_DIM_M
    qm_b[...] = _mx(z[:, :wm])
    gates_f[...] = z[:, wm:]


def _row_spec(tm, w):
    return pl.BlockSpec((tm, w), lambda i: (i, 0))


def _full_spec(shape):
    nd = len(shape)
    return pl.BlockSpec(shape, lambda i: (0,) * nd)


def _hm_spec(nh, tm, d):
    return pl.BlockSpec((nh, tm, d), lambda i: (0, i, 0))


def _projections(x, gain, pos, wts, tm):
    rows, d = x.shape
    grid = (rows // tm,)
    wa = N_HEADS_A * HEAD_DIM_A
    cos64, sin64 = _rope_tables(pos, HEAD_DIM_A)
    cos128, sin128 = _rope_tables(pos, DK_R)
    g2 = gain.reshape(1, d)
    sds = jax.ShapeDtypeStruct
    t_spec = pl.BlockSpec((N_HEADS_A, HEAD_DIM_A, tm), lambda i: (0, 0, i))
    outs_a = pl.pallas_call(
        _proj_a_kernel,
        grid=grid,
        in_specs=[_row_spec(tm, d), _full_spec((1, d)), _row_spec(tm, LANES), _row_spec(tm, LANES),
                  _full_spec(wts["wa"].shape), _full_spec(wts["wv"].shape)],
        out_specs=[_hm_spec(N_HEADS_A, tm, HEAD_DIM_A), t_spec, _row_spec(tm, wa),
                   _hm_spec(N_HEADS_A, tm, HEAD_DIM_A),
                   _hm_spec(IDX_HEADS, tm, IDX_DIM), _row_spec(tm, IDX_DIM), _row_spec(tm, IDX_DIM),
                   _row_spec(tm, wa), t_spec, _row_spec(tm, LANES)],
        out_shape=[sds((N_HEADS_A, rows, HEAD_DIM_A), MXU_DTYPE), sds((N_HEADS_A, HEAD_DIM_A, rows), MXU_DTYPE),
                   sds((rows, wa), F32),
                   sds((N_HEADS_A, rows, HEAD_DIM_A), MXU_DTYPE), sds((IDX_HEADS, rows, IDX_DIM), MXU_DTYPE),
                   sds((rows, IDX_DIM), F32), sds((rows, IDX_DIM), MXU_DTYPE),
                   sds((rows, wa), F32), sds((N_HEADS_A, HEAD_DIM_A, rows), MXU_DTYPE), sds((rows, LANES), F32)],
        compiler_params=_cparams(("parallel",)),
        name="proj_a",
    )(x, g2, cos64, sin64, wts["wa"], wts["wv"])
    names_a = ("qa_hm", "qa_t", "ka_f", "ka_hm", "qi_hm", "ki_f", "ki_b", "va_f", "va_t", "wi_f")
    wr, wv = N_HEADS_R * DK_R, N_HEADS_R * DV_R
    outs_b = pl.pallas_call(
        _proj_b_kernel,
        grid=grid,
        in_specs=[_row_spec(tm, d), _full_spec((1, d)), _row_spec(tm, LANES), _row_spec(tm, LANES),
                  _full_spec(wts["wr"].shape), _full_spec(wts["wvg"].shape)],
        out_specs=[_row_spec(tm, wr), _row_spec(tm, wr), _row_spec(tm, wv), _row_spec(tm, wv)],
        out_shape=[sds((rows, wr), F32), sds((rows, wr), F32), sds((rows, wv), MXU_DTYPE), sds((rows, wv), F32)],
        compiler_params=_cparams(("parallel",)),
        name="proj_b",
    )(x, g2, cos128, sin128, wts["wr"], wts["wvg"])
    names_b = ("qr_f", "kr_f", "vr_b", "gr_f")
    wm = N_HEADS_M * HEAD_DIM_M
    outs_c = pl.pallas_call(
        _proj_c_kernel,
        grid=grid,
        in_specs=[_row_spec(tm, d), _full_spec((1, d)), _full_spec(wts["wc"].shape)],
        out_specs=[_row_spec(tm, wm), _row_spec(tm, 3 * d)],
        out_shape=[sds((rows, wm), MXU_DTYPE), sds((rows, 3 * d), F32)],
        compiler_params=_cparams(("parallel",)),
        name="proj_c",
    )(x, g2, wts["wc"])
    names_c = ("qm_b", "gates_f")
    out = dict(zip(names_a, outs_a))
    out.update(zip(names_b, outs_b))
    out.update(zip(names_c, outs_c))
    return out


def _prep_in_weights(w_in):
    d = w_in.shape[0]
    wa = N_HEADS_A * HEAD_DIM_A
    widths = (wa, wa, wa, IDX_HEADS * IDX_DIM, IDX_DIM, IDX_HEADS,
              N_HEADS_R * DK_R, N_HEADS_R * DK_R, N_HEADS_R * DV_R, N_HEADS_R * DV_R,
              N_HEADS_M * HEAD_DIM_M, 3 * d)
    offs = np.concatenate([[0], np.cumsum(widths)])
    seg = [w_in[:, int(offs[i]):int(offs[i + 1])] for i in range(len(widths))]
    q_a, k_a, v_a, q_i, k_i, w_i, q_r, k_r, v_r, g_r, q_m, gates = seg
    zpad = lambda n: jnp.zeros((d, n), w_in.dtype)
    return {
        "wa": _mx(jnp.concatenate([q_a, k_a, q_i, k_i, zpad(LANES - IDX_DIM)], axis=1)),
        "wv": _mx(jnp.concatenate([v_a, w_i, zpad(LANES - IDX_HEADS)], axis=1)),
        "wr": _mx(jnp.concatenate([q_r, k_r], axis=1)),
        "wvg": _mx(jnp.concatenate([v_r, g_r], axis=1)),
        "wc": _mx(jnp.concatenate([q_m, gates], axis=1)),
    }


SEL_ROWS = 64
SEL_WIDE = 512
SEL_GROUPS = 2 * LANES
NO_LIMIT = 2 ** 30
MIN_NORMAL_KEY = 0x00800000
SEARCH_PERIOD = 4
SEARCH_CAP = SEARCH_PERIOD * 33


def _key_to_f32(key):
    bits = jnp.where(key >= 0, key, key ^ jnp.int32(0x7FFFFFFF))
    return pltpu.bitcast(bits, F32)


def _f32_to_key(f):
    bits = pltpu.bitcast(f, I32)
    return jnp.where(f == 0.0, 0, jnp.where(bits >= 0, bits, bits ^ jnp.int32(0x7FFFFFFF)))


def _sweep(segments, rs, init, fn):
    acc = init
    base = 0
    for ref, n_wide in segments:
        def body(c, a, ref=ref, base=base):
            off = c * SEL_WIDE
            for k in range(SEL_WIDE // LANES):
                x = ref[rs, pl.ds(pl.multiple_of(off + k * LANES, LANES), LANES)]
                a = fn(a, x, base + off + k * LANES, k)
            return a
        acc = lax.fori_loop(0, n_wide, body, acc)
        base = base + n_wide * SEL_WIDE
    return acc


def _count(segments, rows, make_pred):
    outs = []
    for g in range(rows // SEL_ROWS):
        rs = slice(g * SEL_ROWS, (g + 1) * SEL_ROWS)
        pred = make_pred(rs)
        acc = _sweep(segments, rs, jnp.zeros((SEL_ROWS, LANES), F32),
                     lambda a, x, idx0, k: a + jnp.where(pred(x, idx0), 1.0, 0.0))
        outs.append(jnp.sum(acc, axis=1, keepdims=True))
    return _col_to_row(jnp.concatenate(outs, axis=0) if len(outs) > 1 else outs[0])


def _col_to_row(col):
    return jnp.broadcast_to(col, (col.shape[0], LANES)).T[0:1]


def _row_to_lanes(vec):
    return jnp.broadcast_to(vec, (LANES, vec.shape[1])).T


def _search(count_fn, lo, hi, c_lo, c_hi, target, alive):
    def unfinished(lo, hi, c_lo):
        return jnp.logical_and(alive, jnp.logical_and(c_lo > target, lo + 1 < hi))

    def any_row(flag):
        return jnp.max(jnp.where(flag, 1, 0).astype(I32))

    def cond(carry):
        return jnp.logical_and(carry[0] < SEARCH_CAP, carry[1] > 0)

    def body(carry):
        it, _, phase, lo, hi, c_lo, c_hi, w_lo, w_hi, last = carry
        act = unfinished(lo, hi, c_lo)
        bis = (lo >> 1) + (hi >> 1) + (lo & hi & 1)
        width = hi - lo
        a = (c_lo - target + 0.5) * w_lo
        b = (target - 0.5 - c_hi) * w_hi
        frac = a / jnp.maximum(a + b, 1e-6)
        step = (frac * width.astype(F32)).astype(I32)
        itp = lo + jnp.clip(step, 1, jnp.maximum(width - 1, 1))
        use_itp = jnp.logical_and((lo ^ hi) >= 0, (jnp.zeros_like(lo) + phase) != SEARCH_PERIOD - 1)
        v = jnp.where(use_itp, itp, bis)
        cnt = count_fn(v)
        up = jnp.logical_and(act, cnt >= target)
        dn = jnp.logical_and(act, cnt < target)
        lo = jnp.where(up, v, lo)
        c_lo = jnp.where(up, cnt, c_lo)
        hi = jnp.where(dn, v, hi)
        c_hi = jnp.where(dn, cnt, c_hi)
        w_hi = jnp.where(up, jnp.where(last == 1, w_hi * 0.5, 1.0), jnp.where(dn, 1.0, w_hi))
        w_lo = jnp.where(dn, jnp.where(last == -1, w_lo * 0.5, 1.0), jnp.where(up, 1.0, w_lo))
        last = jnp.where(up, 1, jnp.where(dn, -1, last))
        phase = jnp.where(phase == SEARCH_PERIOD - 1, 0, phase + 1)
        return it + 1, any_row(unfinished(lo, hi, c_lo)), phase, lo, hi, c_lo, c_hi, w_lo, w_hi, last

    go = any_row(unfinished(lo, hi, c_lo))
    one = jnp.ones(lo.shape, F32)
    out = lax.while_loop(cond, body, (jnp.int32(0), go, jnp.int32(0), lo, hi, c_lo, c_hi,
                                      one, one, jnp.zeros(lo.shape, I32)))
    return out[3], out[5], out[6]


def _select_threshold(segments, rows, topk, alive=None):
    assert topk <= SEL_GROUPS
    kf = jnp.full((1, rows), float(topk), F32)
    if alive is None:
        alive = jnp.full((1, rows), True)

    los, his = [], []
    for g in range(rows // SEL_ROWS):
        rs = slice(g * SEL_ROWS, (g + 1) * SEL_ROWS)
        ninf = jnp.full((SEL_ROWS, LANES), -jnp.inf, F32)
        ga, gb = _sweep(segments, rs, (ninf, ninf),
                        lambda a, x, idx0, k: ((jnp.maximum(a[0], x), a[1]) if k % 2 == 0
                                               else (a[0], jnp.maximum(a[1], x))))
        los.append(jnp.min(jnp.minimum(ga, gb), axis=1, keepdims=True))
        his.append(jnp.max(jnp.maximum(ga, gb), axis=1, keepdims=True))
    cat = lambda xs: _col_to_row(jnp.concatenate(xs, axis=0) if len(xs) > 1 else xs[0])
    lo = _f32_to_key(jnp.maximum(cat(los), F32_LOWEST))
    hi = _f32_to_key(cat(his)) + 1

    def count_ge_f(thr):
        thr_l = _row_to_lanes(thr)

        def make_pred(rs):
            t = thr_l[rs]
            return lambda x, idx0: x >= t
        return _count(segments, rows, make_pred)

    count_ge = lambda v: count_ge_f(_key_to_f32(v))
    c_lo = count_ge(lo)
    c_hi = jnp.zeros((1, rows), F32)
    for probe in (0, MIN_NORMAL_KEY):
        v = jnp.full((1, rows), probe, I32)
        cnt = count_ge_f(jnp.full((1, rows), np.int32(probe).view(np.float32), F32))
        inside = jnp.logical_and(lo < v, v < hi)
        up = jnp.logical_and(inside, cnt >= kf)
        dn = jnp.logical_and(inside, cnt < kf)
        lo, c_lo = jnp.where(up, v, lo), jnp.where(up, cnt, c_lo)
        hi, c_hi = jnp.where(dn, v, hi), jnp.where(dn, cnt, c_hi)
    at_zero = jnp.logical_and(lo == 0, hi == MIN_NORMAL_KEY)
    lo, c_lo, c_hi = _search(count_ge, lo, hi, c_lo, c_hi, kf, jnp.logical_and(alive, jnp.logical_not(at_zero)))
    tau = _key_to_f32(lo)

    need = jnp.logical_and(alive, c_lo > kf)
    room = jnp.where(need, kf - c_hi, float(NO_LIMIT))
    return _row_to_lanes(tau), _row_to_lanes(room)


def _write_mask(segments_out, rows, tau, room):
    r_io = lax.broadcasted_iota(I32, (LANES, 2 * LANES), 0)
    c_io = lax.broadcasted_iota(I32, (LANES, 2 * LANES), 1)
    tri = jnp.where(jnp.logical_or(r_io <= c_io, c_io >= LANES), 1.0, 0.0).astype(MXU_DTYPE)
    for g in range(rows // SEL_ROWS):
        rs = slice(g * SEL_ROWS, (g + 1) * SEL_ROWS)
        t, rm = tau[rs], room[rs]
        seen = jnp.zeros((SEL_ROWS, LANES), F32)
        for src, dst, n_wide, n_total in segments_out:
            def body(c, seen, src=src, dst=dst):
                off = c * SEL_WIDE
                nsub = SEL_WIDE // LANES
                dss = [pl.ds(pl.multiple_of(off + k * LANES, LANES), LANES) for k in range(nsub)]
                xs = [src[rs, ds] for ds in dss]
                eqs = [jnp.where(x == t, 1.0, 0.0) for x in xs]
                run = jnp.dot(jnp.concatenate(eqs, axis=0).astype(MXU_DTYPE), tri, preferred_element_type=F32)
                for k in range(nsub):
                    run_k = run[k * SEL_ROWS:(k + 1) * SEL_ROWS]
                    before = seen + run_k[:, :LANES] - eqs[k]
                    keep_eq = jnp.where(before < rm, eqs[k], 0.0)
                    dst[rs, dss[k]] = jnp.where(xs[k] > t, 1.0, keep_eq).astype(dst.dtype)
                    seen = seen + run_k[:, LANES:]
                return seen

            seen = lax.fori_loop(0, n_wide, body, seen)

            def zbody(c, carry, dst=dst):
                off = pl.multiple_of(c * SEL_WIDE, SEL_WIDE)
                dst[rs, pl.ds(off, SEL_WIDE)] = jnp.zeros((SEL_ROWS, SEL_WIDE), dst.dtype)
                return carry

            lax.fori_loop(n_wide, n_total, zbody, 0)


def _index_scores(qi_ref, w, kb, transposed_keys=False):
    acc = None
    for h in range(IDX_HEADS):
        s = _dot(_mx(qi_ref[h]), kb) if transposed_keys else _dot_nt(_mx(qi_ref[h]), kb)
        t = w[:, h:h + 1] * jnp.maximum(s, 0.0)
        acc = t if acc is None else acc + t
    return acc


def _prompt_select_kernel(qi_ref, wi_ref, kidx_ref, mask_ref, i_scr, *, tq, tk, topk):
    i, j = pl.program_id(0), pl.program_id(1)
    nk = pl.num_programs(1)
    q_lo = i * tq
    n_wide = (q_lo + tq - 1) // tk + 1

    @pl.when(j < n_wide)
    def _():
        acc = _index_scores(qi_ref, wi_ref[...], kidx_ref[...])
        qpos = q_lo + lax.broadcasted_iota(I32, acc.shape, 0)
        kpos = j * tk + lax.broadcasted_iota(I32, acc.shape, 1)
        i_scr[:, pl.ds(pl.multiple_of(j * tk, tk), tk)] = jnp.where(kpos <= qpos, acc, -jnp.inf)

    @pl.when(j == nk - 1)
    def _():
        seg = [(i_scr, n_wide)]
        tau, room = _select_threshold(seg, tq, topk)
        _write_mask([(i_scr, mask_ref, n_wide, nk)], tq, tau, room)


def _prompt_select(qi_hm, wi_f, ki_b, topk, tq, tk):
    s = ki_b.shape[0]
    assert tk == SEL_WIDE and s % tk == 0 and s % tq == 0 and tq % SEL_ROWS == 0
    nq, nk = s // tq, s // tk
    kmap = lambda i, j: (jnp.minimum(j, (i * tq + tq - 1) // tk), 0)
    return pl.pallas_call(
        functools.partial(_prompt_select_kernel, tq=tq, tk=tk, topk=topk),
        grid=(nq, nk),
        in_specs=[pl.BlockSpec((IDX_HEADS, tq, IDX_DIM), lambda i, j: (0, i, 0)),
                  pl.BlockSpec((tq, LANES), lambda i, j: (i, 0)),
                  pl.BlockSpec((tk, IDX_DIM), kmap)],
        out_specs=pl.BlockSpec((tq, s), lambda i, j: (i, 0)),
        out_shape=jax.ShapeDtypeStruct((s, s), MXU_DTYPE),
        scratch_shapes=[pltpu.VMEM((tq, s), F32)],
        compiler_params=_cparams(("parallel", "arbitrary")),
        name="prompt_select",
    )(qi_hm, wi_f, ki_b)


def _spread(a, n):
    if n <= LANES:
        return a[:, :n]
    return jnp.concatenate([a] * (n // LANES), axis=1)


def _flash_update(s, v, m_prev, l_prev, acc_prev):
    m_new = jnp.maximum(m_prev, jnp.max(s, axis=1, keepdims=True))
    alpha = jnp.exp(m_prev - m_new)
    p = jnp.exp(s - _spread(m_new, s.shape[1]))
    l_new = alpha * l_prev + jnp.sum(p, axis=1, keepdims=True)
    acc_new = _spread(alpha, acc_prev.shape[1]) * acc_prev + _dot(_mx(p), v)
    return m_new, l_new, acc_new


def _prompt_attend_kernel(qt_ref, k_ref, vt_ref, mask_ref, o_ref, m_scr, l_scr, acc_scr, *, tq, tk):
    i, j = pl.program_id(0), pl.program_id(1)
    nk = pl.num_programs(1)

    @pl.when(j == 0)
    def _():
        m_scr[...] = jnp.full(m_scr.shape, NEG_BIG, F32)
        l_scr[...] = jnp.zeros(l_scr.shape, F32)
        acc_scr[...] = jnp.zeros(acc_scr.shape, F32)

    @pl.when(j * tk <= i * tq + tq - 1)
    def _():
        keep = mask_ref[...].astype(F32).T > 0
        scores = [jnp.where(keep, _dot(k_ref[h], qt_ref[h]), NEG_BIG) for h in range(N_HEADS_A)]
        for h, s in enumerate(scores):
            m_prev = m_scr[h]
            m_new = jnp.maximum(m_prev, jnp.max(s, axis=0, keepdims=True))
            alpha = jnp.exp(m_prev - m_new)
            p = jnp.exp(s - m_new[0:1])
            l_scr[h] = alpha * l_scr[h] + jnp.sum(p, axis=0, keepdims=True)
            acc_scr[h] = alpha[0:1] * acc_scr[h] + _dot(vt_ref[h], _mx(p))
            m_scr[h] = m_new

    @pl.when(j == nk - 1)
    def _():
        ot = jnp.concatenate([acc_scr[h] / l_scr[h][0:1] for h in range(N_HEADS_A)], axis=0)
        o_ref[...] = ot.T.astype(o_ref.dtype)


def _prompt_attend(qa_t, ka_hm, va_t, mask, tq, tk):
    nh, dh, s = qa_t.shape
    nq, nk = s // tq, s // tk
    diag = lambda i, j: jnp.minimum(j, (i * tq + tq - 1) // tk)
    return pl.pallas_call(
        functools.partial(_prompt_attend_kernel, tq=tq, tk=tk),
        grid=(nq, nk),
        in_specs=[pl.BlockSpec((nh, dh, tq), lambda i, j: (0, 0, i)),
                  pl.BlockSpec((nh, tk, dh), lambda i, j: (0, diag(i, j), 0)),
                  pl.BlockSpec((nh, dh, tk), lambda i, j: (0, 0, diag(i, j))),
                  pl.BlockSpec((tq, tk), lambda i, j: (i, diag(i, j)))],
        out_specs=pl.BlockSpec((tq, nh * dh), lambda i, j: (i, 0)),
        out_shape=jax.ShapeDtypeStruct((s, nh * dh), MXU_DTYPE),
        scratch_shapes=[pltpu.VMEM((nh, SUBLANES, tq), F32), pltpu.VMEM((nh, SUBLANES, tq), F32),
                        pltpu.VMEM((nh, dh, tq), F32)],
        compiler_params=_cparams(("parallel", "arbitrary")),
        name="prompt_attend",
    )(qa_t, ka_hm, va_t, mask)


PAGES_PER_STEP = 8


def _sample_index_kernel(pt_ref, qi_ref, wi_ref, *refs, page):
    del pt_ref
    pages, out_ref = refs[:-1], refs[-1]
    w = wi_ref[0]
    for p, kref in enumerate(pages):
        out_ref[0, :, p * page:(p + 1) * page] = _index_scores(qi_ref, w, _mx(kref[0]), transposed_keys=True)


def _sample_index(page_table, qi_hm, wi_f, cache_kidx_t):
    db, n_pages = page_table.shape
    _, idim, page = cache_kidx_t.shape
    pps = math.gcd(PAGES_PER_STEP, n_pages)
    nsteps = n_pages // pps
    pt = page_table.reshape(-1).astype(I32)

    def kspec(p):
        return pl.BlockSpec((1, idim, page), lambda b, j, pt: (pt[b * n_pages + j * pps + p], 0, 0))

    grid_spec = pltpu.PrefetchScalarGridSpec(
        num_scalar_prefetch=1,
        grid=(db, nsteps),
        in_specs=[pl.BlockSpec((IDX_HEADS, T_PAD, idim), lambda b, j, pt: (0, b, 0)),
                  pl.BlockSpec((1, T_PAD, LANES), lambda b, j, pt: (b, 0, 0))]
                 + [kspec(p) for p in range(pps)],
        out_specs=pl.BlockSpec((1, T_PAD, pps * page), lambda b, j, pt: (b, 0, j)),
    )
    return pl.pallas_call(
        functools.partial(_sample_index_kernel, page=page),
        grid_spec=grid_spec,
        out_shape=jax.ShapeDtypeStruct((db, T_PAD, n_pages * page), F32),
        compiler_params=_cparams(("parallel", "arbitrary")),
        name="sample_index",
    )(pt, qi_hm, wi_f.reshape(db, T_PAD, LANES), *([cache_kidx_t] * pps))


def _sample_select_kernel(ipast_ref, qi_ref, wi_ref, kin_ref, mpast_ref, mnew_ref, inew_scr, *, t_real, topk):
    rows = ipast_ref.shape[0]
    acc = _index_scores(qi_ref, wi_ref[...], kin_ref[...])
    r = lax.broadcasted_iota(I32, acc.shape, 0)
    c = lax.broadcasted_iota(I32, acc.shape, 1)
    same = (r // T_PAD) == (c // T_PAD)
    tq, tc = r % T_PAD, c % T_PAD
    ok = jnp.logical_and(same, jnp.logical_and(tc <= tq, tc < t_real))
    inew_scr[...] = jnp.full(inew_scr.shape, -jnp.inf, F32)
    inew_scr[:, :rows] = jnp.where(ok, acc, -jnp.inf)
    n_past = ipast_ref.shape[1] // SEL_WIDE
    n_new = inew_scr.shape[1] // SEL_WIDE
    alive = lax.rem(lax.broadcasted_iota(I32, (1, rows), 1), T_PAD) < t_real
    tau, room = _select_threshold([(ipast_ref, n_past), (inew_scr, n_new)], rows, topk, alive)
    _write_mask([(ipast_ref, mpast_ref, n_past, n_past), (inew_scr, mnew_ref, n_new, n_new)], rows, tau, room)


def _sample_select(i_past, qi_hm, wi_f, ki_b, t_real, topk):
    rows, past = i_past.shape
    assert past % SEL_WIDE == 0 and rows % SEL_ROWS == 0
    wnew = -(-rows // SEL_WIDE) * SEL_WIDE
    return pl.pallas_call(
        functools.partial(_sample_select_kernel, t_real=t_real, topk=topk),
        out_shape=[jax.ShapeDtypeStruct((rows, past), F32), jax.ShapeDtypeStruct((rows, wnew), F32)],
        scratch_shapes=[pltpu.VMEM((rows, wnew), F32)],
        compiler_params=pltpu.CompilerParams(vmem_limit_bytes=VMEM_LIMIT),
        name="sample_select",
    )(i_past, qi_hm, wi_f, ki_b)


def _sample_attend_kernel(pt_ref, q_ref, mp_ref, mn_ref, kn_ref, vn_ref, *refs, page, pps):
    del pt_ref
    kpages, vpages = refs[:pps], refs[pps:2 * pps]
    o_ref, m_scr, l_scr, acc_scr = refs[2 * pps:]
    j = pl.program_id(1)
    nj = pl.num_programs(1)
    q = q_ref[0]
    nh = q.shape[0] // T_PAD

    def scores(m_t, kt):
        keep = jnp.concatenate([m_t] * nh, axis=0) > 0
        return jnp.where(keep, _dot(q, kt), NEG_BIG)

    def update(s_list, vt_list):
        m_prev = m_scr[...]
        m_blk = functools.reduce(jnp.maximum, [jnp.max(s, axis=1, keepdims=True) for s in s_list])
        m_new = jnp.maximum(m_prev, m_blk)
        alpha = jnp.exp(m_prev - m_new)
        l_new = alpha * l_scr[...]
        acc = _spread(alpha, acc_scr.shape[1]) * acc_scr[...]
        for s, vt in zip(s_list, vt_list):
            p = jnp.exp(s - _spread(m_new, s.shape[1]))
            l_new = l_new + jnp.sum(p, axis=1, keepdims=True)
            acc = acc + _dot_nt(_mx(p), vt)
        m_scr[...], l_scr[...], acc_scr[...] = m_new, l_new, acc

    @pl.when(j == 0)
    def _():
        m_scr[...] = jnp.full(m_scr.shape, NEG_BIG, F32)
        l_scr[...] = jnp.zeros(l_scr.shape, F32)
        acc_scr[...] = jnp.zeros(acc_scr.shape, F32)

    update([scores(mp_ref[0, :, p * page:(p + 1) * page], _mx(kpages[p][0])) for p in range(pps)],
           [_mx(vpages[p][0]) for p in range(pps)])

    @pl.when(j == nj - 1)
    def _():
        update([scores(mn_ref[0], kn_ref[0])], [vn_ref[0]])
        full = acc_scr[...] / _spread(l_scr[...], acc_scr.shape[1])
        lane = lax.broadcasted_iota(I32, (T_PAD, full.shape[1]), 1)
        out = jnp.zeros((T_PAD, full.shape[1]), F32)
        for h in range(nh):
            out = out + jnp.where((lane // HEAD_DIM_A) == h, full[h * T_PAD:(h + 1) * T_PAD], 0.0)
        o_ref[0] = out


def _sample_attend(page_table, q_bd, m_past, m_new, kt_new, vt_new, cache_kt, cache_vt):
    db, n_pages = page_table.shape
    _, hd, page = cache_kt.shape
    pps = math.gcd(PAGES_PER_STEP, n_pages)
    nsteps = n_pages // pps
    nnew = kt_new.shape[2]
    nq = q_bd.shape[1]
    pt = page_table.reshape(-1).astype(I32)

    def pspec(p):
        return pl.BlockSpec((1, hd, page), lambda b, j, pt: (pt[b * n_pages + j * pps + p], 0, 0))

    bspec = lambda shape: pl.BlockSpec((1,) + shape, lambda b, j, pt: (b, 0, 0))
    grid_spec = pltpu.PrefetchScalarGridSpec(
        num_scalar_prefetch=1,
        grid=(db, nsteps),
        in_specs=[bspec((nq, hd)),
                  pl.BlockSpec((1, T_PAD, pps * page), lambda b, j, pt: (b, 0, j)),
                  bspec((T_PAD, nnew)), bspec((hd, nnew)), bspec((hd, nnew))]
                 + [pspec(p) for p in range(pps)] * 2,
        out_specs=bspec((T_PAD, hd)),
        scratch_shapes=[pltpu.VMEM((nq, LANES), F32), pltpu.VMEM((nq, LANES), F32), pltpu.VMEM((nq, hd), F32)],
    )
    return pl.pallas_call(
        functools.partial(_sample_attend_kernel, page=page, pps=pps),
        grid_spec=grid_spec,
        out_shape=jax.ShapeDtypeStruct((db, T_PAD, hd), F32),
        compiler_params=_cparams(("parallel", "arbitrary")),
        name="sample_attend",
    )(pt, q_bd, m_past, m_new, kt_new, vt_new, *([cache_kt] * pps), *([cache_vt] * pps))


def _retention_tables(c_real, c_pad):
    h = np.arange(N_HEADS_R, dtype=np.float64)
    log_g = np.log1p(-np.exp2(-5.0 - h))
    i = np.arange(c_pad, dtype=np.float64)
    diff = i[:, None] - i[None, :]
    live = (diff >= 0) & (i[:, None] < c_real) & (i[None, :] < c_real)
    inner = np.where(live[None], np.exp(np.maximum(diff, 0.0)[None] * log_g[:, None, None]), 0.0)
    q_dec = np.exp((i + 1.0)[None, :] * log_g[:, None])
    k_dec = np.where(i[None, :] < c_real, np.exp((c_real - 1.0 - i)[None, :] * log_g[:, None]), 0.0)
    c_dec = np.exp(c_real * log_g)
    f = lambda a: jnp.asarray(a, F32)
    return f(inner), f(q_dec[:, :, None]), f(k_dec[:, :, None]), [float(v) for v in c_dec]


def _retention_kernel(q_ref, k_ref, v_ref, g_ref, s0_ref, inner_ref, qdec_ref, kdec_ref,
                      o_ref, s_out_ref, s_scr, *, c_dec):
    j = pl.program_id(1)
    nj = pl.num_programs(1)

    @pl.when(j == 0)
    def _():
        s_scr[...] = s0_ref[0]

    for h in range(N_HEADS_R):
        q = q_ref[:, h * DK_R:(h + 1) * DK_R]
        k = k_ref[:, h * DK_R:(h + 1) * DK_R]
        v = v_ref[:, h * DV_R:(h + 1) * DV_R]
        s_prev = s_scr[h]
        a = _dot_nt(_mx(q), _mx(k)) * inner_ref[h]
        o = _dot(_mx(a), v) + _dot(_mx(q), _mx(s_prev)) * qdec_ref[h]
        kd = k * kdec_ref[h]
        s_scr[h] = s_prev * c_dec[h] + _dot(_mx(kd.T), v)
        mu = jnp.mean(o, axis=-1, keepdims=True)
        var = jnp.mean(jnp.square(o - mu), axis=-1, keepdims=True)
        gn = (o - mu) * lax.rsqrt(var + EPS)
        g = g_ref[:, h * DV_R:(h + 1) * DV_R]
        o_ref[:, h * DV_R:(h + 1) * DV_R] = (gn * (g * _sigmoid(g))).astype(o_ref.dtype)

    @pl.when(j == nj - 1)
    def _():
        s_out_ref[0] = s_scr[...]


def _retention(qr, kr, vr, gr, s0, c_real):
    b = s0.shape[0]
    c = RET_CHUNK
    n = qr.shape[0] // (b * c)
    inner, qdec, kdec, c_dec = _retention_tables(c_real, c)
    wr, wv = N_HEADS_R * DK_R, N_HEADS_R * DV_R
    rmap = lambda bi, j: (bi * n + j, 0)
    full3 = lambda shape: pl.BlockSpec(shape, lambda bi, j: (0, 0, 0))
    return pl.pallas_call(
        functools.partial(_retention_kernel, c_dec=c_dec),
        grid=(b, n),
        in_specs=[pl.BlockSpec((c, wr), rmap), pl.BlockSpec((c, wr), rmap), pl.BlockSpec((c, wv), rmap),
                  pl.BlockSpec((c, wv), rmap),
                  pl.BlockSpec((1, N_HEADS_R, DK_R, DV_R), lambda bi, j: (bi, 0, 0, 0)),
                  full3(inner.shape), full3(qdec.shape), full3(kdec.shape)],
        out_specs=[pl.BlockSpec((c, wv), rmap),
                   pl.BlockSpec((1, N_HEADS_R, DK_R, DV_R), lambda bi, j: (bi, 0, 0, 0))],
        out_shape=[jax.ShapeDtypeStruct((b * n * c, wv), MXU_DTYPE),
                   jax.ShapeDtypeStruct((b, N_HEADS_R, DK_R, DV_R), F32)],
        scratch_shapes=[pltpu.VMEM((N_HEADS_R, DK_R, DV_R), F32)],
        compiler_params=_cparams(("parallel", "arbitrary")),
        name="retention",
    )(qr, kr, vr, gr, s0, inner, qdec, kdec)


def _cross_kernel(q_ref, mk_ref, mv_ref, o_ref):
    scale = HEAD_DIM_M ** -0.5
    for h in range(N_HEADS_M):
        sl = slice(h * HEAD_DIM_M, (h + 1) * HEAD_DIM_M)
        s = _dot_nt(_mx(q_ref[:, sl]), _mx(mk_ref[0, :, sl])) * scale
        p = jnp.exp(s - jnp.max(s, axis=1, keepdims=True))
        p = p / jnp.sum(p, axis=1, keepdims=True)
        o_ref[:, sl] = _dot(_mx(p), _mx(mv_ref[0, :, sl])).astype(o_ref.dtype)


def _cross_attend(qm, mk, mv, tm, out_dtype):
    b, n_mem, hd = mk.shape
    nt = qm.shape[0] // (b * tm)
    return pl.pallas_call(
        _cross_kernel,
        grid=(b, nt),
        in_specs=[pl.BlockSpec((tm, hd), lambda bi, i: (bi * nt + i, 0)),
                  pl.BlockSpec((1, n_mem, hd), lambda bi, i: (bi, 0, 0)),
                  pl.BlockSpec((1, n_mem, hd), lambda bi, i: (bi, 0, 0))],
        out_specs=pl.BlockSpec((tm, hd), lambda bi, i: (bi * nt + i, 0)),
        out_shape=jax.ShapeDtypeStruct(qm.shape, out_dtype),
        compiler_params=_cparams(("parallel", "parallel")),
        name="cross_attend",
    )(qm, mk, mv)


def _memkv_kernel(x_ref, g_ref, w_ref, o_ref):
    o_ref[...] = _dot(_mx(_rms(x_ref[...], g_ref[...])), w_ref[...])


def _memory_kv(mem, gain, w):
    rows, d = mem.shape
    return pl.pallas_call(
        _memkv_kernel,
        out_shape=jax.ShapeDtypeStruct((rows, w.shape[1]), F32),
        compiler_params=pltpu.CompilerParams(vmem_limit_bytes=VMEM_LIMIT),
        name="memory_kv",
    )(mem, gain.reshape(1, d), _mx(w))


def _merge_kernel(x_ref, oa_ref, or_ref, om_ref, gates_ref, wpa_ref, wpb_ref, wpc_ref, wo_ref, g_ref, h_ref):
    d = x_ref.shape[1]
    gt = gates_ref[...]
    mixed = (_sigmoid(gt[:, :d]) * _dot(_mx(oa_ref[...]), wpa_ref[...])
             + _sigmoid(gt[:, d:2 * d]) * _dot(_mx(or_ref[...]), wpb_ref[...])
             + _sigmoid(gt[:, 2 * d:]) * _dot(_mx(om_ref[...]), wpc_ref[...]))
    z = _dot(_mx(mixed), wo_ref[...])
    h_ref[...] = x_ref[...] + _rms(z, g_ref[...])


def _merge(x, oa, o_r, om, gates, wpa, wpb, wpc, wo, gain, tm):
    rows, d = x.shape
    return pl.pallas_call(
        _merge_kernel,
        grid=(rows // tm,),
        in_specs=[_row_spec(tm, d), _row_spec(tm, oa.shape[1]), _row_spec(tm, o_r.shape[1]),
                  _row_spec(tm, om.shape[1]), _row_spec(tm, 3 * d),
                  _full_spec(wpa.shape), _full_spec(wpb.shape), _full_spec(wpc.shape), _full_spec(wo.shape),
                  _full_spec((1, d))],
        out_specs=_row_spec(tm, d),
        out_shape=jax.ShapeDtypeStruct((rows, d), F32),
        compiler_params=_cparams(("parallel",)),
        name="merge",
    )(x, oa, o_r, om, gates, wpa, wpb, wpc, wo, gain.reshape(1, d))


HALO = BF16_ROWS


def _ffn_kernel(h_ref, halo_ref, s0_ref, s1_ref, g1_ref, g2_ref, wu_ref, wg_ref, cw_ref, cb_ref, wd_ref,
                y_ref, utail_ref, x_scr, u_scr, *, tm, seq, keep):
    i = pl.program_id(0)
    h = h_ref[...]
    hn = _rms(h, g1_ref[...])
    x_scr[HALO:, :] = _mx(hn)
    x_scr[:HALO, :] = _mx(_rms(halo_ref[...], g1_ref[...]))
    xc = x_scr[...]
    u_scr[...] = _dot(xc, wu_ref[...])
    gate = _dot(xc[HALO:], wg_ref[...])
    cur = u_scr[HALO:, :]
    prev1 = u_scr[HALO - 1:HALO - 1 + tm, :]
    prev2 = u_scr[HALO - 2:HALO - 2 + tm, :]
    seq_loc = min(seq, tm)
    t = lax.rem(lax.broadcasted_iota(I32, (tm, 1), 0), seq_loc)
    t = jnp.where(lax.rem(i * tm, seq) == 0, t, CONV_W)
    st0, st1 = s0_ref[...], s1_ref[...]
    if st0.shape[0] != tm:
        st0, st1 = st0[0:1], st1[0:1]
    prev1 = jnp.where(t == 0, st1, prev1)
    prev2 = jnp.where(t == 0, st0, jnp.where(t == 1, st1, prev2))
    c = cb_ref[...] + prev2 * cw_ref[0:1, :] + prev1 * cw_ref[1:2, :] + cur * cw_ref[2:3, :]
    act = jax.nn.gelu(c, approximate=True) * gate
    ff = _dot(_mx(act), wd_ref[...])
    y_ref[...] = h + _rms(ff, g2_ref[...])
    utail_ref[...] = u_scr[HALO + tm - keep:, :]


def _conv_ffn(h, s0e, s1e, g1, g2, wu, wg, cw, cb, wd, tm, seq, keep):
    rows, d = h.shape
    f = wu.shape[1]
    nt = rows // tm
    hb = tm // HALO
    sr = s0e.shape[0]
    return pl.pallas_call(
        functools.partial(_ffn_kernel, tm=tm, seq=seq, keep=keep),
        grid=(nt,),
        in_specs=[_row_spec(tm, d),
                  pl.BlockSpec((HALO, d), lambda i: (jnp.maximum(i * hb - 1, 0), 0)),
                  _full_spec((sr, f)), _full_spec((sr, f)),
                  _full_spec((1, d)), _full_spec((1, d)),
                  _full_spec(wu.shape), _full_spec(wg.shape), _full_spec(cw.shape), _full_spec((1, f)),
                  _full_spec(wd.shape)],
        out_specs=[_row_spec(tm, d), _row_spec(keep, f)],
        out_shape=[jax.ShapeDtypeStruct((rows, d), F32), jax.ShapeDtypeStruct((nt * keep, f), F32)],
        scratch_shapes=[pltpu.VMEM((tm + HALO, d), MXU_DTYPE), pltpu.VMEM((tm + HALO, f), F32)],
        compiler_params=_cparams(("parallel",)),
        name="conv_ffn",
    )(h, h, s0e, s1e, g1.reshape(1, d), g2.reshape(1, d), wu, wg, cw, cb.reshape(1, f), wd)


def _layer_weights(l, w_in, w_proj_a, w_proj_b, w_proj_c, w_out, w_up, w_down):
    wts = _prep_in_weights(w_in[l])
    f = w_down.shape[1]
    wts.update(wpa=_mx(w_proj_a[l]), wpb=_mx(w_proj_b[l]), wpc=_mx(w_proj_c[l]), wo=_mx(w_out[l]),
               wu=_mx(w_up[l][:, :f]), wg=_mx(w_up[l][:, f:]), wd=_mx(w_down[l]))
    return wts


def _prompt_layer(x, mem, wts, norms, conv_w, conv_b, w_mem_kv, tiles):
    s, d = x.shape
    f = wts["wd"].shape[0]
    pos = jnp.arange(s)
    pr = _projections(x, norms["pre_mix"], pos, wts, tiles["proj"])
    topk = min(TOPK_MAX, s // 4)
    mask = _prompt_select(pr["qi_hm"], pr["wi_f"], pr["ki_b"], topk, tiles["sel_q"], SEL_WIDE)
    o_a = _prompt_attend(pr["qa_t"], pr["ka_hm"], pr["va_t"], mask, tiles["att_q"], tiles["att_k"])
    s0 = jnp.zeros((1, N_HEADS_R, DK_R, DV_R), F32)
    o_r, ret_new = _retention(pr["qr_f"], pr["kr_f"], pr["vr_b"], pr["gr_f"], s0, RET_CHUNK)
    kv = _memory_kv(mem, norms["mem"], w_mem_kv)
    wm = N_HEADS_M * HEAD_DIM_M
    mk, mv = kv[:, :wm], kv[:, wm:]
    o_m = _cross_attend(pr["qm_b"], mk[None], mv[None], tiles["cross"], MXU_DTYPE)
    h = _merge(x, o_a, o_r, o_m, pr["gates_f"], wts["wpa"], wts["wpb"], wts["wpc"], wts["wo"],
               norms["post_mix"], tiles["merge"])
    zst = jnp.zeros((SUBLANES, f), F32)
    y, utail = _conv_ffn(h, zst, zst, norms["pre_ffn"], norms["post_ffn"], wts["wu"], wts["wg"], conv_w, conv_b,
                         wts["wd"], tiles["ffn"], s, SUBLANES)
    conv_new = utail[-(CONV_W - 1):]
    return y, pr["ka_f"], pr["va_f"], pr["ki_f"], ret_new, conv_new, mk, mv


def _sample_layer(x, wts, norms, conv_w, conv_b, cache_k, cache_v, cache_kidx, mem_k, mem_v,
                  state_ret, state_conv, page_table):
    db, t, d = x.shape
    f = wts["wd"].shape[0]
    n_pages = page_table.shape[1]
    page = cache_k.shape[1]
    past = n_pages * page
    rows = db * T_PAD
    xp = jnp.pad(x, ((0, 0), (0, T_PAD - t), (0, 0))).reshape(rows, d)
    pos = jnp.tile(past + jnp.arange(T_PAD), db)
    pr = _projections(xp, norms["pre_mix"], pos, wts, rows)
    hd = N_HEADS_A * HEAD_DIM_A

    topk = min(TOPK_MAX, (past + t) // 4)
    i_past = _sample_index(page_table, pr["qi_hm"].astype(F32), pr["wi_f"], cache_kidx.transpose(0, 2, 1)).reshape(rows, past)
    m_past, m_new = _sample_select(i_past, pr["qi_hm"], pr["wi_f"], pr["ki_b"], t, topk)
    nnew = m_new.shape[1]
    own = m_new[:, :rows].reshape(db, T_PAD, db, T_PAD)[jnp.arange(db), :, jnp.arange(db), :]
    m_new_own = jnp.pad(own, ((0, 0), (0, 0), (0, LANES - T_PAD)))
    q_rows = pr["qa_hm"].reshape(N_HEADS_A, db, T_PAD, HEAD_DIM_A)
    eye = jnp.eye(N_HEADS_A, dtype=MXU_DTYPE)
    q_bd = jnp.einsum("hbtd,hg->bhtgd", q_rows, eye).reshape(db, N_HEADS_A * T_PAD, hd)
    new_t = lambda a: jnp.pad(_mx(a).reshape(db, T_PAD, hd).transpose(0, 2, 1), ((0, 0), (0, 0), (0, LANES - T_PAD)))
    paged_t = lambda c: c.transpose(0, 2, 3, 1).reshape(c.shape[0], hd, page)
    o_a = _sample_attend(page_table, q_bd, m_past.reshape(db, T_PAD, past), m_new_own,
                         new_t(pr["ka_f"]), new_t(pr["va_f"]), paged_t(cache_k), paged_t(cache_v)).reshape(rows, hd)

    padc = lambda a: jnp.pad(a.reshape(db, T_PAD, -1), ((0, 0), (0, RET_CHUNK - T_PAD), (0, 0))).reshape(db * RET_CHUNK, -1)
    o_r, ret_new = _retention(padc(pr["qr_f"]), padc(pr["kr_f"]), padc(pr["vr_b"]), padc(pr["gr_f"]), state_ret, t)
    o_r = o_r.reshape(db, RET_CHUNK, -1)[:, :T_PAD].reshape(rows, -1)

    wm = N_HEADS_M * HEAD_DIM_M
    o_m = _cross_attend(pr["qm_b"].astype(F32), mem_k.reshape(db, -1, wm), mem_v.reshape(db, -1, wm), T_PAD, F32)

    h = _merge(xp, o_a, o_r, o_m, pr["gates_f"], wts["wpa"], wts["wpb"], wts["wpc"], wts["wo"],
               norms["post_mix"], rows)
    s0e = jnp.repeat(state_conv[:, 0], T_PAD, axis=0)
    s1e = jnp.repeat(state_conv[:, 1], T_PAD, axis=0)
    y, u_all = _conv_ffn(h, s0e, s1e, norms["pre_ffn"], norms["post_ffn"], wts["wu"], wts["wg"], conv_w, conv_b,
                         wts["wd"], rows, T_PAD, rows)
    ext = jnp.concatenate([state_conv.astype(F32), u_all.reshape(db, T_PAD, f)[:, :t]], axis=1)
    conv_new = ext[:, t:]
    unpad = lambda a: a.reshape(db, T_PAD, -1)[:, :t]
    return (unpad(y), unpad(pr["ka_f"]), unpad(pr["va_f"]), unpad(pr["ki_f"]), ret_new, conv_new)


PROMPT_TILES = dict(proj=512, sel_q=256, att_q=256, att_k=1024, cross=512, merge=512, ffn=256)


def kernel(x_prompt, x_sample, cache_k, cache_v, cache_kidx, cache_mem_k, cache_mem_v, state_ret, state_conv,
           page_table, mem_prompt, norm_pre_mix, norm_post_mix, norm_pre_ffn, norm_post_ffn, norm_mem,
           w_in, w_mem_kv, w_proj_a, w_proj_b, w_proj_c, w_out, w_up, conv_w, conv_b, w_down):
    bp, s, d = x_prompt.shape
    db, t, _ = x_sample.shape
    depth = w_in.shape[0]
    assert bp == 1 and t <= T_PAD and CONV_W - 1 <= t
    tiles = {k: min(v, s) for k, v in PROMPT_TILES.items()}
    yp, ys = x_prompt[0], x_sample
    outs = [[] for _ in range(12)]
    for l in range(depth):
        wts = _layer_weights(l, w_in, w_proj_a, w_proj_b, w_proj_c, w_out, w_up, w_down)
        norms = dict(pre_mix=norm_pre_mix[l], post_mix=norm_post_mix[l], pre_ffn=norm_pre_ffn[l],
                     post_ffn=norm_post_ffn[l], mem=norm_mem[l])
        yp, kp, vp, kip, rp, cp, mk, mv = _prompt_layer(yp, mem_prompt[0], wts, norms, conv_w[l], conv_b[l],
                                                        w_mem_kv[l], tiles)
        ys, ks, vs, kis, rs, cs = _sample_layer(ys, wts, norms, conv_w[l], conv_b[l], cache_k[l], cache_v[l],
                                                cache_kidx[l], cache_mem_k[l], cache_mem_v[l], state_ret[l],
                                                state_conv[l], page_table)
        n_mem = mk.shape[0]
        vals = (kp.reshape(1, s, N_HEADS_A, HEAD_DIM_A), vp.reshape(1, s, N_HEADS_A, HEAD_DIM_A),
                kip.reshape(1, s, IDX_DIM), rp, cp[None],
                mk.reshape(1, n_mem, N_HEADS_M, HEAD_DIM_M), mv.reshape(1, n_mem, N_HEADS_M, HEAD_DIM_M),
                ks.reshape(db, t, N_HEADS_A, HEAD_DIM_A), vs.reshape(db, t, N_HEADS_A, HEAD_DIM_A),
                kis, rs, cs)
        for o, v in zip(outs, vals):
            o.append(v)
    stacked = [jnp.stack(o) for o in outs]
    return (yp[None], ys, *stacked)
```

```python
import functools
import math

import numpy as np
import jax
import jax.numpy as jnp
from jax import lax
from jax.experimental import pallas as pl
from jax.experimental.pallas import tpu as pltpu

F32 = jnp.float32
I32 = jnp.int32
MXU_DTYPE = jnp.bfloat16

N_HEADS_A, HEAD_DIM_A = 8, 64
IDX_HEADS, IDX_DIM = 4, 64
TOPK_MAX = 256
N_HEADS_R, DK_R, DV_R = 4, 128, 256
RET_CHUNK = 128
N_HEADS_M, HEAD_DIM_M = 4, 128
CONV_W = 3
ROPE_THETA = 10000.0
EPS = 1e-6

LANES = 128
SUBLANES = 8
BF16_ROWS = 16
VMEM_LIMIT = 56 * 1024 * 1024
NEG_BIG = -1e30
F32_LOWEST = float(np.finfo(np.float32).min)
T_PAD = 8


def _cparams(sem):
    return pltpu.CompilerParams(dimension_semantics=sem, vmem_limit_bytes=VMEM_LIMIT)


def _dot(a, b):
    return jnp.dot(a, b, preferred_element_type=F32)


def _dot_nt(a, b):
    return lax.dot_general(a, b, (((1,), (1,)), ((), ())), preferred_element_type=F32)


def _mx(a):
    return a.astype(MXU_DTYPE)


def _rms(x, g):
    return x * lax.rsqrt(jnp.mean(x * x, axis=-1, keepdims=True) + EPS) * g


def _sigmoid(x):
    return 1.0 / (1.0 + jnp.exp(-x))


def _rope_tables(pos, d):
    half = d // 2
    inv = 1.0 / (ROPE_THETA ** (jnp.arange(half, dtype=F32) * 2.0 / d))
    ang = pos.astype(F32)[:, None] * inv[None, :]
    cos, sin = jnp.cos(ang), jnp.sin(ang)
    reps = LANES // d
    cos_t = jnp.tile(jnp.concatenate([cos, cos], axis=1), (1, reps))
    sin_t = jnp.tile(jnp.concatenate([-sin, sin], axis=1), (1, reps))
    return cos_t, sin_t


def _rope(y, cos, sin, d):
    w = y.shape[1]
    half = d // 2
    reps = w // LANES
    c = jnp.concatenate([cos] * reps, axis=1) if reps > 1 else cos
    s = jnp.concatenate([sin] * reps, axis=1) if reps > 1 else sin
    lane = lax.broadcasted_iota(I32, y.shape, 1)
    first = (lane & (d - 1)) < half
    rot = jnp.where(first, pltpu.roll(y, w - half, 1), pltpu.roll(y, half, 1))
    return y * c + rot * s


def _proj_a_kernel(x_ref, g_ref, cos_ref, sin_ref, wa_ref, wv_ref,
                   qa_hm, qa_t, ka_f, ka_hm, qi_hm, ki_f, ki_b, va_f, va_t, wi_f):
    xn = _mx(_rms(x_ref[...], g_ref[...]))
    wa = N_HEADS_A * HEAD_DIM_A
    y = _rope(_dot(xn, wa_ref[...]), cos_ref[...], sin_ref[...], HEAD_DIM_A)
    q = y[:, :wa] * (HEAD_DIM_A ** -0.5)
    k = y[:, wa:2 * wa]
    ka_f[...] = k
    qt = q.T
    for h in range(N_HEADS_A):
        sl = slice(h * HEAD_DIM_A, (h + 1) * HEAD_DIM_A)
        qa_hm[h] = _mx(q[:, sl])
        qa_t[h] = _mx(qt[sl, :])
        ka_hm[h] = _mx(k[:, sl])
    qi = y[:, 2 * wa:2 * wa + IDX_HEADS * IDX_DIM]
    for h in range(IDX_HEADS):
        qi_hm[h] = _mx(qi[:, h * IDX_DIM:(h + 1) * IDX_DIM])
    ki = y[:, 2 * wa + IDX_HEADS * IDX_DIM:2 * wa + IDX_HEADS * IDX_DIM + IDX_DIM]
    ki_f[...] = ki
    ki_b[...] = _mx(ki)
    z = _dot(xn, wv_ref[...])
    v = z[:, :wa]
    va_f[...] = v
    vt = v.T
    for h in range(N_HEADS_A):
        va_t[h] = _mx(vt[h * HEAD_DIM_A:(h + 1) * HEAD_DIM_A, :])
    wi_f[...] = z[:, wa:wa + LANES]


def _proj_b_kernel(x_ref, g_ref, cos_ref, sin_ref, wr_ref, wvg_ref, qr_f, kr_f, vr_b, gr_f):
    xn = _mx(_rms(x_ref[...], g_ref[...]))
    wr = N_HEADS_R * DK_R
    y = _rope(_dot(xn, wr_ref[...]), cos_ref[...], sin_ref[...], DK_R)
    qr_f[...] = y[:, :wr]
    kr_f[...] = y[:, wr:] * (DK_R ** -0.5)
    z = _dot(xn, wvg_ref[...])
    wv = N_HEADS_R * DV_R
    vr_b[...] = _mx(z[:, :wv])
    gr_f[...] = z[:, wv:]


def _proj_c_kernel(x_ref, g_ref, wc_ref, qm_b, gates_f):
    xn = _mx(_rms(x_ref[...], g_ref[...]))
    z = _dot(xn, wc_ref[...])
    wm = N_HEADS_M * HEAD_DIM_M
    qm_b[...] = _mx(z[:, :wm])
    gates_f[...] = z[:, wm:]


def _row_spec(tm, w):
    return pl.BlockSpec((tm, w), lambda i: (i, 0))


def _full_spec(shape):
    nd = len(shape)
    return pl.BlockSpec(shape, lambda i: (0,) * nd)


def _hm_spec(nh, tm, d):
    return pl.BlockSpec((nh, tm, d), lambda i: (0, i, 0))


def _projections(x, gain, pos, wts, tm):
    rows, d = x.shape
    grid = (rows // tm,)
    wa = N_HEADS_A * HEAD_DIM_A
    cos64, sin64 = _rope_tables(pos, HEAD_DIM_A)
    cos128, sin128 = _rope_tables(pos, DK_R)
    g2 = gain.reshape(1, d)
    sds = jax.ShapeDtypeStruct
    t_spec = pl.BlockSpec((N_HEADS_A, HEAD_DIM_A, tm), lambda i: (0, 0, i))
    outs_a = pl.pallas_call(
        _proj_a_kernel,
        grid=grid,
        in_specs=[_row_spec(tm, d), _full_spec((1, d)), _row_spec(tm, LANES), _row_spec(tm, LANES),
                  _full_spec(wts["wa"].shape), _full_spec(wts["wv"].shape)],
        out_specs=[_hm_spec(N_HEADS_A, tm, HEAD_DIM_A), t_spec, _row_spec(tm, wa),
                   _hm_spec(N_HEADS_A, tm, HEAD_DIM_A),
                   _hm_spec(IDX_HEADS, tm, IDX_DIM), _row_spec(tm, IDX_DIM), _row_spec(tm, IDX_DIM),
                   _row_spec(tm, wa), t_spec, _row_spec(tm, LANES)],
        out_shape=[sds((N_HEADS_A, rows, HEAD_DIM_A), MXU_DTYPE), sds((N_HEADS_A, HEAD_DIM_A, rows), MXU_DTYPE),
                   sds((rows, wa), F32),
                   sds((N_HEADS_A, rows, HEAD_DIM_A), MXU_DTYPE), sds((IDX_HEADS, rows, IDX_DIM), MXU_DTYPE),
                   sds((rows, IDX_DIM), F32), sds((rows, IDX_DIM), MXU_DTYPE),
                   sds((rows, wa), F32), sds((N_HEADS_A, HEAD_DIM_A, rows), MXU_DTYPE), sds((rows, LANES), F32)],
        compiler_params=_cparams(("parallel",)),
        name="proj_a",
    )(x, g2, cos64, sin64, wts["wa"], wts["wv"])
    names_a = ("qa_hm", "qa_t", "ka_f", "ka_hm", "qi_hm", "ki_f", "ki_b", "va_f", "va_t", "wi_f")
    wr, wv = N_HEADS_R * DK_R, N_HEADS_R * DV_R
    outs_b = pl.pallas_call(
        _proj_b_kernel,
        grid=grid,
        in_specs=[_row_spec(tm, d), _full_spec((1, d)), _row_spec(tm, LANES), _row_spec(tm, LANES),
                  _full_spec(wts["wr"].shape), _full_spec(wts["wvg"].shape)],
        out_specs=[_row_spec(tm, wr), _row_spec(tm, wr), _row_spec(tm, wv), _row_spec(tm, wv)],
        out_shape=[sds((rows, wr), F32), sds((rows, wr), F32), sds((rows, wv), MXU_DTYPE), sds((rows, wv), F32)],
        compiler_params=_cparams(("parallel",)),
        name="proj_b",
    )(x, g2, cos128, sin128, wts["wr"], wts["wvg"])
    names_b = ("qr_f", "kr_f", "vr_b", "gr_f")
    wm = N_HEADS_M * HEAD_DIM_M
    outs_c = pl.pallas_call(
        _proj_c_kernel,
        grid=grid,
        in_specs=[_row_spec(tm, d), _full_spec((1, d)), _full_spec(wts["wc"].shape)],
        out_specs=[_row_spec(tm, wm), _row_spec(tm, 3 * d)],
        out_shape=[sds((rows, wm), MXU_DTYPE), sds((rows, 3 * d), F32)],
        compiler_params=_cparams(("parallel",)),
        name="proj_c",
    )(x, g2, wts["wc"])
    names_c = ("qm_b", "gates_f")
    out = dict(zip(names_a, outs_a))
    out.update(zip(names_b, outs_b))
    out.update(zip(names_c, outs_c))
    return out


def _prep_in_weights(w_in):
    d = w_in.shape[0]
    wa = N_HEADS_A * HEAD_DIM_A
    widths = (wa, wa, wa, IDX_HEADS * IDX_DIM, IDX_DIM, IDX_HEADS,
              N_HEADS_R * DK_R, N_HEADS_R * DK_R, N_HEADS_R * DV_R, N_HEADS_R * DV_R,
              N_HEADS_M * HEAD_DIM_M, 3 * d)
    offs = np.concatenate([[0], np.cumsum(widths)])
    seg = [w_in[:, int(offs[i]):int(offs[i + 1])] for i in range(len(widths))]
    q_a, k_a, v_a, q_i, k_i, w_i, q_r, k_r, v_r, g_r, q_m, gates = seg
    zpad = lambda n: jnp.zeros((d, n), w_in.dtype)
    return {
        "wa": _mx(jnp.concatenate([q_a, k_a, q_i, k_i, zpad(LANES - IDX_DIM)], axis=1)),
        "wv": _mx(jnp.concatenate([v_a, w_i, zpad(LANES - IDX_HEADS)], axis=1)),
        "wr": _mx(jnp.concatenate([q_r, k_r], axis=1)),
        "wvg": _mx(jnp.concatenate([v_r, g_r], axis=1)),
        "wc": _mx(jnp.concatenate([q_m, gates], axis=1)),
    }


SEL_ROWS = 64
SEL_WIDE = 512
SEL_GROUPS = 2 * LANES
NO_LIMIT = 2 ** 30
MIN_NORMAL_KEY = 0x00800000
SEARCH_PERIOD = 4
SEARCH_CAP = SEARCH_PERIOD * 33


def _key_to_f32(key):
    bits = jnp.where(key >= 0, key, key ^ jnp.int32(0x7FFFFFFF))
    return pltpu.bitcast(bits, F32)


def _f32_to_key(f):
    bits = pltpu.bitcast(f, I32)
    return jnp.where(f == 0.0, 0, jnp.where(bits >= 0, bits, bits ^ jnp.int32(0x7FFFFFFF)))


def _sweep(segments, rs, init, fn):
    acc = init
    base = 0
    for ref, n_wide in segments:
        def body(c, a, ref=ref, base=base):
            off = c * SEL_WIDE
            for k in range(SEL_WIDE // LANES):
                x = ref[rs, pl.ds(pl.multiple_of(off + k * LANES, LANES), LANES)]
                a = fn(a, x, base + off + k * LANES, k)
            return a
        pairs = n_wide // 2
        acc = lax.fori_loop(0, pairs, lambda c, a, body=body: body(2 * c + 1, body(2 * c, a)), acc)
        acc = lax.fori_loop(2 * pairs, n_wide, body, acc)
        base = base + n_wide * SEL_WIDE
    return acc


def _count(segments, rows, make_pred):
    outs = []
    for g in range(rows // SEL_ROWS):
        rs = slice(g * SEL_ROWS, (g + 1) * SEL_ROWS)
        pred = make_pred(rs)
        acc = _sweep(segments, rs, jnp.zeros((SEL_ROWS, LANES), F32),
                     lambda a, x, idx0, k: a + jnp.where(pred(x, idx0), 1.0, 0.0))
        outs.append(jnp.sum(acc, axis=1, keepdims=True))
    return _col_to_row(jnp.concatenate(outs, axis=0) if len(outs) > 1 else outs[0])


def _col_to_row(col):
    return jnp.broadcast_to(col, (col.shape[0], LANES)).T[0:1]


def _row_to_lanes(vec):
    return jnp.broadcast_to(vec, (LANES, vec.shape[1])).T


def _search(count_fn, lo, hi, c_lo, c_hi, target, alive):
    def unfinished(lo, hi, c_lo):
        return jnp.logical_and(alive, jnp.logical_and(c_lo > target, lo + 1 < hi))

    def any_row(flag):
        return jnp.max(jnp.where(flag, 1, 0).astype(I32))

    def cond(carry):
        return jnp.logical_and(carry[0] < SEARCH_CAP, carry[1] > 0)

    def body(carry):
        it, _, phase, lo, hi, c_lo, c_hi, w_lo, w_hi, last = carry
        act = unfinished(lo, hi, c_lo)
        bis = (lo >> 1) + (hi >> 1) + (lo & hi & 1)
        width = hi - lo
        a = (c_lo - target + 0.5) * w_lo
        b = (target - 0.5 - c_hi) * w_hi
        frac = a / jnp.maximum(a + b, 1e-6)
        step = (frac * width.astype(F32)).astype(I32)
        itp = lo + jnp.clip(step, 1, jnp.maximum(width - 1, 1))
        use_itp = jnp.logical_and((lo ^ hi) >= 0, (jnp.zeros_like(lo) + phase) != SEARCH_PERIOD - 1)
        v = jnp.where(use_itp, itp, bis)
        cnt = count_fn(v)
        up = jnp.logical_and(act, cnt >= target)
        dn = jnp.logical_and(act, cnt < target)
        lo = jnp.where(up, v, lo)
        c_lo = jnp.where(up, cnt, c_lo)
        hi = jnp.where(dn, v, hi)
        c_hi = jnp.where(dn, cnt, c_hi)
        w_hi = jnp.where(up, jnp.where(last == 1, w_hi * 0.5, 1.0), jnp.where(dn, 1.0, w_hi))
        w_lo = jnp.where(dn, jnp.where(last == -1, w_lo * 0.5, 1.0), jnp.where(up, 1.0, w_lo))
        last = jnp.where(up, 1, jnp.where(dn, -1, last))
        phase = jnp.where(phase == SEARCH_PERIOD - 1, 0, phase + 1)
        return it + 1, any_row(unfinished(lo, hi, c_lo)), phase, lo, hi, c_lo, c_hi, w_lo, w_hi, last

    go = any_row(unfinished(lo, hi, c_lo))
    one = jnp.ones(lo.shape, F32)
    out = lax.while_loop(cond, body, (jnp.int32(0), go, jnp.int32(0), lo, hi, c_lo, c_hi,
                                      one, one, jnp.zeros(lo.shape, I32)))
    return out[3], out[5], out[6]


def _select_threshold(segments, rows, topk, alive=None):
    assert topk <= SEL_GROUPS
    kf = jnp.full((1, rows), float(topk), F32)
    if alive is None:
        alive = jnp.full((1, rows), True)

    los, his = [], []
    for g in range(rows // SEL_ROWS):
        rs = slice(g * SEL_ROWS, (g + 1) * SEL_ROWS)
        ninf = jnp.full((SEL_ROWS, LANES), -jnp.inf, F32)
        ga, gb = _sweep(segments, rs, (ninf, ninf),
                        lambda a, x, idx0, k: ((jnp.maximum(a[0], x), a[1]) if k % 2 == 0
                                               else (a[0], jnp.maximum(a[1], x))))
        los.append(jnp.min(jnp.minimum(ga, gb), axis=1, keepdims=True))
        his.append(jnp.max(jnp.maximum(ga, gb), axis=1, keepdims=True))
    cat = lambda xs: _col_to_row(jnp.concatenate(xs, axis=0) if len(xs) > 1 else xs[0])
    lo = _f32_to_key(jnp.maximum(cat(los), F32_LOWEST))
    hi = _f32_to_key(cat(his)) + 1

    def count_ge_f(thr):
        thr_l = _row_to_lanes(thr)

        def make_pred(rs):
            t = thr_l[rs]
            return lambda x, idx0: x >= t
        return _count(segments, rows, make_pred)

    count_ge = lambda v: count_ge_f(_key_to_f32(v))
    c_lo = count_ge(lo)
    c_hi = jnp.zeros((1, rows), F32)
    def zero_probes(state):
        lo, hi, c_lo, c_hi = state
        for probe in (0, MIN_NORMAL_KEY):
            v = jnp.full((1, rows), probe, I32)
            cnt = count_ge_f(jnp.full((1, rows), np.int32(probe).view(np.float32), F32))
            inside = jnp.logical_and(lo < v, v < hi)
            up = jnp.logical_and(inside, cnt >= kf)
            dn = jnp.logical_and(inside, cnt < kf)
            lo, c_lo = jnp.where(up, v, lo), jnp.where(up, cnt, c_lo)
            hi, c_hi = jnp.where(dn, v, hi), jnp.where(dn, cnt, c_hi)
        return lo, hi, c_lo, c_hi

    straddles = jnp.logical_or(jnp.logical_and(lo < 0, hi > 0),
                               jnp.logical_and(lo < MIN_NORMAL_KEY, hi > MIN_NORMAL_KEY))
    lo, hi, c_lo, c_hi = lax.cond(jnp.max(jnp.where(straddles, 1, 0).astype(I32)) > 0,
                                  zero_probes, lambda s: s, (lo, hi, c_lo, c_hi))
    at_zero = jnp.logical_and(lo == 0, hi == MIN_NORMAL_KEY)
    lo, c_lo, c_hi = _search(count_ge, lo, hi, c_lo, c_hi, kf, jnp.logical_and(alive, jnp.logical_not(at_zero)))
    tau = _key_to_f32(lo)

    need = jnp.logical_and(alive, c_lo > kf)
    room = jnp.where(need, kf - c_hi, float(NO_LIMIT))
    return _row_to_lanes(tau), _row_to_lanes(room)


def _write_mask(segments_out, rows, tau, room):
    r_io = lax.broadcasted_iota(I32, (LANES, 2 * LANES), 0)
    c_io = lax.broadcasted_iota(I32, (LANES, 2 * LANES), 1)
    tri = jnp.where(jnp.logical_or(r_io <= c_io, c_io >= LANES), 1.0, 0.0).astype(MXU_DTYPE)
    for g in range(rows // SEL_ROWS):
        rs = slice(g * SEL_ROWS, (g + 1) * SEL_ROWS)
        t, rm = tau[rs], room[rs]
        seen = jnp.zeros((SEL_ROWS, LANES), F32)
        for src, dst, n_wide, n_total in segments_out:
            def body(c, seen, src=src, dst=dst):
                off = c * SEL_WIDE
                nsub = SEL_WIDE // LANES
                dss = [pl.ds(pl.multiple_of(off + k * LANES, LANES), LANES) for k in range(nsub)]
                xs = [src[rs, ds] for ds in dss]
                eqs = [jnp.where(x == t, 1.0, 0.0) for x in xs]
                run = jnp.dot(jnp.concatenate(eqs, axis=0).astype(MXU_DTYPE), tri, preferred_element_type=F32)
                for k in range(nsub):
                    run_k = run[k * SEL_ROWS:(k + 1) * SEL_ROWS]
                    before = seen + run_k[:, :LANES] - eqs[k]
                    keep_eq = jnp.where(before < rm, eqs[k], 0.0)
                    dst[rs, dss[k]] = jnp.where(xs[k] > t, 1.0, keep_eq).astype(dst.dtype)
                    seen = seen + run_k[:, LANES:]
                return seen

            seen = lax.fori_loop(0, n_wide, body, seen)

            def zbody(c, carry, dst=dst):
                off = pl.multiple_of(c * SEL_WIDE, SEL_WIDE)
                dst[rs, pl.ds(off, SEL_WIDE)] = jnp.zeros((SEL_ROWS, SEL_WIDE), dst.dtype)
                return carry

            lax.fori_loop(n_wide, n_total, zbody, 0)


def _index_scores(qi_ref, w, kb, transposed_keys=False):
    acc = None
    for h in range(IDX_HEADS):
        s = _dot(_mx(qi_ref[h]), kb) if transposed_keys else _dot_nt(_mx(qi_ref[h]), kb)
        t = w[:, h:h + 1] * jnp.maximum(s, 0.0)
        acc = t if acc is None else acc + t
    return acc


def _prompt_select_kernel(qi_ref, wi_ref, kidx_ref, mask_ref, i_scr, *, tq, tk, topk):
    i, j = pl.program_id(0), pl.program_id(1)
    nk = pl.num_programs(1)
    q_lo = i * tq
    n_wide = (q_lo + tq - 1) // tk + 1

    @pl.when(j < n_wide)
    def _():
        acc = _index_scores(qi_ref, wi_ref[...], kidx_ref[...])
        qpos = q_lo + lax.broadcasted_iota(I32, acc.shape, 0)
        kpos = j * tk + lax.broadcasted_iota(I32, acc.shape, 1)
        i_scr[:, pl.ds(pl.multiple_of(j * tk, tk), tk)] = jnp.where(kpos <= qpos, acc, -jnp.inf)

    @pl.when(j == nk - 1)
    def _():
        seg = [(i_scr, n_wide)]
        tau, room = _select_threshold(seg, tq, topk)
        _write_mask([(i_scr, mask_ref, n_wide, nk)], tq, tau, room)


def _prompt_select(qi_hm, wi_f, ki_b, topk, tq, tk):
    s = ki_b.shape[0]
    assert tk == SEL_WIDE and s % tk == 0 and s % tq == 0 and tq % SEL_ROWS == 0
    nq, nk = s // tq, s // tk
    kmap = lambda i, j: (jnp.minimum(j, (i * tq + tq - 1) // tk), 0)
    return pl.pallas_call(
        functools.partial(_prompt_select_kernel, tq=tq, tk=tk, topk=topk),
        grid=(nq, nk),
        in_specs=[pl.BlockSpec((IDX_HEADS, tq, IDX_DIM), lambda i, j: (0, i, 0)),
                  pl.BlockSpec((tq, LANES), lambda i, j: (i, 0)),
                  pl.BlockSpec((tk, IDX_DIM), kmap)],
        out_specs=pl.BlockSpec((tq, s), lambda i, j: (i, 0)),
        out_shape=jax.ShapeDtypeStruct((s, s), MXU_DTYPE),
        scratch_shapes=[pltpu.VMEM((tq, s), F32)],
        compiler_params=_cparams(("parallel", "arbitrary")),
        name="prompt_select",
    )(qi_hm, wi_f, ki_b)


def _spread(a, n):
    if n <= LANES:
        return a[:, :n]
    return jnp.concatenate([a] * (n // LANES), axis=1)


def _flash_update(s, v, m_prev, l_prev, acc_prev):
    m_new = jnp.maximum(m_prev, jnp.max(s, axis=1, keepdims=True))
    alpha = jnp.exp(m_prev - m_new)
    p = jnp.exp(s - _spread(m_new, s.shape[1]))
    l_new = alpha * l_prev + jnp.sum(p, axis=1, keepdims=True)
    acc_new = _spread(alpha, acc_prev.shape[1]) * acc_prev + _dot(_mx(p), v)
    return m_new, l_new, acc_new


def _prompt_attend_kernel(qt_ref, k_ref, vt_ref, mask_ref, o_ref, m_scr, l_scr, acc_scr, *, tq, tk):
    i, j = pl.program_id(0), pl.program_id(1)
    nk = pl.num_programs(1)

    @pl.when(j == 0)
    def _():
        m_scr[...] = jnp.full(m_scr.shape, NEG_BIG, F32)
        l_scr[...] = jnp.zeros(l_scr.shape, F32)
        acc_scr[...] = jnp.zeros(acc_scr.shape, F32)

    @pl.when(j * tk <= i * tq + tq - 1)
    def _():
        keep = mask_ref[...].astype(F32).T > 0
        scores = [jnp.where(keep, _dot(k_ref[h], qt_ref[h]), NEG_BIG) for h in range(N_HEADS_A)]
        for h, s in enumerate(scores):
            m_prev = m_scr[h]
            m_new = jnp.maximum(m_prev, jnp.max(s, axis=0, keepdims=True))
            alpha = jnp.exp(m_prev - m_new)
            p = jnp.exp(s - m_new[0:1])
            l_scr[h] = alpha * l_scr[h] + jnp.sum(p, axis=0, keepdims=True)
            acc_scr[h] = alpha[0:1] * acc_scr[h] + _dot(vt_ref[h], _mx(p))
            m_scr[h] = m_new

    @pl.when(j == nk - 1)
    def _():
        ot = jnp.concatenate([acc_scr[h] / l_scr[h][0:1] for h in range(N_HEADS_A)], axis=0)
        o_ref[...] = ot.T.astype(o_ref.dtype)


def _prompt_attend(qa_t, ka_hm, va_t, mask, tq, tk):
    nh, dh, s = qa_t.shape
    nq, nk = s // tq, s // tk
    diag = lambda i, j: jnp.minimum(j, (i * tq + tq - 1) // tk)
    return pl.pallas_call(
        functools.partial(_prompt_attend_kernel, tq=tq, tk=tk),
        grid=(nq, nk),
        in_specs=[pl.BlockSpec((nh, dh, tq), lambda i, j: (0, 0, i)),
                  pl.BlockSpec((nh, tk, dh), lambda i, j: (0, diag(i, j), 0)),
                  pl.BlockSpec((nh, dh, tk), lambda i, j: (0, 0, diag(i, j))),
                  pl.BlockSpec((tq, tk), lambda i, j: (i, diag(i, j)))],
        out_specs=pl.BlockSpec((tq, nh * dh), lambda i, j: (i, 0)),
        out_shape=jax.ShapeDtypeStruct((s, nh * dh), MXU_DTYPE),
        scratch_shapes=[pltpu.VMEM((nh, SUBLANES, tq), F32), pltpu.VMEM((nh, SUBLANES, tq), F32),
                        pltpu.VMEM((nh, dh, tq), F32)],
        compiler_params=_cparams(("parallel", "arbitrary")),
        name="prompt_attend",
    )(qa_t, ka_hm, va_t, mask)


PAGES_PER_STEP = 8
INDEX_PAGES_PER_STEP = 16


def _sample_index_kernel(pt_ref, qi_ref, wi_ref, *refs, page):
    del pt_ref
    pages, out_ref = refs[:-1], refs[-1]
    w = wi_ref[0]
    for p, kref in enumerate(pages):
        out_ref[0, :, p * page:(p + 1) * page] = _index_scores(qi_ref, w, _mx(kref[0]), transposed_keys=True)


def _sample_index(page_table, qi_hm, wi_f, cache_kidx_t):
    db, n_pages = page_table.shape
    _, idim, page = cache_kidx_t.shape
    pps = math.gcd(INDEX_PAGES_PER_STEP, n_pages)
    nsteps = n_pages // pps
    pt = page_table.reshape(-1).astype(I32)

    def kspec(p):
        return pl.BlockSpec((1, idim, page), lambda b, j, pt: (pt[b * n_pages + j * pps + p], 0, 0))

    grid_spec = pltpu.PrefetchScalarGridSpec(
        num_scalar_prefetch=1,
        grid=(db, nsteps),
        in_specs=[pl.BlockSpec((IDX_HEADS, T_PAD, idim), lambda b, j, pt: (0, b, 0)),
                  pl.BlockSpec((1, T_PAD, LANES), lambda b, j, pt: (b, 0, 0))]
                 + [kspec(p) for p in range(pps)],
        out_specs=pl.BlockSpec((1, T_PAD, pps * page), lambda b, j, pt: (b, 0, j)),
    )
    return pl.pallas_call(
        functools.partial(_sample_index_kernel, page=page),
        grid_spec=grid_spec,
        out_shape=jax.ShapeDtypeStruct((db, T_PAD, n_pages * page), F32),
        compiler_params=_cparams(("parallel", "arbitrary")),
        name="sample_index",
    )(pt, qi_hm, wi_f.reshape(db, T_PAD, LANES), *([cache_kidx_t] * pps))


def _sample_select_kernel(ipast_ref, qi_ref, wi_ref, kin_ref, mpast_ref, mnew_ref, inew_scr, *, t_real, topk):
    rows = ipast_ref.shape[0]
    acc = _index_scores(qi_ref, wi_ref[...], kin_ref[...])
    r = lax.broadcasted_iota(I32, acc.shape, 0)
    c = lax.broadcasted_iota(I32, acc.shape, 1)
    same = (r // T_PAD) == (c // T_PAD)
    tq, tc = r % T_PAD, c % T_PAD
    ok = jnp.logical_and(same, jnp.logical_and(tc <= tq, tc < t_real))
    inew_scr[...] = jnp.full(inew_scr.shape, -jnp.inf, F32)
    inew_scr[:, :rows] = jnp.where(ok, acc, -jnp.inf)
    n_past = ipast_ref.shape[1] // SEL_WIDE
    n_new = inew_scr.shape[1] // SEL_WIDE
    alive = lax.rem(lax.broadcasted_iota(I32, (1, rows), 1), T_PAD) < t_real
    tau, room = _select_threshold([(ipast_ref, n_past), (inew_scr, n_new)], rows, topk, alive)
    _write_mask([(ipast_ref, mpast_ref, n_past, n_past), (inew_scr, mnew_ref, n_new, n_new)], rows, tau, room)


def _sample_select(i_past, qi_hm, wi_f, ki_b, t_real, topk):
    rows, past = i_past.shape
    assert past % SEL_WIDE == 0 and rows % SEL_ROWS == 0
    wnew = -(-rows // SEL_WIDE) * SEL_WIDE
    return pl.pallas_call(
        functools.partial(_sample_select_kernel, t_real=t_real, topk=topk),
        out_shape=[jax.ShapeDtypeStruct((rows, past), F32), jax.ShapeDtypeStruct((rows, wnew), F32)],
        scratch_shapes=[pltpu.VMEM((rows, wnew), F32)],
        compiler_params=pltpu.CompilerParams(vmem_limit_bytes=VMEM_LIMIT),
        name="sample_select",
    )(i_past, qi_hm, wi_f, ki_b)


def _sample_attend_kernel(pt_ref, q_ref, mp_ref, mn_ref, kn_ref, vn_ref, *refs, page, pps):
    del pt_ref
    kpages, vpages = refs[:pps], refs[pps:2 * pps]
    o_ref, m_scr, l_scr, acc_scr = refs[2 * pps:]
    j = pl.program_id(1)
    nj = pl.num_programs(1)
    q = q_ref[0]
    nh = q.shape[0] // T_PAD

    def scores(m_t, kt):
        keep = jnp.concatenate([m_t] * nh, axis=0) > 0
        return jnp.where(keep, _dot(q, kt), NEG_BIG)

    def update(s_list, vt_list):
        m_prev = m_scr[...]
        m_blk = functools.reduce(jnp.maximum, [jnp.max(s, axis=1, keepdims=True) for s in s_list])
        m_new = jnp.maximum(m_prev, m_blk)
        alpha = jnp.exp(m_prev - m_new)
        l_new = alpha * l_scr[...]
        acc = _spread(alpha, acc_scr.shape[1]) * acc_scr[...]
        for s, vt in zip(s_list, vt_list):
            p = jnp.exp(s - _spread(m_new, s.shape[1]))
            l_new = l_new + jnp.sum(p, axis=1, keepdims=True)
            acc = acc + _dot_nt(_mx(p), vt)
        m_scr[...], l_scr[...], acc_scr[...] = m_new, l_new, acc

    @pl.when(j == 0)
    def _():
        m_scr[...] = jnp.full(m_scr.shape, NEG_BIG, F32)
        l_scr[...] = jnp.zeros(l_scr.shape, F32)
        acc_scr[...] = jnp.zeros(acc_scr.shape, F32)

    update([scores(mp_ref[0, :, p * page:(p + 1) * page], _mx(kpages[p][0])) for p in range(pps)],
           [_mx(vpages[p][0]) for p in range(pps)])

    @pl.when(j == nj - 1)
    def _():
        update([scores(mn_ref[0], kn_ref[0])], [vn_ref[0]])
        full = acc_scr[...] / _spread(l_scr[...], acc_scr.shape[1])
        lane = lax.broadcasted_iota(I32, (T_PAD, full.shape[1]), 1)
        out = jnp.zeros((T_PAD, full.shape[1]), F32)
        for h in range(nh):
            out = out + jnp.where((lane // HEAD_DIM_A) == h, full[h * T_PAD:(h + 1) * T_PAD], 0.0)
        o_ref[0] = out


def _sample_attend(page_table, q_bd, m_past, m_new, kt_new, vt_new, cache_kt, cache_vt):
    db, n_pages = page_table.shape
    _, hd, page = cache_kt.shape
    pps = math.gcd(PAGES_PER_STEP, n_pages)
    nsteps = n_pages // pps
    nnew = kt_new.shape[2]
    nq = q_bd.shape[1]
    pt = page_table.reshape(-1).astype(I32)

    def pspec(p):
        return pl.BlockSpec((1, hd, page), lambda b, j, pt: (pt[b * n_pages + j * pps + p], 0, 0))

    bspec = lambda shape: pl.BlockSpec((1,) + shape, lambda b, j, pt: (b, 0, 0))
    grid_spec = pltpu.PrefetchScalarGridSpec(
        num_scalar_prefetch=1,
        grid=(db, nsteps),
        in_specs=[bspec((nq, hd)),
                  pl.BlockSpec((1, T_PAD, pps * page), lambda b, j, pt: (b, 0, j)),
                  bspec((T_PAD, nnew)), bspec((hd, nnew)), bspec((hd, nnew))]
                 + [pspec(p) for p in range(pps)] * 2,
        out_specs=bspec((T_PAD, hd)),
        scratch_shapes=[pltpu.VMEM((nq, LANES), F32), pltpu.VMEM((nq, LANES), F32), pltpu.VMEM((nq, hd), F32)],
    )
    return pl.pallas_call(
        functools.partial(_sample_attend_kernel, page=page, pps=pps),
        grid_spec=grid_spec,
        out_shape=jax.ShapeDtypeStruct((db, T_PAD, hd), F32),
        compiler_params=_cparams(("parallel", "arbitrary")),
        name="sample_attend",
    )(pt, q_bd, m_past, m_new, kt_new, vt_new, *([cache_kt] * pps), *([cache_vt] * pps))


def _retention_tables(c_real, c_pad):
    h = np.arange(N_HEADS_R, dtype=np.float64)
    log_g = np.log1p(-np.exp2(-5.0 - h))
    i = np.arange(c_pad, dtype=np.float64)
    diff = i[:, None] - i[None, :]
    live = (diff >= 0) & (i[:, None] < c_real) & (i[None, :] < c_real)
    inner = np.where(live[None], np.exp(np.maximum(diff, 0.0)[None] * log_g[:, None, None]), 0.0)
    q_dec = np.exp((i + 1.0)[None, :] * log_g[:, None])
    k_dec = np.where(i[None, :] < c_real, np.exp((c_real - 1.0 - i)[None, :] * log_g[:, None]), 0.0)
    c_dec = np.exp(c_real * log_g)
    f = lambda a: jnp.asarray(a, F32)
    return f(inner), f(q_dec[:, :, None]), f(k_dec[:, :, None]), [float(v) for v in c_dec]


def _retention_kernel(q_ref, k_ref, v_ref, g_ref, s0_ref, inner_ref, qdec_ref, kdec_ref,
                      o_ref, s_out_ref, s_scr, *, c_dec):
    j = pl.program_id(1)
    nj = pl.num_programs(1)

    @pl.when(j == 0)
    def _():
        s_scr[...] = s0_ref[0]

    for h in range(N_HEADS_R):
        q = q_ref[:, h * DK_R:(h + 1) * DK_R]
        k = k_ref[:, h * DK_R:(h + 1) * DK_R]
        v = v_ref[:, h * DV_R:(h + 1) * DV_R]
        s_prev = s_scr[h]
        a = _dot_nt(_mx(q), _mx(k)) * inner_ref[h]
        o = _dot(_mx(a), v) + _dot(_mx(q), _mx(s_prev)) * qdec_ref[h]
        kd = k * kdec_ref[h]
        s_scr[h] = s_prev * c_dec[h] + _dot(_mx(kd.T), v)
        mu = jnp.mean(o, axis=-1, keepdims=True)
        var = jnp.mean(jnp.square(o - mu), axis=-1, keepdims=True)
        gn = (o - mu) * lax.rsqrt(var + EPS)
        g = g_ref[:, h * DV_R:(h + 1) * DV_R]
        o_ref[:, h * DV_R:(h + 1) * DV_R] = (gn * (g * _sigmoid(g))).astype(o_ref.dtype)

    @pl.when(j == nj - 1)
    def _():
        s_out_ref[0] = s_scr[...]


def _retention(qr, kr, vr, gr, s0, c_real):
    b = s0.shape[0]
    c = RET_CHUNK
    n = qr.shape[0] // (b * c)
    inner, qdec, kdec, c_dec = _retention_tables(c_real, c)
    wr, wv = N_HEADS_R * DK_R, N_HEADS_R * DV_R
    rmap = lambda bi, j: (bi * n + j, 0)
    full3 = lambda shape: pl.BlockSpec(shape, lambda bi, j: (0, 0, 0))
    return pl.pallas_call(
        functools.partial(_retention_kernel, c_dec=c_dec),
        grid=(b, n),
        in_specs=[pl.BlockSpec((c, wr), rmap), pl.BlockSpec((c, wr), rmap), pl.BlockSpec((c, wv), rmap),
                  pl.BlockSpec((c, wv), rmap),
                  pl.BlockSpec((1, N_HEADS_R, DK_R, DV_R), lambda bi, j: (bi, 0, 0, 0)),
                  full3(inner.shape), full3(qdec.shape), full3(kdec.shape)],
        out_specs=[pl.BlockSpec((c, wv), rmap),
                   pl.BlockSpec((1, N_HEADS_R, DK_R, DV_R), lambda bi, j: (bi, 0, 0, 0))],
        out_shape=[jax.ShapeDtypeStruct((b * n * c, wv), MXU_DTYPE),
                   jax.ShapeDtypeStruct((b, N_HEADS_R, DK_R, DV_R), F32)],
        scratch_shapes=[pltpu.VMEM((N_HEADS_R, DK_R, DV_R), F32)],
        compiler_params=_cparams(("parallel", "arbitrary")),
        name="retention",
    )(qr, kr, vr, gr, s0, inner, qdec, kdec)


def _cross_kernel(q_ref, mk_ref, mv_ref, o_ref):
    scale = HEAD_DIM_M ** -0.5
    for h in range(N_HEADS_M):
        sl = slice(h * HEAD_DIM_M, (h + 1) * HEAD_DIM_M)
        s = _dot_nt(_mx(q_ref[:, sl]), _mx(mk_ref[0, :, sl])) * scale
        p = jnp.exp(s - jnp.max(s, axis=1, keepdims=True))
        p = p / jnp.sum(p, axis=1, keepdims=True)
        o_ref[:, sl] = _dot(_mx(p), _mx(mv_ref[0, :, sl])).astype(o_ref.dtype)


def _cross_attend(qm, mk, mv, tm, out_dtype):
    b, n_mem, hd = mk.shape
    nt = qm.shape[0] // (b * tm)
    return pl.pallas_call(
        _cross_kernel,
        grid=(b, nt),
        in_specs=[pl.BlockSpec((tm, hd), lambda bi, i: (bi * nt + i, 0)),
                  pl.BlockSpec((1, n_mem, hd), lambda bi, i: (bi, 0, 0)),
                  pl.BlockSpec((1, n_mem, hd), lambda bi, i: (bi, 0, 0))],
        out_specs=pl.BlockSpec((tm, hd), lambda bi, i: (bi * nt + i, 0)),
        out_shape=jax.ShapeDtypeStruct(qm.shape, out_dtype),
        compiler_params=_cparams(("parallel", "parallel")),
        name="cross_attend",
    )(qm, mk, mv)


def _memkv_kernel(x_ref, g_ref, w_ref, o_ref):
    o_ref[...] = _dot(_mx(_rms(x_ref[...], g_ref[...])), w_ref[...])


def _memory_kv(mem, gain, w):
    rows, d = mem.shape
    return pl.pallas_call(
        _memkv_kernel,
        out_shape=jax.ShapeDtypeStruct((rows, w.shape[1]), F32),
        compiler_params=pltpu.CompilerParams(vmem_limit_bytes=VMEM_LIMIT),
        name="memory_kv",
    )(mem, gain.reshape(1, d), _mx(w))


def _merge_kernel(x_ref, oa_ref, or_ref, om_ref, gates_ref, wpa_ref, wpb_ref, wpc_ref, wo_ref, g_ref, h_ref):
    d = x_ref.shape[1]
    gt = gates_ref[...]
    mixed = (_sigmoid(gt[:, :d]) * _dot(_mx(oa_ref[...]), wpa_ref[...])
             + _sigmoid(gt[:, d:2 * d]) * _dot(_mx(or_ref[...]), wpb_ref[...])
             + _sigmoid(gt[:, 2 * d:]) * _dot(_mx(om_ref[...]), wpc_ref[...]))
    z = _dot(_mx(mixed), wo_ref[...])
    h_ref[...] = x_ref[...] + _rms(z, g_ref[...])


def _merge(x, oa, o_r, om, gates, wpa, wpb, wpc, wo, gain, tm):
    rows, d = x.shape
    return pl.pallas_call(
        _merge_kernel,
        grid=(rows // tm,),
        in_specs=[_row_spec(tm, d), _row_spec(tm, oa.shape[1]), _row_spec(tm, o_r.shape[1]),
                  _row_spec(tm, om.shape[1]), _row_spec(tm, 3 * d),
                  _full_spec(wpa.shape), _full_spec(wpb.shape), _full_spec(wpc.shape), _full_spec(wo.shape),
                  _full_spec((1, d))],
        out_specs=_row_spec(tm, d),
        out_shape=jax.ShapeDtypeStruct((rows, d), F32),
        compiler_params=_cparams(("parallel",)),
        name="merge",
    )(x, oa, o_r, om, gates, wpa, wpb, wpc, wo, gain.reshape(1, d))


HALO = BF16_ROWS


def _ffn_kernel(h_ref, halo_ref, s0_ref, s1_ref, g1_ref, g2_ref, wu_ref, wg_ref, cw_ref, cb_ref, wd_ref,
                y_ref, utail_ref, x_scr, u_scr, *, tm, seq, keep):
    i = pl.program_id(0)
    h = h_ref[...]
    hn = _rms(h, g1_ref[...])
    x_scr[HALO:, :] = _mx(hn)
    x_scr[:HALO, :] = _mx(_rms(halo_ref[...], g1_ref[...]))
    xc = x_scr[...]
    u_scr[...] = _dot(xc, wu_ref[...])
    gate = _dot(xc[HALO:], wg_ref[...])
    cur = u_scr[HALO:, :]
    prev1 = u_scr[HALO - 1:HALO - 1 + tm, :]
    prev2 = u_scr[HALO - 2:HALO - 2 + tm, :]
    seq_loc = min(seq, tm)
    t = lax.rem(lax.broadcasted_iota(I32, (tm, 1), 0), seq_loc)
    t = jnp.where(lax.rem(i * tm, seq) == 0, t, CONV_W)
    st0, st1 = s0_ref[...], s1_ref[...]
    if st0.shape[0] != tm:
        st0, st1 = st0[0:1], st1[0:1]
    prev1 = jnp.where(t == 0, st1, prev1)
    prev2 = jnp.where(t == 0, st0, jnp.where(t == 1, st1, prev2))
    c = cb_ref[...] + prev2 * cw_ref[0:1, :] + prev1 * cw_ref[1:2, :] + cur * cw_ref[2:3, :]
    act = jax.nn.gelu(c, approximate=True) * gate
    ff = _dot(_mx(act), wd_ref[...])
    y_ref[...] = h + _rms(ff, g2_ref[...])
    utail_ref[...] = u_scr[HALO + tm - keep:, :]


def _conv_ffn(h, s0e, s1e, g1, g2, wu, wg, cw, cb, wd, tm, seq, keep):
    rows, d = h.shape
    f = wu.shape[1]
    nt = rows // tm
    hb = tm // HALO
    sr = s0e.shape[0]
    return pl.pallas_call(
        functools.partial(_ffn_kernel, tm=tm, seq=seq, keep=keep),
        grid=(nt,),
        in_specs=[_row_spec(tm, d),
                  pl.BlockSpec((HALO, d), lambda i: (jnp.maximum(i * hb - 1, 0), 0)),
                  _full_spec((sr, f)), _full_spec((sr, f)),
                  _full_spec((1, d)), _full_spec((1, d)),
                  _full_spec(wu.shape), _full_spec(wg.shape), _full_spec(cw.shape), _full_spec((1, f)),
                  _full_spec(wd.shape)],
        out_specs=[_row_spec(tm, d), _row_spec(keep, f)],
        out_shape=[jax.ShapeDtypeStruct((rows, d), F32), jax.ShapeDtypeStruct((nt * keep, f), F32)],
        scratch_shapes=[pltpu.VMEM((tm + HALO, d), MXU_DTYPE), pltpu.VMEM((tm + HALO, f), F32)],
        compiler_params=_cparams(("parallel",)),
        name="conv_ffn",
    )(h, h, s0e, s1e, g1.reshape(1, d), g2.reshape(1, d), wu, wg, cw, cb.reshape(1, f), wd)


def _layer_weights(l, w_in, w_proj_a, w_proj_b, w_proj_c, w_out, w_up, w_down):
    wts = _prep_in_weights(w_in[l])
    f = w_down.shape[1]
    wts.update(wpa=_mx(w_proj_a[l]), wpb=_mx(w_proj_b[l]), wpc=_mx(w_proj_c[l]), wo=_mx(w_out[l]),
               wu=_mx(w_up[l][:, :f]), wg=_mx(w_up[l][:, f:]), wd=_mx(w_down[l]))
    return wts


def _prompt_layer(x, mem, wts, norms, conv_w, conv_b, w_mem_kv, tiles):
    s, d = x.shape
    f = wts["wd"].shape[0]
    pos = jnp.arange(s)
    pr = _projections(x, norms["pre_mix"], pos, wts, tiles["proj"])
    topk = min(TOPK_MAX, s // 4)
    mask = _prompt_select(pr["qi_hm"], pr["wi_f"], pr["ki_b"], topk, tiles["sel_q"], SEL_WIDE)
    o_a = _prompt_attend(pr["qa_t"], pr["ka_hm"], pr["va_t"], mask, tiles["att_q"], tiles["att_k"])
    s0 = jnp.zeros((1, N_HEADS_R, DK_R, DV_R), F32)
    o_r, ret_new = _retention(pr["qr_f"], pr["kr_f"], pr["vr_b"], pr["gr_f"], s0, RET_CHUNK)
    kv = _memory_kv(mem, norms["mem"], w_mem_kv)
    wm = N_HEADS_M * HEAD_DIM_M
    mk, mv = kv[:, :wm], kv[:, wm:]
    o_m = _cross_attend(pr["qm_b"], mk[None], mv[None], tiles["cross"], MXU_DTYPE)
    h = _merge(x, o_a, o_r, o_m, pr["gates_f"], wts["wpa"], wts["wpb"], wts["wpc"], wts["wo"],
               norms["post_mix"], tiles["merge"])
    zst = jnp.zeros((SUBLANES, f), F32)
    y, utail = _conv_ffn(h, zst, zst, norms["pre_ffn"], norms["post_ffn"], wts["wu"], wts["wg"], conv_w, conv_b,
                         wts["wd"], tiles["ffn"], s, SUBLANES)
    conv_new = utail[-(CONV_W - 1):]
    return y, pr["ka_f"], pr["va_f"], pr["ki_f"], ret_new, conv_new, mk, mv


def _sample_layer(x, wts, norms, conv_w, conv_b, cache_k, cache_v, cache_kidx, mem_k, mem_v,
                  state_ret, state_conv, page_table):
    db, t, d = x.shape
    f = wts["wd"].shape[0]
    n_pages = page_table.shape[1]
    page = cache_k.shape[1]
    past = n_pages * page
    rows = db * T_PAD
    xp = jnp.pad(x, ((0, 0), (0, T_PAD - t), (0, 0))).reshape(rows, d)
    pos = jnp.tile(past + jnp.arange(T_PAD), db)
    pr = _projections(xp, norms["pre_mix"], pos, wts, rows)
    hd = N_HEADS_A * HEAD_DIM_A

    topk = min(TOPK_MAX, (past + t) // 4)
    i_past = _sample_index(page_table, pr["qi_hm"].astype(F32), pr["wi_f"], cache_kidx.transpose(0, 2, 1)).reshape(rows, past)
    m_past, m_new = _sample_select(i_past, pr["qi_hm"], pr["wi_f"], pr["ki_b"], t, topk)
    nnew = m_new.shape[1]
    own = m_new[:, :rows].reshape(db, T_PAD, db, T_PAD)[jnp.arange(db), :, jnp.arange(db), :]
    m_new_own = jnp.pad(own, ((0, 0), (0, 0), (0, LANES - T_PAD)))
    q_rows = pr["qa_hm"].reshape(N_HEADS_A, db, T_PAD, HEAD_DIM_A)
    eye = jnp.eye(N_HEADS_A, dtype=MXU_DTYPE)
    q_bd = jnp.einsum("hbtd,hg->bhtgd", q_rows, eye).reshape(db, N_HEADS_A * T_PAD, hd)
    new_t = lambda a: jnp.pad(_mx(a).reshape(db, T_PAD, hd).transpose(0, 2, 1), ((0, 0), (0, 0), (0, LANES - T_PAD)))
    paged_t = lambda c: c.transpose(0, 2, 3, 1).reshape(c.shape[0], hd, page)
    o_a = _sample_attend(page_table, q_bd, m_past.reshape(db, T_PAD, past), m_new_own,
                         new_t(pr["ka_f"]), new_t(pr["va_f"]), paged_t(cache_k), paged_t(cache_v)).reshape(rows, hd)

    padc = lambda a: jnp.pad(a.reshape(db, T_PAD, -1), ((0, 0), (0, RET_CHUNK - T_PAD), (0, 0))).reshape(db * RET_CHUNK, -1)
    o_r, ret_new = _retention(padc(pr["qr_f"]), padc(pr["kr_f"]), padc(pr["vr_b"]), padc(pr["gr_f"]), state_ret, t)
    o_r = o_r.reshape(db, RET_CHUNK, -1)[:, :T_PAD].reshape(rows, -1)

    wm = N_HEADS_M * HEAD_DIM_M
    o_m = _cross_attend(pr["qm_b"].astype(F32), mem_k.reshape(db, -1, wm), mem_v.reshape(db, -1, wm), T_PAD, F32)

    h = _merge(xp, o_a, o_r, o_m, pr["gates_f"], wts["wpa"], wts["wpb"], wts["wpc"], wts["wo"],
               norms["post_mix"], rows)
    s0e = jnp.repeat(state_conv[:, 0], T_PAD, axis=0)
    s1e = jnp.repeat(state_conv[:, 1], T_PAD, axis=0)
    y, u_all = _conv_ffn(h, s0e, s1e, norms["pre_ffn"], norms["post_ffn"], wts["wu"], wts["wg"], conv_w, conv_b,
                         wts["wd"], rows, T_PAD, rows)
    ext = jnp.concatenate([state_conv.astype(F32), u_all.reshape(db, T_PAD, f)[:, :t]], axis=1)
    conv_new = ext[:, t:]
    unpad = lambda a: a.reshape(db, T_PAD, -1)[:, :t]
    return (unpad(y), unpad(pr["ka_f"]), unpad(pr["va_f"]), unpad(pr["ki_f"]), ret_new, conv_new)


PROMPT_TILES = dict(proj=512, sel_q=256, att_q=256, att_k=1024, cross=512, merge=512, ffn=256)


def kernel(x_prompt, x_sample, cache_k, cache_v, cache_kidx, cache_mem_k, cache_mem_v, state_ret, state_conv,
           page_table, mem_prompt, norm_pre_mix, norm_post_mix, norm_pre_ffn, norm_post_ffn, norm_mem,
           w_in, w_mem_kv, w_proj_a, w_proj_b, w_proj_c, w_out, w_up, conv_w, conv_b, w_down):
    bp, s, d = x_prompt.shape
    db, t, _ = x_sample.shape
    depth = w_in.shape[0]
    assert bp == 1 and t <= T_PAD and CONV_W - 1 <= t
    tiles = {k: min(v, s) for k, v in PROMPT_TILES.items()}
    yp, ys = x_prompt[0], x_sample
    outs = [[] for _ in range(12)]
    for l in range(depth):
        wts = _layer_weights(l, w_in, w_proj_a, w_proj_b, w_proj_c, w_out, w_up, w_down)
        norms = dict(pre_mix=norm_pre_mix[l], post_mix=norm_post_mix[l], pre_ffn=norm_pre_ffn[l],
                     post_ffn=norm_post_ffn[l], mem=norm_mem[l])
        yp, kp, vp, kip, rp, cp, mk, mv = _prompt_layer(yp, mem_prompt[0], wts, norms, conv_w[l], conv_b[l],
                                                        w_mem_kv[l], tiles)
        ys, ks, vs, kis, rs, cs = _sample_layer(ys, wts, norms, conv_w[l], conv_b[l], cache_k[l], cache_v[l],
                                                cache_kidx[l], cache_mem_k[l], cache_mem_v[l], state_ret[l],
                                                state_conv[l], page_table)
        n_mem = mk.shape[0]
        vals = (kp.reshape(1, s, N_HEADS_A, HEAD_DIM_A), vp.reshape(1, s, N_HEADS_A, HEAD_DIM_A),
                kip.reshape(1, s, IDX_DIM), rp, cp[None],
                mk.reshape(1, n_mem, N_HEADS_M, HEAD_DIM_M), mv.reshape(1, n_mem, N_HEADS_M, HEAD_DIM_M),
                ks.reshape(db, t, N_HEADS_A, HEAD_DIM_A), vs.reshape(db, t, N_HEADS_A, HEAD_DIM_A),
                kis, rs, cs)
        for o, v in zip(outs, vals):
            o.append(v)
    stacked = [jnp.stack(o) for o in outs]
    return (yp[None], ys, *stacked)
```

```python
import functools
import math

import numpy as np
import jax
import jax.numpy as jnp
from jax import lax
from jax.experimental import pallas as pl
from jax.experimental.pallas import tpu as pltpu

F32 = jnp.float32
I32 = jnp.int32
MXU_DTYPE = jnp.bfloat16

N_HEADS_A, HEAD_DIM_A = 8, 64
IDX_HEADS, IDX_DIM = 4, 64
TOPK_MAX = 256
N_HEADS_R, DK_R, DV_R = 4, 128, 256
RET_CHUNK = 128
N_HEADS_M, HEAD_DIM_M = 4, 128
CONV_W = 3
ROPE_THETA = 10000.0
EPS = 1e-6

LANES = 128
SUBLANES = 8
BF16_ROWS = 16
VMEM_LIMIT = 56 * 1024 * 1024
NEG_BIG = -1e30
F32_LOWEST = float(np.finfo(np.float32).min)
T_PAD = 8


def _cparams(sem):
    return pltpu.CompilerParams(dimension_semantics=sem, vmem_limit_bytes=VMEM_LIMIT)


def _dot(a, b):
    return jnp.dot(a, b, preferred_element_type=F32)


def _dot_nt(a, b):
    return lax.dot_general(a, b, (((1,), (1,)), ((), ())), preferred_element_type=F32)


def _mx(a):
    return a.astype(MXU_DTYPE)


def _rms(x, g):
    return x * lax.rsqrt(jnp.mean(x * x, axis=-1, keepdims=True) + EPS) * g


def _sigmoid(x):
    return 1.0 / (1.0 + jnp.exp(-x))


def _rope_tables(pos, d):
    half = d // 2
    inv = 1.0 / (ROPE_THETA ** (jnp.arange(half, dtype=F32) * 2.0 / d))
    ang = pos.astype(F32)[:, None] * inv[None, :]
    cos, sin = jnp.cos(ang), jnp.sin(ang)
    reps = LANES // d
    cos_t = jnp.tile(jnp.concatenate([cos, cos], axis=1), (1, reps))
    sin_t = jnp.tile(jnp.concatenate([-sin, sin], axis=1), (1, reps))
    return cos_t, sin_t


def _rope(y, cos, sin, d):
    w = y.shape[1]
    half = d // 2
    reps = w // LANES
    c = jnp.concatenate([cos] * reps, axis=1) if reps > 1 else cos
    s = jnp.concatenate([sin] * reps, axis=1) if reps > 1 else sin
    lane = lax.broadcasted_iota(I32, y.shape, 1)
    first = (lane & (d - 1)) < half
    rot = jnp.where(first, pltpu.roll(y, w - half, 1), pltpu.roll(y, half, 1))
    return y * c + rot * s


def _proj_a_kernel(x_ref, g_ref, cos_ref, sin_ref, wa_ref, wv_ref,
                   qa_hm, qa_t, ka_f, ka_hm, qi_hm, ki_f, ki_b, va_f, va_t, wi_f):
    xn = _mx(_rms(x_ref[...], g_ref[...]))
    wa = N_HEADS_A * HEAD_DIM_A
    y = _rope(_dot(xn, wa_ref[...]), cos_ref[...], sin_ref[...], HEAD_DIM_A)
    q = y[:, :wa] * (HEAD_DIM_A ** -0.5)
    k = y[:, wa:2 * wa]
    ka_f[...] = k
    qt = q.T
    for h in range(N_HEADS_A):
        sl = slice(h * HEAD_DIM_A, (h + 1) * HEAD_DIM_A)
        qa_hm[h] = _mx(q[:, sl])
        qa_t[h] = _mx(qt[sl, :])
        ka_hm[h] = _mx(k[:, sl])
    qi = y[:, 2 * wa:2 * wa + IDX_HEADS * IDX_DIM]
    for h in range(IDX_HEADS):
        qi_hm[h] = _mx(qi[:, h * IDX_DIM:(h + 1) * IDX_DIM])
    ki = y[:, 2 * wa + IDX_HEADS * IDX_DIM:2 * wa + IDX_HEADS * IDX_DIM + IDX_DIM]
    ki_f[...] = ki
    ki_b[...] = _mx(ki)
    z = _dot(xn, wv_ref[...])
    v = z[:, :wa]
    va_f[...] = v
    vt = v.T
    for h in range(N_HEADS_A):
        va_t[h] = _mx(vt[h * HEAD_DIM_A:(h + 1) * HEAD_DIM_A, :])
    wi_f[...] = z[:, wa:wa + LANES]


def _proj_b_kernel(x_ref, g_ref, cos_ref, sin_ref, wr_ref, wvg_ref, qr_f, kr_f, vr_b, gr_f):
    xn = _mx(_rms(x_ref[...], g_ref[...]))
    wr = N_HEADS_R * DK_R
    y = _rope(_dot(xn, wr_ref[...]), cos_ref[...], sin_ref[...], DK_R)
    qr_f[...] = y[:, :wr]
    kr_f[...] = y[:, wr:] * (DK_R ** -0.5)
    z = _dot(xn, wvg_ref[...])
    wv = N_HEADS_R * DV_R
    vr_b[...] = _mx(z[:, :wv])
    gr_f[...] = z[:, wv:]


def _proj_c_kernel(x_ref, g_ref, wc_ref, qm_b, gates_f):
    xn = _mx(_rms(x_ref[...], g_ref[...]))
    z = _dot(xn, wc_ref[...])
    wm = N_HEADS_M * HEAD_DIM_M
    qm_b[...] = _mx(z[:, :wm])
    gates_f[...] = z[:, wm:]


def _row_spec(tm, w):
    return pl.BlockSpec((tm, w), lambda i: (i, 0))


def _full_spec(shape):
    nd = len(shape)
    return pl.BlockSpec(shape, lambda i: (0,) * nd)


def _hm_spec(nh, tm, d):
    return pl.BlockSpec((nh, tm, d), lambda i: (0, i, 0))


def _projections(x, gain, pos, wts, tm):
    rows, d = x.shape
    grid = (rows // tm,)
    wa = N_HEADS_A * HEAD_DIM_A
    cos64, sin64 = _rope_tables(pos, HEAD_DIM_A)
    cos128, sin128 = _rope_tables(pos, DK_R)
    g2 = gain.reshape(1, d)
    sds = jax.ShapeDtypeStruct
    t_spec = pl.BlockSpec((N_HEADS_A, HEAD_DIM_A, tm), lambda i: (0, 0, i))
    outs_a = pl.pallas_call(
        _proj_a_kernel,
        grid=grid,
        in_specs=[_row_spec(tm, d), _full_spec((1, d)), _row_spec(tm, LANES), _row_spec(tm, LANES),
                  _full_spec(wts["wa"].shape), _full_spec(wts["wv"].shape)],
        out_specs=[_hm_spec(N_HEADS_A, tm, HEAD_DIM_A), t_spec, _row_spec(tm, wa),
                   _hm_spec(N_HEADS_A, tm, HEAD_DIM_A),
                   _hm_spec(IDX_HEADS, tm, IDX_DIM), _row_spec(tm, IDX_DIM), _row_spec(tm, IDX_DIM),
                   _row_spec(tm, wa), t_spec, _row_spec(tm, LANES)],
        out_shape=[sds((N_HEADS_A, rows, HEAD_DIM_A), MXU_DTYPE), sds((N_HEADS_A, HEAD_DIM_A, rows), MXU_DTYPE),
                   sds((rows, wa), F32),
                   sds((N_HEADS_A, rows, HEAD_DIM_A), MXU_DTYPE), sds((IDX_HEADS, rows, IDX_DIM), MXU_DTYPE),
                   sds((rows, IDX_DIM), F32), sds((rows, IDX_DIM), MXU_DTYPE),
                   sds((rows, wa), F32), sds((N_HEADS_A, HEAD_DIM_A, rows), MXU_DTYPE), sds((rows, LANES), F32)],
        compiler_params=_cparams(("parallel",)),
        name="proj_a",
    )(x, g2, cos64, sin64, wts["wa"], wts["wv"])
    names_a = ("qa_hm", "qa_t", "ka_f", "ka_hm", "qi_hm", "ki_f", "ki_b", "va_f", "va_t", "wi_f")
    wr, wv = N_HEADS_R * DK_R, N_HEADS_R * DV_R
    outs_b = pl.pallas_call(
        _proj_b_kernel,
        grid=grid,
        in_specs=[_row_spec(tm, d), _full_spec((1, d)), _row_spec(tm, LANES), _row_spec(tm, LANES),
                  _full_spec(wts["wr"].shape), _full_spec(wts["wvg"].shape)],
        out_specs=[_row_spec(tm, wr), _row_spec(tm, wr), _row_spec(tm, wv), _row_spec(tm, wv)],
        out_shape=[sds((rows, wr), F32), sds((rows, wr), F32), sds((rows, wv), MXU_DTYPE), sds((rows, wv), F32)],
        compiler_params=_cparams(("parallel",)),
        name="proj_b",
    )(x, g2, cos128, sin128, wts["wr"], wts["wvg"])
    names_b = ("qr_f", "kr_f", "vr_b", "gr_f")
    wm = N_HEADS_M * HEAD_DIM_M
    outs_c = pl.pallas_call(
        _proj_c_kernel,
        grid=grid,
        in_specs=[_row_spec(tm, d), _full_spec((1, d)), _full_spec(wts["wc"].shape)],
        out_specs=[_row_spec(tm, wm), _row_spec(tm, 3 * d)],
        out_shape=[sds((rows, wm), MXU_DTYPE), sds((rows, 3 * d), F32)],
        compiler_params=_cparams(("parallel",)),
        name="proj_c",
    )(x, g2, wts["wc"])
    names_c = ("qm_b", "gates_f")
    out = dict(zip(names_a, outs_a))
    out.update(zip(names_b, outs_b))
    out.update(zip(names_c, outs_c))
    return out


def _prep_in_weights(w_in):
    d = w_in.shape[0]
    wa = N_HEADS_A * HEAD_DIM_A
    widths = (wa, wa, wa, IDX_HEADS * IDX_DIM, IDX_DIM, IDX_HEADS,
              N_HEADS_R * DK_R, N_HEADS_R * DK_R, N_HEADS_R * DV_R, N_HEADS_R * DV_R,
              N_HEADS_M * HEAD_DIM_M, 3 * d)
    offs = np.concatenate([[0], np.cumsum(widths)])
    seg = [w_in[:, int(offs[i]):int(offs[i + 1])] for i in range(len(widths))]
    q_a, k_a, v_a, q_i, k_i, w_i, q_r, k_r, v_r, g_r, q_m, gates = seg
    zpad = lambda n: jnp.zeros((d, n), w_in.dtype)
    return {
        "wa": _mx(jnp.concatenate([q_a, k_a, q_i, k_i, zpad(LANES - IDX_DIM)], axis=1)),
        "wv": _mx(jnp.concatenate([v_a, w_i, zpad(LANES - IDX_HEADS)], axis=1)),
        "wr": _mx(jnp.concatenate([q_r, k_r], axis=1)),
        "wvg": _mx(jnp.concatenate([v_r, g_r], axis=1)),
        "wc": _mx(jnp.concatenate([q_m, gates], axis=1)),
    }


SEL_ROWS = 128
SEL_WIDE = 512
SEL_GROUPS = 2 * LANES
NO_LIMIT = 2 ** 30
MIN_NORMAL_KEY = 0x00800000
SEARCH_PERIOD = 4
SEARCH_CAP = SEARCH_PERIOD * 33


def _key_to_f32(key):
    bits = jnp.where(key >= 0, key, key ^ jnp.int32(0x7FFFFFFF))
    return pltpu.bitcast(bits, F32)


def _f32_to_key(f):
    bits = pltpu.bitcast(f, I32)
    return jnp.where(f == 0.0, 0, jnp.where(bits >= 0, bits, bits ^ jnp.int32(0x7FFFFFFF)))


def _sweep(segments, rs, init, fn):
    acc = init
    base = 0
    for ref, n_wide in segments:
        def body(c, a, ref=ref, base=base):
            off = c * SEL_WIDE
            for k in range(SEL_WIDE // LANES):
                x = ref[rs, pl.ds(pl.multiple_of(off + k * LANES, LANES), LANES)]
                a = fn(a, x, base + off + k * LANES, k)
            return a
        pairs = n_wide // 2
        acc = lax.fori_loop(0, pairs, lambda c, a, body=body: body(2 * c + 1, body(2 * c, a)), acc)
        acc = lax.fori_loop(2 * pairs, n_wide, body, acc)
        base = base + n_wide * SEL_WIDE
    return acc


def _count(segments, rows, make_pred):
    outs = []
    for g in range(rows // SEL_ROWS):
        rs = slice(g * SEL_ROWS, (g + 1) * SEL_ROWS)
        pred = make_pred(rs)
        acc = _sweep(segments, rs, jnp.zeros((SEL_ROWS, LANES), F32),
                     lambda a, x, idx0, k: a + jnp.where(pred(x, idx0), 1.0, 0.0))
        outs.append(jnp.sum(acc.T, axis=0, keepdims=True))
    return jnp.concatenate(outs, axis=1) if len(outs) > 1 else outs[0]


def _col_to_row(col):
    return jnp.broadcast_to(col, (col.shape[0], LANES)).T[0:1]


def _row_to_lanes(vec):
    return jnp.broadcast_to(vec, (LANES, vec.shape[1])).T


def _search(count_fn, lo, hi, c_lo, c_hi, target, alive):
    def unfinished(lo, hi, c_lo):
        return jnp.logical_and(alive, jnp.logical_and(c_lo > target, lo + 1 < hi))

    def any_row(flag):
        return jnp.max(jnp.where(flag, 1, 0).astype(I32))

    def cond(carry):
        return jnp.logical_and(carry[0] < SEARCH_CAP, carry[1] > 0)

    def body(carry):
        it, _, phase, lo, hi, c_lo, c_hi, w_lo, w_hi, last = carry
        act = unfinished(lo, hi, c_lo)
        bis = (lo >> 1) + (hi >> 1) + (lo & hi & 1)
        width = hi - lo
        a = (c_lo - target + 0.5) * w_lo
        b = (target - 0.5 - c_hi) * w_hi
        frac = a / jnp.maximum(a + b, 1e-6)
        step = (frac * width.astype(F32)).astype(I32)
        itp = lo + jnp.clip(step, 1, jnp.maximum(width - 1, 1))
        use_itp = jnp.logical_and((lo ^ hi) >= 0, (jnp.zeros_like(lo) + phase) != SEARCH_PERIOD - 1)
        v = jnp.where(use_itp, itp, bis)
        cnt = count_fn(v)
        up = jnp.logical_and(act, cnt >= target)
        dn = jnp.logical_and(act, cnt < target)
        lo = jnp.where(up, v, lo)
        c_lo = jnp.where(up, cnt, c_lo)
        hi = jnp.where(dn, v, hi)
        c_hi = jnp.where(dn, cnt, c_hi)
        w_hi = jnp.where(up, jnp.where(last == 1, w_hi * 0.5, 1.0), jnp.where(dn, 1.0, w_hi))
        w_lo = jnp.where(dn, jnp.where(last == -1, w_lo * 0.5, 1.0), jnp.where(up, 1.0, w_lo))
        last = jnp.where(up, 1, jnp.where(dn, -1, last))
        phase = jnp.where(phase == SEARCH_PERIOD - 1, 0, phase + 1)
        return it + 1, any_row(unfinished(lo, hi, c_lo)), phase, lo, hi, c_lo, c_hi, w_lo, w_hi, last

    go = any_row(unfinished(lo, hi, c_lo))
    one = jnp.ones(lo.shape, F32)
    out = lax.while_loop(cond, body, (jnp.int32(0), go, jnp.int32(0), lo, hi, c_lo, c_hi,
                                      one, one, jnp.zeros(lo.shape, I32)))
    return out[3], out[5], out[6]


def _select_threshold(segments, rows, topk, alive=None):
    assert topk <= SEL_GROUPS
    kf = jnp.full((1, rows), float(topk), F32)
    if alive is None:
        alive = jnp.full((1, rows), True)

    los, his = [], []
    for g in range(rows // SEL_ROWS):
        rs = slice(g * SEL_ROWS, (g + 1) * SEL_ROWS)
        ninf = jnp.full((SEL_ROWS, LANES), -jnp.inf, F32)
        ga, gb = _sweep(segments, rs, (ninf, ninf),
                        lambda a, x, idx0, k: ((jnp.maximum(a[0], x), a[1]) if k % 2 == 0
                                               else (a[0], jnp.maximum(a[1], x))))
        los.append(jnp.min(jnp.minimum(ga, gb), axis=1, keepdims=True))
        his.append(jnp.max(jnp.maximum(ga, gb), axis=1, keepdims=True))
    cat = lambda xs: _col_to_row(jnp.concatenate(xs, axis=0) if len(xs) > 1 else xs[0])
    lo = _f32_to_key(jnp.maximum(cat(los), F32_LOWEST))
    hi = _f32_to_key(cat(his)) + 1

    def count_ge_f(thr):
        thr_l = _row_to_lanes(thr)

        def make_pred(rs):
            t = thr_l[rs]
            return lambda x, idx0: x >= t
        return _count(segments, rows, make_pred)

    count_ge = lambda v: count_ge_f(_key_to_f32(v))
    c_lo = count_ge(lo)
    c_hi = jnp.zeros((1, rows), F32)
    def zero_probes(state):
        lo, hi, c_lo, c_hi = state
        for probe in (0, MIN_NORMAL_KEY):
            v = jnp.full((1, rows), probe, I32)
            cnt = count_ge_f(jnp.full((1, rows), np.int32(probe).view(np.float32), F32))
            inside = jnp.logical_and(lo < v, v < hi)
            up = jnp.logical_and(inside, cnt >= kf)
            dn = jnp.logical_and(inside, cnt < kf)
            lo, c_lo = jnp.where(up, v, lo), jnp.where(up, cnt, c_lo)
            hi, c_hi = jnp.where(dn, v, hi), jnp.where(dn, cnt, c_hi)
        return lo, hi, c_lo, c_hi

    straddles = jnp.logical_or(jnp.logical_and(lo < 0, hi > 0),
                               jnp.logical_and(lo < MIN_NORMAL_KEY, hi > MIN_NORMAL_KEY))
    lo, hi, c_lo, c_hi = lax.cond(jnp.max(jnp.where(straddles, 1, 0).astype(I32)) > 0,
                                  zero_probes, lambda s: s, (lo, hi, c_lo, c_hi))
    at_zero = jnp.logical_and(lo == 0, hi == MIN_NORMAL_KEY)
    lo, c_lo, c_hi = _search(count_ge, lo, hi, c_lo, c_hi, kf, jnp.logical_and(alive, jnp.logical_not(at_zero)))
    tau = _key_to_f32(lo)

    need = jnp.logical_and(alive, c_lo > kf)
    room = jnp.where(need, kf - c_hi, float(NO_LIMIT))
    return _row_to_lanes(tau), _row_to_lanes(room)


def _write_mask(segments_out, rows, tau, room):
    r_io = lax.broadcasted_iota(I32, (LANES, 2 * LANES), 0)
    c_io = lax.broadcasted_iota(I32, (LANES, 2 * LANES), 1)
    tri = jnp.where(jnp.logical_or(r_io <= c_io, c_io >= LANES), 1.0, 0.0).astype(MXU_DTYPE)
    for g in range(rows // SEL_ROWS):
        rs = slice(g * SEL_ROWS, (g + 1) * SEL_ROWS)
        t, rm = tau[rs], room[rs]
        seen = jnp.zeros((SEL_ROWS, LANES), F32)
        for src, dst, n_wide, n_total in segments_out:
            def body(c, seen, src=src, dst=dst):
                off = c * SEL_WIDE
                nsub = SEL_WIDE // LANES
                dss = [pl.ds(pl.multiple_of(off + k * LANES, LANES), LANES) for k in range(nsub)]
                xs = [src[rs, ds] for ds in dss]
                eqs = [jnp.where(x == t, 1.0, 0.0) for x in xs]
                run = jnp.dot(jnp.concatenate(eqs, axis=0).astype(MXU_DTYPE), tri, preferred_element_type=F32)
                for k in range(nsub):
                    run_k = run[k * SEL_ROWS:(k + 1) * SEL_ROWS]
                    before = seen + run_k[:, :LANES] - eqs[k]
                    keep_eq = jnp.where(before < rm, eqs[k], 0.0)
                    dst[rs, dss[k]] = jnp.where(xs[k] > t, 1.0, keep_eq).astype(dst.dtype)
                    seen = seen + run_k[:, LANES:]
                return seen

            seen = lax.fori_loop(0, n_wide, body, seen)

            def zbody(c, carry, dst=dst):
                off = pl.multiple_of(c * SEL_WIDE, SEL_WIDE)
                dst[rs, pl.ds(off, SEL_WIDE)] = jnp.zeros((SEL_ROWS, SEL_WIDE), dst.dtype)
                return carry

            lax.fori_loop(n_wide, n_total, zbody, 0)


def _index_scores(qi_ref, w, kb, transposed_keys=False):
    acc = None
    for h in range(IDX_HEADS):
        s = _dot(_mx(qi_ref[h]), kb) if transposed_keys else _dot_nt(_mx(qi_ref[h]), kb)
        t = w[:, h:h + 1] * jnp.maximum(s, 0.0)
        acc = t if acc is None else acc + t
    return acc


def _prompt_select_kernel(qi_ref, wi_ref, kidx_ref, mask_ref, i_scr, *, tq, tk, topk):
    i, j = pl.program_id(0), pl.program_id(1)
    nk = pl.num_programs(1)
    q_lo = i * tq
    n_wide = (q_lo + tq - 1) // tk + 1

    @pl.when(j < n_wide)
    def _():
        acc = _index_scores(qi_ref, wi_ref[...], kidx_ref[...])
        qpos = q_lo + lax.broadcasted_iota(I32, acc.shape, 0)
        kpos = j * tk + lax.broadcasted_iota(I32, acc.shape, 1)
        i_scr[:, pl.ds(pl.multiple_of(j * tk, tk), tk)] = jnp.where(kpos <= qpos, acc, -jnp.inf)

    @pl.when(j == nk - 1)
    def _():
        seg = [(i_scr, n_wide)]
        tau, room = _select_threshold(seg, tq, topk)
        _write_mask([(i_scr, mask_ref, n_wide, nk)], tq, tau, room)


def _prompt_select(qi_hm, wi_f, ki_b, topk, tq, tk):
    s = ki_b.shape[0]
    assert tk == SEL_WIDE and s % tk == 0 and s % tq == 0 and tq % SEL_ROWS == 0
    nq, nk = s // tq, s // tk
    kmap = lambda i, j: (jnp.minimum(j, (i * tq + tq - 1) // tk), 0)
    return pl.pallas_call(
        functools.partial(_prompt_select_kernel, tq=tq, tk=tk, topk=topk),
        grid=(nq, nk),
        in_specs=[pl.BlockSpec((IDX_HEADS, tq, IDX_DIM), lambda i, j: (0, i, 0)),
                  pl.BlockSpec((tq, LANES), lambda i, j: (i, 0)),
                  pl.BlockSpec((tk, IDX_DIM), kmap)],
        out_specs=pl.BlockSpec((tq, s), lambda i, j: (i, 0)),
        out_shape=jax.ShapeDtypeStruct((s, s), MXU_DTYPE),
        scratch_shapes=[pltpu.VMEM((tq, s), F32)],
        compiler_params=_cparams(("parallel", "arbitrary")),
        name="prompt_select",
    )(qi_hm, wi_f, ki_b)


def _spread(a, n):
    if n <= LANES:
        return a[:, :n]
    return jnp.concatenate([a] * (n // LANES), axis=1)


def _flash_update(s, v, m_prev, l_prev, acc_prev):
    m_new = jnp.maximum(m_prev, jnp.max(s, axis=1, keepdims=True))
    alpha = jnp.exp(m_prev - m_new)
    p = jnp.exp(s - _spread(m_new, s.shape[1]))
    l_new = alpha * l_prev + jnp.sum(p, axis=1, keepdims=True)
    acc_new = _spread(alpha, acc_prev.shape[1]) * acc_prev + _dot(_mx(p), v)
    return m_new, l_new, acc_new


def _prompt_attend_kernel(qt_ref, k_ref, vt_ref, mask_ref, o_ref, m_scr, l_scr, acc_scr, *, tq, tk):
    i, j = pl.program_id(0), pl.program_id(1)
    nk = pl.num_programs(1)

    @pl.when(j == 0)
    def _():
        m_scr[...] = jnp.full(m_scr.shape, NEG_BIG, F32)
        l_scr[...] = jnp.zeros(l_scr.shape, F32)
        acc_scr[...] = jnp.zeros(acc_scr.shape, F32)

    @pl.when(j * tk <= i * tq + tq - 1)
    def _():
        bias = (1.0 - mask_ref[...].astype(F32).T) * NEG_BIG
        scores = [_dot(k_ref[h], qt_ref[h]) + bias for h in range(N_HEADS_A)]
        for h, s in enumerate(scores):
            m_prev = m_scr[h]
            m_new = jnp.maximum(m_prev, jnp.max(s, axis=0, keepdims=True))
            alpha = jnp.exp(m_prev - m_new)
            p = jnp.exp(s - m_new[0:1])
            l_scr[h] = alpha * l_scr[h] + jnp.sum(p, axis=0, keepdims=True)
            acc_scr[h] = alpha[0:1] * acc_scr[h] + _dot(vt_ref[h], _mx(p))
            m_scr[h] = m_new

    @pl.when(j == nk - 1)
    def _():
        ot = jnp.concatenate([acc_scr[h] / l_scr[h][0:1] for h in range(N_HEADS_A)], axis=0)
        o_ref[...] = ot.T.astype(o_ref.dtype)


def _prompt_attend(qa_t, ka_hm, va_t, mask, tq, tk):
    nh, dh, s = qa_t.shape
    nq, nk = s // tq, s // tk
    diag = lambda i, j: jnp.minimum(j, (i * tq + tq - 1) // tk)
    return pl.pallas_call(
        functools.partial(_prompt_attend_kernel, tq=tq, tk=tk),
        grid=(nq, nk),
        in_specs=[pl.BlockSpec((nh, dh, tq), lambda i, j: (0, 0, i)),
                  pl.BlockSpec((nh, tk, dh), lambda i, j: (0, diag(i, j), 0)),
                  pl.BlockSpec((nh, dh, tk), lambda i, j: (0, 0, diag(i, j))),
                  pl.BlockSpec((tq, tk), lambda i, j: (i, diag(i, j)))],
        out_specs=pl.BlockSpec((tq, nh * dh), lambda i, j: (i, 0)),
        out_shape=jax.ShapeDtypeStruct((s, nh * dh), MXU_DTYPE),
        scratch_shapes=[pltpu.VMEM((nh, SUBLANES, tq), F32), pltpu.VMEM((nh, SUBLANES, tq), F32),
                        pltpu.VMEM((nh, dh, tq), F32)],
        compiler_params=_cparams(("parallel", "arbitrary")),
        name="prompt_attend",
    )(qa_t, ka_hm, va_t, mask)


PAGES_PER_STEP = 8
INDEX_PAGES_PER_STEP = 16


def _sample_index_kernel(pt_ref, qi_ref, wi_ref, *refs, page):
    del pt_ref
    pages, out_ref = refs[:-1], refs[-1]
    w = wi_ref[0]
    for p, kref in enumerate(pages):
        out_ref[0, :, p * page:(p + 1) * page] = _index_scores(qi_ref, w, _mx(kref[0]), transposed_keys=True)


def _sample_index(page_table, qi_hm, wi_f, cache_kidx_t):
    db, n_pages = page_table.shape
    _, idim, page = cache_kidx_t.shape
    pps = math.gcd(INDEX_PAGES_PER_STEP, n_pages)
    nsteps = n_pages // pps
    pt = page_table.reshape(-1).astype(I32)

    def kspec(p):
        return pl.BlockSpec((1, idim, page), lambda b, j, pt: (pt[b * n_pages + j * pps + p], 0, 0))

    grid_spec = pltpu.PrefetchScalarGridSpec(
        num_scalar_prefetch=1,
        grid=(db, nsteps),
        in_specs=[pl.BlockSpec((IDX_HEADS, T_PAD, idim), lambda b, j, pt: (0, b, 0)),
                  pl.BlockSpec((1, T_PAD, LANES), lambda b, j, pt: (b, 0, 0))]
                 + [kspec(p) for p in range(pps)],
        out_specs=pl.BlockSpec((1, T_PAD, pps * page), lambda b, j, pt: (b, 0, j)),
    )
    return pl.pallas_call(
        functools.partial(_sample_index_kernel, page=page),
        grid_spec=grid_spec,
        out_shape=jax.ShapeDtypeStruct((db, T_PAD, n_pages * page), F32),
        compiler_params=_cparams(("parallel", "arbitrary")),
        name="sample_index",
    )(pt, qi_hm, wi_f.reshape(db, T_PAD, LANES), *([cache_kidx_t] * pps))


def _sample_select_kernel(ipast_ref, qi_ref, wi_ref, kin_ref, mpast_ref, mnew_ref, inew_scr, *, t_real, topk):
    rows = ipast_ref.shape[0]
    acc = _index_scores(qi_ref, wi_ref[...], kin_ref[...])
    r = lax.broadcasted_iota(I32, acc.shape, 0)
    c = lax.broadcasted_iota(I32, acc.shape, 1)
    same = (r // T_PAD) == (c // T_PAD)
    tq, tc = r % T_PAD, c % T_PAD
    ok = jnp.logical_and(same, jnp.logical_and(tc <= tq, tc < t_real))
    inew_scr[...] = jnp.full(inew_scr.shape, -jnp.inf, F32)
    inew_scr[:, :rows] = jnp.where(ok, acc, -jnp.inf)
    n_past = ipast_ref.shape[1] // SEL_WIDE
    n_new = inew_scr.shape[1] // SEL_WIDE
    alive = lax.rem(lax.broadcasted_iota(I32, (1, rows), 1), T_PAD) < t_real
    tau, room = _select_threshold([(ipast_ref, n_past), (inew_scr, n_new)], rows, topk, alive)
    _write_mask([(ipast_ref, mpast_ref, n_past, n_past), (inew_scr, mnew_ref, n_new, n_new)], rows, tau, room)


def _sample_select(i_past, qi_hm, wi_f, ki_b, t_real, topk):
    rows, past = i_past.shape
    assert past % SEL_WIDE == 0 and rows % SEL_ROWS == 0
    wnew = -(-rows // SEL_WIDE) * SEL_WIDE
    return pl.pallas_call(
        functools.partial(_sample_select_kernel, t_real=t_real, topk=topk),
        out_shape=[jax.ShapeDtypeStruct((rows, past), F32), jax.ShapeDtypeStruct((rows, wnew), F32)],
        scratch_shapes=[pltpu.VMEM((rows, wnew), F32)],
        compiler_params=pltpu.CompilerParams(vmem_limit_bytes=VMEM_LIMIT),
        name="sample_select",
    )(i_past, qi_hm, wi_f, ki_b)


def _sample_attend_kernel(pt_ref, q_ref, mp_ref, mn_ref, kn_ref, vn_ref, *refs, page, pps):
    del pt_ref
    kpages, vpages = refs[:pps], refs[pps:2 * pps]
    o_ref, m_scr, l_scr, acc_scr = refs[2 * pps:]
    j = pl.program_id(1)
    nj = pl.num_programs(1)
    q = q_ref[0]
    nh = q.shape[0] // T_PAD

    def scores(m_t, kt):
        keep = jnp.concatenate([m_t] * nh, axis=0) > 0
        return jnp.where(keep, _dot(q, kt), NEG_BIG)

    def update(s_list, vt_list):
        m_prev = m_scr[...]
        m_blk = functools.reduce(jnp.maximum, [jnp.max(s, axis=1, keepdims=True) for s in s_list])
        m_new = jnp.maximum(m_prev, m_blk)
        alpha = jnp.exp(m_prev - m_new)
        l_new = alpha * l_scr[...]
        acc = _spread(alpha, acc_scr.shape[1]) * acc_scr[...]
        for s, vt in zip(s_list, vt_list):
            p = jnp.exp(s - _spread(m_new, s.shape[1]))
            l_new = l_new + jnp.sum(p, axis=1, keepdims=True)
            acc = acc + _dot_nt(_mx(p), vt)
        m_scr[...], l_scr[...], acc_scr[...] = m_new, l_new, acc

    @pl.when(j == 0)
    def _():
        m_scr[...] = jnp.full(m_scr.shape, NEG_BIG, F32)
        l_scr[...] = jnp.zeros(l_scr.shape, F32)
        acc_scr[...] = jnp.zeros(acc_scr.shape, F32)

    update([scores(mp_ref[0, :, p * page:(p + 1) * page], _mx(kpages[p][0])) for p in range(pps)],
           [_mx(vpages[p][0]) for p in range(pps)])

    @pl.when(j == nj - 1)
    def _():
        update([scores(mn_ref[0], kn_ref[0])], [vn_ref[0]])
        full = acc_scr[...] / _spread(l_scr[...], acc_scr.shape[1])
        lane = lax.broadcasted_iota(I32, (T_PAD, full.shape[1]), 1)
        out = jnp.zeros((T_PAD, full.shape[1]), F32)
        for h in range(nh):
            out = out + jnp.where((lane // HEAD_DIM_A) == h, full[h * T_PAD:(h + 1) * T_PAD], 0.0)
        o_ref[0] = out


def _sample_attend(page_table, q_bd, m_past, m_new, kt_new, vt_new, cache_kt, cache_vt):
    db, n_pages = page_table.shape
    _, hd, page = cache_kt.shape
    pps = math.gcd(PAGES_PER_STEP, n_pages)
    nsteps = n_pages // pps
    nnew = kt_new.shape[2]
    nq = q_bd.shape[1]
    pt = page_table.reshape(-1).astype(I32)

    def pspec(p):
        return pl.BlockSpec((1, hd, page), lambda b, j, pt: (pt[b * n_pages + j * pps + p], 0, 0))

    bspec = lambda shape: pl.BlockSpec((1,) + shape, lambda b, j, pt: (b, 0, 0))
    grid_spec = pltpu.PrefetchScalarGridSpec(
        num_scalar_prefetch=1,
        grid=(db, nsteps),
        in_specs=[bspec((nq, hd)),
                  pl.BlockSpec((1, T_PAD, pps * page), lambda b, j, pt: (b, 0, j)),
                  bspec((T_PAD, nnew)), bspec((hd, nnew)), bspec((hd, nnew))]
                 + [pspec(p) for p in range(pps)] * 2,
        out_specs=bspec((T_PAD, hd)),
        scratch_shapes=[pltpu.VMEM((nq, LANES), F32), pltpu.VMEM((nq, LANES), F32), pltpu.VMEM((nq, hd), F32)],
    )
    return pl.pallas_call(
        functools.partial(_sample_attend_kernel, page=page, pps=pps),
        grid_spec=grid_spec,
        out_shape=jax.ShapeDtypeStruct((db, T_PAD, hd), F32),
        compiler_params=_cparams(("parallel", "arbitrary")),
        name="sample_attend",
    )(pt, q_bd, m_past, m_new, kt_new, vt_new, *([cache_kt] * pps), *([cache_vt] * pps))


def _retention_tables(c_real, c_pad):
    h = np.arange(N_HEADS_R, dtype=np.float64)
    log_g = np.log1p(-np.exp2(-5.0 - h))
    i = np.arange(c_pad, dtype=np.float64)
    diff = i[:, None] - i[None, :]
    live = (diff >= 0) & (i[:, None] < c_real) & (i[None, :] < c_real)
    inner = np.where(live[None], np.exp(np.maximum(diff, 0.0)[None] * log_g[:, None, None]), 0.0)
    q_dec = np.exp((i + 1.0)[None, :] * log_g[:, None])
    k_dec = np.where(i[None, :] < c_real, np.exp((c_real - 1.0 - i)[None, :] * log_g[:, None]), 0.0)
    c_dec = np.exp(c_real * log_g)
    f = lambda a: jnp.asarray(a, F32)
    return f(inner), f(q_dec[:, :, None]), f(k_dec[:, :, None]), [float(v) for v in c_dec]


def _retention_kernel(q_ref, k_ref, v_ref, g_ref, s0_ref, inner_ref, qdec_ref, kdec_ref,
                      o_ref, s_out_ref, s_scr, *, c_dec):
    j = pl.program_id(1)
    nj = pl.num_programs(1)

    @pl.when(j == 0)
    def _():
        s_scr[...] = s0_ref[0]

    for h in range(N_HEADS_R):
        q = q_ref[:, h * DK_R:(h + 1) * DK_R]
        k = k_ref[:, h * DK_R:(h + 1) * DK_R]
        v = v_ref[:, h * DV_R:(h + 1) * DV_R]
        s_prev = s_scr[h]
        a = _dot_nt(_mx(q), _mx(k)) * inner_ref[h]
        o = _dot(_mx(a), v) + _dot(_mx(q), _mx(s_prev)) * qdec_ref[h]
        kd = k * kdec_ref[h]
        s_scr[h] = s_prev * c_dec[h] + _dot(_mx(kd.T), v)
        mu = jnp.mean(o, axis=-1, keepdims=True)
        var = jnp.mean(jnp.square(o - mu), axis=-1, keepdims=True)
        gn = (o - mu) * lax.rsqrt(var + EPS)
        g = g_ref[:, h * DV_R:(h + 1) * DV_R]
        o_ref[:, h * DV_R:(h + 1) * DV_R] = (gn * (g * _sigmoid(g))).astype(o_ref.dtype)

    @pl.when(j == nj - 1)
    def _():
        s_out_ref[0] = s_scr[...]


def _retention(qr, kr, vr, gr, s0, c_real):
    b = s0.shape[0]
    c = RET_CHUNK
    n = qr.shape[0] // (b * c)
    inner, qdec, kdec, c_dec = _retention_tables(c_real, c)
    wr, wv = N_HEADS_R * DK_R, N_HEADS_R * DV_R
    rmap = lambda bi, j: (bi * n + j, 0)
    full3 = lambda shape: pl.BlockSpec(shape, lambda bi, j: (0, 0, 0))
    return pl.pallas_call(
        functools.partial(_retention_kernel, c_dec=c_dec),
        grid=(b, n),
        in_specs=[pl.BlockSpec((c, wr), rmap), pl.BlockSpec((c, wr), rmap), pl.BlockSpec((c, wv), rmap),
                  pl.BlockSpec((c, wv), rmap),
                  pl.BlockSpec((1, N_HEADS_R, DK_R, DV_R), lambda bi, j: (bi, 0, 0, 0)),
                  full3(inner.shape), full3(qdec.shape), full3(kdec.shape)],
        out_specs=[pl.BlockSpec((c, wv), rmap),
                   pl.BlockSpec((1, N_HEADS_R, DK_R, DV_R), lambda bi, j: (bi, 0, 0, 0))],
        out_shape=[jax.ShapeDtypeStruct((b * n * c, wv), MXU_DTYPE),
                   jax.ShapeDtypeStruct((b, N_HEADS_R, DK_R, DV_R), F32)],
        scratch_shapes=[pltpu.VMEM((N_HEADS_R, DK_R, DV_R), F32)],
        compiler_params=_cparams(("parallel", "arbitrary")),
        name="retention",
    )(qr, kr, vr, gr, s0, inner, qdec, kdec)


def _cross_kernel(q_ref, mk_ref, mv_ref, o_ref):
    scale = HEAD_DIM_M ** -0.5
    for h in range(N_HEADS_M):
        sl = slice(h * HEAD_DIM_M, (h + 1) * HEAD_DIM_M)
        s = _dot_nt(_mx(q_ref[:, sl]), _mx(mk_ref[0, :, sl])) * scale
        p = jnp.exp(s - jnp.max(s, axis=1, keepdims=True))
        p = p / jnp.sum(p, axis=1, keepdims=True)
        o_ref[:, sl] = _dot(_mx(p), _mx(mv_ref[0, :, sl])).astype(o_ref.dtype)


def _cross_attend(qm, mk, mv, tm, out_dtype):
    b, n_mem, hd = mk.shape
    nt = qm.shape[0] // (b * tm)
    return pl.pallas_call(
        _cross_kernel,
        grid=(b, nt),
        in_specs=[pl.BlockSpec((tm, hd), lambda bi, i: (bi * nt + i, 0)),
                  pl.BlockSpec((1, n_mem, hd), lambda bi, i: (bi, 0, 0)),
                  pl.BlockSpec((1, n_mem, hd), lambda bi, i: (bi, 0, 0))],
        out_specs=pl.BlockSpec((tm, hd), lambda bi, i: (bi * nt + i, 0)),
        out_shape=jax.ShapeDtypeStruct(qm.shape, out_dtype),
        compiler_params=_cparams(("parallel", "parallel")),
        name="cross_attend",
    )(qm, mk, mv)


def _memkv_kernel(x_ref, g_ref, w_ref, o_ref):
    o_ref[...] = _dot(_mx(_rms(x_ref[...], g_ref[...])), w_ref[...])


def _memory_kv(mem, gain, w):
    rows, d = mem.shape
    return pl.pallas_call(
        _memkv_kernel,
        out_shape=jax.ShapeDtypeStruct((rows, w.shape[1]), F32),
        compiler_params=pltpu.CompilerParams(vmem_limit_bytes=VMEM_LIMIT),
        name="memory_kv",
    )(mem, gain.reshape(1, d), _mx(w))


def _merge_kernel(x_ref, oa_ref, or_ref, om_ref, gates_ref, wpa_ref, wpb_ref, wpc_ref, wo_ref, g_ref, h_ref):
    d = x_ref.shape[1]
    gt = gates_ref[...]
    mixed = (_sigmoid(gt[:, :d]) * _dot(_mx(oa_ref[...]), wpa_ref[...])
             + _sigmoid(gt[:, d:2 * d]) * _dot(_mx(or_ref[...]), wpb_ref[...])
             + _sigmoid(gt[:, 2 * d:]) * _dot(_mx(om_ref[...]), wpc_ref[...]))
    z = _dot(_mx(mixed), wo_ref[...])
    h_ref[...] = x_ref[...] + _rms(z, g_ref[...])


def _merge(x, oa, o_r, om, gates, wpa, wpb, wpc, wo, gain, tm):
    rows, d = x.shape
    return pl.pallas_call(
        _merge_kernel,
        grid=(rows // tm,),
        in_specs=[_row_spec(tm, d), _row_spec(tm, oa.shape[1]), _row_spec(tm, o_r.shape[1]),
                  _row_spec(tm, om.shape[1]), _row_spec(tm, 3 * d),
                  _full_spec(wpa.shape), _full_spec(wpb.shape), _full_spec(wpc.shape), _full_spec(wo.shape),
                  _full_spec((1, d))],
        out_specs=_row_spec(tm, d),
        out_shape=jax.ShapeDtypeStruct((rows, d), F32),
        compiler_params=_cparams(("parallel",)),
        name="merge",
    )(x, oa, o_r, om, gates, wpa, wpb, wpc, wo, gain.reshape(1, d))


HALO = BF16_ROWS


def _ffn_kernel(h_ref, halo_ref, s0_ref, s1_ref, g1_ref, g2_ref, wu_ref, wg_ref, cw_ref, cb_ref, wd_ref,
                y_ref, utail_ref, x_scr, u_scr, *, tm, seq, keep):
    i = pl.program_id(0)
    h = h_ref[...]
    hn = _rms(h, g1_ref[...])
    x_scr[HALO:, :] = _mx(hn)
    x_scr[:HALO, :] = _mx(_rms(halo_ref[...], g1_ref[...]))
    xc = x_scr[...]
    u_scr[...] = _dot(xc, wu_ref[...])
    gate = _dot(xc[HALO:], wg_ref[...])
    cur = u_scr[HALO:, :]
    prev1 = u_scr[HALO - 1:HALO - 1 + tm, :]
    prev2 = u_scr[HALO - 2:HALO - 2 + tm, :]
    seq_loc = min(seq, tm)
    t = lax.rem(lax.broadcasted_iota(I32, (tm, 1), 0), seq_loc)
    t = jnp.where(lax.rem(i * tm, seq) == 0, t, CONV_W)
    st0, st1 = s0_ref[...], s1_ref[...]
    if st0.shape[0] != tm:
        st0, st1 = st0[0:1], st1[0:1]
    prev1 = jnp.where(t == 0, st1, prev1)
    prev2 = jnp.where(t == 0, st0, jnp.where(t == 1, st1, prev2))
    c = cb_ref[...] + prev2 * cw_ref[0:1, :] + prev1 * cw_ref[1:2, :] + cur * cw_ref[2:3, :]
    act = jax.nn.gelu(c, approximate=True) * gate
    ff = _dot(_mx(act), wd_ref[...])
    y_ref[...] = h + _rms(ff, g2_ref[...])
    utail_ref[...] = u_scr[HALO + tm - keep:, :]


def _conv_ffn(h, s0e, s1e, g1, g2, wu, wg, cw, cb, wd, tm, seq, keep):
    rows, d = h.shape
    f = wu.shape[1]
    nt = rows // tm
    hb = tm // HALO
    sr = s0e.shape[0]
    return pl.pallas_call(
        functools.partial(_ffn_kernel, tm=tm, seq=seq, keep=keep),
        grid=(nt,),
        in_specs=[_row_spec(tm, d),
                  pl.BlockSpec((HALO, d), lambda i: (jnp.maximum(i * hb - 1, 0), 0)),
                  _full_spec((sr, f)), _full_spec((sr, f)),
                  _full_spec((1, d)), _full_spec((1, d)),
                  _full_spec(wu.shape), _full_spec(wg.shape), _full_spec(cw.shape), _full_spec((1, f)),
                  _full_spec(wd.shape)],
        out_specs=[_row_spec(tm, d), _row_spec(keep, f)],
        out_shape=[jax.ShapeDtypeStruct((rows, d), F32), jax.ShapeDtypeStruct((nt * keep, f), F32)],
        scratch_shapes=[pltpu.VMEM((tm + HALO, d), MXU_DTYPE), pltpu.VMEM((tm + HALO, f), F32)],
        compiler_params=_cparams(("parallel",)),
        name="conv_ffn",
    )(h, h, s0e, s1e, g1.reshape(1, d), g2.reshape(1, d), wu, wg, cw, cb.reshape(1, f), wd)


def _layer_weights(l, w_in, w_proj_a, w_proj_b, w_proj_c, w_out, w_up, w_down):
    wts = _prep_in_weights(w_in[l])
    f = w_down.shape[1]
    wts.update(wpa=_mx(w_proj_a[l]), wpb=_mx(w_proj_b[l]), wpc=_mx(w_proj_c[l]), wo=_mx(w_out[l]),
               wu=_mx(w_up[l][:, :f]), wg=_mx(w_up[l][:, f:]), wd=_mx(w_down[l]))
    return wts


def _prompt_layer(x, mem, wts, norms, conv_w, conv_b, w_mem_kv, tiles):
    s, d = x.shape
    f = wts["wd"].shape[0]
    pos = jnp.arange(s)
    pr = _projections(x, norms["pre_mix"], pos, wts, tiles["proj"])
    topk = min(TOPK_MAX, s // 4)
    mask = _prompt_select(pr["qi_hm"], pr["wi_f"], pr["ki_b"], topk, tiles["sel_q"], SEL_WIDE)
    o_a = _prompt_attend(pr["qa_t"], pr["ka_hm"], pr["va_t"], mask, tiles["att_q"], tiles["att_k"])
    s0 = jnp.zeros((1, N_HEADS_R, DK_R, DV_R), F32)
    o_r, ret_new = _retention(pr["qr_f"], pr["kr_f"], pr["vr_b"], pr["gr_f"], s0, RET_CHUNK)
    kv = _memory_kv(mem, norms["mem"], w_mem_kv)
    wm = N_HEADS_M * HEAD_DIM_M
    mk, mv = kv[:, :wm], kv[:, wm:]
    o_m = _cross_attend(pr["qm_b"], mk[None], mv[None], tiles["cross"], MXU_DTYPE)
    h = _merge(x, o_a, o_r, o_m, pr["gates_f"], wts["wpa"], wts["wpb"], wts["wpc"], wts["wo"],
               norms["post_mix"], tiles["merge"])
    zst = jnp.zeros((SUBLANES, f), F32)
    y, utail = _conv_ffn(h, zst, zst, norms["pre_ffn"], norms["post_ffn"], wts["wu"], wts["wg"], conv_w, conv_b,
                         wts["wd"], tiles["ffn"], s, SUBLANES)
    conv_new = utail[-(CONV_W - 1):]
    return y, pr["ka_f"], pr["va_f"], pr["ki_f"], ret_new, conv_new, mk, mv


def _sample_layer(x, wts, norms, conv_w, conv_b, cache_k, cache_v, cache_kidx, mem_k, mem_v,
                  state_ret, state_conv, page_table):
    db, t, d = x.shape
    f = wts["wd"].shape[0]
    n_pages = page_table.shape[1]
    page = cache_k.shape[1]
    past = n_pages * page
    rows = db * T_PAD
    xp = jnp.pad(x, ((0, 0), (0, T_PAD - t), (0, 0))).reshape(rows, d)
    pos = jnp.tile(past + jnp.arange(T_PAD), db)
    pr = _projections(xp, norms["pre_mix"], pos, wts, rows)
    hd = N_HEADS_A * HEAD_DIM_A

    topk = min(TOPK_MAX, (past + t) // 4)
    i_past = _sample_index(page_table, pr["qi_hm"].astype(F32), pr["wi_f"], cache_kidx.transpose(0, 2, 1)).reshape(rows, past)
    m_past, m_new = _sample_select(i_past, pr["qi_hm"], pr["wi_f"], pr["ki_b"], t, topk)
    nnew = m_new.shape[1]
    own = m_new[:, :rows].reshape(db, T_PAD, db, T_PAD)[jnp.arange(db), :, jnp.arange(db), :]
    m_new_own = jnp.pad(own, ((0, 0), (0, 0), (0, LANES - T_PAD)))
    q_rows = pr["qa_hm"].reshape(N_HEADS_A, db, T_PAD, HEAD_DIM_A)
    eye = jnp.eye(N_HEADS_A, dtype=MXU_DTYPE)
    q_bd = jnp.einsum("hbtd,hg->bhtgd", q_rows, eye).reshape(db, N_HEADS_A * T_PAD, hd)
    new_t = lambda a: jnp.pad(_mx(a).reshape(db, T_PAD, hd).transpose(0, 2, 1), ((0, 0), (0, 0), (0, LANES - T_PAD)))
    paged_t = lambda c: c.transpose(0, 2, 3, 1).reshape(c.shape[0], hd, page)
    o_a = _sample_attend(page_table, q_bd, m_past.reshape(db, T_PAD, past), m_new_own,
                         new_t(pr["ka_f"]), new_t(pr["va_f"]), paged_t(cache_k), paged_t(cache_v)).reshape(rows, hd)

    padc = lambda a: jnp.pad(a.reshape(db, T_PAD, -1), ((0, 0), (0, RET_CHUNK - T_PAD), (0, 0))).reshape(db * RET_CHUNK, -1)
    o_r, ret_new = _retention(padc(pr["qr_f"]), padc(pr["kr_f"]), padc(pr["vr_b"]), padc(pr["gr_f"]), state_ret, t)
    o_r = o_r.reshape(db, RET_CHUNK, -1)[:, :T_PAD].reshape(rows, -1)

    wm = N_HEADS_M * HEAD_DIM_M
    o_m = _cross_attend(pr["qm_b"].astype(F32), mem_k.reshape(db, -1, wm), mem_v.reshape(db, -1, wm), T_PAD, F32)

    h = _merge(xp, o_a, o_r, o_m, pr["gates_f"], wts["wpa"], wts["wpb"], wts["wpc"], wts["wo"],
               norms["post_mix"], rows)
    s0e = jnp.repeat(state_conv[:, 0], T_PAD, axis=0)
    s1e = jnp.repeat(state_conv[:, 1], T_PAD, axis=0)
    y, u_all = _conv_ffn(h, s0e, s1e, norms["pre_ffn"], norms["post_ffn"], wts["wu"], wts["wg"], conv_w, conv_b,
                         wts["wd"], rows, T_PAD, rows)
    ext = jnp.concatenate([state_conv.astype(F32), u_all.reshape(db, T_PAD, f)[:, :t]], axis=1)
    conv_new = ext[:, t:]
    unpad = lambda a: a.reshape(db, T_PAD, -1)[:, :t]
    return (unpad(y), unpad(pr["ka_f"]), unpad(pr["va_f"]), unpad(pr["ki_f"]), ret_new, conv_new)


PROMPT_TILES = dict(proj=512, sel_q=256, att_q=256, att_k=1024, cross=512, merge=512, ffn=256)


def kernel(x_prompt, x_sample, cache_k, cache_v, cache_kidx, cache_mem_k, cache_mem_v, state_ret, state_conv,
           page_table, mem_prompt, norm_pre_mix, norm_post_mix, norm_pre_ffn, norm_post_ffn, norm_mem,
           w_in, w_mem_kv, w_proj_a, w_proj_b, w_proj_c, w_out, w_up, conv_w, conv_b, w_down):
    bp, s, d = x_prompt.shape
    db, t, _ = x_sample.shape
    depth = w_in.shape[0]
    assert bp == 1 and t <= T_PAD and CONV_W - 1 <= t
    tiles = {k: min(v, s) for k, v in PROMPT_TILES.items()}
    yp, ys = x_prompt[0], x_sample
    outs = [[] for _ in range(12)]
    for l in range(depth):
        wts = _layer_weights(l, w_in, w_proj_a, w_proj_b, w_proj_c, w_out, w_up, w_down)
        norms = dict(pre_mix=norm_pre_mix[l], post_mix=norm_post_mix[l], pre_ffn=norm_pre_ffn[l],
                     post_ffn=norm_post_ffn[l], mem=norm_mem[l])
        yp, kp, vp, kip, rp, cp, mk, mv = _prompt_layer(yp, mem_prompt[0], wts, norms, conv_w[l], conv_b[l],
                                                        w_mem_kv[l], tiles)
        ys, ks, vs, kis, rs, cs = _sample_layer(ys, wts, norms, conv_w[l], conv_b[l], cache_k[l], cache_v[l],
                                                cache_kidx[l], cache_mem_k[l], cache_mem_v[l], state_ret[l],
                                                state_conv[l], page_table)
        n_mem = mk.shape[0]
        vals = (kp.reshape(1, s, N_HEADS_A, HEAD_DIM_A), vp.reshape(1, s, N_HEADS_A, HEAD_DIM_A),
                kip.reshape(1, s, IDX_DIM), rp, cp[None],
                mk.reshape(1, n_mem, N_HEADS_M, HEAD_DIM_M), mv.reshape(1, n_mem, N_HEADS_M, HEAD_DIM_M),
                ks.reshape(db, t, N_HEADS_A, HEAD_DIM_A), vs.reshape(db, t, N_HEADS_A, HEAD_DIM_A),
                kis, rs, cs)
        for o, v in zip(outs, vals):
            o.append(v)
    stacked = [jnp.stack(o) for o in outs]
    return (yp[None], ys, *stacked)
```

```python
import functools
import math

import numpy as np
import jax
import jax.numpy as jnp
from jax import lax
from jax.experimental import pallas as pl
from jax.experimental.pallas import tpu as pltpu

F32 = jnp.float32
I32 = jnp.int32
MXU_DTYPE = jnp.bfloat16

N_HEADS_A, HEAD_DIM_A = 8, 64
IDX_HEADS, IDX_DIM = 4, 64
TOPK_MAX = 256
N_HEADS_R, DK_R, DV_R = 4, 128, 256
RET_CHUNK = 128
N_HEADS_M, HEAD_DIM_M = 4, 128
CONV_W = 3
ROPE_THETA = 10000.0
EPS = 1e-6

LANES = 128
SUBLANES = 8
BF16_ROWS = 16
VMEM_LIMIT = 56 * 1024 * 1024
NEG_BIG = -1e30
F32_LOWEST = float(np.finfo(np.float32).min)
T_PAD = 8


def _cparams(sem):
    return pltpu.CompilerParams(dimension_semantics=sem, vmem_limit_bytes=VMEM_LIMIT)


def _dot(a, b):
    return jnp.dot(a, b, preferred_element_type=F32)


def _dot_nt(a, b):
    return lax.dot_general(a, b, (((1,), (1,)), ((), ())), preferred_element_type=F32)


def _mx(a):
    return a.astype(MXU_DTYPE)


def _rms(x, g):
    return x * lax.rsqrt(jnp.mean(x * x, axis=-1, keepdims=True) + EPS) * g


def _sigmoid(x):
    return 1.0 / (1.0 + jnp.exp(-x))


def _rope_tables(pos, d):
    half = d // 2
    inv = 1.0 / (ROPE_THETA ** (jnp.arange(half, dtype=F32) * 2.0 / d))
    ang = pos.astype(F32)[:, None] * inv[None, :]
    cos, sin = jnp.cos(ang), jnp.sin(ang)
    reps = LANES // d
    cos_t = jnp.tile(jnp.concatenate([cos, cos], axis=1), (1, reps))
    sin_t = jnp.tile(jnp.concatenate([-sin, sin], axis=1), (1, reps))
    return cos_t, sin_t


def _rope(y, cos, sin, d):
    w = y.shape[1]
    half = d // 2
    reps = w // LANES
    c = jnp.concatenate([cos] * reps, axis=1) if reps > 1 else cos
    s = jnp.concatenate([sin] * reps, axis=1) if reps > 1 else sin
    lane = lax.broadcasted_iota(I32, y.shape, 1)
    first = (lane & (d - 1)) < half
    rot = jnp.where(first, pltpu.roll(y, w - half, 1), pltpu.roll(y, half, 1))
    return y * c + rot * s


def _proj_a_kernel(x_ref, g_ref, cos_ref, sin_ref, wa_ref, wv_ref,
                   qa_hm, qa_t, ka_f, ka_hm, qi_hm, ki_f, ki_b, va_f, va_t, wi_f):
    xn = _mx(_rms(x_ref[...], g_ref[...]))
    wa = N_HEADS_A * HEAD_DIM_A
    y = _rope(_dot(xn, wa_ref[...]), cos_ref[...], sin_ref[...], HEAD_DIM_A)
    q = y[:, :wa] * (HEAD_DIM_A ** -0.5)
    k = y[:, wa:2 * wa]
    ka_f[...] = k
    qt = q.T
    for h in range(N_HEADS_A):
        sl = slice(h * HEAD_DIM_A, (h + 1) * HEAD_DIM_A)
        qa_hm[h] = _mx(q[:, sl])
        qa_t[h] = _mx(qt[sl, :])
        ka_hm[h] = _mx(k[:, sl])
    qi = y[:, 2 * wa:2 * wa + IDX_HEADS * IDX_DIM]
    for h in range(IDX_HEADS):
        qi_hm[h] = _mx(qi[:, h * IDX_DIM:(h + 1) * IDX_DIM])
    ki = y[:, 2 * wa + IDX_HEADS * IDX_DIM:2 * wa + IDX_HEADS * IDX_DIM + IDX_DIM]
    ki_f[...] = ki
    ki_b[...] = _mx(ki)
    z = _dot(xn, wv_ref[...])
    v = z[:, :wa]
    va_f[...] = v
    vt = v.T
    for h in range(N_HEADS_A):
        va_t[h] = _mx(vt[h * HEAD_DIM_A:(h + 1) * HEAD_DIM_A, :])
    wi_f[...] = z[:, wa:wa + LANES]


def _proj_b_kernel(x_ref, g_ref, cos_ref, sin_ref, wr_ref, wvg_ref, qr_f, kr_f, vr_b, gr_f):
    xn = _mx(_rms(x_ref[...], g_ref[...]))
    wr = N_HEADS_R * DK_R
    y = _rope(_dot(xn, wr_ref[...]), cos_ref[...], sin_ref[...], DK_R)
    qr_f[...] = y[:, :wr]
    kr_f[...] = y[:, wr:] * (DK_R ** -0.5)
    z = _dot(xn, wvg_ref[...])
    wv = N_HEADS_R * DV_R
    vr_b[...] = _mx(z[:, :wv])
    gr_f[...] = z[:, wv:]


def _proj_c_kernel(x_ref, g_ref, wc_ref, qm_b, gates_f):
    xn = _mx(_rms(x_ref[...], g_ref[...]))
    z = _dot(xn, wc_ref[...])
    wm = N_HEADS_M * HEAD_DIM_M
    qm_b[...] = _mx(z[:, :wm])
    gates_f[...] = z[:, wm:]


def _row_spec(tm, w):
    return pl.BlockSpec((tm, w), lambda i: (i, 0))


def _full_spec(shape):
    nd = len(shape)
    return pl.BlockSpec(shape, lambda i: (0,) * nd)


def _hm_spec(nh, tm, d):
    return pl.BlockSpec((nh, tm, d), lambda i: (0, i, 0))


def _projections(x, gain, pos, wts, tm):
    rows, d = x.shape
    grid = (rows // tm,)
    wa = N_HEADS_A * HEAD_DIM_A
    cos64, sin64 = _rope_tables(pos, HEAD_DIM_A)
    cos128, sin128 = _rope_tables(pos, DK_R)
    g2 = gain.reshape(1, d)
    sds = jax.ShapeDtypeStruct
    t_spec = pl.BlockSpec((N_HEADS_A, HEAD_DIM_A, tm), lambda i: (0, 0, i))
    outs_a = pl.pallas_call(
        _proj_a_kernel,
        grid=grid,
        in_specs=[_row_spec(tm, d), _full_spec((1, d)), _row_spec(tm, LANES), _row_spec(tm, LANES),
                  _full_spec(wts["wa"].shape), _full_spec(wts["wv"].shape)],
        out_specs=[_hm_spec(N_HEADS_A, tm, HEAD_DIM_A), t_spec, _row_spec(tm, wa),
                   _hm_spec(N_HEADS_A, tm, HEAD_DIM_A),
                   _hm_spec(IDX_HEADS, tm, IDX_DIM), _row_spec(tm, IDX_DIM), _row_spec(tm, IDX_DIM),
                   _row_spec(tm, wa), t_spec, _row_spec(tm, LANES)],
        out_shape=[sds((N_HEADS_A, rows, HEAD_DIM_A), MXU_DTYPE), sds((N_HEADS_A, HEAD_DIM_A, rows), MXU_DTYPE),
                   sds((rows, wa), F32),
                   sds((N_HEADS_A, rows, HEAD_DIM_A), MXU_DTYPE), sds((IDX_HEADS, rows, IDX_DIM), MXU_DTYPE),
                   sds((rows, IDX_DIM), F32), sds((rows, IDX_DIM), MXU_DTYPE),
                   sds((rows, wa), F32), sds((N_HEADS_A, HEAD_DIM_A, rows), MXU_DTYPE), sds((rows, LANES), F32)],
        compiler_params=_cparams(("parallel",)),
        name="proj_a",
    )(x, g2, cos64, sin64, wts["wa"], wts["wv"])
    names_a = ("qa_hm", "qa_t", "ka_f", "ka_hm", "qi_hm", "ki_f", "ki_b", "va_f", "va_t", "wi_f")
    wr, wv = N_HEADS_R * DK_R, N_HEADS_R * DV_R
    outs_b = pl.pallas_call(
        _proj_b_kernel,
        grid=grid,
        in_specs=[_row_spec(tm, d), _full_spec((1, d)), _row_spec(tm, LANES), _row_spec(tm, LANES),
                  _full_spec(wts["wr"].shape), _full_spec(wts["wvg"].shape)],
        out_specs=[_row_spec(tm, wr), _row_spec(tm, wr), _row_spec(tm, wv), _row_spec(tm, wv)],
        out_shape=[sds((rows, wr), F32), sds((rows, wr), F32), sds((rows, wv), MXU_DTYPE), sds((rows, wv), F32)],
        compiler_params=_cparams(("parallel",)),
        name="proj_b",
    )(x, g2, cos128, sin128, wts["wr"], wts["wvg"])
    names_b = ("qr_f", "kr_f", "vr_b", "gr_f")
    wm = N_HEADS_M * HEAD_DIM_M
    outs_c = pl.pallas_call(
        _proj_c_kernel,
        grid=grid,
        in_specs=[_row_spec(tm, d), _full_spec((1, d)), _full_spec(wts["wc"].shape)],
        out_specs=[_row_spec(tm, wm), _row_spec(tm, 3 * d)],
        out_shape=[sds((rows, wm), MXU_DTYPE), sds((rows, 3 * d), F32)],
        compiler_params=_cparams(("parallel",)),
        name="proj_c",
    )(x, g2, wts["wc"])
    names_c = ("qm_b", "gates_f")
    out = dict(zip(names_a, outs_a))
    out.update(zip(names_b, outs_b))
    out.update(zip(names_c, outs_c))
    return out


def _prep_in_weights(w_in):
    d = w_in.shape[0]
    wa = N_HEADS_A * HEAD_DIM_A
    widths = (wa, wa, wa, IDX_HEADS * IDX_DIM, IDX_DIM, IDX_HEADS,
              N_HEADS_R * DK_R, N_HEADS_R * DK_R, N_HEADS_R * DV_R, N_HEADS_R * DV_R,
              N_HEADS_M * HEAD_DIM_M, 3 * d)
    offs = np.concatenate([[0], np.cumsum(widths)])
    seg = [w_in[:, int(offs[i]):int(offs[i + 1])] for i in range(len(widths))]
    q_a, k_a, v_a, q_i, k_i, w_i, q_r, k_r, v_r, g_r, q_m, gates = seg
    zpad = lambda n: jnp.zeros((d, n), w_in.dtype)
    return {
        "wa": _mx(jnp.concatenate([q_a, k_a, q_i, k_i, zpad(LANES - IDX_DIM)], axis=1)),
        "wv": _mx(jnp.concatenate([v_a, w_i, zpad(LANES - IDX_HEADS)], axis=1)),
        "wr": _mx(jnp.concatenate([q_r, k_r], axis=1)),
        "wvg": _mx(jnp.concatenate([v_r, g_r], axis=1)),
        "wc": _mx(jnp.concatenate([q_m, gates], axis=1)),
    }


SEL_ROWS = 128
SEL_WIDE = 512
SEL_GROUPS = 2 * LANES
NO_LIMIT = 2 ** 30
MIN_NORMAL_KEY = 0x00800000
SEARCH_PERIOD = 4
SEARCH_CAP = SEARCH_PERIOD * 33


def _key_to_f32(key):
    bits = jnp.where(key >= 0, key, key ^ jnp.int32(0x7FFFFFFF))
    return pltpu.bitcast(bits, F32)


def _f32_to_key(f):
    bits = pltpu.bitcast(f, I32)
    return jnp.where(f == 0.0, 0, jnp.where(bits >= 0, bits, bits ^ jnp.int32(0x7FFFFFFF)))


def _sweep(segments, rs, init, fn):
    acc = init
    base = 0
    for ref, n_wide in segments:
        def body(c, a, ref=ref, base=base):
            off = c * SEL_WIDE
            for k in range(SEL_WIDE // LANES):
                x = ref[rs, pl.ds(pl.multiple_of(off + k * LANES, LANES), LANES)]
                a = fn(a, x, base + off + k * LANES, k)
            return a
        pairs = n_wide // 2
        acc = lax.fori_loop(0, pairs, lambda c, a, body=body: body(2 * c + 1, body(2 * c, a)), acc)
        acc = lax.fori_loop(2 * pairs, n_wide, body, acc)
        base = base + n_wide * SEL_WIDE
    return acc


def _count(segments, rows, make_pred):
    outs = []
    for g in range(rows // SEL_ROWS):
        rs = slice(g * SEL_ROWS, (g + 1) * SEL_ROWS)
        pred = make_pred(rs)
        acc = _sweep(segments, rs, jnp.zeros((SEL_ROWS, LANES), F32),
                     lambda a, x, idx0, k: a + jnp.where(pred(x, idx0), 1.0, 0.0))
        outs.append(jnp.sum(acc.T, axis=0, keepdims=True))
    return jnp.concatenate(outs, axis=1) if len(outs) > 1 else outs[0]


def _col_to_row(col):
    return jnp.broadcast_to(col, (col.shape[0], LANES)).T[0:1]


def _row_to_lanes(vec):
    return jnp.broadcast_to(vec, (LANES, vec.shape[1])).T


def _search(count_fn, lo, hi, c_lo, c_hi, target, alive):
    def unfinished(lo, hi, c_lo):
        return jnp.logical_and(alive, jnp.logical_and(c_lo > target, lo + 1 < hi))

    def any_row(flag):
        return jnp.max(jnp.where(flag, 1, 0).astype(I32))

    def cond(carry):
        return jnp.logical_and(carry[0] < SEARCH_CAP, carry[1] > 0)

    def body(carry):
        it, _, phase, lo, hi, c_lo, c_hi, w_lo, w_hi, last = carry
        act = unfinished(lo, hi, c_lo)
        bis = (lo >> 1) + (hi >> 1) + (lo & hi & 1)
        width = hi - lo
        a = (c_lo - target + 0.5) * w_lo
        b = (target - 0.5 - c_hi) * w_hi
        frac = a / jnp.maximum(a + b, 1e-6)
        step = (frac * width.astype(F32)).astype(I32)
        itp = lo + jnp.clip(step, 1, jnp.maximum(width - 1, 1))
        use_itp = jnp.logical_and((lo ^ hi) >= 0, (jnp.zeros_like(lo) + phase) != SEARCH_PERIOD - 1)
        v = jnp.where(use_itp, itp, bis)
        cnt = count_fn(v)
        up = jnp.logical_and(act, cnt >= target)
        dn = jnp.logical_and(act, cnt < target)
        lo = jnp.where(up, v, lo)
        c_lo = jnp.where(up, cnt, c_lo)
        hi = jnp.where(dn, v, hi)
        c_hi = jnp.where(dn, cnt, c_hi)
        w_hi = jnp.where(up, jnp.where(last == 1, w_hi * 0.5, 1.0), jnp.where(dn, 1.0, w_hi))
        w_lo = jnp.where(dn, jnp.where(last == -1, w_lo * 0.5, 1.0), jnp.where(up, 1.0, w_lo))
        last = jnp.where(up, 1, jnp.where(dn, -1, last))
        phase = jnp.where(phase == SEARCH_PERIOD - 1, 0, phase + 1)
        return it + 1, any_row(unfinished(lo, hi, c_lo)), phase, lo, hi, c_lo, c_hi, w_lo, w_hi, last

    go = any_row(unfinished(lo, hi, c_lo))
    one = jnp.ones(lo.shape, F32)
    out = lax.while_loop(cond, body, (jnp.int32(0), go, jnp.int32(0), lo, hi, c_lo, c_hi,
                                      one, one, jnp.zeros(lo.shape, I32)))
    return out[3], out[5], out[6]


def _select_threshold(segments, rows, topk, alive=None):
    assert topk <= SEL_GROUPS
    kf = jnp.full((1, rows), float(topk), F32)
    if alive is None:
        alive = jnp.full((1, rows), True)

    los, his = [], []
    for g in range(rows // SEL_ROWS):
        rs = slice(g * SEL_ROWS, (g + 1) * SEL_ROWS)
        ninf = jnp.full((SEL_ROWS, LANES), -jnp.inf, F32)
        ga, gb = _sweep(segments, rs, (ninf, ninf),
                        lambda a, x, idx0, k: ((jnp.maximum(a[0], x), a[1]) if k % 2 == 0
                                               else (a[0], jnp.maximum(a[1], x))))
        los.append(jnp.min(jnp.minimum(ga, gb), axis=1, keepdims=True))
        his.append(jnp.max(jnp.maximum(ga, gb), axis=1, keepdims=True))
    cat = lambda xs: _col_to_row(jnp.concatenate(xs, axis=0) if len(xs) > 1 else xs[0])
    lo = _f32_to_key(jnp.maximum(cat(los), F32_LOWEST))
    hi = _f32_to_key(cat(his)) + 1

    def count_ge_f(thr):
        thr_l = _row_to_lanes(thr)

        def make_pred(rs):
            t = thr_l[rs]
            return lambda x, idx0: x >= t
        return _count(segments, rows, make_pred)

    count_ge = lambda v: count_ge_f(_key_to_f32(v))
    c_lo = count_ge(lo)
    c_hi = jnp.zeros((1, rows), F32)
    def zero_probes(state):
        lo, hi, c_lo, c_hi = state
        for probe in (0, MIN_NORMAL_KEY):
            v = jnp.full((1, rows), probe, I32)
            cnt = count_ge_f(jnp.full((1, rows), np.int32(probe).view(np.float32), F32))
            inside = jnp.logical_and(lo < v, v < hi)
            up = jnp.logical_and(inside, cnt >= kf)
            dn = jnp.logical_and(inside, cnt < kf)
            lo, c_lo = jnp.where(up, v, lo), jnp.where(up, cnt, c_lo)
            hi, c_hi = jnp.where(dn, v, hi), jnp.where(dn, cnt, c_hi)
        return lo, hi, c_lo, c_hi

    straddles = jnp.logical_or(jnp.logical_and(lo < 0, hi > 0),
                               jnp.logical_and(lo < MIN_NORMAL_KEY, hi > MIN_NORMAL_KEY))
    lo, hi, c_lo, c_hi = lax.cond(jnp.max(jnp.where(straddles, 1, 0).astype(I32)) > 0,
                                  zero_probes, lambda s: s, (lo, hi, c_lo, c_hi))
    at_zero = jnp.logical_and(lo == 0, hi == MIN_NORMAL_KEY)
    lo, c_lo, c_hi = _search(count_ge, lo, hi, c_lo, c_hi, kf, jnp.logical_and(alive, jnp.logical_not(at_zero)))
    tau = _key_to_f32(lo)

    need = jnp.logical_and(alive, c_lo > kf)
    room = jnp.where(need, kf - c_hi, float(NO_LIMIT))
    return _row_to_lanes(tau), _row_to_lanes(room)


def _write_mask(segments_out, rows, tau, room):
    r_io = lax.broadcasted_iota(I32, (LANES, 2 * LANES), 0)
    c_io = lax.broadcasted_iota(I32, (LANES, 2 * LANES), 1)
    tri = jnp.where(jnp.logical_or(r_io <= c_io, c_io >= LANES), 1.0, 0.0).astype(MXU_DTYPE)
    for g in range(rows // SEL_ROWS):
        rs = slice(g * SEL_ROWS, (g + 1) * SEL_ROWS)
        t, rm = tau[rs], room[rs]
        seen = jnp.zeros((SEL_ROWS, LANES), F32)
        for src, dst, n_wide, n_total in segments_out:
            def body(c, seen, src=src, dst=dst):
                off = c * SEL_WIDE
                nsub = SEL_WIDE // LANES
                dss = [pl.ds(pl.multiple_of(off + k * LANES, LANES), LANES) for k in range(nsub)]
                xs = [src[rs, ds] for ds in dss]
                eqs = [jnp.where(x == t, 1.0, 0.0) for x in xs]
                run = jnp.dot(jnp.concatenate(eqs, axis=0).astype(MXU_DTYPE), tri, preferred_element_type=F32)
                for k in range(nsub):
                    run_k = run[k * SEL_ROWS:(k + 1) * SEL_ROWS]
                    before = seen + run_k[:, :LANES] - eqs[k]
                    keep_eq = jnp.where(before < rm, eqs[k], 0.0)
                    dst[rs, dss[k]] = jnp.where(xs[k] > t, 1.0, keep_eq).astype(dst.dtype)
                    seen = seen + run_k[:, LANES:]
                return seen

            seen = lax.fori_loop(0, n_wide, body, seen)

            def zbody(c, carry, dst=dst):
                off = pl.multiple_of(c * SEL_WIDE, SEL_WIDE)
                dst[rs, pl.ds(off, SEL_WIDE)] = jnp.zeros((SEL_ROWS, SEL_WIDE), dst.dtype)
                return carry

            lax.fori_loop(n_wide, n_total, zbody, 0)


def _index_scores(qi_ref, w, kb, transposed_keys=False):
    acc = None
    for h in range(IDX_HEADS):
        s = _dot(_mx(qi_ref[h]), kb) if transposed_keys else _dot_nt(_mx(qi_ref[h]), kb)
        t = w[:, h:h + 1] * jnp.maximum(s, 0.0)
        acc = t if acc is None else acc + t
    return acc


def _causal_pairs(nq, tq, tk):
    pairs = [(i, j) for i in range(nq) for j in range((i * tq + tq - 1) // tk + 1)]
    return jnp.asarray([p[0] for p in pairs], I32), jnp.asarray([p[1] for p in pairs], I32)


def _prompt_select_kernel(qb_ref, kb_ref, qi_ref, wi_ref, kidx_ref, mask_ref, i_scr, *, tq, tk, topk):
    p = pl.program_id(0)
    i, j = qb_ref[p], kb_ref[p]
    q_lo = i * tq
    n_wide = (q_lo + tq - 1) // tk + 1

    acc = _index_scores(qi_ref, wi_ref[...], kidx_ref[...])
    cols = pl.ds(pl.multiple_of(j * tk, tk), tk)
    below_diagonal = (j + 1) * tk <= q_lo + 1

    @pl.when(below_diagonal)
    def _():
        i_scr[:, cols] = acc

    @pl.when(jnp.logical_not(below_diagonal))
    def _():
        qpos = q_lo + lax.broadcasted_iota(I32, acc.shape, 0)
        kpos = j * tk + lax.broadcasted_iota(I32, acc.shape, 1)
        i_scr[:, cols] = jnp.where(kpos <= qpos, acc, -jnp.inf)

    @pl.when(j == n_wide - 1)
    def _():
        seg = [(i_scr, n_wide)]
        tau, room = _select_threshold(seg, tq, topk)
        _write_mask([(i_scr, mask_ref, n_wide, mask_ref.shape[1] // tk)], tq, tau, room)


def _prompt_select(qi_hm, wi_f, ki_b, topk, tq, tk):
    s = ki_b.shape[0]
    assert tk == SEL_WIDE and s % tk == 0 and s % tq == 0 and tq % SEL_ROWS == 0
    qb, kb = _causal_pairs(s // tq, tq, tk)
    grid_spec = pltpu.PrefetchScalarGridSpec(
        num_scalar_prefetch=2,
        grid=(qb.shape[0],),
        in_specs=[pl.BlockSpec((IDX_HEADS, tq, IDX_DIM), lambda p, qb, kb: (0, qb[p], 0)),
                  pl.BlockSpec((tq, LANES), lambda p, qb, kb: (qb[p], 0)),
                  pl.BlockSpec((tk, IDX_DIM), lambda p, qb, kb: (kb[p], 0))],
        out_specs=pl.BlockSpec((tq, s), lambda p, qb, kb: (qb[p], 0)),
        scratch_shapes=[pltpu.VMEM((tq, s), F32)],
    )
    return pl.pallas_call(
        functools.partial(_prompt_select_kernel, tq=tq, tk=tk, topk=topk),
        grid_spec=grid_spec,
        out_shape=jax.ShapeDtypeStruct((s, s), MXU_DTYPE),
        compiler_params=_cparams(("arbitrary",)),
        name="prompt_select",
    )(qb, kb, qi_hm, wi_f, ki_b)


def _spread(a, n):
    if n <= LANES:
        return a[:, :n]
    return jnp.concatenate([a] * (n // LANES), axis=1)


def _flash_update(s, v, m_prev, l_prev, acc_prev):
    m_new = jnp.maximum(m_prev, jnp.max(s, axis=1, keepdims=True))
    alpha = jnp.exp(m_prev - m_new)
    p = jnp.exp(s - _spread(m_new, s.shape[1]))
    l_new = alpha * l_prev + jnp.sum(p, axis=1, keepdims=True)
    acc_new = _spread(alpha, acc_prev.shape[1]) * acc_prev + _dot(_mx(p), v)
    return m_new, l_new, acc_new


def _prompt_attend_kernel(qb_ref, kb_ref, qt_ref, k_ref, vt_ref, mask_ref, o_ref, m_scr, l_scr, acc_scr, *, tq, tk):
    p_id = pl.program_id(0)
    i, j = qb_ref[p_id], kb_ref[p_id]

    @pl.when(j == 0)
    def _():
        m_scr[...] = jnp.full(m_scr.shape, NEG_BIG, F32)
        l_scr[...] = jnp.zeros(l_scr.shape, F32)
        acc_scr[...] = jnp.zeros(acc_scr.shape, F32)

    bias = (1.0 - mask_ref[...].astype(F32).T) * NEG_BIG
    scores = [_dot(k_ref[h], qt_ref[h]) + bias for h in range(N_HEADS_A)]
    for h, s in enumerate(scores):
        m_prev = m_scr[h]
        m_new = jnp.maximum(m_prev, jnp.max(s, axis=0, keepdims=True))
        alpha = jnp.exp(m_prev - m_new)
        p = jnp.exp(s - m_new[0:1])
        l_scr[h] = alpha * l_scr[h] + jnp.sum(p, axis=0, keepdims=True)
        acc_scr[h] = alpha[0:1] * acc_scr[h] + _dot(vt_ref[h], _mx(p))
        m_scr[h] = m_new

    @pl.when(j == (i * tq + tq - 1) // tk)
    def _():
        ot = jnp.concatenate([acc_scr[h] / l_scr[h][0:1] for h in range(N_HEADS_A)], axis=0)
        o_ref[...] = ot.T.astype(o_ref.dtype)


def _prompt_attend(qa_t, ka_hm, va_t, mask, tq, tk):
    nh, dh, s = qa_t.shape
    qb, kb = _causal_pairs(s // tq, tq, tk)
    grid_spec = pltpu.PrefetchScalarGridSpec(
        num_scalar_prefetch=2,
        grid=(qb.shape[0],),
        in_specs=[pl.BlockSpec((nh, dh, tq), lambda p, qb, kb: (0, 0, qb[p])),
                  pl.BlockSpec((nh, tk, dh), lambda p, qb, kb: (0, kb[p], 0)),
                  pl.BlockSpec((nh, dh, tk), lambda p, qb, kb: (0, 0, kb[p])),
                  pl.BlockSpec((tq, tk), lambda p, qb, kb: (qb[p], kb[p]))],
        out_specs=pl.BlockSpec((tq, nh * dh), lambda p, qb, kb: (qb[p], 0)),
        scratch_shapes=[pltpu.VMEM((nh, SUBLANES, tq), F32), pltpu.VMEM((nh, SUBLANES, tq), F32),
                        pltpu.VMEM((nh, dh, tq), F32)],
    )
    return pl.pallas_call(
        functools.partial(_prompt_attend_kernel, tq=tq, tk=tk),
        grid_spec=grid_spec,
        out_shape=jax.ShapeDtypeStruct((s, nh * dh), MXU_DTYPE),
        compiler_params=_cparams(("arbitrary",)),
        name="prompt_attend",
    )(qb, kb, qa_t, ka_hm, va_t, mask)


PAGES_PER_STEP = 8
INDEX_PAGES_PER_STEP = 16


def _sample_index_kernel(pt_ref, qi_ref, wi_ref, *refs, page):
    del pt_ref
    pages, out_ref = refs[:-1], refs[-1]
    w = wi_ref[0]
    for p, kref in enumerate(pages):
        out_ref[0, :, p * page:(p + 1) * page] = _index_scores(qi_ref, w, _mx(kref[0]), transposed_keys=True)


def _sample_index(page_table, qi_hm, wi_f, cache_kidx_t):
    db, n_pages = page_table.shape
    _, idim, page = cache_kidx_t.shape
    pps = math.gcd(INDEX_PAGES_PER_STEP, n_pages)
    nsteps = n_pages // pps
    pt = page_table.reshape(-1).astype(I32)

    def kspec(p):
        return pl.BlockSpec((1, idim, page), lambda b, j, pt: (pt[b * n_pages + j * pps + p], 0, 0))

    grid_spec = pltpu.PrefetchScalarGridSpec(
        num_scalar_prefetch=1,
        grid=(db, nsteps),
        in_specs=[pl.BlockSpec((IDX_HEADS, T_PAD, idim), lambda b, j, pt: (0, b, 0)),
                  pl.BlockSpec((1, T_PAD, LANES), lambda b, j, pt: (b, 0, 0))]
                 + [kspec(p) for p in range(pps)],
        out_specs=pl.BlockSpec((1, T_PAD, pps * page), lambda b, j, pt: (b, 0, j)),
    )
    return pl.pallas_call(
        functools.partial(_sample_index_kernel, page=page),
        grid_spec=grid_spec,
        out_shape=jax.ShapeDtypeStruct((db, T_PAD, n_pages * page), F32),
        compiler_params=_cparams(("parallel", "arbitrary")),
        name="sample_index",
    )(pt, qi_hm, wi_f.reshape(db, T_PAD, LANES), *([cache_kidx_t] * pps))


def _sample_select_kernel(ipast_ref, qi_ref, wi_ref, kin_ref, mpast_ref, mnew_ref, inew_scr, *, t_real, topk):
    rows = ipast_ref.shape[0]
    acc = _index_scores(qi_ref, wi_ref[...], kin_ref[...])
    r = lax.broadcasted_iota(I32, acc.shape, 0)
    c = lax.broadcasted_iota(I32, acc.shape, 1)
    same = (r // T_PAD) == (c // T_PAD)
    tq, tc = r % T_PAD, c % T_PAD
    ok = jnp.logical_and(same, jnp.logical_and(tc <= tq, tc < t_real))
    inew_scr[...] = jnp.full(inew_scr.shape, -jnp.inf, F32)
    inew_scr[:, :rows] = jnp.where(ok, acc, -jnp.inf)
    n_past = ipast_ref.shape[1] // SEL_WIDE
    n_new = inew_scr.shape[1] // SEL_WIDE
    alive = lax.rem(lax.broadcasted_iota(I32, (1, rows), 1), T_PAD) < t_real
    tau, room = _select_threshold([(ipast_ref, n_past), (inew_scr, n_new)], rows, topk, alive)
    _write_mask([(ipast_ref, mpast_ref, n_past, n_past), (inew_scr, mnew_ref, n_new, n_new)], rows, tau, room)


def _sample_select(i_past, qi_hm, wi_f, ki_b, t_real, topk):
    rows, past = i_past.shape
    assert past % SEL_WIDE == 0 and rows % SEL_ROWS == 0
    wnew = -(-rows // SEL_WIDE) * SEL_WIDE
    return pl.pallas_call(
        functools.partial(_sample_select_kernel, t_real=t_real, topk=topk),
        out_shape=[jax.ShapeDtypeStruct((rows, past), F32), jax.ShapeDtypeStruct((rows, wnew), F32)],
        scratch_shapes=[pltpu.VMEM((rows, wnew), F32)],
        compiler_params=pltpu.CompilerParams(vmem_limit_bytes=VMEM_LIMIT),
        name="sample_select",
    )(i_past, qi_hm, wi_f, ki_b)


def _sample_attend_kernel(pt_ref, q_ref, mp_ref, mn_ref, kn_ref, vn_ref, *refs, page, pps):
    del pt_ref
    kpages, vpages = refs[:pps], refs[pps:2 * pps]
    o_ref, m_scr, l_scr, acc_scr = refs[2 * pps:]
    j = pl.program_id(1)
    nj = pl.num_programs(1)
    q = q_ref[0]
    nh = q.shape[0] // T_PAD

    def scores(m_t, kt):
        keep = jnp.concatenate([m_t] * nh, axis=0) > 0
        return jnp.where(keep, _dot(q, kt), NEG_BIG)

    def update(s_list, vt_list):
        m_prev = m_scr[...]
        m_blk = functools.reduce(jnp.maximum, [jnp.max(s, axis=1, keepdims=True) for s in s_list])
        m_new = jnp.maximum(m_prev, m_blk)
        alpha = jnp.exp(m_prev - m_new)
        l_new = alpha * l_scr[...]
        acc = _spread(alpha, acc_scr.shape[1]) * acc_scr[...]
        for s, vt in zip(s_list, vt_list):
            p = jnp.exp(s - _spread(m_new, s.shape[1]))
            l_new = l_new + jnp.sum(p, axis=1, keepdims=True)
            acc = acc + _dot_nt(_mx(p), vt)
        m_scr[...], l_scr[...], acc_scr[...] = m_new, l_new, acc

    @pl.when(j == 0)
    def _():
        m_scr[...] = jnp.full(m_scr.shape, NEG_BIG, F32)
        l_scr[...] = jnp.zeros(l_scr.shape, F32)
        acc_scr[...] = jnp.zeros(acc_scr.shape, F32)

    update([scores(mp_ref[0, :, p * page:(p + 1) * page], _mx(kpages[p][0])) for p in range(pps)],
           [_mx(vpages[p][0]) for p in range(pps)])

    @pl.when(j == nj - 1)
    def _():
        update([scores(mn_ref[0], kn_ref[0])], [vn_ref[0]])
        full = acc_scr[...] / _spread(l_scr[...], acc_scr.shape[1])
        lane = lax.broadcasted_iota(I32, (T_PAD, full.shape[1]), 1)
        out = jnp.zeros((T_PAD, full.shape[1]), F32)
        for h in range(nh):
            out = out + jnp.where((lane // HEAD_DIM_A) == h, full[h * T_PAD:(h + 1) * T_PAD], 0.0)
        o_ref[0] = out


def _sample_attend(page_table, q_bd, m_past, m_new, kt_new, vt_new, cache_kt, cache_vt):
    db, n_pages = page_table.shape
    _, hd, page = cache_kt.shape
    pps = math.gcd(PAGES_PER_STEP, n_pages)
    nsteps = n_pages // pps
    nnew = kt_new.shape[2]
    nq = q_bd.shape[1]
    pt = page_table.reshape(-1).astype(I32)

    def pspec(p):
        return pl.BlockSpec((1, hd, page), lambda b, j, pt: (pt[b * n_pages + j * pps + p], 0, 0))

    bspec = lambda shape: pl.BlockSpec((1,) + shape, lambda b, j, pt: (b, 0, 0))
    grid_spec = pltpu.PrefetchScalarGridSpec(
        num_scalar_prefetch=1,
        grid=(db, nsteps),
        in_specs=[bspec((nq, hd)),
                  pl.BlockSpec((1, T_PAD, pps * page), lambda b, j, pt: (b, 0, j)),
                  bspec((T_PAD, nnew)), bspec((hd, nnew)), bspec((hd, nnew))]
                 + [pspec(p) for p in range(pps)] * 2,
        out_specs=bspec((T_PAD, hd)),
        scratch_shapes=[pltpu.VMEM((nq, LANES), F32), pltpu.VMEM((nq, LANES), F32), pltpu.VMEM((nq, hd), F32)],
    )
    return pl.pallas_call(
        functools.partial(_sample_attend_kernel, page=page, pps=pps),
        grid_spec=grid_spec,
        out_shape=jax.ShapeDtypeStruct((db, T_PAD, hd), F32),
        compiler_params=_cparams(("parallel", "arbitrary")),
        name="sample_attend",
    )(pt, q_bd, m_past, m_new, kt_new, vt_new, *([cache_kt] * pps), *([cache_vt] * pps))


def _retention_tables(c_real, c_pad):
    h = np.arange(N_HEADS_R, dtype=np.float64)
    log_g = np.log1p(-np.exp2(-5.0 - h))
    i = np.arange(c_pad, dtype=np.float64)
    diff = i[:, None] - i[None, :]
    live = (diff >= 0) & (i[:, None] < c_real) & (i[None, :] < c_real)
    inner = np.where(live[None], np.exp(np.maximum(diff, 0.0)[None] * log_g[:, None, None]), 0.0)
    q_dec = np.exp((i + 1.0)[None, :] * log_g[:, None])
    k_dec = np.where(i[None, :] < c_real, np.exp((c_real - 1.0 - i)[None, :] * log_g[:, None]), 0.0)
    c_dec = np.exp(c_real * log_g)
    f = lambda a: jnp.asarray(a, F32)
    return f(inner), f(q_dec[:, :, None]), f(k_dec[:, :, None]), [float(v) for v in c_dec]


def _retention_kernel(q_ref, k_ref, v_ref, g_ref, s0_ref, inner_ref, qdec_ref, kdec_ref,
                      o_ref, s_out_ref, s_scr, *, c_dec):
    j = pl.program_id(1)
    nj = pl.num_programs(1)

    @pl.when(j == 0)
    def _():
        s_scr[...] = s0_ref[0]

    for h in range(N_HEADS_R):
        q = q_ref[:, h * DK_R:(h + 1) * DK_R]
        k = k_ref[:, h * DK_R:(h + 1) * DK_R]
        v = v_ref[:, h * DV_R:(h + 1) * DV_R]
        s_prev = s_scr[h]
        a = _dot_nt(_mx(q), _mx(k)) * inner_ref[h]
        o = _dot(_mx(a), v) + _dot(_mx(q), _mx(s_prev)) * qdec_ref[h]
        kd = k * kdec_ref[h]
        s_scr[h] = s_prev * c_dec[h] + _dot(_mx(kd.T), v)
        mu = jnp.mean(o, axis=-1, keepdims=True)
        var = jnp.mean(jnp.square(o - mu), axis=-1, keepdims=True)
        gn = (o - mu) * lax.rsqrt(var + EPS)
        g = g_ref[:, h * DV_R:(h + 1) * DV_R]
        o_ref[:, h * DV_R:(h + 1) * DV_R] = (gn * (g * _sigmoid(g))).astype(o_ref.dtype)

    @pl.when(j == nj - 1)
    def _():
        s_out_ref[0] = s_scr[...]


def _retention(qr, kr, vr, gr, s0, c_real):
    b = s0.shape[0]
    c = RET_CHUNK
    n = qr.shape[0] // (b * c)
    inner, qdec, kdec, c_dec = _retention_tables(c_real, c)
    wr, wv = N_HEADS_R * DK_R, N_HEADS_R * DV_R
    rmap = lambda bi, j: (bi * n + j, 0)
    full3 = lambda shape: pl.BlockSpec(shape, lambda bi, j: (0, 0, 0))
    return pl.pallas_call(
        functools.partial(_retention_kernel, c_dec=c_dec),
        grid=(b, n),
        in_specs=[pl.BlockSpec((c, wr), rmap), pl.BlockSpec((c, wr), rmap), pl.BlockSpec((c, wv), rmap),
                  pl.BlockSpec((c, wv), rmap),
                  pl.BlockSpec((1, N_HEADS_R, DK_R, DV_R), lambda bi, j: (bi, 0, 0, 0)),
                  full3(inner.shape), full3(qdec.shape), full3(kdec.shape)],
        out_specs=[pl.BlockSpec((c, wv), rmap),
                   pl.BlockSpec((1, N_HEADS_R, DK_R, DV_R), lambda bi, j: (bi, 0, 0, 0))],
        out_shape=[jax.ShapeDtypeStruct((b * n * c, wv), MXU_DTYPE),
                   jax.ShapeDtypeStruct((b, N_HEADS_R, DK_R, DV_R), F32)],
        scratch_shapes=[pltpu.VMEM((N_HEADS_R, DK_R, DV_R), F32)],
        compiler_params=_cparams(("parallel", "arbitrary")),
        name="retention",
    )(qr, kr, vr, gr, s0, inner, qdec, kdec)


def _cross_kernel(q_ref, mk_ref, mv_ref, o_ref):
    scale = HEAD_DIM_M ** -0.5
    for h in range(N_HEADS_M):
        sl = slice(h * HEAD_DIM_M, (h + 1) * HEAD_DIM_M)
        s = _dot_nt(_mx(q_ref[:, sl]), _mx(mk_ref[0, :, sl])) * scale
        p = jnp.exp(s - jnp.max(s, axis=1, keepdims=True))
        p = p / jnp.sum(p, axis=1, keepdims=True)
        o_ref[:, sl] = _dot(_mx(p), _mx(mv_ref[0, :, sl])).astype(o_ref.dtype)


def _cross_attend(qm, mk, mv, tm, out_dtype):
    b, n_mem, hd = mk.shape
    nt = qm.shape[0] // (b * tm)
    return pl.pallas_call(
        _cross_kernel,
        grid=(b, nt),
        in_specs=[pl.BlockSpec((tm, hd), lambda bi, i: (bi * nt + i, 0)),
                  pl.BlockSpec((1, n_mem, hd), lambda bi, i: (bi, 0, 0)),
                  pl.BlockSpec((1, n_mem, hd), lambda bi, i: (bi, 0, 0))],
        out_specs=pl.BlockSpec((tm, hd), lambda bi, i: (bi * nt + i, 0)),
        out_shape=jax.ShapeDtypeStruct(qm.shape, out_dtype),
        compiler_params=_cparams(("parallel", "parallel")),
        name="cross_attend",
    )(qm, mk, mv)


def _memkv_kernel(x_ref, g_ref, w_ref, o_ref):
    o_ref[...] = _dot(_mx(_rms(x_ref[...], g_ref[...])), w_ref[...])


def _memory_kv(mem, gain, w):
    rows, d = mem.shape
    return pl.pallas_call(
        _memkv_kernel,
        out_shape=jax.ShapeDtypeStruct((rows, w.shape[1]), F32),
        compiler_params=pltpu.CompilerParams(vmem_limit_bytes=VMEM_LIMIT),
        name="memory_kv",
    )(mem, gain.reshape(1, d), _mx(w))


def _merge_kernel(x_ref, oa_ref, or_ref, om_ref, gates_ref, wpa_ref, wpb_ref, wpc_ref, wo_ref, g_ref, h_ref):
    d = x_ref.shape[1]
    gt = gates_ref[...]
    mixed = (_sigmoid(gt[:, :d]) * _dot(_mx(oa_ref[...]), wpa_ref[...])
             + _sigmoid(gt[:, d:2 * d]) * _dot(_mx(or_ref[...]), wpb_ref[...])
             + _sigmoid(gt[:, 2 * d:]) * _dot(_mx(om_ref[...]), wpc_ref[...]))
    z = _dot(_mx(mixed), wo_ref[...])
    h_ref[...] = x_ref[...] + _rms(z, g_ref[...])


def _merge(x, oa, o_r, om, gates, wpa, wpb, wpc, wo, gain, tm):
    rows, d = x.shape
    return pl.pallas_call(
        _merge_kernel,
        grid=(rows // tm,),
        in_specs=[_row_spec(tm, d), _row_spec(tm, oa.shape[1]), _row_spec(tm, o_r.shape[1]),
                  _row_spec(tm, om.shape[1]), _row_spec(tm, 3 * d),
                  _full_spec(wpa.shape), _full_spec(wpb.shape), _full_spec(wpc.shape), _full_spec(wo.shape),
                  _full_spec((1, d))],
        out_specs=_row_spec(tm, d),
        out_shape=jax.ShapeDtypeStruct((rows, d), F32),
        compiler_params=_cparams(("parallel",)),
        name="merge",
    )(x, oa, o_r, om, gates, wpa, wpb, wpc, wo, gain.reshape(1, d))


HALO = BF16_ROWS


def _ffn_kernel(h_ref, halo_ref, s0_ref, s1_ref, g1_ref, g2_ref, wu_ref, wg_ref, cw_ref, cb_ref, wd_ref,
                y_ref, utail_ref, x_scr, u_scr, *, tm, seq, keep):
    i = pl.program_id(0)
    h = h_ref[...]
    hn = _rms(h, g1_ref[...])
    x_scr[HALO:, :] = _mx(hn)
    x_scr[:HALO, :] = _mx(_rms(halo_ref[...], g1_ref[...]))
    xc = x_scr[...]
    u_scr[...] = _dot(xc, wu_ref[...])
    gate = _dot(xc[HALO:], wg_ref[...])
    cur = u_scr[HALO:, :]
    prev1 = u_scr[HALO - 1:HALO - 1 + tm, :]
    prev2 = u_scr[HALO - 2:HALO - 2 + tm, :]
    seq_loc = min(seq, tm)
    t = lax.rem(lax.broadcasted_iota(I32, (tm, 1), 0), seq_loc)
    t = jnp.where(lax.rem(i * tm, seq) == 0, t, CONV_W)
    st0, st1 = s0_ref[...], s1_ref[...]
    if st0.shape[0] != tm:
        st0, st1 = st0[0:1], st1[0:1]
    prev1 = jnp.where(t == 0, st1, prev1)
    prev2 = jnp.where(t == 0, st0, jnp.where(t == 1, st1, prev2))
    c = cb_ref[...] + prev2 * cw_ref[0:1, :] + prev1 * cw_ref[1:2, :] + cur * cw_ref[2:3, :]
    act = jax.nn.gelu(c, approximate=True) * gate
    ff = _dot(_mx(act), wd_ref[...])
    y_ref[...] = h + _rms(ff, g2_ref[...])
    utail_ref[...] = u_scr[HALO + tm - keep:, :]


def _conv_ffn(h, s0e, s1e, g1, g2, wu, wg, cw, cb, wd, tm, seq, keep):
    rows, d = h.shape
    f = wu.shape[1]
    nt = rows // tm
    hb = tm // HALO
    sr = s0e.shape[0]
    return pl.pallas_call(
        functools.partial(_ffn_kernel, tm=tm, seq=seq, keep=keep),
        grid=(nt,),
        in_specs=[_row_spec(tm, d),
                  pl.BlockSpec((HALO, d), lambda i: (jnp.maximum(i * hb - 1, 0), 0)),
                  _full_spec((sr, f)), _full_spec((sr, f)),
                  _full_spec((1, d)), _full_spec((1, d)),
                  _full_spec(wu.shape), _full_spec(wg.shape), _full_spec(cw.shape), _full_spec((1, f)),
                  _full_spec(wd.shape)],
        out_specs=[_row_spec(tm, d), _row_spec(keep, f)],
        out_shape=[jax.ShapeDtypeStruct((rows, d), F32), jax.ShapeDtypeStruct((nt * keep, f), F32)],
        scratch_shapes=[pltpu.VMEM((tm + HALO, d), MXU_DTYPE), pltpu.VMEM((tm + HALO, f), F32)],
        compiler_params=_cparams(("parallel",)),
        name="conv_ffn",
    )(h, h, s0e, s1e, g1.reshape(1, d), g2.reshape(1, d), wu, wg, cw, cb.reshape(1, f), wd)


def _layer_weights(l, w_in, w_proj_a, w_proj_b, w_proj_c, w_out, w_up, w_down):
    wts = _prep_in_weights(w_in[l])
    f = w_down.shape[1]
    wts.update(wpa=_mx(w_proj_a[l]), wpb=_mx(w_proj_b[l]), wpc=_mx(w_proj_c[l]), wo=_mx(w_out[l]),
               wu=_mx(w_up[l][:, :f]), wg=_mx(w_up[l][:, f:]), wd=_mx(w_down[l]))
    return wts


def _prompt_layer(x, mem, wts, norms, conv_w, conv_b, w_mem_kv, tiles):
    s, d = x.shape
    f = wts["wd"].shape[0]
    pos = jnp.arange(s)
    pr = _projections(x, norms["pre_mix"], pos, wts, tiles["proj"])
    topk = min(TOPK_MAX, s // 4)
    mask = _prompt_select(pr["qi_hm"], pr["wi_f"], pr["ki_b"], topk, tiles["sel_q"], SEL_WIDE)
    o_a = _prompt_attend(pr["qa_t"], pr["ka_hm"], pr["va_t"], mask, tiles["att_q"], tiles["att_k"])
    s0 = jnp.zeros((1, N_HEADS_R, DK_R, DV_R), F32)
    o_r, ret_new = _retention(pr["qr_f"], pr["kr_f"], pr["vr_b"], pr["gr_f"], s0, RET_CHUNK)
    kv = _memory_kv(mem, norms["mem"], w_mem_kv)
    wm = N_HEADS_M * HEAD_DIM_M
    mk, mv = kv[:, :wm], kv[:, wm:]
    o_m = _cross_attend(pr["qm_b"], mk[None], mv[None], tiles["cross"], MXU_DTYPE)
    h = _merge(x, o_a, o_r, o_m, pr["gates_f"], wts["wpa"], wts["wpb"], wts["wpc"], wts["wo"],
               norms["post_mix"], tiles["merge"])
    zst = jnp.zeros((SUBLANES, f), F32)
    y, utail = _conv_ffn(h, zst, zst, norms["pre_ffn"], norms["post_ffn"], wts["wu"], wts["wg"], conv_w, conv_b,
                         wts["wd"], tiles["ffn"], s, SUBLANES)
    conv_new = utail[-(CONV_W - 1):]
    return y, pr["ka_f"], pr["va_f"], pr["ki_f"], ret_new, conv_new, mk, mv


def _sample_layer(x, wts, norms, conv_w, conv_b, cache_k, cache_v, cache_kidx, mem_k, mem_v,
                  state_ret, state_conv, page_table):
    db, t, d = x.shape
    f = wts["wd"].shape[0]
    n_pages = page_table.shape[1]
    page = cache_k.shape[1]
    past = n_pages * page
    rows = db * T_PAD
    xp = jnp.pad(x, ((0, 0), (0, T_PAD - t), (0, 0))).reshape(rows, d)
    pos = jnp.tile(past + jnp.arange(T_PAD), db)
    pr = _projections(xp, norms["pre_mix"], pos, wts, rows)
    hd = N_HEADS_A * HEAD_DIM_A

    topk = min(TOPK_MAX, (past + t) // 4)
    i_past = _sample_index(page_table, pr["qi_hm"].astype(F32), pr["wi_f"], cache_kidx.transpose(0, 2, 1)).reshape(rows, past)
    m_past, m_new = _sample_select(i_past, pr["qi_hm"], pr["wi_f"], pr["ki_b"], t, topk)
    nnew = m_new.shape[1]
    own = m_new[:, :rows].reshape(db, T_PAD, db, T_PAD)[jnp.arange(db), :, jnp.arange(db), :]
    m_new_own = jnp.pad(own, ((0, 0), (0, 0), (0, LANES - T_PAD)))
    q_rows = pr["qa_hm"].reshape(N_HEADS_A, db, T_PAD, HEAD_DIM_A)
    eye = jnp.eye(N_HEADS_A, dtype=MXU_DTYPE)
    q_bd = jnp.einsum("hbtd,hg->bhtgd", q_rows, eye).reshape(db, N_HEADS_A * T_PAD, hd)
    new_t = lambda a: jnp.pad(_mx(a).reshape(db, T_PAD, hd).transpose(0, 2, 1), ((0, 0), (0, 0), (0, LANES - T_PAD)))
    paged_t = lambda c: c.transpose(0, 2, 3, 1).reshape(c.shape[0], hd, page)
    o_a = _sample_attend(page_table, q_bd, m_past.reshape(db, T_PAD, past), m_new_own,
                         new_t(pr["ka_f"]), new_t(pr["va_f"]), paged_t(cache_k), paged_t(cache_v)).reshape(rows, hd)

    padc = lambda a: jnp.pad(a.reshape(db, T_PAD, -1), ((0, 0), (0, RET_CHUNK - T_PAD), (0, 0))).reshape(db * RET_CHUNK, -1)
    o_r, ret_new = _retention(padc(pr["qr_f"]), padc(pr["kr_f"]), padc(pr["vr_b"]), padc(pr["gr_f"]), state_ret, t)
    o_r = o_r.reshape(db, RET_CHUNK, -1)[:, :T_PAD].reshape(rows, -1)

    wm = N_HEADS_M * HEAD_DIM_M
    o_m = _cross_attend(pr["qm_b"].astype(F32), mem_k.reshape(db, -1, wm), mem_v.reshape(db, -1, wm), T_PAD, F32)

    h = _merge(xp, o_a, o_r, o_m, pr["gates_f"], wts["wpa"], wts["wpb"], wts["wpc"], wts["wo"],
               norms["post_mix"], rows)
    s0e = jnp.repeat(state_conv[:, 0], T_PAD, axis=0)
    s1e = jnp.repeat(state_conv[:, 1], T_PAD, axis=0)
    y, u_all = _conv_ffn(h, s0e, s1e, norms["pre_ffn"], norms["post_ffn"], wts["wu"], wts["wg"], conv_w, conv_b,
                         wts["wd"], rows, T_PAD, rows)
    ext = jnp.concatenate([state_conv.astype(F32), u_all.reshape(db, T_PAD, f)[:, :t]], axis=1)
    conv_new = ext[:, t:]
    unpad = lambda a: a.reshape(db, T_PAD, -1)[:, :t]
    return (unpad(y), unpad(pr["ka_f"]), unpad(pr["va_f"]), unpad(pr["ki_f"]), ret_new, conv_new)


PROMPT_TILES = dict(proj=512, sel_q=256, att_q=256, att_k=1024, cross=512, merge=512, ffn=256)


def kernel(x_prompt, x_sample, cache_k, cache_v, cache_kidx, cache_mem_k, cache_mem_v, state_ret, state_conv,
           page_table, mem_prompt, norm_pre_mix, norm_post_mix, norm_pre_ffn, norm_post_ffn, norm_mem,
           w_in, w_mem_kv, w_proj_a, w_proj_b, w_proj_c, w_out, w_up, conv_w, conv_b, w_down):
    bp, s, d = x_prompt.shape
    db, t, _ = x_sample.shape
    depth = w_in.shape[0]
    assert bp == 1 and t <= T_PAD and CONV_W - 1 <= t
    tiles = {k: min(v, s) for k, v in PROMPT_TILES.items()}
    yp, ys = x_prompt[0], x_sample
    outs = [[] for _ in range(12)]
    for l in range(depth):
        wts = _layer_weights(l, w_in, w_proj_a, w_proj_b, w_proj_c, w_out, w_up, w_down)
        norms = dict(pre_mix=norm_pre_mix[l], post_mix=norm_post_mix[l], pre_ffn=norm_pre_ffn[l],
                     post_ffn=norm_post_ffn[l], mem=norm_mem[l])
        yp, kp, vp, kip, rp, cp, mk, mv = _prompt_layer(yp, mem_prompt[0], wts, norms, conv_w[l], conv_b[l],
                                                        w_mem_kv[l], tiles)
        ys, ks, vs, kis, rs, cs = _sample_layer(ys, wts, norms, conv_w[l], conv_b[l], cache_k[l], cache_v[l],
                                                cache_kidx[l], cache_mem_k[l], cache_mem_v[l], state_ret[l],
                                                state_conv[l], page_table)
        n_mem = mk.shape[0]
        vals = (kp.reshape(1, s, N_HEADS_A, HEAD_DIM_A), vp.reshape(1, s, N_HEADS_A, HEAD_DIM_A),
                kip.reshape(1, s, IDX_DIM), rp, cp[None],
                mk.reshape(1, n_mem, N_HEADS_M, HEAD_DIM_M), mv.reshape(1, n_mem, N_HEADS_M, HEAD_DIM_M),
                ks.reshape(db, t, N_HEADS_A, HEAD_DIM_A), vs.reshape(db, t, N_HEADS_A, HEAD_DIM_A),
                kis, rs, cs)
        for o, v in zip(outs, vals):
            o.append(v)
    stacked = [jnp.stack(o) for o in outs]
    return (yp[None], ys, *stacked)
```

```python
import functools
import math

import numpy as np
import jax
import jax.numpy as jnp
from jax import lax
from jax.experimental import pallas as pl
from jax.experimental.pallas import tpu as pltpu

F32 = jnp.float32
I32 = jnp.int32
MXU_DTYPE = jnp.bfloat16

N_HEADS_A, HEAD_DIM_A = 8, 64
IDX_HEADS, IDX_DIM = 4, 64
TOPK_MAX = 256
N_HEADS_R, DK_R, DV_R = 4, 128, 256
RET_CHUNK = 128
N_HEADS_M, HEAD_DIM_M = 4, 128
CONV_W = 3
ROPE_THETA = 10000.0
EPS = 1e-6

LANES = 128
SUBLANES = 8
BF16_ROWS = 16
VMEM_LIMIT = 56 * 1024 * 1024
NEG_BIG = -1e30
F32_LOWEST = float(np.finfo(np.float32).min)
T_PAD = 8


def _cparams(sem):
    return pltpu.CompilerParams(dimension_semantics=sem, vmem_limit_bytes=VMEM_LIMIT)


def _dot(a, b):
    return jnp.dot(a, b, preferred_element_type=F32)


def _dot_nt(a, b):
    return lax.dot_general(a, b, (((1,), (1,)), ((), ())), preferred_element_type=F32)


def _mx(a):
    return a.astype(MXU_DTYPE)


def _rms(x, g):
    return x * lax.rsqrt(jnp.mean(x * x, axis=-1, keepdims=True) + EPS) * g


def _sigmoid(x):
    return 1.0 / (1.0 + jnp.exp(-x))


def _rope_tables(pos, d):
    half = d // 2
    inv = 1.0 / (ROPE_THETA ** (jnp.arange(half, dtype=F32) * 2.0 / d))
    ang = pos.astype(F32)[:, None] * inv[None, :]
    cos, sin = jnp.cos(ang), jnp.sin(ang)
    reps = LANES // d
    cos_t = jnp.tile(jnp.concatenate([cos, cos], axis=1), (1, reps))
    sin_t = jnp.tile(jnp.concatenate([-sin, sin], axis=1), (1, reps))
    return cos_t, sin_t


def _rope(y, cos, sin, d):
    w = y.shape[1]
    half = d // 2
    reps = w // LANES
    c = jnp.concatenate([cos] * reps, axis=1) if reps > 1 else cos
    s = jnp.concatenate([sin] * reps, axis=1) if reps > 1 else sin
    lane = lax.broadcasted_iota(I32, y.shape, 1)
    first = (lane & (d - 1)) < half
    rot = jnp.where(first, pltpu.roll(y, w - half, 1), pltpu.roll(y, half, 1))
    return y * c + rot * s


def _proj_a_kernel(x_ref, g_ref, cos_ref, sin_ref, wa_ref, wv_ref,
                   qa_hm, qa_t, ka_f, ka_hm, qi_hm, ki_f, ki_b, va_f, va_t, wi_f):
    xn = _mx(_rms(x_ref[...], g_ref[...]))
    wa = N_HEADS_A * HEAD_DIM_A
    y = _rope(_dot(xn, wa_ref[...]), cos_ref[...], sin_ref[...], HEAD_DIM_A)
    q = y[:, :wa] * (HEAD_DIM_A ** -0.5)
    k = y[:, wa:2 * wa]
    ka_f[...] = k
    qt = q.T
    for h in range(N_HEADS_A):
        sl = slice(h * HEAD_DIM_A, (h + 1) * HEAD_DIM_A)
        qa_hm[h] = _mx(q[:, sl])
        qa_t[h] = _mx(qt[sl, :])
        ka_hm[h] = _mx(k[:, sl])
    qi = y[:, 2 * wa:2 * wa + IDX_HEADS * IDX_DIM]
    for h in range(IDX_HEADS):
        qi_hm[h] = _mx(qi[:, h * IDX_DIM:(h + 1) * IDX_DIM])
    ki = y[:, 2 * wa + IDX_HEADS * IDX_DIM:2 * wa + IDX_HEADS * IDX_DIM + IDX_DIM]
    ki_f[...] = ki
    ki_b[...] = _mx(ki)
    z = _dot(xn, wv_ref[...])
    v = z[:, :wa]
    va_f[...] = v
    vt = v.T
    for h in range(N_HEADS_A):
        va_t[h] = _mx(vt[h * HEAD_DIM_A:(h + 1) * HEAD_DIM_A, :])
    wi_f[...] = z[:, wa:wa + LANES]


def _proj_b_kernel(x_ref, g_ref, cos_ref, sin_ref, wr_ref, wvg_ref, qr_f, kr_f, vr_b, gr_f):
    xn = _mx(_rms(x_ref[...], g_ref[...]))
    wr = N_HEADS_R * DK_R
    y = _rope(_dot(xn, wr_ref[...]), cos_ref[...], sin_ref[...], DK_R)
    qr_f[...] = y[:, :wr]
    kr_f[...] = y[:, wr:] * (DK_R ** -0.5)
    z = _dot(xn, wvg_ref[...])
    wv = N_HEADS_R * DV_R
    vr_b[...] = _mx(z[:, :wv])
    gr_f[...] = z[:, wv:]


def _proj_c_kernel(x_ref, g_ref, wc_ref, qm_b, gates_f):
    xn = _mx(_rms(x_ref[...], g_ref[...]))
    z = _dot(xn, wc_ref[...])
    wm = N_HEADS_M * HEAD_DIM_M
    qm_b[...] = _mx(z[:, :wm])
    gates_f[...] = z[:, wm:]


def _row_spec(tm, w):
    return pl.BlockSpec((tm, w), lambda i: (i, 0))


def _full_spec(shape):
    nd = len(shape)
    return pl.BlockSpec(shape, lambda i: (0,) * nd)


def _hm_spec(nh, tm, d):
    return pl.BlockSpec((nh, tm, d), lambda i: (0, i, 0))


def _projections(x, gain, pos, wts, tm):
    rows, d = x.shape
    grid = (rows // tm,)
    wa = N_HEADS_A * HEAD_DIM_A
    cos64, sin64 = _rope_tables(pos, HEAD_DIM_A)
    cos128, sin128 = _rope_tables(pos, DK_R)
    g2 = gain.reshape(1, d)
    sds = jax.ShapeDtypeStruct
    t_spec = pl.BlockSpec((N_HEADS_A, HEAD_DIM_A, tm), lambda i: (0, 0, i))
    outs_a = pl.pallas_call(
        _proj_a_kernel,
        grid=grid,
        in_specs=[_row_spec(tm, d), _full_spec((1, d)), _row_spec(tm, LANES), _row_spec(tm, LANES),
                  _full_spec(wts["wa"].shape), _full_spec(wts["wv"].shape)],
        out_specs=[_hm_spec(N_HEADS_A, tm, HEAD_DIM_A), t_spec, _row_spec(tm, wa),
                   _hm_spec(N_HEADS_A, tm, HEAD_DIM_A),
                   _hm_spec(IDX_HEADS, tm, IDX_DIM), _row_spec(tm, IDX_DIM), _row_spec(tm, IDX_DIM),
                   _row_spec(tm, wa), t_spec, _row_spec(tm, LANES)],
        out_shape=[sds((N_HEADS_A, rows, HEAD_DIM_A), MXU_DTYPE), sds((N_HEADS_A, HEAD_DIM_A, rows), MXU_DTYPE),
                   sds((rows, wa), F32),
                   sds((N_HEADS_A, rows, HEAD_DIM_A), MXU_DTYPE), sds((IDX_HEADS, rows, IDX_DIM), MXU_DTYPE),
                   sds((rows, IDX_DIM), F32), sds((rows, IDX_DIM), MXU_DTYPE),
                   sds((rows, wa), F32), sds((N_HEADS_A, HEAD_DIM_A, rows), MXU_DTYPE), sds((rows, LANES), F32)],
        compiler_params=_cparams(("parallel",)),
        name="proj_a",
    )(x, g2, cos64, sin64, wts["wa"], wts["wv"])
    names_a = ("qa_hm", "qa_t", "ka_f", "ka_hm", "qi_hm", "ki_f", "ki_b", "va_f", "va_t", "wi_f")
    wr, wv = N_HEADS_R * DK_R, N_HEADS_R * DV_R
    outs_b = pl.pallas_call(
        _proj_b_kernel,
        grid=grid,
        in_specs=[_row_spec(tm, d), _full_spec((1, d)), _row_spec(tm, LANES), _row_spec(tm, LANES),
                  _full_spec(wts["wr"].shape), _full_spec(wts["wvg"].shape)],
        out_specs=[_row_spec(tm, wr), _row_spec(tm, wr), _row_spec(tm, wv), _row_spec(tm, wv)],
        out_shape=[sds((rows, wr), F32), sds((rows, wr), F32), sds((rows, wv), MXU_DTYPE), sds((rows, wv), F32)],
        compiler_params=_cparams(("parallel",)),
        name="proj_b",
    )(x, g2, cos128, sin128, wts["wr"], wts["wvg"])
    names_b = ("qr_f", "kr_f", "vr_b", "gr_f")
    wm = N_HEADS_M * HEAD_DIM_M
    outs_c = pl.pallas_call(
        _proj_c_kernel,
        grid=grid,
        in_specs=[_row_spec(tm, d), _full_spec((1, d)), _full_spec(wts["wc"].shape)],
        out_specs=[_row_spec(tm, wm), _row_spec(tm, 3 * d)],
        out_shape=[sds((rows, wm), MXU_DTYPE), sds((rows, 3 * d), F32)],
        compiler_params=_cparams(("parallel",)),
        name="proj_c",
    )(x, g2, wts["wc"])
    names_c = ("qm_b", "gates_f")
    out = dict(zip(names_a, outs_a))
    out.update(zip(names_b, outs_b))
    out.update(zip(names_c, outs_c))
    return out


def _prep_in_weights(w_in):
    d = w_in.shape[0]
    wa = N_HEADS_A * HEAD_DIM_A
    widths = (wa, wa, wa, IDX_HEADS * IDX_DIM, IDX_DIM, IDX_HEADS,
              N_HEADS_R * DK_R, N_HEADS_R * DK_R, N_HEADS_R * DV_R, N_HEADS_R * DV_R,
              N_HEADS_M * HEAD_DIM_M, 3 * d)
    offs = np.concatenate([[0], np.cumsum(widths)])
    seg = [w_in[:, int(offs[i]):int(offs[i + 1])] for i in range(len(widths))]
    q_a, k_a, v_a, q_i, k_i, w_i, q_r, k_r, v_r, g_r, q_m, gates = seg
    zpad = lambda n: jnp.zeros((d, n), w_in.dtype)
    return {
        "wa": _mx(jnp.concatenate([q_a, k_a, q_i, k_i, zpad(LANES - IDX_DIM)], axis=1)),
        "wv": _mx(jnp.concatenate([v_a, w_i, zpad(LANES - IDX_HEADS)], axis=1)),
        "wr": _mx(jnp.concatenate([q_r, k_r], axis=1)),
        "wvg": _mx(jnp.concatenate([v_r, g_r], axis=1)),
        "wc": _mx(jnp.concatenate([q_m, gates], axis=1)),
    }


SEL_ROWS = 128
SEL_WIDE = 512
SEL_GROUPS = 2 * LANES
NO_LIMIT = 2 ** 30
MIN_NORMAL_KEY = 0x00800000
SEARCH_PERIOD = 4
SEARCH_CAP = SEARCH_PERIOD * 33


def _key_to_f32(key):
    bits = jnp.where(key >= 0, key, key ^ jnp.int32(0x7FFFFFFF))
    return pltpu.bitcast(bits, F32)


def _f32_to_key(f):
    bits = pltpu.bitcast(f, I32)
    return jnp.where(f == 0.0, 0, jnp.where(bits >= 0, bits, bits ^ jnp.int32(0x7FFFFFFF)))


def _sweep(segments, rs, init, fn):
    acc = init
    base = 0
    for ref, n_wide in segments:
        def body(c, a, ref=ref, base=base):
            off = c * SEL_WIDE
            for k in range(SEL_WIDE // LANES):
                x = ref[rs, pl.ds(pl.multiple_of(off + k * LANES, LANES), LANES)]
                a = fn(a, x, base + off + k * LANES, k)
            return a
        pairs = n_wide // 2
        acc = lax.fori_loop(0, pairs, lambda c, a, body=body: body(2 * c + 1, body(2 * c, a)), acc)
        acc = lax.fori_loop(2 * pairs, n_wide, body, acc)
        base = base + n_wide * SEL_WIDE
    return acc


def _count(segments, rows, make_pred):
    outs = []
    for g in range(rows // SEL_ROWS):
        rs = slice(g * SEL_ROWS, (g + 1) * SEL_ROWS)
        pred = make_pred(rs)
        acc = _sweep(segments, rs, jnp.zeros((SEL_ROWS, LANES), F32),
                     lambda a, x, idx0, k: a + jnp.where(pred(x, idx0), 1.0, 0.0))
        outs.append(jnp.sum(acc.T, axis=0, keepdims=True))
    return jnp.concatenate(outs, axis=1) if len(outs) > 1 else outs[0]


def _col_to_row(col):
    return jnp.broadcast_to(col, (col.shape[0], LANES)).T[0:1]


def _row_to_lanes(vec):
    return jnp.broadcast_to(vec, (LANES, vec.shape[1])).T


def _search(count_fn, lo, hi, c_lo, c_hi, target, alive):
    def unfinished(lo, hi, c_lo):
        return jnp.logical_and(alive, jnp.logical_and(c_lo > target, lo + 1 < hi))

    def any_row(flag):
        return jnp.max(jnp.where(flag, 1, 0).astype(I32))

    def cond(carry):
        return jnp.logical_and(carry[0] < SEARCH_CAP, carry[1] > 0)

    def body(carry):
        it, _, phase, lo, hi, c_lo, c_hi, w_lo, w_hi, last = carry
        act = unfinished(lo, hi, c_lo)
        bis = (lo >> 1) + (hi >> 1) + (lo & hi & 1)
        width = hi - lo
        a = (c_lo - target + 0.5) * w_lo
        b = (target - 0.5 - c_hi) * w_hi
        frac = a / jnp.maximum(a + b, 1e-6)
        step = (frac * width.astype(F32)).astype(I32)
        itp = lo + jnp.clip(step, 1, jnp.maximum(width - 1, 1))
        use_itp = jnp.logical_and((lo ^ hi) >= 0, (jnp.zeros_like(lo) + phase) != SEARCH_PERIOD - 1)
        v = jnp.where(use_itp, itp, bis)
        cnt = count_fn(v)
        up = jnp.logical_and(act, cnt >= target)
        dn = jnp.logical_and(act, cnt < target)
        lo = jnp.where(up, v, lo)
        c_lo = jnp.where(up, cnt, c_lo)
        hi = jnp.where(dn, v, hi)
        c_hi = jnp.where(dn, cnt, c_hi)
        w_hi = jnp.where(up, jnp.where(last == 1, w_hi * 0.5, 1.0), jnp.where(dn, 1.0, w_hi))
        w_lo = jnp.where(dn, jnp.where(last == -1, w_lo * 0.5, 1.0), jnp.where(up, 1.0, w_lo))
        last = jnp.where(up, 1, jnp.where(dn, -1, last))
        phase = jnp.where(phase == SEARCH_PERIOD - 1, 0, phase + 1)
        return it + 1, any_row(unfinished(lo, hi, c_lo)), phase, lo, hi, c_lo, c_hi, w_lo, w_hi, last

    go = any_row(unfinished(lo, hi, c_lo))
    one = jnp.ones(lo.shape, F32)
    out = lax.while_loop(cond, body, (jnp.int32(0), go, jnp.int32(0), lo, hi, c_lo, c_hi,
                                      one, one, jnp.zeros(lo.shape, I32)))
    return out[3], out[5], out[6]


def _select_threshold(segments, rows, topk, alive=None):
    assert topk <= SEL_GROUPS
    kf = jnp.full((1, rows), float(topk), F32)
    if alive is None:
        alive = jnp.full((1, rows), True)

    los, his = [], []
    for g in range(rows // SEL_ROWS):
        rs = slice(g * SEL_ROWS, (g + 1) * SEL_ROWS)
        ninf = jnp.full((SEL_ROWS, LANES), -jnp.inf, F32)
        ga, gb = _sweep(segments, rs, (ninf, ninf),
                        lambda a, x, idx0, k: ((jnp.maximum(a[0], x), a[1]) if k % 2 == 0
                                               else (a[0], jnp.maximum(a[1], x))))
        los.append(jnp.min(jnp.minimum(ga, gb), axis=1, keepdims=True))
        his.append(jnp.max(jnp.maximum(ga, gb), axis=1, keepdims=True))
    cat = lambda xs: _col_to_row(jnp.concatenate(xs, axis=0) if len(xs) > 1 else xs[0])
    lo = _f32_to_key(jnp.maximum(cat(los), F32_LOWEST))
    hi = _f32_to_key(cat(his)) + 1

    def count_ge_f(thr):
        thr_l = _row_to_lanes(thr)

        def make_pred(rs):
            t = thr_l[rs]
            return lambda x, idx0: x >= t
        return _count(segments, rows, make_pred)

    count_ge = lambda v: count_ge_f(_key_to_f32(v))
    c_lo = count_ge(lo)
    c_hi = jnp.zeros((1, rows), F32)
    def zero_probes(state):
        lo, hi, c_lo, c_hi = state
        for probe in (0, MIN_NORMAL_KEY):
            v = jnp.full((1, rows), probe, I32)
            cnt = count_ge_f(jnp.full((1, rows), np.int32(probe).view(np.float32), F32))
            inside = jnp.logical_and(lo < v, v < hi)
            up = jnp.logical_and(inside, cnt >= kf)
            dn = jnp.logical_and(inside, cnt < kf)
            lo, c_lo = jnp.where(up, v, lo), jnp.where(up, cnt, c_lo)
            hi, c_hi = jnp.where(dn, v, hi), jnp.where(dn, cnt, c_hi)
        return lo, hi, c_lo, c_hi

    straddles = jnp.logical_or(jnp.logical_and(lo < 0, hi > 0),
                               jnp.logical_and(lo < MIN_NORMAL_KEY, hi > MIN_NORMAL_KEY))
    lo, hi, c_lo, c_hi = lax.cond(jnp.max(jnp.where(straddles, 1, 0).astype(I32)) > 0,
                                  zero_probes, lambda s: s, (lo, hi, c_lo, c_hi))
    at_zero = jnp.logical_and(lo == 0, hi == MIN_NORMAL_KEY)
    lo, c_lo, c_hi = _search(count_ge, lo, hi, c_lo, c_hi, kf, jnp.logical_and(alive, jnp.logical_not(at_zero)))
    tau = _key_to_f32(lo)

    need = jnp.logical_and(alive, c_lo > kf)
    room = jnp.where(need, kf - c_hi, float(NO_LIMIT))
    return _row_to_lanes(tau), _row_to_lanes(room)


def _write_mask(segments_out, rows, tau, room):
    r_io = lax.broadcasted_iota(I32, (LANES, 2 * LANES), 0)
    c_io = lax.broadcasted_iota(I32, (LANES, 2 * LANES), 1)
    tri = jnp.where(jnp.logical_or(r_io <= c_io, c_io >= LANES), 1.0, 0.0).astype(MXU_DTYPE)
    for g in range(rows // SEL_ROWS):
        rs = slice(g * SEL_ROWS, (g + 1) * SEL_ROWS)
        t, rm = tau[rs], room[rs]
        seen = jnp.zeros((SEL_ROWS, LANES), F32)
        for src, dst, n_wide, n_total in segments_out:
            def body(c, seen, src=src, dst=dst):
                off = c * SEL_WIDE
                nsub = SEL_WIDE // LANES
                dss = [pl.ds(pl.multiple_of(off + k * LANES, LANES), LANES) for k in range(nsub)]
                xs = [src[rs, ds] for ds in dss]
                eqs = [jnp.where(x == t, 1.0, 0.0) for x in xs]
                run = jnp.dot(jnp.concatenate(eqs, axis=0).astype(MXU_DTYPE), tri, preferred_element_type=F32)
                for k in range(nsub):
                    run_k = run[k * SEL_ROWS:(k + 1) * SEL_ROWS]
                    before = seen + run_k[:, :LANES] - eqs[k]
                    keep_eq = jnp.where(before < rm, eqs[k], 0.0)
                    dst[rs, dss[k]] = jnp.where(xs[k] > t, 1.0, keep_eq).astype(dst.dtype)
                    seen = seen + run_k[:, LANES:]
                return seen

            seen = lax.fori_loop(0, n_wide, body, seen)

            def zbody(c, carry, dst=dst):
                off = pl.multiple_of(c * SEL_WIDE, SEL_WIDE)
                dst[rs, pl.ds(off, SEL_WIDE)] = jnp.zeros((SEL_ROWS, SEL_WIDE), dst.dtype)
                return carry

            lax.fori_loop(n_wide, n_total, zbody, 0)


def _index_scores(qi_ref, w, kb, transposed_keys=False):
    acc = None
    for h in range(IDX_HEADS):
        s = _dot(_mx(qi_ref[h]), kb) if transposed_keys else _dot_nt(_mx(qi_ref[h]), kb)
        t = w[:, h:h + 1] * jnp.maximum(s, 0.0)
        acc = t if acc is None else acc + t
    return acc


def _causal_pairs(nq, tq, tk):
    pairs = [(i, j) for i in range(nq) for j in range((i * tq + tq - 1) // tk + 1)]
    return jnp.asarray([p[0] for p in pairs], I32), jnp.asarray([p[1] for p in pairs], I32)


def _prompt_select_kernel(qb_ref, kb_ref, qi_ref, wi_ref, kidx_ref, mask_ref, i_scr, *, tq, tk, topk):
    p = pl.program_id(0)
    i, j = qb_ref[p], kb_ref[p]
    q_lo = i * tq
    n_wide = (q_lo + tq - 1) // tk + 1

    acc = _index_scores(qi_ref, wi_ref[...], kidx_ref[...])
    cols = pl.ds(pl.multiple_of(j * tk, tk), tk)
    below_diagonal = (j + 1) * tk <= q_lo + 1

    @pl.when(below_diagonal)
    def _():
        i_scr[:, cols] = acc

    @pl.when(jnp.logical_not(below_diagonal))
    def _():
        qpos = q_lo + lax.broadcasted_iota(I32, acc.shape, 0)
        kpos = j * tk + lax.broadcasted_iota(I32, acc.shape, 1)
        i_scr[:, cols] = jnp.where(kpos <= qpos, acc, -jnp.inf)

    @pl.when(j == n_wide - 1)
    def _():
        seg = [(i_scr, n_wide)]
        tau, room = _select_threshold(seg, tq, topk)
        _write_mask([(i_scr, mask_ref, n_wide, mask_ref.shape[1] // tk)], tq, tau, room)


def _prompt_select(qi_hm, wi_f, ki_b, topk, tq, tk):
    s = ki_b.shape[0]
    assert tk == SEL_WIDE and s % tk == 0 and s % tq == 0 and tq % SEL_ROWS == 0
    qb, kb = _causal_pairs(s // tq, tq, tk)
    grid_spec = pltpu.PrefetchScalarGridSpec(
        num_scalar_prefetch=2,
        grid=(qb.shape[0],),
        in_specs=[pl.BlockSpec((IDX_HEADS, tq, IDX_DIM), lambda p, qb, kb: (0, qb[p], 0)),
                  pl.BlockSpec((tq, LANES), lambda p, qb, kb: (qb[p], 0)),
                  pl.BlockSpec((tk, IDX_DIM), lambda p, qb, kb: (kb[p], 0))],
        out_specs=pl.BlockSpec((tq, s), lambda p, qb, kb: (qb[p], 0), pipeline_mode=pl.Buffered(1)),
        scratch_shapes=[pltpu.VMEM((tq, s), F32)],
    )
    return pl.pallas_call(
        functools.partial(_prompt_select_kernel, tq=tq, tk=tk, topk=topk),
        grid_spec=grid_spec,
        out_shape=jax.ShapeDtypeStruct((s, s), MXU_DTYPE),
        compiler_params=_cparams(("arbitrary",)),
        name="prompt_select",
    )(qb, kb, qi_hm, wi_f, ki_b)


def _spread(a, n):
    if n <= LANES:
        return a[:, :n]
    return jnp.concatenate([a] * (n // LANES), axis=1)


def _flash_update(s, v, m_prev, l_prev, acc_prev):
    m_new = jnp.maximum(m_prev, jnp.max(s, axis=1, keepdims=True))
    alpha = jnp.exp(m_prev - m_new)
    p = jnp.exp(s - _spread(m_new, s.shape[1]))
    l_new = alpha * l_prev + jnp.sum(p, axis=1, keepdims=True)
    acc_new = _spread(alpha, acc_prev.shape[1]) * acc_prev + _dot(_mx(p), v)
    return m_new, l_new, acc_new


def _prompt_attend_kernel(qb_ref, kb_ref, qt_ref, k_ref, vt_ref, mask_ref, o_ref, m_scr, l_scr, acc_scr, *, tq, tk):
    p_id = pl.program_id(0)
    i, j = qb_ref[p_id], kb_ref[p_id]

    @pl.when(j == 0)
    def _():
        m_scr[...] = jnp.full(m_scr.shape, NEG_BIG, F32)
        l_scr[...] = jnp.zeros(l_scr.shape, F32)
        acc_scr[...] = jnp.zeros(acc_scr.shape, F32)

    bias = (1.0 - mask_ref[...].astype(F32).T) * NEG_BIG
    scores = [_dot(k_ref[h], qt_ref[h]) + bias for h in range(N_HEADS_A)]
    for h, s in enumerate(scores):
        m_prev = m_scr[h]
        m_new = jnp.maximum(m_prev, jnp.max(s, axis=0, keepdims=True))
        alpha = jnp.exp(m_prev - m_new)
        p = jnp.exp(s - m_new[0:1])
        l_scr[h] = alpha * l_scr[h] + jnp.sum(p, axis=0, keepdims=True)
        acc_scr[h] = alpha[0:1] * acc_scr[h] + _dot(vt_ref[h], _mx(p))
        m_scr[h] = m_new

    @pl.when(j == (i * tq + tq - 1) // tk)
    def _():
        ot = jnp.concatenate([acc_scr[h] / l_scr[h][0:1] for h in range(N_HEADS_A)], axis=0)
        o_ref[...] = ot.T.astype(o_ref.dtype)


def _prompt_attend(qa_t, ka_hm, va_t, mask, tq, tk):
    nh, dh, s = qa_t.shape
    qb, kb = _causal_pairs(s // tq, tq, tk)
    grid_spec = pltpu.PrefetchScalarGridSpec(
        num_scalar_prefetch=2,
        grid=(qb.shape[0],),
        in_specs=[pl.BlockSpec((nh, dh, tq), lambda p, qb, kb: (0, 0, qb[p])),
                  pl.BlockSpec((nh, tk, dh), lambda p, qb, kb: (0, kb[p], 0)),
                  pl.BlockSpec((nh, dh, tk), lambda p, qb, kb: (0, 0, kb[p])),
                  pl.BlockSpec((tq, tk), lambda p, qb, kb: (qb[p], kb[p]))],
        out_specs=pl.BlockSpec((tq, nh * dh), lambda p, qb, kb: (qb[p], 0)),
        scratch_shapes=[pltpu.VMEM((nh, SUBLANES, tq), F32), pltpu.VMEM((nh, SUBLANES, tq), F32),
                        pltpu.VMEM((nh, dh, tq), F32)],
    )
    return pl.pallas_call(
        functools.partial(_prompt_attend_kernel, tq=tq, tk=tk),
        grid_spec=grid_spec,
        out_shape=jax.ShapeDtypeStruct((s, nh * dh), MXU_DTYPE),
        compiler_params=_cparams(("arbitrary",)),
        name="prompt_attend",
    )(qb, kb, qa_t, ka_hm, va_t, mask)


PAGES_PER_STEP = 8
INDEX_PAGES_PER_STEP = 16


def _sample_index_kernel(pt_ref, qi_ref, wi_ref, *refs, page):
    del pt_ref
    pages, out_ref = refs[:-1], refs[-1]
    w = wi_ref[0]
    for p, kref in enumerate(pages):
        out_ref[0, :, p * page:(p + 1) * page] = _index_scores(qi_ref, w, _mx(kref[0]), transposed_keys=True)


def _sample_index(page_table, qi_hm, wi_f, cache_kidx_t):
    db, n_pages = page_table.shape
    _, idim, page = cache_kidx_t.shape
    pps = math.gcd(INDEX_PAGES_PER_STEP, n_pages)
    nsteps = n_pages // pps
    pt = page_table.reshape(-1).astype(I32)

    def kspec(p):
        return pl.BlockSpec((1, idim, page), lambda b, j, pt: (pt[b * n_pages + j * pps + p], 0, 0))

    grid_spec = pltpu.PrefetchScalarGridSpec(
        num_scalar_prefetch=1,
        grid=(db, nsteps),
        in_specs=[pl.BlockSpec((IDX_HEADS, T_PAD, idim), lambda b, j, pt: (0, b, 0)),
                  pl.BlockSpec((1, T_PAD, LANES), lambda b, j, pt: (b, 0, 0))]
                 + [kspec(p) for p in range(pps)],
        out_specs=pl.BlockSpec((1, T_PAD, pps * page), lambda b, j, pt: (b, 0, j)),
    )
    return pl.pallas_call(
        functools.partial(_sample_index_kernel, page=page),
        grid_spec=grid_spec,
        out_shape=jax.ShapeDtypeStruct((db, T_PAD, n_pages * page), F32),
        compiler_params=_cparams(("parallel", "arbitrary")),
        name="sample_index",
    )(pt, qi_hm, wi_f.reshape(db, T_PAD, LANES), *([cache_kidx_t] * pps))


def _sample_select_kernel(ipast_ref, qi_ref, wi_ref, kin_ref, mpast_ref, mnew_ref, inew_scr, *, t_real, topk):
    rows = ipast_ref.shape[0]
    acc = _index_scores(qi_ref, wi_ref[...], kin_ref[...])
    r = lax.broadcasted_iota(I32, acc.shape, 0)
    c = lax.broadcasted_iota(I32, acc.shape, 1)
    same = (r // T_PAD) == (c // T_PAD)
    tq, tc = r % T_PAD, c % T_PAD
    ok = jnp.logical_and(same, jnp.logical_and(tc <= tq, tc < t_real))
    inew_scr[...] = jnp.full(inew_scr.shape, -jnp.inf, F32)
    inew_scr[:, :rows] = jnp.where(ok, acc, -jnp.inf)
    n_past = ipast_ref.shape[1] // SEL_WIDE
    n_new = inew_scr.shape[1] // SEL_WIDE
    alive = lax.rem(lax.broadcasted_iota(I32, (1, rows), 1), T_PAD) < t_real
    tau, room = _select_threshold([(ipast_ref, n_past), (inew_scr, n_new)], rows, topk, alive)
    _write_mask([(ipast_ref, mpast_ref, n_past, n_past), (inew_scr, mnew_ref, n_new, n_new)], rows, tau, room)


def _sample_select(i_past, qi_hm, wi_f, ki_b, t_real, topk):
    rows, past = i_past.shape
    assert past % SEL_WIDE == 0 and rows % SEL_ROWS == 0
    wnew = -(-rows // SEL_WIDE) * SEL_WIDE
    return pl.pallas_call(
        functools.partial(_sample_select_kernel, t_real=t_real, topk=topk),
        out_shape=[jax.ShapeDtypeStruct((rows, past), F32), jax.ShapeDtypeStruct((rows, wnew), F32)],
        scratch_shapes=[pltpu.VMEM((rows, wnew), F32)],
        compiler_params=pltpu.CompilerParams(vmem_limit_bytes=VMEM_LIMIT),
        name="sample_select",
    )(i_past, qi_hm, wi_f, ki_b)


def _sample_attend_kernel(pt_ref, q_ref, mp_ref, mn_ref, kn_ref, vn_ref, *refs, page, pps):
    del pt_ref
    kpages, vpages = refs[:pps], refs[pps:2 * pps]
    o_ref, m_scr, l_scr, acc_scr = refs[2 * pps:]
    j = pl.program_id(1)
    nj = pl.num_programs(1)
    q = q_ref[0]
    nh = q.shape[0] // T_PAD

    def scores(m_t, kt):
        keep = jnp.concatenate([m_t] * nh, axis=0) > 0
        return jnp.where(keep, _dot(q, kt), NEG_BIG)

    def update(s_list, vt_list):
        m_prev = m_scr[...]
        m_blk = functools.reduce(jnp.maximum, [jnp.max(s, axis=1, keepdims=True) for s in s_list])
        m_new = jnp.maximum(m_prev, m_blk)
        alpha = jnp.exp(m_prev - m_new)
        l_new = alpha * l_scr[...]
        acc = _spread(alpha, acc_scr.shape[1]) * acc_scr[...]
        for s, vt in zip(s_list, vt_list):
            p = jnp.exp(s - _spread(m_new, s.shape[1]))
            l_new = l_new + jnp.sum(p, axis=1, keepdims=True)
            acc = acc + _dot_nt(_mx(p), vt)
        m_scr[...], l_scr[...], acc_scr[...] = m_new, l_new, acc

    @pl.when(j == 0)
    def _():
        m_scr[...] = jnp.full(m_scr.shape, NEG_BIG, F32)
        l_scr[...] = jnp.zeros(l_scr.shape, F32)
        acc_scr[...] = jnp.zeros(acc_scr.shape, F32)

    update([scores(mp_ref[0, :, p * page:(p + 1) * page], _mx(kpages[p][0])) for p in range(pps)],
           [_mx(vpages[p][0]) for p in range(pps)])

    @pl.when(j == nj - 1)
    def _():
        update([scores(mn_ref[0], kn_ref[0])], [vn_ref[0]])
        full = acc_scr[...] / _spread(l_scr[...], acc_scr.shape[1])
        lane = lax.broadcasted_iota(I32, (T_PAD, full.shape[1]), 1)
        out = jnp.zeros((T_PAD, full.shape[1]), F32)
        for h in range(nh):
            out = out + jnp.where((lane // HEAD_DIM_A) == h, full[h * T_PAD:(h + 1) * T_PAD], 0.0)
        o_ref[0] = out


def _sample_attend(page_table, q_bd, m_past, m_new, kt_new, vt_new, cache_kt, cache_vt):
    db, n_pages = page_table.shape
    _, hd, page = cache_kt.shape
    pps = math.gcd(PAGES_PER_STEP, n_pages)
    nsteps = n_pages // pps
    nnew = kt_new.shape[2]
    nq = q_bd.shape[1]
    pt = page_table.reshape(-1).astype(I32)

    def pspec(p):
        return pl.BlockSpec((1, hd, page), lambda b, j, pt: (pt[b * n_pages + j * pps + p], 0, 0))

    bspec = lambda shape: pl.BlockSpec((1,) + shape, lambda b, j, pt: (b, 0, 0))
    grid_spec = pltpu.PrefetchScalarGridSpec(
        num_scalar_prefetch=1,
        grid=(db, nsteps),
        in_specs=[bspec((nq, hd)),
                  pl.BlockSpec((1, T_PAD, pps * page), lambda b, j, pt: (b, 0, j)),
                  bspec((T_PAD, nnew)), bspec((hd, nnew)), bspec((hd, nnew))]
                 + [pspec(p) for p in range(pps)] * 2,
        out_specs=bspec((T_PAD, hd)),
        scratch_shapes=[pltpu.VMEM((nq, LANES), F32), pltpu.VMEM((nq, LANES), F32), pltpu.VMEM((nq, hd), F32)],
    )
    return pl.pallas_call(
        functools.partial(_sample_attend_kernel, page=page, pps=pps),
        grid_spec=grid_spec,
        out_shape=jax.ShapeDtypeStruct((db, T_PAD, hd), F32),
        compiler_params=_cparams(("parallel", "arbitrary")),
        name="sample_attend",
    )(pt, q_bd, m_past, m_new, kt_new, vt_new, *([cache_kt] * pps), *([cache_vt] * pps))


def _retention_tables(c_real, c_pad):
    h = np.arange(N_HEADS_R, dtype=np.float64)
    log_g = np.log1p(-np.exp2(-5.0 - h))
    i = np.arange(c_pad, dtype=np.float64)
    diff = i[:, None] - i[None, :]
    live = (diff >= 0) & (i[:, None] < c_real) & (i[None, :] < c_real)
    inner = np.where(live[None], np.exp(np.maximum(diff, 0.0)[None] * log_g[:, None, None]), 0.0)
    q_dec = np.exp((i + 1.0)[None, :] * log_g[:, None])
    k_dec = np.where(i[None, :] < c_real, np.exp((c_real - 1.0 - i)[None, :] * log_g[:, None]), 0.0)
    c_dec = np.exp(c_real * log_g)
    f = lambda a: jnp.asarray(a, F32)
    return f(inner), f(q_dec[:, :, None]), f(k_dec[:, :, None]), [float(v) for v in c_dec]


def _retention_kernel(q_ref, k_ref, v_ref, g_ref, s0_ref, inner_ref, qdec_ref, kdec_ref,
                      o_ref, s_out_ref, s_scr, *, c_dec):
    j = pl.program_id(1)
    nj = pl.num_programs(1)

    @pl.when(j == 0)
    def _():
        s_scr[...] = s0_ref[0]

    for h in range(N_HEADS_R):
        q = q_ref[:, h * DK_R:(h + 1) * DK_R]
        k = k_ref[:, h * DK_R:(h + 1) * DK_R]
        v = v_ref[:, h * DV_R:(h + 1) * DV_R]
        s_prev = s_scr[h]
        a = _dot_nt(_mx(q), _mx(k)) * inner_ref[h]
        o = _dot(_mx(a), v) + _dot(_mx(q), _mx(s_prev)) * qdec_ref[h]
        kd = k * kdec_ref[h]
        s_scr[h] = s_prev * c_dec[h] + _dot(_mx(kd.T), v)
        mu = jnp.mean(o, axis=-1, keepdims=True)
        var = jnp.mean(jnp.square(o - mu), axis=-1, keepdims=True)
        gn = (o - mu) * lax.rsqrt(var + EPS)
        g = g_ref[:, h * DV_R:(h + 1) * DV_R]
        o_ref[:, h * DV_R:(h + 1) * DV_R] = (gn * (g * _sigmoid(g))).astype(o_ref.dtype)

    @pl.when(j == nj - 1)
    def _():
        s_out_ref[0] = s_scr[...]


def _retention(qr, kr, vr, gr, s0, c_real):
    b = s0.shape[0]
    c = RET_CHUNK
    n = qr.shape[0] // (b * c)
    inner, qdec, kdec, c_dec = _retention_tables(c_real, c)
    wr, wv = N_HEADS_R * DK_R, N_HEADS_R * DV_R
    rmap = lambda bi, j: (bi * n + j, 0)
    full3 = lambda shape: pl.BlockSpec(shape, lambda bi, j: (0, 0, 0))
    return pl.pallas_call(
        functools.partial(_retention_kernel, c_dec=c_dec),
        grid=(b, n),
        in_specs=[pl.BlockSpec((c, wr), rmap), pl.BlockSpec((c, wr), rmap), pl.BlockSpec((c, wv), rmap),
                  pl.BlockSpec((c, wv), rmap),
                  pl.BlockSpec((1, N_HEADS_R, DK_R, DV_R), lambda bi, j: (bi, 0, 0, 0)),
                  full3(inner.shape), full3(qdec.shape), full3(kdec.shape)],
        out_specs=[pl.BlockSpec((c, wv), rmap),
                   pl.BlockSpec((1, N_HEADS_R, DK_R, DV_R), lambda bi, j: (bi, 0, 0, 0))],
        out_shape=[jax.ShapeDtypeStruct((b * n * c, wv), MXU_DTYPE),
                   jax.ShapeDtypeStruct((b, N_HEADS_R, DK_R, DV_R), F32)],
        scratch_shapes=[pltpu.VMEM((N_HEADS_R, DK_R, DV_R), F32)],
        compiler_params=_cparams(("parallel", "arbitrary")),
        name="retention",
    )(qr, kr, vr, gr, s0, inner, qdec, kdec)


def _cross_kernel(q_ref, mk_ref, mv_ref, o_ref):
    scale = HEAD_DIM_M ** -0.5
    for h in range(N_HEADS_M):
        sl = slice(h * HEAD_DIM_M, (h + 1) * HEAD_DIM_M)
        s = _dot_nt(_mx(q_ref[:, sl]), _mx(mk_ref[0, :, sl])) * scale
        p = jnp.exp(s - jnp.max(s, axis=1, keepdims=True))
        p = p / jnp.sum(p, axis=1, keepdims=True)
        o_ref[:, sl] = _dot(_mx(p), _mx(mv_ref[0, :, sl])).astype(o_ref.dtype)


def _cross_attend(qm, mk, mv, tm, out_dtype):
    b, n_mem, hd = mk.shape
    nt = qm.shape[0] // (b * tm)
    return pl.pallas_call(
        _cross_kernel,
        grid=(b, nt),
        in_specs=[pl.BlockSpec((tm, hd), lambda bi, i: (bi * nt + i, 0)),
                  pl.BlockSpec((1, n_mem, hd), lambda bi, i: (bi, 0, 0)),
                  pl.BlockSpec((1, n_mem, hd), lambda bi, i: (bi, 0, 0))],
        out_specs=pl.BlockSpec((tm, hd), lambda bi, i: (bi * nt + i, 0)),
        out_shape=jax.ShapeDtypeStruct(qm.shape, out_dtype),
        compiler_params=_cparams(("parallel", "parallel")),
        name="cross_attend",
    )(qm, mk, mv)


def _memkv_kernel(x_ref, g_ref, w_ref, o_ref):
    o_ref[...] = _dot(_mx(_rms(x_ref[...], g_ref[...])), w_ref[...])


def _memory_kv(mem, gain, w):
    rows, d = mem.shape
    return pl.pallas_call(
        _memkv_kernel,
        out_shape=jax.ShapeDtypeStruct((rows, w.shape[1]), F32),
        compiler_params=pltpu.CompilerParams(vmem_limit_bytes=VMEM_LIMIT),
        name="memory_kv",
    )(mem, gain.reshape(1, d), _mx(w))


def _merge_kernel(x_ref, oa_ref, or_ref, om_ref, gates_ref, wpa_ref, wpb_ref, wpc_ref, wo_ref, g_ref, h_ref):
    d = x_ref.shape[1]
    gt = gates_ref[...]
    mixed = (_sigmoid(gt[:, :d]) * _dot(_mx(oa_ref[...]), wpa_ref[...])
             + _sigmoid(gt[:, d:2 * d]) * _dot(_mx(or_ref[...]), wpb_ref[...])
             + _sigmoid(gt[:, 2 * d:]) * _dot(_mx(om_ref[...]), wpc_ref[...]))
    z = _dot(_mx(mixed), wo_ref[...])
    h_ref[...] = x_ref[...] + _rms(z, g_ref[...])


def _merge(x, oa, o_r, om, gates, wpa, wpb, wpc, wo, gain, tm):
    rows, d = x.shape
    return pl.pallas_call(
        _merge_kernel,
        grid=(rows // tm,),
        in_specs=[_row_spec(tm, d), _row_spec(tm, oa.shape[1]), _row_spec(tm, o_r.shape[1]),
                  _row_spec(tm, om.shape[1]), _row_spec(tm, 3 * d),
                  _full_spec(wpa.shape), _full_spec(wpb.shape), _full_spec(wpc.shape), _full_spec(wo.shape),
                  _full_spec((1, d))],
        out_specs=_row_spec(tm, d),
        out_shape=jax.ShapeDtypeStruct((rows, d), F32),
        compiler_params=_cparams(("parallel",)),
        name="merge",
    )(x, oa, o_r, om, gates, wpa, wpb, wpc, wo, gain.reshape(1, d))


HALO = BF16_ROWS


def _ffn_kernel(h_ref, halo_ref, s0_ref, s1_ref, g1_ref, g2_ref, wu_ref, wg_ref, cw_ref, cb_ref, wd_ref,
                y_ref, utail_ref, x_scr, u_scr, *, tm, seq, keep):
    i = pl.program_id(0)
    h = h_ref[...]
    hn = _rms(h, g1_ref[...])
    x_scr[HALO:, :] = _mx(hn)
    x_scr[:HALO, :] = _mx(_rms(halo_ref[...], g1_ref[...]))
    xc = x_scr[...]
    u_scr[...] = _dot(xc, wu_ref[...])
    gate = _dot(xc[HALO:], wg_ref[...])
    cur = u_scr[HALO:, :]
    prev1 = u_scr[HALO - 1:HALO - 1 + tm, :]
    prev2 = u_scr[HALO - 2:HALO - 2 + tm, :]
    seq_loc = min(seq, tm)
    t = lax.rem(lax.broadcasted_iota(I32, (tm, 1), 0), seq_loc)
    t = jnp.where(lax.rem(i * tm, seq) == 0, t, CONV_W)
    st0, st1 = s0_ref[...], s1_ref[...]
    if st0.shape[0] != tm:
        st0, st1 = st0[0:1], st1[0:1]
    prev1 = jnp.where(t == 0, st1, prev1)
    prev2 = jnp.where(t == 0, st0, jnp.where(t == 1, st1, prev2))
    c = cb_ref[...] + prev2 * cw_ref[0:1, :] + prev1 * cw_ref[1:2, :] + cur * cw_ref[2:3, :]
    act = jax.nn.gelu(c, approximate=True) * gate
    ff = _dot(_mx(act), wd_ref[...])
    y_ref[...] = h + _rms(ff, g2_ref[...])
    utail_ref[...] = u_scr[HALO + tm - keep:, :]


def _conv_ffn(h, s0e, s1e, g1, g2, wu, wg, cw, cb, wd, tm, seq, keep):
    rows, d = h.shape
    f = wu.shape[1]
    nt = rows // tm
    hb = tm // HALO
    sr = s0e.shape[0]
    return pl.pallas_call(
        functools.partial(_ffn_kernel, tm=tm, seq=seq, keep=keep),
        grid=(nt,),
        in_specs=[_row_spec(tm, d),
                  pl.BlockSpec((HALO, d), lambda i: (jnp.maximum(i * hb - 1, 0), 0)),
                  _full_spec((sr, f)), _full_spec((sr, f)),
                  _full_spec((1, d)), _full_spec((1, d)),
                  _full_spec(wu.shape), _full_spec(wg.shape), _full_spec(cw.shape), _full_spec((1, f)),
                  _full_spec(wd.shape)],
        out_specs=[_row_spec(tm, d), _row_spec(keep, f)],
        out_shape=[jax.ShapeDtypeStruct((rows, d), F32), jax.ShapeDtypeStruct((nt * keep, f), F32)],
        scratch_shapes=[pltpu.VMEM((tm + HALO, d), MXU_DTYPE), pltpu.VMEM((tm + HALO, f), F32)],
        compiler_params=_cparams(("parallel",)),
        name="conv_ffn",
    )(h, h, s0e, s1e, g1.reshape(1, d), g2.reshape(1, d), wu, wg, cw, cb.reshape(1, f), wd)


def _layer_weights(l, w_in, w_proj_a, w_proj_b, w_proj_c, w_out, w_up, w_down):
    wts = _prep_in_weights(w_in[l])
    f = w_down.shape[1]
    wts.update(wpa=_mx(w_proj_a[l]), wpb=_mx(w_proj_b[l]), wpc=_mx(w_proj_c[l]), wo=_mx(w_out[l]),
               wu=_mx(w_up[l][:, :f]), wg=_mx(w_up[l][:, f:]), wd=_mx(w_down[l]))
    return wts


def _prompt_layer(x, mem, wts, norms, conv_w, conv_b, w_mem_kv, tiles):
    s, d = x.shape
    f = wts["wd"].shape[0]
    pos = jnp.arange(s)
    pr = _projections(x, norms["pre_mix"], pos, wts, tiles["proj"])
    topk = min(TOPK_MAX, s // 4)
    mask = _prompt_select(pr["qi_hm"], pr["wi_f"], pr["ki_b"], topk, tiles["sel_q"], SEL_WIDE)
    o_a = _prompt_attend(pr["qa_t"], pr["ka_hm"], pr["va_t"], mask, tiles["att_q"], tiles["att_k"])
    s0 = jnp.zeros((1, N_HEADS_R, DK_R, DV_R), F32)
    o_r, ret_new = _retention(pr["qr_f"], pr["kr_f"], pr["vr_b"], pr["gr_f"], s0, RET_CHUNK)
    kv = _memory_kv(mem, norms["mem"], w_mem_kv)
    wm = N_HEADS_M * HEAD_DIM_M
    mk, mv = kv[:, :wm], kv[:, wm:]
    o_m = _cross_attend(pr["qm_b"], mk[None], mv[None], tiles["cross"], MXU_DTYPE)
    h = _merge(x, o_a, o_r, o_m, pr["gates_f"], wts["wpa"], wts["wpb"], wts["wpc"], wts["wo"],
               norms["post_mix"], tiles["merge"])
    zst = jnp.zeros((SUBLANES, f), F32)
    y, utail = _conv_ffn(h, zst, zst, norms["pre_ffn"], norms["post_ffn"], wts["wu"], wts["wg"], conv_w, conv_b,
                         wts["wd"], tiles["ffn"], s, SUBLANES)
    conv_new = utail[-(CONV_W - 1):]
    return y, pr["ka_f"], pr["va_f"], pr["ki_f"], ret_new, conv_new, mk, mv


def _sample_layer(x, wts, norms, conv_w, conv_b, cache_k, cache_v, cache_kidx, mem_k, mem_v,
                  state_ret, state_conv, page_table):
    db, t, d = x.shape
    f = wts["wd"].shape[0]
    n_pages = page_table.shape[1]
    page = cache_k.shape[1]
    past = n_pages * page
    rows = db * T_PAD
    xp = jnp.pad(x, ((0, 0), (0, T_PAD - t), (0, 0))).reshape(rows, d)
    pos = jnp.tile(past + jnp.arange(T_PAD), db)
    pr = _projections(xp, norms["pre_mix"], pos, wts, rows)
    hd = N_HEADS_A * HEAD_DIM_A

    topk = min(TOPK_MAX, (past + t) // 4)
    i_past = _sample_index(page_table, pr["qi_hm"].astype(F32), pr["wi_f"], cache_kidx.transpose(0, 2, 1)).reshape(rows, past)
    m_past, m_new = _sample_select(i_past, pr["qi_hm"], pr["wi_f"], pr["ki_b"], t, topk)
    nnew = m_new.shape[1]
    own = m_new[:, :rows].reshape(db, T_PAD, db, T_PAD)[jnp.arange(db), :, jnp.arange(db), :]
    m_new_own = jnp.pad(own, ((0, 0), (0, 0), (0, LANES - T_PAD)))
    q_rows = pr["qa_hm"].reshape(N_HEADS_A, db, T_PAD, HEAD_DIM_A)
    eye = jnp.eye(N_HEADS_A, dtype=MXU_DTYPE)
    q_bd = jnp.einsum("hbtd,hg->bhtgd", q_rows, eye).reshape(db, N_HEADS_A * T_PAD, hd)
    new_t = lambda a: jnp.pad(_mx(a).reshape(db, T_PAD, hd).transpose(0, 2, 1), ((0, 0), (0, 0), (0, LANES - T_PAD)))
    paged_t = lambda c: c.transpose(0, 2, 3, 1).reshape(c.shape[0], hd, page)
    o_a = _sample_attend(page_table, q_bd, m_past.reshape(db, T_PAD, past), m_new_own,
                         new_t(pr["ka_f"]), new_t(pr["va_f"]), paged_t(cache_k), paged_t(cache_v)).reshape(rows, hd)

    padc = lambda a: jnp.pad(a.reshape(db, T_PAD, -1), ((0, 0), (0, RET_CHUNK - T_PAD), (0, 0))).reshape(db * RET_CHUNK, -1)
    o_r, ret_new = _retention(padc(pr["qr_f"]), padc(pr["kr_f"]), padc(pr["vr_b"]), padc(pr["gr_f"]), state_ret, t)
    o_r = o_r.reshape(db, RET_CHUNK, -1)[:, :T_PAD].reshape(rows, -1)

    wm = N_HEADS_M * HEAD_DIM_M
    o_m = _cross_attend(pr["qm_b"].astype(F32), mem_k.reshape(db, -1, wm), mem_v.reshape(db, -1, wm), T_PAD, F32)

    h = _merge(xp, o_a, o_r, o_m, pr["gates_f"], wts["wpa"], wts["wpb"], wts["wpc"], wts["wo"],
               norms["post_mix"], rows)
    s0e = jnp.repeat(state_conv[:, 0], T_PAD, axis=0)
    s1e = jnp.repeat(state_conv[:, 1], T_PAD, axis=0)
    y, u_all = _conv_ffn(h, s0e, s1e, norms["pre_ffn"], norms["post_ffn"], wts["wu"], wts["wg"], conv_w, conv_b,
                         wts["wd"], rows, T_PAD, rows)
    ext = jnp.concatenate([state_conv.astype(F32), u_all.reshape(db, T_PAD, f)[:, :t]], axis=1)
    conv_new = ext[:, t:]
    unpad = lambda a: a.reshape(db, T_PAD, -1)[:, :t]
    return (unpad(y), unpad(pr["ka_f"]), unpad(pr["va_f"]), unpad(pr["ki_f"]), ret_new, conv_new)


PROMPT_TILES = dict(proj=512, sel_q=512, att_q=512, att_k=1024, cross=512, merge=512, ffn=256)


def kernel(x_prompt, x_sample, cache_k, cache_v, cache_kidx, cache_mem_k, cache_mem_v, state_ret, state_conv,
           page_table, mem_prompt, norm_pre_mix, norm_post_mix, norm_pre_ffn, norm_post_ffn, norm_mem,
           w_in, w_mem_kv, w_proj_a, w_proj_b, w_proj_c, w_out, w_up, conv_w, conv_b, w_down):
    bp, s, d = x_prompt.shape
    db, t, _ = x_sample.shape
    depth = w_in.shape[0]
    assert bp == 1 and t <= T_PAD and CONV_W - 1 <= t
    tiles = {k: min(v, s) for k, v in PROMPT_TILES.items()}
    yp, ys = x_prompt[0], x_sample
    outs = [[] for _ in range(12)]
    for l in range(depth):
        wts = _layer_weights(l, w_in, w_proj_a, w_proj_b, w_proj_c, w_out, w_up, w_down)
        norms = dict(pre_mix=norm_pre_mix[l], post_mix=norm_post_mix[l], pre_ffn=norm_pre_ffn[l],
                     post_ffn=norm_post_ffn[l], mem=norm_mem[l])
        yp, kp, vp, kip, rp, cp, mk, mv = _prompt_layer(yp, mem_prompt[0], wts, norms, conv_w[l], conv_b[l],
                                                        w_mem_kv[l], tiles)
        ys, ks, vs, kis, rs, cs = _sample_layer(ys, wts, norms, conv_w[l], conv_b[l], cache_k[l], cache_v[l],
                                                cache_kidx[l], cache_mem_k[l], cache_mem_v[l], state_ret[l],
                                                state_conv[l], page_table)
        n_mem = mk.shape[0]
        vals = (kp.reshape(1, s, N_HEADS_A, HEAD_DIM_A), vp.reshape(1, s, N_HEADS_A, HEAD_DIM_A),
                kip.reshape(1, s, IDX_DIM), rp, cp[None],
                mk.reshape(1, n_mem, N_HEADS_M, HEAD_DIM_M), mv.reshape(1, n_mem, N_HEADS_M, HEAD_DIM_M),
                ks.reshape(db, t, N_HEADS_A, HEAD_DIM_A), vs.reshape(db, t, N_HEADS_A, HEAD_DIM_A),
                kis, rs, cs)
        for o, v in zip(outs, vals):
            o.append(v)
    stacked = [jnp.stack(o) for o in outs]
    return (yp[None], ys, *stacked)
```

```python
import functools
import math

import numpy as np
import jax
import jax.numpy as jnp
from jax import lax
from jax.experimental import pallas as pl
from jax.experimental.pallas import tpu as pltpu

F32 = jnp.float32
I32 = jnp.int32
MXU_DTYPE = jnp.bfloat16

N_HEADS_A, HEAD_DIM_A = 8, 64
IDX_HEADS, IDX_DIM = 4, 64
TOPK_MAX = 256
N_HEADS_R, DK_R, DV_R = 4, 128, 256
RET_CHUNK = 128
N_HEADS_M, HEAD_DIM_M = 4, 128
CONV_W = 3
ROPE_THETA = 10000.0
EPS = 1e-6

LANES = 128
SUBLANES = 8
BF16_ROWS = 16
VMEM_LIMIT = 56 * 1024 * 1024
NEG_BIG = -1e30
F32_LOWEST = float(np.finfo(np.float32).min)
T_PAD = 8


def _cparams(sem):
    return pltpu.CompilerParams(dimension_semantics=sem, vmem_limit_bytes=VMEM_LIMIT)


def _dot(a, b):
    return jnp.dot(a, b, preferred_element_type=F32)


def _dot_nt(a, b):
    return lax.dot_general(a, b, (((1,), (1,)), ((), ())), preferred_element_type=F32)


def _mx(a):
    return a.astype(MXU_DTYPE)


def _rms(x, g):
    return x * lax.rsqrt(jnp.mean(x * x, axis=-1, keepdims=True) + EPS) * g


def _sigmoid(x):
    return 1.0 / (1.0 + jnp.exp(-x))


def _rope_tables(pos, d):
    half = d // 2
    inv = 1.0 / (ROPE_THETA ** (jnp.arange(half, dtype=F32) * 2.0 / d))
    ang = pos.astype(F32)[:, None] * inv[None, :]
    cos, sin = jnp.cos(ang), jnp.sin(ang)
    reps = LANES // d
    cos_t = jnp.tile(jnp.concatenate([cos, cos], axis=1), (1, reps))
    sin_t = jnp.tile(jnp.concatenate([-sin, sin], axis=1), (1, reps))
    return cos_t, sin_t


def _rope(y, cos, sin, d):
    w = y.shape[1]
    half = d // 2
    reps = w // LANES
    c = jnp.concatenate([cos] * reps, axis=1) if reps > 1 else cos
    s = jnp.concatenate([sin] * reps, axis=1) if reps > 1 else sin
    lane = lax.broadcasted_iota(I32, y.shape, 1)
    first = (lane & (d - 1)) < half
    rot = jnp.where(first, pltpu.roll(y, w - half, 1), pltpu.roll(y, half, 1))
    return y * c + rot * s


def _proj_a_kernel(x_ref, g_ref, cos_ref, sin_ref, wa_ref, wv_ref,
                   qa_hm, qa_t, ka_f, ka_hm, qi_hm, ki_f, ki_b, va_f, va_t, wi_f):
    xn = _mx(_rms(x_ref[...], g_ref[...]))
    wa = N_HEADS_A * HEAD_DIM_A
    y = _rope(_dot(xn, wa_ref[...]), cos_ref[...], sin_ref[...], HEAD_DIM_A)
    q = y[:, :wa] * (HEAD_DIM_A ** -0.5)
    k = y[:, wa:2 * wa]
    ka_f[...] = k
    qt = q.T
    for h in range(N_HEADS_A):
        sl = slice(h * HEAD_DIM_A, (h + 1) * HEAD_DIM_A)
        qa_hm[h] = _mx(q[:, sl])
        qa_t[h] = _mx(qt[sl, :])
        ka_hm[h] = _mx(k[:, sl])
    qi = y[:, 2 * wa:2 * wa + IDX_HEADS * IDX_DIM]
    for h in range(IDX_HEADS):
        qi_hm[h] = _mx(qi[:, h * IDX_DIM:(h + 1) * IDX_DIM])
    ki = y[:, 2 * wa + IDX_HEADS * IDX_DIM:2 * wa + IDX_HEADS * IDX_DIM + IDX_DIM]
    ki_f[...] = ki
    ki_b[...] = _mx(ki)
    z = _dot(xn, wv_ref[...])
    v = z[:, :wa]
    va_f[...] = v
    vt = v.T
    for h in range(N_HEADS_A):
        va_t[h] = _mx(vt[h * HEAD_DIM_A:(h + 1) * HEAD_DIM_A, :])
    wi_f[...] = z[:, wa:wa + LANES]


def _proj_b_kernel(x_ref, g_ref, cos_ref, sin_ref, wr_ref, wvg_ref, qr_f, kr_f, vr_b, gr_f):
    xn = _mx(_rms(x_ref[...], g_ref[...]))
    wr = N_HEADS_R * DK_R
    y = _rope(_dot(xn, wr_ref[...]), cos_ref[...], sin_ref[...], DK_R)
    qr_f[...] = y[:, :wr]
    kr_f[...] = y[:, wr:] * (DK_R ** -0.5)
    z = _dot(xn, wvg_ref[...])
    wv = N_HEADS_R * DV_R
    vr_b[...] = _mx(z[:, :wv])
    gr_f[...] = z[:, wv:]


def _proj_c_kernel(x_ref, g_ref, wc_ref, qm_b, gates_f):
    xn = _mx(_rms(x_ref[...], g_ref[...]))
    z = _dot(xn, wc_ref[...])
    wm = N_HEADS_M * HEAD_DIM_M
    qm_b[...] = _mx(z[:, :wm])
    gates_f[...] = z[:, wm:]


def _row_spec(tm, w):
    return pl.BlockSpec((tm, w), lambda i: (i, 0))


def _full_spec(shape):
    nd = len(shape)
    return pl.BlockSpec(shape, lambda i: (0,) * nd)


def _hm_spec(nh, tm, d):
    return pl.BlockSpec((nh, tm, d), lambda i: (0, i, 0))


def _projections(x, gain, pos, wts, tm):
    rows, d = x.shape
    grid = (rows // tm,)
    wa = N_HEADS_A * HEAD_DIM_A
    cos64, sin64 = _rope_tables(pos, HEAD_DIM_A)
    cos128, sin128 = _rope_tables(pos, DK_R)
    g2 = gain.reshape(1, d)
    sds = jax.ShapeDtypeStruct
    t_spec = pl.BlockSpec((N_HEADS_A, HEAD_DIM_A, tm), lambda i: (0, 0, i))
    outs_a = pl.pallas_call(
        _proj_a_kernel,
        grid=grid,
        in_specs=[_row_spec(tm, d), _full_spec((1, d)), _row_spec(tm, LANES), _row_spec(tm, LANES),
                  _full_spec(wts["wa"].shape), _full_spec(wts["wv"].shape)],
        out_specs=[_hm_spec(N_HEADS_A, tm, HEAD_DIM_A), t_spec, _row_spec(tm, wa),
                   _hm_spec(N_HEADS_A, tm, HEAD_DIM_A),
                   _hm_spec(IDX_HEADS, tm, IDX_DIM), _row_spec(tm, IDX_DIM), _row_spec(tm, IDX_DIM),
                   _row_spec(tm, wa), t_spec, _row_spec(tm, LANES)],
        out_shape=[sds((N_HEADS_A, rows, HEAD_DIM_A), MXU_DTYPE), sds((N_HEADS_A, HEAD_DIM_A, rows), MXU_DTYPE),
                   sds((rows, wa), F32),
                   sds((N_HEADS_A, rows, HEAD_DIM_A), MXU_DTYPE), sds((IDX_HEADS, rows, IDX_DIM), MXU_DTYPE),
                   sds((rows, IDX_DIM), F32), sds((rows, IDX_DIM), MXU_DTYPE),
                   sds((rows, wa), F32), sds((N_HEADS_A, HEAD_DIM_A, rows), MXU_DTYPE), sds((rows, LANES), F32)],
        compiler_params=_cparams(("parallel",)),
        name="proj_a",
    )(x, g2, cos64, sin64, wts["wa"], wts["wv"])
    names_a = ("qa_hm", "qa_t", "ka_f", "ka_hm", "qi_hm", "ki_f", "ki_b", "va_f", "va_t", "wi_f")
    wr, wv = N_HEADS_R * DK_R, N_HEADS_R * DV_R
    outs_b = pl.pallas_call(
        _proj_b_kernel,
        grid=grid,
        in_specs=[_row_spec(tm, d), _full_spec((1, d)), _row_spec(tm, LANES), _row_spec(tm, LANES),
                  _full_spec(wts["wr"].shape), _full_spec(wts["wvg"].shape)],
        out_specs=[_row_spec(tm, wr), _row_spec(tm, wr), _row_spec(tm, wv), _row_spec(tm, wv)],
        out_shape=[sds((rows, wr), F32), sds((rows, wr), F32), sds((rows, wv), MXU_DTYPE), sds((rows, wv), F32)],
        compiler_params=_cparams(("parallel",)),
        name="proj_b",
    )(x, g2, cos128, sin128, wts["wr"], wts["wvg"])
    names_b = ("qr_f", "kr_f", "vr_b", "gr_f")
    wm = N_HEADS_M * HEAD_DIM_M
    outs_c = pl.pallas_call(
        _proj_c_kernel,
        grid=grid,
        in_specs=[_row_spec(tm, d), _full_spec((1, d)), _full_spec(wts["wc"].shape)],
        out_specs=[_row_spec(tm, wm), _row_spec(tm, 3 * d)],
        out_shape=[sds((rows, wm), MXU_DTYPE), sds((rows, 3 * d), F32)],
        compiler_params=_cparams(("parallel",)),
        name="proj_c",
    )(x, g2, wts["wc"])
    names_c = ("qm_b", "gates_f")
    out = dict(zip(names_a, outs_a))
    out.update(zip(names_b, outs_b))
    out.update(zip(names_c, outs_c))
    return out


def _prep_in_weights(w_in):
    d = w_in.shape[0]
    wa = N_HEADS_A * HEAD_DIM_A
    widths = (wa, wa, wa, IDX_HEADS * IDX_DIM, IDX_DIM, IDX_HEADS,
              N_HEADS_R * DK_R, N_HEADS_R * DK_R, N_HEADS_R * DV_R, N_HEADS_R * DV_R,
              N_HEADS_M * HEAD_DIM_M, 3 * d)
    offs = np.concatenate([[0], np.cumsum(widths)])
    seg = [w_in[:, int(offs[i]):int(offs[i + 1])] for i in range(len(widths))]
    q_a, k_a, v_a, q_i, k_i, w_i, q_r, k_r, v_r, g_r, q_m, gates = seg
    zpad = lambda n: jnp.zeros((d, n), w_in.dtype)
    return {
        "wa": _mx(jnp.concatenate([q_a, k_a, q_i, k_i, zpad(LANES - IDX_DIM)], axis=1)),
        "wv": _mx(jnp.concatenate([v_a, w_i, zpad(LANES - IDX_HEADS)], axis=1)),
        "wr": _mx(jnp.concatenate([q_r, k_r], axis=1)),
        "wvg": _mx(jnp.concatenate([v_r, g_r], axis=1)),
        "wc": _mx(jnp.concatenate([q_m, gates], axis=1)),
    }


SEL_ROWS = 128
SEL_WIDE = 512
SEL_GROUPS = 2 * LANES
NO_LIMIT = 2 ** 30
MIN_NORMAL_KEY = 0x00800000
SEARCH_PERIOD = 4
SEARCH_CAP = SEARCH_PERIOD * 33


def _key_to_f32(key):
    bits = jnp.where(key >= 0, key, key ^ jnp.int32(0x7FFFFFFF))
    return pltpu.bitcast(bits, F32)


def _f32_to_key(f):
    bits = pltpu.bitcast(f, I32)
    return jnp.where(f == 0.0, 0, jnp.where(bits >= 0, bits, bits ^ jnp.int32(0x7FFFFFFF)))


def _sweep(segments, rs, init, fn):
    acc = init
    base = 0
    for ref, n_wide in segments:
        def body(c, a, ref=ref, base=base):
            off = c * SEL_WIDE
            for k in range(SEL_WIDE // LANES):
                x = ref[rs, pl.ds(pl.multiple_of(off + k * LANES, LANES), LANES)]
                a = fn(a, x, base + off + k * LANES, k)
            return a
        pairs = n_wide // 2
        acc = lax.fori_loop(0, pairs, lambda c, a, body=body: body(2 * c + 1, body(2 * c, a)), acc)
        acc = lax.fori_loop(2 * pairs, n_wide, body, acc)
        base = base + n_wide * SEL_WIDE
    return acc


def _count(segments, rows, make_pred):
    outs = []
    for g in range(rows // SEL_ROWS):
        rs = slice(g * SEL_ROWS, (g + 1) * SEL_ROWS)
        pred = make_pred(rs)
        acc = _sweep(segments, rs, jnp.zeros((SEL_ROWS, LANES), F32),
                     lambda a, x, idx0, k: a + jnp.where(pred(x, idx0), 1.0, 0.0))
        outs.append(jnp.sum(acc.T, axis=0, keepdims=True))
    return jnp.concatenate(outs, axis=1) if len(outs) > 1 else outs[0]


def _col_to_row(col):
    return jnp.broadcast_to(col, (col.shape[0], LANES)).T[0:1]


def _row_to_lanes(vec):
    return jnp.broadcast_to(vec, (LANES, vec.shape[1])).T


def _search(count_fn, lo, hi, c_lo, c_hi, target, alive):
    def unfinished(lo, hi, c_lo):
        return jnp.logical_and(alive, jnp.logical_and(c_lo > target, lo + 1 < hi))

    def any_row(flag):
        return jnp.max(jnp.where(flag, 1, 0).astype(I32))

    def cond(carry):
        return jnp.logical_and(carry[0] < SEARCH_CAP, carry[1] > 0)

    def body(carry):
        it, _, phase, lo, hi, c_lo, c_hi, w_lo, w_hi, last = carry
        act = unfinished(lo, hi, c_lo)
        bis = (lo >> 1) + (hi >> 1) + (lo & hi & 1)
        width = hi - lo
        a = (c_lo - target + 0.5) * w_lo
        b = (target - 0.5 - c_hi) * w_hi
        frac = a / jnp.maximum(a + b, 1e-6)
        step = (frac * width.astype(F32)).astype(I32)
        itp = lo + jnp.clip(step, 1, jnp.maximum(width - 1, 1))
        use_itp = jnp.logical_and((lo ^ hi) >= 0, (jnp.zeros_like(lo) + phase) != SEARCH_PERIOD - 1)
        v = jnp.where(use_itp, itp, bis)
        cnt = count_fn(v)
        up = jnp.logical_and(act, cnt >= target)
        dn = jnp.logical_and(act, cnt < target)
        lo = jnp.where(up, v, lo)
        c_lo = jnp.where(up, cnt, c_lo)
        hi = jnp.where(dn, v, hi)
        c_hi = jnp.where(dn, cnt, c_hi)
        w_hi = jnp.where(up, jnp.where(last == 1, w_hi * 0.5, 1.0), jnp.where(dn, 1.0, w_hi))
        w_lo = jnp.where(dn, jnp.where(last == -1, w_lo * 0.5, 1.0), jnp.where(up, 1.0, w_lo))
        last = jnp.where(up, 1, jnp.where(dn, -1, last))
        phase = jnp.where(phase == SEARCH_PERIOD - 1, 0, phase + 1)
        return it + 1, any_row(unfinished(lo, hi, c_lo)), phase, lo, hi, c_lo, c_hi, w_lo, w_hi, last

    go = any_row(unfinished(lo, hi, c_lo))
    one = jnp.ones(lo.shape, F32)
    out = lax.while_loop(cond, body, (jnp.int32(0), go, jnp.int32(0), lo, hi, c_lo, c_hi,
                                      one, one, jnp.zeros(lo.shape, I32)))
    return out[3], out[5], out[6]


def _select_threshold(segments, rows, topk, alive=None):
    assert topk <= SEL_GROUPS
    kf = jnp.full((1, rows), float(topk), F32)
    if alive is None:
        alive = jnp.full((1, rows), True)

    los, his = [], []
    for g in range(rows // SEL_ROWS):
        rs = slice(g * SEL_ROWS, (g + 1) * SEL_ROWS)
        ninf = jnp.full((SEL_ROWS, LANES), -jnp.inf, F32)
        ga, gb = _sweep(segments, rs, (ninf, ninf),
                        lambda a, x, idx0, k: ((jnp.maximum(a[0], x), a[1]) if k % 2 == 0
                                               else (a[0], jnp.maximum(a[1], x))))
        los.append(jnp.min(jnp.minimum(ga, gb), axis=1, keepdims=True))
        his.append(jnp.max(jnp.maximum(ga, gb), axis=1, keepdims=True))
    cat = lambda xs: _col_to_row(jnp.concatenate(xs, axis=0) if len(xs) > 1 else xs[0])
    lo = _f32_to_key(jnp.maximum(cat(los), F32_LOWEST))
    hi = _f32_to_key(cat(his)) + 1

    def count_ge_f(thr):
        thr_l = _row_to_lanes(thr)

        def make_pred(rs):
            t = thr_l[rs]
            return lambda x, idx0: x >= t
        return _count(segments, rows, make_pred)

    count_ge = lambda v: count_ge_f(_key_to_f32(v))
    c_lo = count_ge(lo)
    c_hi = jnp.zeros((1, rows), F32)
    def zero_probes(state):
        lo, hi, c_lo, c_hi = state
        for probe in (0, MIN_NORMAL_KEY):
            v = jnp.full((1, rows), probe, I32)
            cnt = count_ge_f(jnp.full((1, rows), np.int32(probe).view(np.float32), F32))
            inside = jnp.logical_and(lo < v, v < hi)
            up = jnp.logical_and(inside, cnt >= kf)
            dn = jnp.logical_and(inside, cnt < kf)
            lo, c_lo = jnp.where(up, v, lo), jnp.where(up, cnt, c_lo)
            hi, c_hi = jnp.where(dn, v, hi), jnp.where(dn, cnt, c_hi)
        return lo, hi, c_lo, c_hi

    straddles = jnp.logical_or(jnp.logical_and(lo < 0, hi > 0),
                               jnp.logical_and(lo < MIN_NORMAL_KEY, hi > MIN_NORMAL_KEY))
    lo, hi, c_lo, c_hi = lax.cond(jnp.max(jnp.where(straddles, 1, 0).astype(I32)) > 0,
                                  zero_probes, lambda s: s, (lo, hi, c_lo, c_hi))
    at_zero = jnp.logical_and(lo == 0, hi == MIN_NORMAL_KEY)
    lo, c_lo, c_hi = _search(count_ge, lo, hi, c_lo, c_hi, kf, jnp.logical_and(alive, jnp.logical_not(at_zero)))
    tau = _key_to_f32(lo)

    need = jnp.logical_and(alive, c_lo > kf)
    room = jnp.where(need, kf - c_hi, float(NO_LIMIT))
    return _row_to_lanes(tau), _row_to_lanes(room)


def _write_mask(segments_out, rows, tau, room):
    r_io = lax.broadcasted_iota(I32, (LANES, 2 * LANES), 0)
    c_io = lax.broadcasted_iota(I32, (LANES, 2 * LANES), 1)
    tri = jnp.where(jnp.logical_or(r_io <= c_io, c_io >= LANES), 1.0, 0.0).astype(MXU_DTYPE)
    for g in range(rows // SEL_ROWS):
        rs = slice(g * SEL_ROWS, (g + 1) * SEL_ROWS)
        t, rm = tau[rs], room[rs]
        seen = jnp.zeros((SEL_ROWS, LANES), F32)
        for src, dst, n_wide, n_total in segments_out:
            def body(c, seen, src=src, dst=dst):
                off = c * SEL_WIDE
                nsub = SEL_WIDE // LANES
                dss = [pl.ds(pl.multiple_of(off + k * LANES, LANES), LANES) for k in range(nsub)]
                xs = [src[rs, ds] for ds in dss]
                eqs = [jnp.where(x == t, 1.0, 0.0) for x in xs]
                run = jnp.dot(jnp.concatenate(eqs, axis=0).astype(MXU_DTYPE), tri, preferred_element_type=F32)
                for k in range(nsub):
                    run_k = run[k * SEL_ROWS:(k + 1) * SEL_ROWS]
                    before = seen + run_k[:, :LANES] - eqs[k]
                    keep_eq = jnp.where(before < rm, eqs[k], 0.0)
                    dst[rs, dss[k]] = jnp.where(xs[k] > t, 1.0, keep_eq).astype(dst.dtype)
                    seen = seen + run_k[:, LANES:]
                return seen

            pairs = n_wide // 2
            seen = lax.fori_loop(0, pairs, lambda c, sn, body=body: body(2 * c + 1, body(2 * c, sn)), seen)
            seen = lax.fori_loop(2 * pairs, n_wide, body, seen)

            def zbody(c, carry, dst=dst):
                off = pl.multiple_of(c * SEL_WIDE, SEL_WIDE)
                dst[rs, pl.ds(off, SEL_WIDE)] = jnp.zeros((SEL_ROWS, SEL_WIDE), dst.dtype)
                return carry

            lax.fori_loop(n_wide, n_total, zbody, 0)


def _index_scores(qi_ref, w, kb, transposed_keys=False):
    acc = None
    for h in range(IDX_HEADS):
        s = _dot(_mx(qi_ref[h]), kb) if transposed_keys else _dot_nt(_mx(qi_ref[h]), kb)
        t = w[:, h:h + 1] * jnp.maximum(s, 0.0)
        acc = t if acc is None else acc + t
    return acc


def _causal_pairs(nq, tq, tk):
    pairs = [(i, j) for i in range(nq) for j in range((i * tq + tq - 1) // tk + 1)]
    return jnp.asarray([p[0] for p in pairs], I32), jnp.asarray([p[1] for p in pairs], I32)


def _prompt_select_kernel(qb_ref, kb_ref, qi_ref, wi_ref, kidx_ref, mask_ref, i_scr, *, tq, tk, topk):
    p = pl.program_id(0)
    i, j = qb_ref[p], kb_ref[p]
    q_lo = i * tq
    n_wide = (q_lo + tq - 1) // tk + 1

    acc = _index_scores(qi_ref, wi_ref[...], kidx_ref[...])
    cols = pl.ds(pl.multiple_of(j * tk, tk), tk)
    below_diagonal = (j + 1) * tk <= q_lo + 1

    @pl.when(below_diagonal)
    def _():
        i_scr[:, cols] = acc

    @pl.when(jnp.logical_not(below_diagonal))
    def _():
        qpos = q_lo + lax.broadcasted_iota(I32, acc.shape, 0)
        kpos = j * tk + lax.broadcasted_iota(I32, acc.shape, 1)
        i_scr[:, cols] = jnp.where(kpos <= qpos, acc, -jnp.inf)

    @pl.when(j == n_wide - 1)
    def _():
        seg = [(i_scr, n_wide)]
        tau, room = _select_threshold(seg, tq, topk)
        _write_mask([(i_scr, mask_ref, n_wide, mask_ref.shape[1] // tk)], tq, tau, room)


def _prompt_select(qi_hm, wi_f, ki_b, topk, tq, tk):
    s = ki_b.shape[0]
    assert tk == SEL_WIDE and s % tk == 0 and s % tq == 0 and tq % SEL_ROWS == 0
    qb, kb = _causal_pairs(s // tq, tq, tk)
    grid_spec = pltpu.PrefetchScalarGridSpec(
        num_scalar_prefetch=2,
        grid=(qb.shape[0],),
        in_specs=[pl.BlockSpec((IDX_HEADS, tq, IDX_DIM), lambda p, qb, kb: (0, qb[p], 0)),
                  pl.BlockSpec((tq, LANES), lambda p, qb, kb: (qb[p], 0)),
                  pl.BlockSpec((tk, IDX_DIM), lambda p, qb, kb: (kb[p], 0))],
        out_specs=pl.BlockSpec((tq, s), lambda p, qb, kb: (qb[p], 0), pipeline_mode=pl.Buffered(1)),
        scratch_shapes=[pltpu.VMEM((tq, s), F32)],
    )
    return pl.pallas_call(
        functools.partial(_prompt_select_kernel, tq=tq, tk=tk, topk=topk),
        grid_spec=grid_spec,
        out_shape=jax.ShapeDtypeStruct((s, s), MXU_DTYPE),
        compiler_params=_cparams(("arbitrary",)),
        name="prompt_select",
    )(qb, kb, qi_hm, wi_f, ki_b)


def _spread(a, n):
    if n <= LANES:
        return a[:, :n]
    return jnp.concatenate([a] * (n // LANES), axis=1)


def _flash_update(s, v, m_prev, l_prev, acc_prev):
    m_new = jnp.maximum(m_prev, jnp.max(s, axis=1, keepdims=True))
    alpha = jnp.exp(m_prev - m_new)
    p = jnp.exp(s - _spread(m_new, s.shape[1]))
    l_new = alpha * l_prev + jnp.sum(p, axis=1, keepdims=True)
    acc_new = _spread(alpha, acc_prev.shape[1]) * acc_prev + _dot(_mx(p), v)
    return m_new, l_new, acc_new


def _prompt_attend_kernel(qb_ref, kb_ref, qt_ref, k_ref, vt_ref, mask_ref, o_ref, m_scr, l_scr, acc_scr, *, tq, tk):
    p_id = pl.program_id(0)
    i, j = qb_ref[p_id], kb_ref[p_id]

    @pl.when(j == 0)
    def _():
        m_scr[...] = jnp.full(m_scr.shape, NEG_BIG, F32)
        l_scr[...] = jnp.zeros(l_scr.shape, F32)
        acc_scr[...] = jnp.zeros(acc_scr.shape, F32)

    bias = (1.0 - mask_ref[...].astype(F32).T) * NEG_BIG
    scores = [_dot(k_ref[h], qt_ref[h]) + bias for h in range(N_HEADS_A)]
    for h, s in enumerate(scores):
        m_prev = m_scr[h]
        m_new = jnp.maximum(m_prev, jnp.max(s, axis=0, keepdims=True))
        alpha = jnp.exp(m_prev - m_new)
        p = jnp.exp(s - m_new[0:1])
        l_scr[h] = alpha * l_scr[h] + jnp.sum(p, axis=0, keepdims=True)
        acc_scr[h] = alpha[0:1] * acc_scr[h] + _dot(vt_ref[h], _mx(p))
        m_scr[h] = m_new

    @pl.when(j == (i * tq + tq - 1) // tk)
    def _():
        ot = jnp.concatenate([acc_scr[h] / l_scr[h][0:1] for h in range(N_HEADS_A)], axis=0)
        o_ref[...] = ot.T.astype(o_ref.dtype)


def _prompt_attend(qa_t, ka_hm, va_t, mask, tq, tk):
    nh, dh, s = qa_t.shape
    qb, kb = _causal_pairs(s // tq, tq, tk)
    grid_spec = pltpu.PrefetchScalarGridSpec(
        num_scalar_prefetch=2,
        grid=(qb.shape[0],),
        in_specs=[pl.BlockSpec((nh, dh, tq), lambda p, qb, kb: (0, 0, qb[p])),
                  pl.BlockSpec((nh, tk, dh), lambda p, qb, kb: (0, kb[p], 0)),
                  pl.BlockSpec((nh, dh, tk), lambda p, qb, kb: (0, 0, kb[p])),
                  pl.BlockSpec((tq, tk), lambda p, qb, kb: (qb[p], kb[p]))],
        out_specs=pl.BlockSpec((tq, nh * dh), lambda p, qb, kb: (qb[p], 0)),
        scratch_shapes=[pltpu.VMEM((nh, SUBLANES, tq), F32), pltpu.VMEM((nh, SUBLANES, tq), F32),
                        pltpu.VMEM((nh, dh, tq), F32)],
    )
    return pl.pallas_call(
        functools.partial(_prompt_attend_kernel, tq=tq, tk=tk),
        grid_spec=grid_spec,
        out_shape=jax.ShapeDtypeStruct((s, nh * dh), MXU_DTYPE),
        compiler_params=_cparams(("arbitrary",)),
        name="prompt_attend",
    )(qb, kb, qa_t, ka_hm, va_t, mask)


PAGES_PER_STEP = 16
INDEX_PAGES_PER_STEP = 32


def _sample_index_kernel(pt_ref, qi_ref, wi_ref, *refs, page):
    del pt_ref
    pages, out_ref = refs[:-1], refs[-1]
    w = wi_ref[0]
    for p, kref in enumerate(pages):
        out_ref[0, :, p * page:(p + 1) * page] = _index_scores(qi_ref, w, _mx(kref[0]), transposed_keys=True)


def _sample_index(page_table, qi_hm, wi_f, cache_kidx_t):
    db, n_pages = page_table.shape
    _, idim, page = cache_kidx_t.shape
    pps = math.gcd(INDEX_PAGES_PER_STEP, n_pages)
    nsteps = n_pages // pps
    pt = page_table.reshape(-1).astype(I32)

    def kspec(p):
        return pl.BlockSpec((1, idim, page), lambda b, j, pt: (pt[b * n_pages + j * pps + p], 0, 0))

    grid_spec = pltpu.PrefetchScalarGridSpec(
        num_scalar_prefetch=1,
        grid=(db, nsteps),
        in_specs=[pl.BlockSpec((IDX_HEADS, T_PAD, idim), lambda b, j, pt: (0, b, 0)),
                  pl.BlockSpec((1, T_PAD, LANES), lambda b, j, pt: (b, 0, 0))]
                 + [kspec(p) for p in range(pps)],
        out_specs=pl.BlockSpec((1, T_PAD, pps * page), lambda b, j, pt: (b, 0, j)),
    )
    return pl.pallas_call(
        functools.partial(_sample_index_kernel, page=page),
        grid_spec=grid_spec,
        out_shape=jax.ShapeDtypeStruct((db, T_PAD, n_pages * page), F32),
        compiler_params=_cparams(("parallel", "arbitrary")),
        name="sample_index",
    )(pt, qi_hm, wi_f.reshape(db, T_PAD, LANES), *([cache_kidx_t] * pps))


def _sample_select_kernel(ipast_ref, qi_ref, wi_ref, kin_ref, mpast_ref, mnew_ref, inew_scr, *, t_real, topk):
    rows = ipast_ref.shape[0]
    acc = _index_scores(qi_ref, wi_ref[...], kin_ref[...])
    r = lax.broadcasted_iota(I32, acc.shape, 0)
    c = lax.broadcasted_iota(I32, acc.shape, 1)
    same = (r // T_PAD) == (c // T_PAD)
    tq, tc = r % T_PAD, c % T_PAD
    ok = jnp.logical_and(same, jnp.logical_and(tc <= tq, tc < t_real))
    inew_scr[...] = jnp.full(inew_scr.shape, -jnp.inf, F32)
    inew_scr[:, :rows] = jnp.where(ok, acc, -jnp.inf)
    n_past = ipast_ref.shape[1] // SEL_WIDE
    n_new = inew_scr.shape[1] // SEL_WIDE
    alive = lax.rem(lax.broadcasted_iota(I32, (1, rows), 1), T_PAD) < t_real
    tau, room = _select_threshold([(ipast_ref, n_past), (inew_scr, n_new)], rows, topk, alive)
    _write_mask([(ipast_ref, mpast_ref, n_past, n_past), (inew_scr, mnew_ref, n_new, n_new)], rows, tau, room)


def _sample_select(i_past, qi_hm, wi_f, ki_b, t_real, topk):
    rows, past = i_past.shape
    assert past % SEL_WIDE == 0 and rows % SEL_ROWS == 0
    wnew = -(-rows // SEL_WIDE) * SEL_WIDE
    return pl.pallas_call(
        functools.partial(_sample_select_kernel, t_real=t_real, topk=topk),
        out_shape=[jax.ShapeDtypeStruct((rows, past), F32), jax.ShapeDtypeStruct((rows, wnew), F32)],
        scratch_shapes=[pltpu.VMEM((rows, wnew), F32)],
        compiler_params=pltpu.CompilerParams(vmem_limit_bytes=VMEM_LIMIT),
        name="sample_select",
    )(i_past, qi_hm, wi_f, ki_b)


def _sample_attend_kernel(pt_ref, q_ref, mp_ref, mn_ref, kn_ref, vn_ref, *refs, page, pps):
    del pt_ref
    kpages, vpages = refs[:pps], refs[pps:2 * pps]
    o_ref, m_scr, l_scr, acc_scr = refs[2 * pps:]
    j = pl.program_id(1)
    nj = pl.num_programs(1)
    q = q_ref[0]
    nh = q.shape[0] // T_PAD

    def scores(m_t, kt):
        keep = jnp.concatenate([m_t] * nh, axis=0) > 0
        return jnp.where(keep, _dot(q, kt), NEG_BIG)

    def update(s_list, vt_list):
        m_prev = m_scr[...]
        m_blk = functools.reduce(jnp.maximum, [jnp.max(s, axis=1, keepdims=True) for s in s_list])
        m_new = jnp.maximum(m_prev, m_blk)
        alpha = jnp.exp(m_prev - m_new)
        l_new = alpha * l_scr[...]
        acc = _spread(alpha, acc_scr.shape[1]) * acc_scr[...]
        for s, vt in zip(s_list, vt_list):
            p = jnp.exp(s - _spread(m_new, s.shape[1]))
            l_new = l_new + jnp.sum(p, axis=1, keepdims=True)
            acc = acc + _dot_nt(_mx(p), vt)
        m_scr[...], l_scr[...], acc_scr[...] = m_new, l_new, acc

    @pl.when(j == 0)
    def _():
        m_scr[...] = jnp.full(m_scr.shape, NEG_BIG, F32)
        l_scr[...] = jnp.zeros(l_scr.shape, F32)
        acc_scr[...] = jnp.zeros(acc_scr.shape, F32)

    update([scores(mp_ref[0, :, p * page:(p + 1) * page], _mx(kpages[p][0])) for p in range(pps)],
           [_mx(vpages[p][0]) for p in range(pps)])

    @pl.when(j == nj - 1)
    def _():
        update([scores(mn_ref[0], kn_ref[0])], [vn_ref[0]])
        full = acc_scr[...] / _spread(l_scr[...], acc_scr.shape[1])
        lane = lax.broadcasted_iota(I32, (T_PAD, full.shape[1]), 1)
        out = jnp.zeros((T_PAD, full.shape[1]), F32)
        for h in range(nh):
            out = out + jnp.where((lane // HEAD_DIM_A) == h, full[h * T_PAD:(h + 1) * T_PAD], 0.0)
        o_ref[0] = out


def _sample_attend(page_table, q_bd, m_past, m_new, kt_new, vt_new, cache_kt, cache_vt):
    db, n_pages = page_table.shape
    _, hd, page = cache_kt.shape
    pps = math.gcd(PAGES_PER_STEP, n_pages)
    nsteps = n_pages // pps
    nnew = kt_new.shape[2]
    nq = q_bd.shape[1]
    pt = page_table.reshape(-1).astype(I32)

    def pspec(p):
        return pl.BlockSpec((1, hd, page), lambda b, j, pt: (pt[b * n_pages + j * pps + p], 0, 0))

    bspec = lambda shape: pl.BlockSpec((1,) + shape, lambda b, j, pt: (b, 0, 0))
    grid_spec = pltpu.PrefetchScalarGridSpec(
        num_scalar_prefetch=1,
        grid=(db, nsteps),
        in_specs=[bspec((nq, hd)),
                  pl.BlockSpec((1, T_PAD, pps * page), lambda b, j, pt: (b, 0, j)),
                  bspec((T_PAD, nnew)), bspec((hd, nnew)), bspec((hd, nnew))]
                 + [pspec(p) for p in range(pps)] * 2,
        out_specs=bspec((T_PAD, hd)),
        scratch_shapes=[pltpu.VMEM((nq, LANES), F32), pltpu.VMEM((nq, LANES), F32), pltpu.VMEM((nq, hd), F32)],
    )
    return pl.pallas_call(
        functools.partial(_sample_attend_kernel, page=page, pps=pps),
        grid_spec=grid_spec,
        out_shape=jax.ShapeDtypeStruct((db, T_PAD, hd), F32),
        compiler_params=_cparams(("parallel", "arbitrary")),
        name="sample_attend",
    )(pt, q_bd, m_past, m_new, kt_new, vt_new, *([cache_kt] * pps), *([cache_vt] * pps))


def _retention_tables(c_real, c_pad):
    h = np.arange(N_HEADS_R, dtype=np.float64)
    log_g = np.log1p(-np.exp2(-5.0 - h))
    i = np.arange(c_pad, dtype=np.float64)
    diff = i[:, None] - i[None, :]
    live = (diff >= 0) & (i[:, None] < c_real) & (i[None, :] < c_real)
    inner = np.where(live[None], np.exp(np.maximum(diff, 0.0)[None] * log_g[:, None, None]), 0.0)
    q_dec = np.exp((i + 1.0)[None, :] * log_g[:, None])
    k_dec = np.where(i[None, :] < c_real, np.exp((c_real - 1.0 - i)[None, :] * log_g[:, None]), 0.0)
    c_dec = np.exp(c_real * log_g)
    f = lambda a: jnp.asarray(a, F32)
    return f(inner), f(q_dec[:, :, None]), f(k_dec[:, :, None]), [float(v) for v in c_dec]


def _retention_kernel(q_ref, k_ref, v_ref, g_ref, s0_ref, inner_ref, qdec_ref, kdec_ref,
                      o_ref, s_out_ref, s_scr, *, c_dec):
    j = pl.program_id(1)
    nj = pl.num_programs(1)

    @pl.when(j == 0)
    def _():
        s_scr[...] = s0_ref[0]

    for h in range(N_HEADS_R):
        q = q_ref[:, h * DK_R:(h + 1) * DK_R]
        k = k_ref[:, h * DK_R:(h + 1) * DK_R]
        v = v_ref[:, h * DV_R:(h + 1) * DV_R]
        s_prev = s_scr[h]
        a = _dot_nt(_mx(q), _mx(k)) * inner_ref[h]
        o = _dot(_mx(a), v) + _dot(_mx(q), _mx(s_prev)) * qdec_ref[h]
        kd = k * kdec_ref[h]
        s_scr[h] = s_prev * c_dec[h] + _dot(_mx(kd.T), v)
        mu = jnp.mean(o, axis=-1, keepdims=True)
        var = jnp.mean(jnp.square(o - mu), axis=-1, keepdims=True)
        gn = (o - mu) * lax.rsqrt(var + EPS)
        g = g_ref[:, h * DV_R:(h + 1) * DV_R]
        o_ref[:, h * DV_R:(h + 1) * DV_R] = (gn * (g * _sigmoid(g))).astype(o_ref.dtype)

    @pl.when(j == nj - 1)
    def _():
        s_out_ref[0] = s_scr[...]


def _retention(qr, kr, vr, gr, s0, c_real):
    b = s0.shape[0]
    c = RET_CHUNK
    n = qr.shape[0] // (b * c)
    inner, qdec, kdec, c_dec = _retention_tables(c_real, c)
    wr, wv = N_HEADS_R * DK_R, N_HEADS_R * DV_R
    rmap = lambda bi, j: (bi * n + j, 0)
    full3 = lambda shape: pl.BlockSpec(shape, lambda bi, j: (0, 0, 0))
    return pl.pallas_call(
        functools.partial(_retention_kernel, c_dec=c_dec),
        grid=(b, n),
        in_specs=[pl.BlockSpec((c, wr), rmap), pl.BlockSpec((c, wr), rmap), pl.BlockSpec((c, wv), rmap),
                  pl.BlockSpec((c, wv), rmap),
                  pl.BlockSpec((1, N_HEADS_R, DK_R, DV_R), lambda bi, j: (bi, 0, 0, 0)),
                  full3(inner.shape), full3(qdec.shape), full3(kdec.shape)],
        out_specs=[pl.BlockSpec((c, wv), rmap),
                   pl.BlockSpec((1, N_HEADS_R, DK_R, DV_R), lambda bi, j: (bi, 0, 0, 0))],
        out_shape=[jax.ShapeDtypeStruct((b * n * c, wv), MXU_DTYPE),
                   jax.ShapeDtypeStruct((b, N_HEADS_R, DK_R, DV_R), F32)],
        scratch_shapes=[pltpu.VMEM((N_HEADS_R, DK_R, DV_R), F32)],
        compiler_params=_cparams(("parallel", "arbitrary")),
        name="retention",
    )(qr, kr, vr, gr, s0, inner, qdec, kdec)


def _cross_kernel(q_ref, mk_ref, mv_ref, o_ref):
    scale = HEAD_DIM_M ** -0.5
    for h in range(N_HEADS_M):
        sl = slice(h * HEAD_DIM_M, (h + 1) * HEAD_DIM_M)
        s = _dot_nt(_mx(q_ref[:, sl]), _mx(mk_ref[0, :, sl])) * scale
        p = jnp.exp(s - jnp.max(s, axis=1, keepdims=True))
        p = p / jnp.sum(p, axis=1, keepdims=True)
        o_ref[:, sl] = _dot(_mx(p), _mx(mv_ref[0, :, sl])).astype(o_ref.dtype)


def _cross_attend(qm, mk, mv, tm, out_dtype):
    b, n_mem, hd = mk.shape
    nt = qm.shape[0] // (b * tm)
    return pl.pallas_call(
        _cross_kernel,
        grid=(b, nt),
        in_specs=[pl.BlockSpec((tm, hd), lambda bi, i: (bi * nt + i, 0)),
                  pl.BlockSpec((1, n_mem, hd), lambda bi, i: (bi, 0, 0)),
                  pl.BlockSpec((1, n_mem, hd), lambda bi, i: (bi, 0, 0))],
        out_specs=pl.BlockSpec((tm, hd), lambda bi, i: (bi * nt + i, 0)),
        out_shape=jax.ShapeDtypeStruct(qm.shape, out_dtype),
        compiler_params=_cparams(("parallel", "parallel")),
        name="cross_attend",
    )(qm, mk, mv)


def _memkv_kernel(x_ref, g_ref, w_ref, o_ref):
    o_ref[...] = _dot(_mx(_rms(x_ref[...], g_ref[...])), w_ref[...])


def _memory_kv(mem, gain, w):
    rows, d = mem.shape
    return pl.pallas_call(
        _memkv_kernel,
        out_shape=jax.ShapeDtypeStruct((rows, w.shape[1]), F32),
        compiler_params=pltpu.CompilerParams(vmem_limit_bytes=VMEM_LIMIT),
        name="memory_kv",
    )(mem, gain.reshape(1, d), _mx(w))


def _merge_kernel(x_ref, oa_ref, or_ref, om_ref, gates_ref, wpa_ref, wpb_ref, wpc_ref, wo_ref, g_ref, h_ref):
    d = x_ref.shape[1]
    gt = gates_ref[...]
    mixed = (_sigmoid(gt[:, :d]) * _dot(_mx(oa_ref[...]), wpa_ref[...])
             + _sigmoid(gt[:, d:2 * d]) * _dot(_mx(or_ref[...]), wpb_ref[...])
             + _sigmoid(gt[:, 2 * d:]) * _dot(_mx(om_ref[...]), wpc_ref[...]))
    z = _dot(_mx(mixed), wo_ref[...])
    h_ref[...] = x_ref[...] + _rms(z, g_ref[...])


def _merge(x, oa, o_r, om, gates, wpa, wpb, wpc, wo, gain, tm):
    rows, d = x.shape
    return pl.pallas_call(
        _merge_kernel,
        grid=(rows // tm,),
        in_specs=[_row_spec(tm, d), _row_spec(tm, oa.shape[1]), _row_spec(tm, o_r.shape[1]),
                  _row_spec(tm, om.shape[1]), _row_spec(tm, 3 * d),
                  _full_spec(wpa.shape), _full_spec(wpb.shape), _full_spec(wpc.shape), _full_spec(wo.shape),
                  _full_spec((1, d))],
        out_specs=_row_spec(tm, d),
        out_shape=jax.ShapeDtypeStruct((rows, d), F32),
        compiler_params=_cparams(("parallel",)),
        name="merge",
    )(x, oa, o_r, om, gates, wpa, wpb, wpc, wo, gain.reshape(1, d))


HALO = BF16_ROWS


def _ffn_kernel(h_ref, halo_ref, s0_ref, s1_ref, g1_ref, g2_ref, wu_ref, wg_ref, cw_ref, cb_ref, wd_ref,
                y_ref, utail_ref, x_scr, u_scr, *, tm, seq, keep):
    i = pl.program_id(0)
    h = h_ref[...]
    hn = _rms(h, g1_ref[...])
    x_scr[HALO:, :] = _mx(hn)
    x_scr[:HALO, :] = _mx(_rms(halo_ref[...], g1_ref[...]))
    xc = x_scr[...]
    u_scr[...] = _dot(xc, wu_ref[...])
    gate = _dot(xc[HALO:], wg_ref[...])
    cur = u_scr[HALO:, :]
    prev1 = u_scr[HALO - 1:HALO - 1 + tm, :]
    prev2 = u_scr[HALO - 2:HALO - 2 + tm, :]
    seq_loc = min(seq, tm)
    t = lax.rem(lax.broadcasted_iota(I32, (tm, 1), 0), seq_loc)
    t = jnp.where(lax.rem(i * tm, seq) == 0, t, CONV_W)
    st0, st1 = s0_ref[...], s1_ref[...]
    if st0.shape[0] != tm:
        st0, st1 = st0[0:1], st1[0:1]
    prev1 = jnp.where(t == 0, st1, prev1)
    prev2 = jnp.where(t == 0, st0, jnp.where(t == 1, st1, prev2))
    c = cb_ref[...] + prev2 * cw_ref[0:1, :] + prev1 * cw_ref[1:2, :] + cur * cw_ref[2:3, :]
    act = jax.nn.gelu(c, approximate=True) * gate
    ff = _dot(_mx(act), wd_ref[...])
    y_ref[...] = h + _rms(ff, g2_ref[...])
    utail_ref[...] = u_scr[HALO + tm - keep:, :]


def _conv_ffn(h, s0e, s1e, g1, g2, wu, wg, cw, cb, wd, tm, seq, keep):
    rows, d = h.shape
    f = wu.shape[1]
    nt = rows // tm
    hb = tm // HALO
    sr = s0e.shape[0]
    return pl.pallas_call(
        functools.partial(_ffn_kernel, tm=tm, seq=seq, keep=keep),
        grid=(nt,),
        in_specs=[_row_spec(tm, d),
                  pl.BlockSpec((HALO, d), lambda i: (jnp.maximum(i * hb - 1, 0), 0)),
                  _full_spec((sr, f)), _full_spec((sr, f)),
                  _full_spec((1, d)), _full_spec((1, d)),
                  _full_spec(wu.shape), _full_spec(wg.shape), _full_spec(cw.shape), _full_spec((1, f)),
                  _full_spec(wd.shape)],
        out_specs=[_row_spec(tm, d), _row_spec(keep, f)],
        out_shape=[jax.ShapeDtypeStruct((rows, d), F32), jax.ShapeDtypeStruct((nt * keep, f), F32)],
        scratch_shapes=[pltpu.VMEM((tm + HALO, d), MXU_DTYPE), pltpu.VMEM((tm + HALO, f), F32)],
        compiler_params=_cparams(("parallel",)),
        name="conv_ffn",
    )(h, h, s0e, s1e, g1.reshape(1, d), g2.reshape(1, d), wu, wg, cw, cb.reshape(1, f), wd)


def _layer_weights(l, w_in, w_proj_a, w_proj_b, w_proj_c, w_out, w_up, w_down):
    wts = _prep_in_weights(w_in[l])
    f = w_down.shape[1]
    wts.update(wpa=_mx(w_proj_a[l]), wpb=_mx(w_proj_b[l]), wpc=_mx(w_proj_c[l]), wo=_mx(w_out[l]),
               wu=_mx(w_up[l][:, :f]), wg=_mx(w_up[l][:, f:]), wd=_mx(w_down[l]))
    return wts


def _prompt_layer(x, mem, wts, norms, conv_w, conv_b, w_mem_kv, tiles):
    s, d = x.shape
    f = wts["wd"].shape[0]
    pos = jnp.arange(s)
    pr = _projections(x, norms["pre_mix"], pos, wts, tiles["proj"])
    topk = min(TOPK_MAX, s // 4)
    mask = _prompt_select(pr["qi_hm"], pr["wi_f"], pr["ki_b"], topk, tiles["sel_q"], SEL_WIDE)
    o_a = _prompt_attend(pr["qa_t"], pr["ka_hm"], pr["va_t"], mask, tiles["att_q"], tiles["att_k"])
    s0 = jnp.zeros((1, N_HEADS_R, DK_R, DV_R), F32)
    o_r, ret_new = _retention(pr["qr_f"], pr["kr_f"], pr["vr_b"], pr["gr_f"], s0, RET_CHUNK)
    kv = _memory_kv(mem, norms["mem"], w_mem_kv)
    wm = N_HEADS_M * HEAD_DIM_M
    mk, mv = kv[:, :wm], kv[:, wm:]
    o_m = _cross_attend(pr["qm_b"], mk[None], mv[None], tiles["cross"], MXU_DTYPE)
    h = _merge(x, o_a, o_r, o_m, pr["gates_f"], wts["wpa"], wts["wpb"], wts["wpc"], wts["wo"],
               norms["post_mix"], tiles["merge"])
    zst = jnp.zeros((SUBLANES, f), F32)
    y, utail = _conv_ffn(h, zst, zst, norms["pre_ffn"], norms["post_ffn"], wts["wu"], wts["wg"], conv_w, conv_b,
                         wts["wd"], tiles["ffn"], s, SUBLANES)
    conv_new = utail[-(CONV_W - 1):]
    return y, pr["ka_f"], pr["va_f"], pr["ki_f"], ret_new, conv_new, mk, mv


def _sample_layer(x, wts, norms, conv_w, conv_b, cache_k, cache_v, cache_kidx, mem_k, mem_v,
                  state_ret, state_conv, page_table):
    db, t, d = x.shape
    f = wts["wd"].shape[0]
    n_pages = page_table.shape[1]
    page = cache_k.shape[1]
    past = n_pages * page
    rows = db * T_PAD
    xp = jnp.pad(x, ((0, 0), (0, T_PAD - t), (0, 0))).reshape(rows, d)
    pos = jnp.tile(past + jnp.arange(T_PAD), db)
    pr = _projections(xp, norms["pre_mix"], pos, wts, rows)
    hd = N_HEADS_A * HEAD_DIM_A

    topk = min(TOPK_MAX, (past + t) // 4)
    i_past = _sample_index(page_table, pr["qi_hm"].astype(F32), pr["wi_f"], cache_kidx.transpose(0, 2, 1)).reshape(rows, past)
    m_past, m_new = _sample_select(i_past, pr["qi_hm"], pr["wi_f"], pr["ki_b"], t, topk)
    nnew = m_new.shape[1]
    own = m_new[:, :rows].reshape(db, T_PAD, db, T_PAD)[jnp.arange(db), :, jnp.arange(db), :]
    m_new_own = jnp.pad(own, ((0, 0), (0, 0), (0, LANES - T_PAD)))
    q_rows = pr["qa_hm"].reshape(N_HEADS_A, db, T_PAD, HEAD_DIM_A)
    eye = jnp.eye(N_HEADS_A, dtype=MXU_DTYPE)
    q_bd = jnp.einsum("hbtd,hg->bhtgd", q_rows, eye).reshape(db, N_HEADS_A * T_PAD, hd)
    new_t = lambda a: jnp.pad(_mx(a).reshape(db, T_PAD, hd).transpose(0, 2, 1), ((0, 0), (0, 0), (0, LANES - T_PAD)))
    paged_t = lambda c: c.transpose(0, 2, 3, 1).reshape(c.shape[0], hd, page)
    o_a = _sample_attend(page_table, q_bd, m_past.reshape(db, T_PAD, past), m_new_own,
                         new_t(pr["ka_f"]), new_t(pr["va_f"]), paged_t(cache_k), paged_t(cache_v)).reshape(rows, hd)

    padc = lambda a: jnp.pad(a.reshape(db, T_PAD, -1), ((0, 0), (0, RET_CHUNK - T_PAD), (0, 0))).reshape(db * RET_CHUNK, -1)
    o_r, ret_new = _retention(padc(pr["qr_f"]), padc(pr["kr_f"]), padc(pr["vr_b"]), padc(pr["gr_f"]), state_ret, t)
    o_r = o_r.reshape(db, RET_CHUNK, -1)[:, :T_PAD].reshape(rows, -1)

    wm = N_HEADS_M * HEAD_DIM_M
    o_m = _cross_attend(pr["qm_b"].astype(F32), mem_k.reshape(db, -1, wm), mem_v.reshape(db, -1, wm), T_PAD, F32)

    h = _merge(xp, o_a, o_r, o_m, pr["gates_f"], wts["wpa"], wts["wpb"], wts["wpc"], wts["wo"],
               norms["post_mix"], rows)
    s0e = jnp.repeat(state_conv[:, 0], T_PAD, axis=0)
    s1e = jnp.repeat(state_conv[:, 1], T_PAD, axis=0)
    y, u_all = _conv_ffn(h, s0e, s1e, norms["pre_ffn"], norms["post_ffn"], wts["wu"], wts["wg"], conv_w, conv_b,
                         wts["wd"], rows, T_PAD, rows)
    ext = jnp.concatenate([state_conv.astype(F32), u_all.reshape(db, T_PAD, f)[:, :t]], axis=1)
    conv_new = ext[:, t:]
    unpad = lambda a: a.reshape(db, T_PAD, -1)[:, :t]
    return (unpad(y), unpad(pr["ka_f"]), unpad(pr["va_f"]), unpad(pr["ki_f"]), ret_new, conv_new)


PROMPT_TILES = dict(proj=512, sel_q=512, att_q=512, att_k=1024, cross=512, merge=512, ffn=256)


def kernel(x_prompt, x_sample, cache_k, cache_v, cache_kidx, cache_mem_k, cache_mem_v, state_ret, state_conv,
           page_table, mem_prompt, norm_pre_mix, norm_post_mix, norm_pre_ffn, norm_post_ffn, norm_mem,
           w_in, w_mem_kv, w_proj_a, w_proj_b, w_proj_c, w_out, w_up, conv_w, conv_b, w_down):
    bp, s, d = x_prompt.shape
    db, t, _ = x_sample.shape
    depth = w_in.shape[0]
    assert bp == 1 and t <= T_PAD and CONV_W - 1 <= t
    tiles = {k: min(v, s) for k, v in PROMPT_TILES.items()}
    yp, ys = x_prompt[0], x_sample
    outs = [[] for _ in range(12)]
    for l in range(depth):
        wts = _layer_weights(l, w_in, w_proj_a, w_proj_b, w_proj_c, w_out, w_up, w_down)
        norms = dict(pre_mix=norm_pre_mix[l], post_mix=norm_post_mix[l], pre_ffn=norm_pre_ffn[l],
                     post_ffn=norm_post_ffn[l], mem=norm_mem[l])
        yp, kp, vp, kip, rp, cp, mk, mv = _prompt_layer(yp, mem_prompt[0], wts, norms, conv_w[l], conv_b[l],
                                                        w_mem_kv[l], tiles)
        ys, ks, vs, kis, rs, cs = _sample_layer(ys, wts, norms, conv_w[l], conv_b[l], cache_k[l], cache_v[l],
                                                cache_kidx[l], cache_mem_k[l], cache_mem_v[l], state_ret[l],
                                                state_conv[l], page_table)
        n_mem = mk.shape[0]
        vals = (kp.reshape(1, s, N_HEADS_A, HEAD_DIM_A), vp.reshape(1, s, N_HEADS_A, HEAD_DIM_A),
                kip.reshape(1, s, IDX_DIM), rp, cp[None],
                mk.reshape(1, n_mem, N_HEADS_M, HEAD_DIM_M), mv.reshape(1, n_mem, N_HEADS_M, HEAD_DIM_M),
                ks.reshape(db, t, N_HEADS_A, HEAD_DIM_A), vs.reshape(db, t, N_HEADS_A, HEAD_DIM_A),
                kis, rs, cs)
        for o, v in zip(outs, vals):
            o.append(v)
    stacked = [jnp.stack(o) for o in outs]
    return (yp[None], ys, *stacked)
```

```python
import functools
import math

import numpy as np
import jax
import jax.numpy as jnp
from jax import lax
from jax.experimental import pallas as pl
from jax.experimental.pallas import tpu as pltpu

F32 = jnp.float32
I32 = jnp.int32
MXU_DTYPE = jnp.bfloat16

N_HEADS_A, HEAD_DIM_A = 8, 64
IDX_HEADS, IDX_DIM = 4, 64
TOPK_MAX = 256
N_HEADS_R, DK_R, DV_R = 4, 128, 256
RET_CHUNK = 128
N_HEADS_M, HEAD_DIM_M = 4, 128
CONV_W = 3
ROPE_THETA = 10000.0
EPS = 1e-6

LANES = 128
SUBLANES = 8
BF16_ROWS = 16
VMEM_LIMIT = 56 * 1024 * 1024
NEG_BIG = -1e30
F32_LOWEST = float(np.finfo(np.float32).min)
T_PAD = 8


def _cparams(sem):
    return pltpu.CompilerParams(dimension_semantics=sem, vmem_limit_bytes=VMEM_LIMIT)


def _dot(a, b):
    return jnp.dot(a, b, preferred_element_type=F32)


def _dot_nt(a, b):
    return lax.dot_general(a, b, (((1,), (1,)), ((), ())), preferred_element_type=F32)


def _mx(a):
    return a.astype(MXU_DTYPE)


def _rms(x, g):
    return x * lax.rsqrt(jnp.mean(x * x, axis=-1, keepdims=True) + EPS) * g


def _sigmoid(x):
    return 1.0 / (1.0 + jnp.exp(-x))


def _rope_tables(pos, d):
    half = d // 2
    inv = 1.0 / (ROPE_THETA ** (jnp.arange(half, dtype=F32) * 2.0 / d))
    ang = pos.astype(F32)[:, None] * inv[None, :]
    cos, sin = jnp.cos(ang), jnp.sin(ang)
    reps = LANES // d
    cos_t = jnp.tile(jnp.concatenate([cos, cos], axis=1), (1, reps))
    sin_t = jnp.tile(jnp.concatenate([-sin, sin], axis=1), (1, reps))
    return cos_t, sin_t


def _rope(y, cos, sin, d):
    w = y.shape[1]
    half = d // 2
    reps = w // LANES
    c = jnp.concatenate([cos] * reps, axis=1) if reps > 1 else cos
    s = jnp.concatenate([sin] * reps, axis=1) if reps > 1 else sin
    lane = lax.broadcasted_iota(I32, y.shape, 1)
    first = (lane & (d - 1)) < half
    rot = jnp.where(first, pltpu.roll(y, w - half, 1), pltpu.roll(y, half, 1))
    return y * c + rot * s


def _proj_a_kernel(x_ref, g_ref, cos_ref, sin_ref, wa_ref, wv_ref,
                   qa_hm, qa_t, ka_f, ka_hm, qi_hm, ki_f, ki_b, va_f, va_t, wi_f):
    xn = _mx(_rms(x_ref[...], g_ref[...]))
    wa = N_HEADS_A * HEAD_DIM_A
    y = _rope(_dot(xn, wa_ref[...]), cos_ref[...], sin_ref[...], HEAD_DIM_A)
    q = y[:, :wa] * (HEAD_DIM_A ** -0.5)
    k = y[:, wa:2 * wa]
    ka_f[...] = k
    qt = q.T
    for h in range(N_HEADS_A):
        sl = slice(h * HEAD_DIM_A, (h + 1) * HEAD_DIM_A)
        qa_hm[h] = _mx(q[:, sl])
        qa_t[h] = _mx(qt[sl, :])
        ka_hm[h] = _mx(k[:, sl])
    qi = y[:, 2 * wa:2 * wa + IDX_HEADS * IDX_DIM]
    for h in range(IDX_HEADS):
        qi_hm[h] = _mx(qi[:, h * IDX_DIM:(h + 1) * IDX_DIM])
    ki = y[:, 2 * wa + IDX_HEADS * IDX_DIM:2 * wa + IDX_HEADS * IDX_DIM + IDX_DIM]
    ki_f[...] = ki
    ki_b[...] = _mx(ki)
    z = _dot(xn, wv_ref[...])
    v = z[:, :wa]
    va_f[...] = v
    vt = v.T
    for h in range(N_HEADS_A):
        va_t[h] = _mx(vt[h * HEAD_DIM_A:(h + 1) * HEAD_DIM_A, :])
    wi_f[...] = z[:, wa:wa + LANES]


def _proj_b_kernel(x_ref, g_ref, cos_ref, sin_ref, wr_ref, wvg_ref, qr_f, kr_f, vr_b, gr_f):
    xn = _mx(_rms(x_ref[...], g_ref[...]))
    wr = N_HEADS_R * DK_R
    y = _rope(_dot(xn, wr_ref[...]), cos_ref[...], sin_ref[...], DK_R)
    qr_f[...] = y[:, :wr]
    kr_f[...] = y[:, wr:] * (DK_R ** -0.5)
    z = _dot(xn, wvg_ref[...])
    wv = N_HEADS_R * DV_R
    vr_b[...] = _mx(z[:, :wv])
    gr_f[...] = z[:, wv:]


def _proj_c_kernel(x_ref, g_ref, wc_ref, qm_b, gates_f):
    xn = _mx(_rms(x_ref[...], g_ref[...]))
    z = _dot(xn, wc_ref[...])
    wm = N_HEADS_M * HEAD_DIM_M
    qm_b[...] = _mx(z[:, :wm])
    gates_f[...] = z[:, wm:]


def _row_spec(tm, w):
    return pl.BlockSpec((tm, w), lambda i: (i, 0))


def _full_spec(shape):
    nd = len(shape)
    return pl.BlockSpec(shape, lambda i: (0,) * nd)


def _hm_spec(nh, tm, d):
    return pl.BlockSpec((nh, tm, d), lambda i: (0, i, 0))


def _projections(x, gain, pos, wts, tm):
    rows, d = x.shape
    grid = (rows // tm,)
    wa = N_HEADS_A * HEAD_DIM_A
    cos64, sin64 = _rope_tables(pos, HEAD_DIM_A)
    cos128, sin128 = _rope_tables(pos, DK_R)
    g2 = gain.reshape(1, d)
    sds = jax.ShapeDtypeStruct
    t_spec = pl.BlockSpec((N_HEADS_A, HEAD_DIM_A, tm), lambda i: (0, 0, i))
    outs_a = pl.pallas_call(
        _proj_a_kernel,
        grid=grid,
        in_specs=[_row_spec(tm, d), _full_spec((1, d)), _row_spec(tm, LANES), _row_spec(tm, LANES),
                  _full_spec(wts["wa"].shape), _full_spec(wts["wv"].shape)],
        out_specs=[_hm_spec(N_HEADS_A, tm, HEAD_DIM_A), t_spec, _row_spec(tm, wa),
                   _hm_spec(N_HEADS_A, tm, HEAD_DIM_A),
                   _hm_spec(IDX_HEADS, tm, IDX_DIM), _row_spec(tm, IDX_DIM), _row_spec(tm, IDX_DIM),
                   _row_spec(tm, wa), t_spec, _row_spec(tm, LANES)],
        out_shape=[sds((N_HEADS_A, rows, HEAD_DIM_A), MXU_DTYPE), sds((N_HEADS_A, HEAD_DIM_A, rows), MXU_DTYPE),
                   sds((rows, wa), F32),
                   sds((N_HEADS_A, rows, HEAD_DIM_A), MXU_DTYPE), sds((IDX_HEADS, rows, IDX_DIM), MXU_DTYPE),
                   sds((rows, IDX_DIM), F32), sds((rows, IDX_DIM), MXU_DTYPE),
                   sds((rows, wa), F32), sds((N_HEADS_A, HEAD_DIM_A, rows), MXU_DTYPE), sds((rows, LANES), F32)],
        compiler_params=_cparams(("parallel",)),
        name="proj_a",
    )(x, g2, cos64, sin64, wts["wa"], wts["wv"])
    names_a = ("qa_hm", "qa_t", "ka_f", "ka_hm", "qi_hm", "ki_f", "ki_b", "va_f", "va_t", "wi_f")
    wr, wv = N_HEADS_R * DK_R, N_HEADS_R * DV_R
    outs_b = pl.pallas_call(
        _proj_b_kernel,
        grid=grid,
        in_specs=[_row_spec(tm, d), _full_spec((1, d)), _row_spec(tm, LANES), _row_spec(tm, LANES),
                  _full_spec(wts["wr"].shape), _full_spec(wts["wvg"].shape)],
        out_specs=[_row_spec(tm, wr), _row_spec(tm, wr), _row_spec(tm, wv), _row_spec(tm, wv)],
        out_shape=[sds((rows, wr), F32), sds((rows, wr), F32), sds((rows, wv), MXU_DTYPE), sds((rows, wv), F32)],
        compiler_params=_cparams(("parallel",)),
        name="proj_b",
    )(x, g2, cos128, sin128, wts["wr"], wts["wvg"])
    names_b = ("qr_f", "kr_f", "vr_b", "gr_f")
    wm = N_HEADS_M * HEAD_DIM_M
    outs_c = pl.pallas_call(
        _proj_c_kernel,
        grid=grid,
        in_specs=[_row_spec(tm, d), _full_spec((1, d)), _full_spec(wts["wc"].shape)],
        out_specs=[_row_spec(tm, wm), _row_spec(tm, 3 * d)],
        out_shape=[sds((rows, wm), MXU_DTYPE), sds((rows, 3 * d), F32)],
        compiler_params=_cparams(("parallel",)),
        name="proj_c",
    )(x, g2, wts["wc"])
    names_c = ("qm_b", "gates_f")
    out = dict(zip(names_a, outs_a))
    out.update(zip(names_b, outs_b))
    out.update(zip(names_c, outs_c))
    return out


def _prep_in_weights(w_in):
    d = w_in.shape[0]
    wa = N_HEADS_A * HEAD_DIM_A
    widths = (wa, wa, wa, IDX_HEADS * IDX_DIM, IDX_DIM, IDX_HEADS,
              N_HEADS_R * DK_R, N_HEADS_R * DK_R, N_HEADS_R * DV_R, N_HEADS_R * DV_R,
              N_HEADS_M * HEAD_DIM_M, 3 * d)
    offs = np.concatenate([[0], np.cumsum(widths)])
    seg = [w_in[:, int(offs[i]):int(offs[i + 1])] for i in range(len(widths))]
    q_a, k_a, v_a, q_i, k_i, w_i, q_r, k_r, v_r, g_r, q_m, gates = seg
    zpad = lambda n: jnp.zeros((d, n), w_in.dtype)
    return {
        "wa": _mx(jnp.concatenate([q_a, k_a, q_i, k_i, zpad(LANES - IDX_DIM)], axis=1)),
        "wv": _mx(jnp.concatenate([v_a, w_i, zpad(LANES - IDX_HEADS)], axis=1)),
        "wr": _mx(jnp.concatenate([q_r, k_r], axis=1)),
        "wvg": _mx(jnp.concatenate([v_r, g_r], axis=1)),
        "wc": _mx(jnp.concatenate([q_m, gates], axis=1)),
    }


SEL_ROWS = 128
SEL_WIDE = 512
SWEEP_UNROLL = 2
SEL_GROUPS = 2 * LANES
NO_LIMIT = 2 ** 30
MIN_NORMAL_KEY = 0x00800000
SEARCH_PERIOD = 4
SEARCH_CAP = SEARCH_PERIOD * 33


def _key_to_f32(key):
    bits = jnp.where(key >= 0, key, key ^ jnp.int32(0x7FFFFFFF))
    return pltpu.bitcast(bits, F32)


def _f32_to_key(f):
    bits = pltpu.bitcast(f, I32)
    return jnp.where(f == 0.0, 0, jnp.where(bits >= 0, bits, bits ^ jnp.int32(0x7FFFFFFF)))


def _sweep(segments, rs, init, fn):
    acc = init
    base = 0
    for ref, n_wide in segments:
        def body(c, a, ref=ref, base=base):
            off = c * SEL_WIDE
            for k in range(SEL_WIDE // LANES):
                x = ref[rs, pl.ds(pl.multiple_of(off + k * LANES, LANES), LANES)]
                a = fn(a, x, base + off + k * LANES, k)
            return a
        def multi(c, a, body=body):
            for u in range(SWEEP_UNROLL):
                a = body(SWEEP_UNROLL * c + u, a)
            return a
        full = n_wide // SWEEP_UNROLL
        acc = lax.fori_loop(0, full, multi, acc)
        acc = lax.fori_loop(SWEEP_UNROLL * full, n_wide, body, acc)
        base = base + n_wide * SEL_WIDE
    return acc


def _count(segments, rows, make_pred, wanted=None):
    outs = []
    for g in range(rows // SEL_ROWS):
        def group_count(_, g=g):
            rs = slice(g * SEL_ROWS, (g + 1) * SEL_ROWS)
            pred = make_pred(rs)
            acc = _sweep(segments, rs, jnp.zeros((SEL_ROWS, LANES), F32),
                         lambda a, x, idx0, k: a + jnp.where(pred(x, idx0), 1.0, 0.0))
            return jnp.sum(acc.T, axis=0, keepdims=True)
        if wanted is None:
            outs.append(group_count(0))
        else:
            outs.append(lax.cond(wanted[g] > 0, group_count, lambda _: jnp.zeros((1, SEL_ROWS), F32), 0))
    return jnp.concatenate(outs, axis=1) if len(outs) > 1 else outs[0]


def _col_to_row(col):
    return jnp.broadcast_to(col, (col.shape[0], LANES)).T[0:1]


def _row_to_lanes(vec):
    return jnp.broadcast_to(vec, (LANES, vec.shape[1])).T


def _search(count_fn, lo, hi, c_lo, c_hi, target, alive):
    def unfinished(lo, hi, c_lo):
        return jnp.logical_and(alive, jnp.logical_and(c_lo > target, lo + 1 < hi))

    def any_row(flag):
        return jnp.max(jnp.where(flag, 1, 0).astype(I32))

    def cond(carry):
        return jnp.logical_and(carry[0] < SEARCH_CAP, carry[1] > 0)

    def body(carry):
        it, _, phase, lo, hi, c_lo, c_hi, w_lo, w_hi, last = carry
        act = unfinished(lo, hi, c_lo)
        bis = (lo >> 1) + (hi >> 1) + (lo & hi & 1)
        width = hi - lo
        a = (c_lo - target + 0.5) * w_lo
        b = (target - 0.5 - c_hi) * w_hi
        frac = a / jnp.maximum(a + b, 1e-6)
        step = (frac * width.astype(F32)).astype(I32)
        itp = lo + jnp.clip(step, 1, jnp.maximum(width - 1, 1))
        use_itp = jnp.logical_and((lo ^ hi) >= 0, (jnp.zeros_like(lo) + phase) != SEARCH_PERIOD - 1)
        v = jnp.where(use_itp, itp, bis)
        n_groups = lo.shape[1] // SEL_ROWS
        cnt = count_fn(v, [any_row(act[:, g * SEL_ROWS:(g + 1) * SEL_ROWS]) for g in range(n_groups)])
        up = jnp.logical_and(act, cnt >= target)
        dn = jnp.logical_and(act, cnt < target)
        lo = jnp.where(up, v, lo)
        c_lo = jnp.where(up, cnt, c_lo)
        hi = jnp.where(dn, v, hi)
        c_hi = jnp.where(dn, cnt, c_hi)
        w_hi = jnp.where(up, jnp.where(last == 1, w_hi * 0.5, 1.0), jnp.where(dn, 1.0, w_hi))
        w_lo = jnp.where(dn, jnp.where(last == -1, w_lo * 0.5, 1.0), jnp.where(up, 1.0, w_lo))
        last = jnp.where(up, 1, jnp.where(dn, -1, last))
        phase = jnp.where(phase == SEARCH_PERIOD - 1, 0, phase + 1)
        return it + 1, any_row(unfinished(lo, hi, c_lo)), phase, lo, hi, c_lo, c_hi, w_lo, w_hi, last

    go = any_row(unfinished(lo, hi, c_lo))
    one = jnp.ones(lo.shape, F32)
    out = lax.while_loop(cond, body, (jnp.int32(0), go, jnp.int32(0), lo, hi, c_lo, c_hi,
                                      one, one, jnp.zeros(lo.shape, I32)))
    return out[3], out[5], out[6]


def _select_threshold(segments, rows, topk, alive=None):
    assert topk <= SEL_GROUPS
    kf = jnp.full((1, rows), float(topk), F32)
    if alive is None:
        alive = jnp.full((1, rows), True)

    los, his = [], []
    for g in range(rows // SEL_ROWS):
        rs = slice(g * SEL_ROWS, (g + 1) * SEL_ROWS)
        ninf = jnp.full((SEL_ROWS, LANES), -jnp.inf, F32)
        ga, gb = _sweep(segments, rs, (ninf, ninf),
                        lambda a, x, idx0, k: ((jnp.maximum(a[0], x), a[1]) if k % 2 == 0
                                               else (a[0], jnp.maximum(a[1], x))))
        los.append(jnp.min(jnp.minimum(ga, gb), axis=1, keepdims=True))
        his.append(jnp.max(jnp.maximum(ga, gb), axis=1, keepdims=True))
    cat = lambda xs: _col_to_row(jnp.concatenate(xs, axis=0) if len(xs) > 1 else xs[0])
    lo = _f32_to_key(jnp.maximum(cat(los), F32_LOWEST))
    hi = _f32_to_key(cat(his)) + 1

    def count_ge_f(thr, wanted=None):
        thr_l = _row_to_lanes(thr)

        def make_pred(rs):
            t = thr_l[rs]
            return lambda x, idx0: x >= t
        return _count(segments, rows, make_pred, wanted)

    count_ge = lambda v, wanted=None: count_ge_f(_key_to_f32(v), wanted)
    c_lo = count_ge(lo)
    c_hi = jnp.zeros((1, rows), F32)
    def zero_probes(state):
        lo, hi, c_lo, c_hi = state
        for probe in (0, MIN_NORMAL_KEY):
            v = jnp.full((1, rows), probe, I32)
            cnt = count_ge_f(jnp.full((1, rows), np.int32(probe).view(np.float32), F32))
            inside = jnp.logical_and(lo < v, v < hi)
            up = jnp.logical_and(inside, cnt >= kf)
            dn = jnp.logical_and(inside, cnt < kf)
            lo, c_lo = jnp.where(up, v, lo), jnp.where(up, cnt, c_lo)
            hi, c_hi = jnp.where(dn, v, hi), jnp.where(dn, cnt, c_hi)
        return lo, hi, c_lo, c_hi

    straddles = jnp.logical_or(jnp.logical_and(lo < 0, hi > 0),
                               jnp.logical_and(lo < MIN_NORMAL_KEY, hi > MIN_NORMAL_KEY))
    lo, hi, c_lo, c_hi = lax.cond(jnp.max(jnp.where(straddles, 1, 0).astype(I32)) > 0,
                                  zero_probes, lambda s: s, (lo, hi, c_lo, c_hi))
    at_zero = jnp.logical_and(lo == 0, hi == MIN_NORMAL_KEY)
    lo, c_lo, c_hi = _search(count_ge, lo, hi, c_lo, c_hi, kf, jnp.logical_and(alive, jnp.logical_not(at_zero)))
    tau = _key_to_f32(lo)

    need = jnp.logical_and(alive, c_lo > kf)
    room = jnp.where(need, kf - c_hi, float(NO_LIMIT))
    return _row_to_lanes(tau), _row_to_lanes(room)


def _write_mask(segments_out, rows, tau, room):
    r_io = lax.broadcasted_iota(I32, (LANES, 2 * LANES), 0)
    c_io = lax.broadcasted_iota(I32, (LANES, 2 * LANES), 1)
    tri = jnp.where(jnp.logical_or(r_io <= c_io, c_io >= LANES), 1.0, 0.0).astype(MXU_DTYPE)
    for g in range(rows // SEL_ROWS):
        rs = slice(g * SEL_ROWS, (g + 1) * SEL_ROWS)
        t, rm = tau[rs], room[rs]
        seen = jnp.zeros((SEL_ROWS, LANES), F32)
        for src, dst, n_wide, n_total in segments_out:
            def body(c, seen, src=src, dst=dst):
                off = c * SEL_WIDE
                nsub = SEL_WIDE // LANES
                dss = [pl.ds(pl.multiple_of(off + k * LANES, LANES), LANES) for k in range(nsub)]
                xs = [src[rs, ds] for ds in dss]
                eqs = [jnp.where(x == t, 1.0, 0.0) for x in xs]
                run = jnp.dot(jnp.concatenate(eqs, axis=0).astype(MXU_DTYPE), tri, preferred_element_type=F32)
                for k in range(nsub):
                    run_k = run[k * SEL_ROWS:(k + 1) * SEL_ROWS]
                    before = seen + run_k[:, :LANES] - eqs[k]
                    keep_eq = jnp.where(before < rm, eqs[k], 0.0)
                    dst[rs, dss[k]] = jnp.where(xs[k] > t, 1.0, keep_eq).astype(dst.dtype)
                    seen = seen + run_k[:, LANES:]
                return seen

            pairs = n_wide // 2
            seen = lax.fori_loop(0, pairs, lambda c, sn, body=body: body(2 * c + 1, body(2 * c, sn)), seen)
            seen = lax.fori_loop(2 * pairs, n_wide, body, seen)

            def zbody(c, carry, dst=dst):
                off = pl.multiple_of(c * SEL_WIDE, SEL_WIDE)
                dst[rs, pl.ds(off, SEL_WIDE)] = jnp.zeros((SEL_ROWS, SEL_WIDE), dst.dtype)
                return carry

            lax.fori_loop(n_wide, n_total, zbody, 0)


def _index_scores(qi_ref, w, kb, transposed_keys=False):
    acc = None
    for h in range(IDX_HEADS):
        s = _dot(_mx(qi_ref[h]), kb) if transposed_keys else _dot_nt(_mx(qi_ref[h]), kb)
        t = w[:, h:h + 1] * jnp.maximum(s, 0.0)
        acc = t if acc is None else acc + t
    return acc


def _causal_pairs(nq, tq, tk):
    pairs = [(i, j) for i in range(nq) for j in range((i * tq + tq - 1) // tk + 1)]
    return jnp.asarray([p[0] for p in pairs], I32), jnp.asarray([p[1] for p in pairs], I32)


def _prompt_select_kernel(qb_ref, kb_ref, qi_ref, wi_ref, kidx_ref, mask_ref, i_scr, *, tq, tk, topk):
    p = pl.program_id(0)
    i, j = qb_ref[p], kb_ref[p]
    q_lo = i * tq
    n_wide = (q_lo + tq - 1) // tk + 1

    acc = _index_scores(qi_ref, wi_ref[...], kidx_ref[...])
    cols = pl.ds(pl.multiple_of(j * tk, tk), tk)
    below_diagonal = (j + 1) * tk <= q_lo + 1

    @pl.when(below_diagonal)
    def _():
        i_scr[:, cols] = acc

    @pl.when(jnp.logical_not(below_diagonal))
    def _():
        qpos = q_lo + lax.broadcasted_iota(I32, acc.shape, 0)
        kpos = j * tk + lax.broadcasted_iota(I32, acc.shape, 1)
        i_scr[:, cols] = jnp.where(kpos <= qpos, acc, -jnp.inf)

    @pl.when(j == n_wide - 1)
    def _():
        seg = [(i_scr, n_wide)]
        tau, room = _select_threshold(seg, tq, topk)
        _write_mask([(i_scr, mask_ref, n_wide, mask_ref.shape[1] // tk)], tq, tau, room)


def _prompt_select(qi_hm, wi_f, ki_b, topk, tq, tk):
    s = ki_b.shape[0]
    assert tk == SEL_WIDE and s % tk == 0 and s % tq == 0 and tq % SEL_ROWS == 0
    qb, kb = _causal_pairs(s // tq, tq, tk)
    grid_spec = pltpu.PrefetchScalarGridSpec(
        num_scalar_prefetch=2,
        grid=(qb.shape[0],),
        in_specs=[pl.BlockSpec((IDX_HEADS, tq, IDX_DIM), lambda p, qb, kb: (0, qb[p], 0)),
                  pl.BlockSpec((tq, LANES), lambda p, qb, kb: (qb[p], 0)),
                  pl.BlockSpec((tk, IDX_DIM), lambda p, qb, kb: (kb[p], 0))],
        out_specs=pl.BlockSpec((tq, s), lambda p, qb, kb: (qb[p], 0), pipeline_mode=pl.Buffered(1)),
        scratch_shapes=[pltpu.VMEM((tq, s), F32)],
    )
    return pl.pallas_call(
        functools.partial(_prompt_select_kernel, tq=tq, tk=tk, topk=topk),
        grid_spec=grid_spec,
        out_shape=jax.ShapeDtypeStruct((s, s), MXU_DTYPE),
        compiler_params=_cparams(("arbitrary",)),
        name="prompt_select",
    )(qb, kb, qi_hm, wi_f, ki_b)


def _spread(a, n):
    if n <= LANES:
        return a[:, :n]
    return jnp.concatenate([a] * (n // LANES), axis=1)


def _prompt_attend_kernel(qb_ref, kb_ref, qt_ref, k_ref, vt_ref, mask_ref, o_ref, m_scr, l_scr, acc_scr, *, tq, tk):
    p_id = pl.program_id(0)
    i, j = qb_ref[p_id], kb_ref[p_id]

    @pl.when(j == 0)
    def _():
        m_scr[...] = jnp.full(m_scr.shape, NEG_BIG, F32)
        l_scr[...] = jnp.zeros(l_scr.shape, F32)
        acc_scr[...] = jnp.zeros(acc_scr.shape, F32)

    bias = (1.0 - mask_ref[...].astype(F32).T) * NEG_BIG
    scores = [_dot(k_ref[h], qt_ref[h]) + bias for h in range(N_HEADS_A)]
    for h, s in enumerate(scores):
        m_prev = m_scr[h]
        m_new = jnp.maximum(m_prev, jnp.max(s, axis=0, keepdims=True))
        alpha = jnp.exp(m_prev - m_new)
        p = jnp.exp(s - m_new[0:1])
        l_scr[h] = alpha * l_scr[h] + jnp.sum(p, axis=0, keepdims=True)
        acc_scr[h] = alpha[0:1] * acc_scr[h] + _dot(vt_ref[h], _mx(p))
        m_scr[h] = m_new

    @pl.when(j == (i * tq + tq - 1) // tk)
    def _():
        ot = jnp.concatenate([acc_scr[h] / l_scr[h][0:1] for h in range(N_HEADS_A)], axis=0)
        o_ref[...] = ot.T.astype(o_ref.dtype)


def _prompt_attend(qa_t, ka_hm, va_t, mask, tq, tk):
    nh, dh, s = qa_t.shape
    qb, kb = _causal_pairs(s // tq, tq, tk)
    grid_spec = pltpu.PrefetchScalarGridSpec(
        num_scalar_prefetch=2,
        grid=(qb.shape[0],),
        in_specs=[pl.BlockSpec((nh, dh, tq), lambda p, qb, kb: (0, 0, qb[p])),
                  pl.BlockSpec((nh, tk, dh), lambda p, qb, kb: (0, kb[p], 0)),
                  pl.BlockSpec((nh, dh, tk), lambda p, qb, kb: (0, 0, kb[p])),
                  pl.BlockSpec((tq, tk), lambda p, qb, kb: (qb[p], kb[p]))],
        out_specs=pl.BlockSpec((tq, nh * dh), lambda p, qb, kb: (qb[p], 0)),
        scratch_shapes=[pltpu.VMEM((nh, SUBLANES, tq), F32), pltpu.VMEM((nh, SUBLANES, tq), F32),
                        pltpu.VMEM((nh, dh, tq), F32)],
    )
    return pl.pallas_call(
        functools.partial(_prompt_attend_kernel, tq=tq, tk=tk),
        grid_spec=grid_spec,
        out_shape=jax.ShapeDtypeStruct((s, nh * dh), MXU_DTYPE),
        compiler_params=_cparams(("arbitrary",)),
        name="prompt_attend",
    )(qb, kb, qa_t, ka_hm, va_t, mask)


PAGES_PER_STEP = 16
INDEX_PAGES_PER_STEP = 32


def _sample_index_kernel(pt_ref, qi_ref, wi_ref, *refs, page):
    del pt_ref
    pages, out_ref = refs[:-1], refs[-1]
    w = wi_ref[0]
    for p, kref in enumerate(pages):
        out_ref[0, :, p * page:(p + 1) * page] = _index_scores(qi_ref, w, _mx(kref[0]), transposed_keys=True)


def _sample_index(page_table, qi_hm, wi_f, cache_kidx_t):
    db, n_pages = page_table.shape
    _, idim, page = cache_kidx_t.shape
    pps = math.gcd(INDEX_PAGES_PER_STEP, n_pages)
    nsteps = n_pages // pps
    pt = page_table.reshape(-1).astype(I32)

    def kspec(p):
        return pl.BlockSpec((1, idim, page), lambda b, j, pt: (pt[b * n_pages + j * pps + p], 0, 0))

    grid_spec = pltpu.PrefetchScalarGridSpec(
        num_scalar_prefetch=1,
        grid=(db, nsteps),
        in_specs=[pl.BlockSpec((IDX_HEADS, T_PAD, idim), lambda b, j, pt: (0, b, 0)),
                  pl.BlockSpec((1, T_PAD, LANES), lambda b, j, pt: (b, 0, 0))]
                 + [kspec(p) for p in range(pps)],
        out_specs=pl.BlockSpec((1, T_PAD, pps * page), lambda b, j, pt: (b, 0, j)),
    )
    return pl.pallas_call(
        functools.partial(_sample_index_kernel, page=page),
        grid_spec=grid_spec,
        out_shape=jax.ShapeDtypeStruct((db, T_PAD, n_pages * page), F32),
        compiler_params=_cparams(("parallel", "arbitrary")),
        name="sample_index",
    )(pt, qi_hm, wi_f.reshape(db, T_PAD, LANES), *([cache_kidx_t] * pps))


def _sample_select_kernel(ipast_ref, qi_ref, wi_ref, kin_ref, mpast_ref, mnew_ref, inew_scr, *, t_real, topk):
    rows = ipast_ref.shape[0]
    acc = _index_scores(qi_ref, wi_ref[...], kin_ref[...])
    r = lax.broadcasted_iota(I32, acc.shape, 0)
    c = lax.broadcasted_iota(I32, acc.shape, 1)
    same = (r // T_PAD) == (c // T_PAD)
    tq, tc = r % T_PAD, c % T_PAD
    ok = jnp.logical_and(same, jnp.logical_and(tc <= tq, tc < t_real))
    inew_scr[...] = jnp.full(inew_scr.shape, -jnp.inf, F32)
    inew_scr[:, :rows] = jnp.where(ok, acc, -jnp.inf)
    n_past = ipast_ref.shape[1] // SEL_WIDE
    n_new = inew_scr.shape[1] // SEL_WIDE
    alive = lax.rem(lax.broadcasted_iota(I32, (1, rows), 1), T_PAD) < t_real
    tau, room = _select_threshold([(ipast_ref, n_past), (inew_scr, n_new)], rows, topk, alive)
    _write_mask([(ipast_ref, mpast_ref, n_past, n_past), (inew_scr, mnew_ref, n_new, n_new)], rows, tau, room)


def _sample_select(i_past, qi_hm, wi_f, ki_b, t_real, topk):
    rows, past = i_past.shape
    assert past % SEL_WIDE == 0 and rows % SEL_ROWS == 0
    wnew = -(-rows // SEL_WIDE) * SEL_WIDE
    return pl.pallas_call(
        functools.partial(_sample_select_kernel, t_real=t_real, topk=topk),
        out_shape=[jax.ShapeDtypeStruct((rows, past), F32), jax.ShapeDtypeStruct((rows, wnew), F32)],
        scratch_shapes=[pltpu.VMEM((rows, wnew), F32)],
        compiler_params=pltpu.CompilerParams(vmem_limit_bytes=VMEM_LIMIT),
        name="sample_select",
    )(i_past, qi_hm, wi_f, ki_b)


def _sample_attend_kernel(pt_ref, q_ref, mp_ref, mn_ref, kn_ref, vn_ref, *refs, page, pps):
    del pt_ref
    kpages, vpages = refs[:pps], refs[pps:2 * pps]
    o_ref, m_scr, l_scr, acc_scr = refs[2 * pps:]
    j = pl.program_id(1)
    nj = pl.num_programs(1)
    q = q_ref[0]
    nh = q.shape[0] // T_PAD

    def scores(m_t, kt):
        keep = jnp.concatenate([m_t] * nh, axis=0) > 0
        return jnp.where(keep, _dot(q, kt), NEG_BIG)

    def update(s_list, vt_list):
        m_prev = m_scr[...]
        m_blk = functools.reduce(jnp.maximum, [jnp.max(s, axis=1, keepdims=True) for s in s_list])
        m_new = jnp.maximum(m_prev, m_blk)
        alpha = jnp.exp(m_prev - m_new)
        l_new = alpha * l_scr[...]
        acc = _spread(alpha, acc_scr.shape[1]) * acc_scr[...]
        for s, vt in zip(s_list, vt_list):
            p = jnp.exp(s - _spread(m_new, s.shape[1]))
            l_new = l_new + jnp.sum(p, axis=1, keepdims=True)
            acc = acc + _dot_nt(_mx(p), vt)
        m_scr[...], l_scr[...], acc_scr[...] = m_new, l_new, acc

    @pl.when(j == 0)
    def _():
        m_scr[...] = jnp.full(m_scr.shape, NEG_BIG, F32)
        l_scr[...] = jnp.zeros(l_scr.shape, F32)
        acc_scr[...] = jnp.zeros(acc_scr.shape, F32)

    update([scores(mp_ref[0, :, p * page:(p + 1) * page], _mx(kpages[p][0])) for p in range(pps)],
           [_mx(vpages[p][0]) for p in range(pps)])

    @pl.when(j == nj - 1)
    def _():
        update([scores(mn_ref[0], kn_ref[0])], [vn_ref[0]])
        full = acc_scr[...] / _spread(l_scr[...], acc_scr.shape[1])
        lane = lax.broadcasted_iota(I32, (T_PAD, full.shape[1]), 1)
        out = jnp.zeros((T_PAD, full.shape[1]), F32)
        for h in range(nh):
            out = out + jnp.where((lane // HEAD_DIM_A) == h, full[h * T_PAD:(h + 1) * T_PAD], 0.0)
        o_ref[0] = out


def _sample_attend(page_table, q_bd, m_past, m_new, kt_new, vt_new, cache_kt, cache_vt):
    db, n_pages = page_table.shape
    _, hd, page = cache_kt.shape
    pps = math.gcd(PAGES_PER_STEP, n_pages)
    nsteps = n_pages // pps
    nnew = kt_new.shape[2]
    nq = q_bd.shape[1]
    pt = page_table.reshape(-1).astype(I32)

    def pspec(p):
        return pl.BlockSpec((1, hd, page), lambda b, j, pt: (pt[b * n_pages + j * pps + p], 0, 0))

    bspec = lambda shape: pl.BlockSpec((1,) + shape, lambda b, j, pt: (b, 0, 0))
    grid_spec = pltpu.PrefetchScalarGridSpec(
        num_scalar_prefetch=1,
        grid=(db, nsteps),
        in_specs=[bspec((nq, hd)),
                  pl.BlockSpec((1, T_PAD, pps * page), lambda b, j, pt: (b, 0, j)),
                  bspec((T_PAD, nnew)), bspec((hd, nnew)), bspec((hd, nnew))]
                 + [pspec(p) for p in range(pps)] * 2,
        out_specs=bspec((T_PAD, hd)),
        scratch_shapes=[pltpu.VMEM((nq, LANES), F32), pltpu.VMEM((nq, LANES), F32), pltpu.VMEM((nq, hd), F32)],
    )
    return pl.pallas_call(
        functools.partial(_sample_attend_kernel, page=page, pps=pps),
        grid_spec=grid_spec,
        out_shape=jax.ShapeDtypeStruct((db, T_PAD, hd), F32),
        compiler_params=_cparams(("parallel", "arbitrary")),
        name="sample_attend",
    )(pt, q_bd, m_past, m_new, kt_new, vt_new, *([cache_kt] * pps), *([cache_vt] * pps))


def _retention_tables(c_real, c_pad):
    h = np.arange(N_HEADS_R, dtype=np.float64)
    log_g = np.log1p(-np.exp2(-5.0 - h))
    i = np.arange(c_pad, dtype=np.float64)
    diff = i[:, None] - i[None, :]
    live = (diff >= 0) & (i[:, None] < c_real) & (i[None, :] < c_real)
    inner = np.where(live[None], np.exp(np.maximum(diff, 0.0)[None] * log_g[:, None, None]), 0.0)
    q_dec = np.exp((i + 1.0)[None, :] * log_g[:, None])
    k_dec = np.where(i[None, :] < c_real, np.exp((c_real - 1.0 - i)[None, :] * log_g[:, None]), 0.0)
    c_dec = np.exp(c_real * log_g)
    f = lambda a: jnp.asarray(a, F32)
    return f(inner), f(q_dec[:, :, None]), f(k_dec[:, :, None]), [float(v) for v in c_dec]


def _retention_kernel(q_ref, k_ref, v_ref, g_ref, s0_ref, inner_ref, qdec_ref, kdec_ref,
                      o_ref, s_out_ref, s_scr, *, c_dec):
    j = pl.program_id(1)
    nj = pl.num_programs(1)

    @pl.when(j == 0)
    def _():
        s_scr[...] = s0_ref[0]

    for h in range(N_HEADS_R):
        q = q_ref[:, h * DK_R:(h + 1) * DK_R]
        k = k_ref[:, h * DK_R:(h + 1) * DK_R]
        v = v_ref[:, h * DV_R:(h + 1) * DV_R]
        s_prev = s_scr[h]
        a = _dot_nt(_mx(q), _mx(k)) * inner_ref[h]
        o = _dot(_mx(a), v) + _dot(_mx(q), _mx(s_prev)) * qdec_ref[h]
        kd = k * kdec_ref[h]
        s_scr[h] = s_prev * c_dec[h] + _dot(_mx(kd.T), v)
        mu = jnp.mean(o, axis=-1, keepdims=True)
        var = jnp.mean(jnp.square(o - mu), axis=-1, keepdims=True)
        gn = (o - mu) * lax.rsqrt(var + EPS)
        g = g_ref[:, h * DV_R:(h + 1) * DV_R]
        o_ref[:, h * DV_R:(h + 1) * DV_R] = (gn * (g * _sigmoid(g))).astype(o_ref.dtype)

    @pl.when(j == nj - 1)
    def _():
        s_out_ref[0] = s_scr[...]


def _retention(qr, kr, vr, gr, s0, c_real):
    b = s0.shape[0]
    c = RET_CHUNK
    n = qr.shape[0] // (b * c)
    inner, qdec, kdec, c_dec = _retention_tables(c_real, c)
    wr, wv = N_HEADS_R * DK_R, N_HEADS_R * DV_R
    rmap = lambda bi, j: (bi * n + j, 0)
    full3 = lambda shape: pl.BlockSpec(shape, lambda bi, j: (0, 0, 0))
    return pl.pallas_call(
        functools.partial(_retention_kernel, c_dec=c_dec),
        grid=(b, n),
        in_specs=[pl.BlockSpec((c, wr), rmap), pl.BlockSpec((c, wr), rmap), pl.BlockSpec((c, wv), rmap),
                  pl.BlockSpec((c, wv), rmap),
                  pl.BlockSpec((1, N_HEADS_R, DK_R, DV_R), lambda bi, j: (bi, 0, 0, 0)),
                  full3(inner.shape), full3(qdec.shape), full3(kdec.shape)],
        out_specs=[pl.BlockSpec((c, wv), rmap),
                   pl.BlockSpec((1, N_HEADS_R, DK_R, DV_R), lambda bi, j: (bi, 0, 0, 0))],
        out_shape=[jax.ShapeDtypeStruct((b * n * c, wv), MXU_DTYPE),
                   jax.ShapeDtypeStruct((b, N_HEADS_R, DK_R, DV_R), F32)],
        scratch_shapes=[pltpu.VMEM((N_HEADS_R, DK_R, DV_R), F32)],
        compiler_params=_cparams(("parallel", "arbitrary")),
        name="retention",
    )(qr, kr, vr, gr, s0, inner, qdec, kdec)


def _cross_kernel(q_ref, mk_ref, mv_ref, o_ref):
    scale = HEAD_DIM_M ** -0.5
    for h in range(N_HEADS_M):
        sl = slice(h * HEAD_DIM_M, (h + 1) * HEAD_DIM_M)
        s = _dot_nt(_mx(q_ref[:, sl]), _mx(mk_ref[0, :, sl])) * scale
        p = jnp.exp(s - jnp.max(s, axis=1, keepdims=True))
        p = p / jnp.sum(p, axis=1, keepdims=True)
        o_ref[:, sl] = _dot(_mx(p), _mx(mv_ref[0, :, sl])).astype(o_ref.dtype)


def _cross_attend(qm, mk, mv, tm, out_dtype):
    b, n_mem, hd = mk.shape
    nt = qm.shape[0] // (b * tm)
    return pl.pallas_call(
        _cross_kernel,
        grid=(b, nt),
        in_specs=[pl.BlockSpec((tm, hd), lambda bi, i: (bi * nt + i, 0)),
                  pl.BlockSpec((1, n_mem, hd), lambda bi, i: (bi, 0, 0)),
                  pl.BlockSpec((1, n_mem, hd), lambda bi, i: (bi, 0, 0))],
        out_specs=pl.BlockSpec((tm, hd), lambda bi, i: (bi * nt + i, 0)),
        out_shape=jax.ShapeDtypeStruct(qm.shape, out_dtype),
        compiler_params=_cparams(("parallel", "parallel")),
        name="cross_attend",
    )(qm, mk, mv)


def _memkv_kernel(x_ref, g_ref, w_ref, o_ref):
    o_ref[...] = _dot(_mx(_rms(x_ref[...], g_ref[...])), w_ref[...])


def _memory_kv(mem, gain, w):
    rows, d = mem.shape
    return pl.pallas_call(
        _memkv_kernel,
        out_shape=jax.ShapeDtypeStruct((rows, w.shape[1]), F32),
        compiler_params=pltpu.CompilerParams(vmem_limit_bytes=VMEM_LIMIT),
        name="memory_kv",
    )(mem, gain.reshape(1, d), _mx(w))


def _merge_kernel(x_ref, oa_ref, or_ref, om_ref, gates_ref, wpa_ref, wpb_ref, wpc_ref, wo_ref, g_ref, h_ref):
    d = x_ref.shape[1]
    gt = gates_ref[...]
    mixed = (_sigmoid(gt[:, :d]) * _dot(_mx(oa_ref[...]), wpa_ref[...])
             + _sigmoid(gt[:, d:2 * d]) * _dot(_mx(or_ref[...]), wpb_ref[...])
             + _sigmoid(gt[:, 2 * d:]) * _dot(_mx(om_ref[...]), wpc_ref[...]))
    z = _dot(_mx(mixed), wo_ref[...])
    h_ref[...] = x_ref[...] + _rms(z, g_ref[...])


def _merge(x, oa, o_r, om, gates, wpa, wpb, wpc, wo, gain, tm):
    rows, d = x.shape
    return pl.pallas_call(
        _merge_kernel,
        grid=(rows // tm,),
        in_specs=[_row_spec(tm, d), _row_spec(tm, oa.shape[1]), _row_spec(tm, o_r.shape[1]),
                  _row_spec(tm, om.shape[1]), _row_spec(tm, 3 * d),
                  _full_spec(wpa.shape), _full_spec(wpb.shape), _full_spec(wpc.shape), _full_spec(wo.shape),
                  _full_spec((1, d))],
        out_specs=_row_spec(tm, d),
        out_shape=jax.ShapeDtypeStruct((rows, d), F32),
        compiler_params=_cparams(("parallel",)),
        name="merge",
    )(x, oa, o_r, om, gates, wpa, wpb, wpc, wo, gain.reshape(1, d))


HALO = BF16_ROWS


def _ffn_kernel(h_ref, halo_ref, s0_ref, s1_ref, g1_ref, g2_ref, wu_ref, wg_ref, cw_ref, cb_ref, wd_ref,
                y_ref, utail_ref, x_scr, u_scr, *, tm, seq, keep):
    i = pl.program_id(0)
    h = h_ref[...]
    hn = _rms(h, g1_ref[...])
    x_scr[HALO:, :] = _mx(hn)
    x_scr[:HALO, :] = _mx(_rms(halo_ref[...], g1_ref[...]))
    xc = x_scr[...]
    u_scr[...] = _dot(xc, wu_ref[...])
    gate = _dot(xc[HALO:], wg_ref[...])
    cur = u_scr[HALO:, :]
    prev1 = u_scr[HALO - 1:HALO - 1 + tm, :]
    prev2 = u_scr[HALO - 2:HALO - 2 + tm, :]
    seq_loc = min(seq, tm)
    t = lax.rem(lax.broadcasted_iota(I32, (tm, 1), 0), seq_loc)
    t = jnp.where(lax.rem(i * tm, seq) == 0, t, CONV_W)
    st0, st1 = s0_ref[...], s1_ref[...]
    if st0.shape[0] != tm:
        st0, st1 = st0[0:1], st1[0:1]
    prev1 = jnp.where(t == 0, st1, prev1)
    prev2 = jnp.where(t == 0, st0, jnp.where(t == 1, st1, prev2))
    c = cb_ref[...] + prev2 * cw_ref[0:1, :] + prev1 * cw_ref[1:2, :] + cur * cw_ref[2:3, :]
    act = jax.nn.gelu(c, approximate=True) * gate
    ff = _dot(_mx(act), wd_ref[...])
    y_ref[...] = h + _rms(ff, g2_ref[...])
    utail_ref[...] = u_scr[HALO + tm - keep:, :]


def _conv_ffn(h, s0e, s1e, g1, g2, wu, wg, cw, cb, wd, tm, seq, keep):
    rows, d = h.shape
    f = wu.shape[1]
    nt = rows // tm
    hb = tm // HALO
    sr = s0e.shape[0]
    return pl.pallas_call(
        functools.partial(_ffn_kernel, tm=tm, seq=seq, keep=keep),
        grid=(nt,),
        in_specs=[_row_spec(tm, d),
                  pl.BlockSpec((HALO, d), lambda i: (jnp.maximum(i * hb - 1, 0), 0)),
                  _full_spec((sr, f)), _full_spec((sr, f)),
                  _full_spec((1, d)), _full_spec((1, d)),
                  _full_spec(wu.shape), _full_spec(wg.shape), _full_spec(cw.shape), _full_spec((1, f)),
                  _full_spec(wd.shape)],
        out_specs=[_row_spec(tm, d), _row_spec(keep, f)],
        out_shape=[jax.ShapeDtypeStruct((rows, d), F32), jax.ShapeDtypeStruct((nt * keep, f), F32)],
        scratch_shapes=[pltpu.VMEM((tm + HALO, d), MXU_DTYPE), pltpu.VMEM((tm + HALO, f), F32)],
        compiler_params=_cparams(("parallel",)),
        name="conv_ffn",
    )(h, h, s0e, s1e, g1.reshape(1, d), g2.reshape(1, d), wu, wg, cw, cb.reshape(1, f), wd)


def _layer_weights(l, w_in, w_proj_a, w_proj_b, w_proj_c, w_out, w_up, w_down):
    wts = _prep_in_weights(w_in[l])
    f = w_down.shape[1]
    wts.update(wpa=_mx(w_proj_a[l]), wpb=_mx(w_proj_b[l]), wpc=_mx(w_proj_c[l]), wo=_mx(w_out[l]),
               wu=_mx(w_up[l][:, :f]), wg=_mx(w_up[l][:, f:]), wd=_mx(w_down[l]))
    return wts


def _prompt_layer(x, mem, wts, norms, conv_w, conv_b, w_mem_kv, tiles):
    s, d = x.shape
    f = wts["wd"].shape[0]
    pos = jnp.arange(s)
    pr = _projections(x, norms["pre_mix"], pos, wts, tiles["proj"])
    topk = min(TOPK_MAX, s // 4)
    mask = _prompt_select(pr["qi_hm"], pr["wi_f"], pr["ki_b"], topk, tiles["sel_q"], SEL_WIDE)
    o_a = _prompt_attend(pr["qa_t"], pr["ka_hm"], pr["va_t"], mask, tiles["att_q"], tiles["att_k"])
    s0 = jnp.zeros((1, N_HEADS_R, DK_R, DV_R), F32)
    o_r, ret_new = _retention(pr["qr_f"], pr["kr_f"], pr["vr_b"], pr["gr_f"], s0, RET_CHUNK)
    kv = _memory_kv(mem, norms["mem"], w_mem_kv)
    wm = N_HEADS_M * HEAD_DIM_M
    mk, mv = kv[:, :wm], kv[:, wm:]
    o_m = _cross_attend(pr["qm_b"], mk[None], mv[None], tiles["cross"], MXU_DTYPE)
    h = _merge(x, o_a, o_r, o_m, pr["gates_f"], wts["wpa"], wts["wpb"], wts["wpc"], wts["wo"],
               norms["post_mix"], tiles["merge"])
    zst = jnp.zeros((SUBLANES, f), F32)
    y, utail = _conv_ffn(h, zst, zst, norms["pre_ffn"], norms["post_ffn"], wts["wu"], wts["wg"], conv_w, conv_b,
                         wts["wd"], tiles["ffn"], s, SUBLANES)
    conv_new = utail[-(CONV_W - 1):]
    return y, pr["ka_f"], pr["va_f"], pr["ki_f"], ret_new, conv_new, mk, mv


def _sample_layer(x, wts, norms, conv_w, conv_b, cache_k, cache_v, cache_kidx, mem_k, mem_v,
                  state_ret, state_conv, page_table):
    db, t, d = x.shape
    f = wts["wd"].shape[0]
    n_pages = page_table.shape[1]
    page = cache_k.shape[1]
    past = n_pages * page
    rows = db * T_PAD
    xp = jnp.pad(x, ((0, 0), (0, T_PAD - t), (0, 0))).reshape(rows, d)
    pos = jnp.tile(past + jnp.arange(T_PAD), db)
    pr = _projections(xp, norms["pre_mix"], pos, wts, rows)
    hd = N_HEADS_A * HEAD_DIM_A

    topk = min(TOPK_MAX, (past + t) // 4)
    i_past = _sample_index(page_table, pr["qi_hm"].astype(F32), pr["wi_f"], cache_kidx.transpose(0, 2, 1)).reshape(rows, past)
    m_past, m_new = _sample_select(i_past, pr["qi_hm"], pr["wi_f"], pr["ki_b"], t, topk)
    own = m_new[:, :rows].reshape(db, T_PAD, db, T_PAD)[jnp.arange(db), :, jnp.arange(db), :]
    m_new_own = jnp.pad(own, ((0, 0), (0, 0), (0, LANES - T_PAD)))
    q_rows = pr["qa_hm"].reshape(N_HEADS_A, db, T_PAD, HEAD_DIM_A)
    eye = jnp.eye(N_HEADS_A, dtype=MXU_DTYPE)
    q_bd = jnp.einsum("hbtd,hg->bhtgd", q_rows, eye).reshape(db, N_HEADS_A * T_PAD, hd)
    new_t = lambda a: jnp.pad(_mx(a).reshape(db, T_PAD, hd).transpose(0, 2, 1), ((0, 0), (0, 0), (0, LANES - T_PAD)))
    paged_t = lambda c: c.transpose(0, 2, 3, 1).reshape(c.shape[0], hd, page)
    o_a = _sample_attend(page_table, q_bd, m_past.reshape(db, T_PAD, past), m_new_own,
                         new_t(pr["ka_f"]), new_t(pr["va_f"]), paged_t(cache_k), paged_t(cache_v)).reshape(rows, hd)

    padc = lambda a: jnp.pad(a.reshape(db, T_PAD, -1), ((0, 0), (0, RET_CHUNK - T_PAD), (0, 0))).reshape(db * RET_CHUNK, -1)
    o_r, ret_new = _retention(padc(pr["qr_f"]), padc(pr["kr_f"]), padc(pr["vr_b"]), padc(pr["gr_f"]), state_ret, t)
    o_r = o_r.reshape(db, RET_CHUNK, -1)[:, :T_PAD].reshape(rows, -1)

    wm = N_HEADS_M * HEAD_DIM_M
    o_m = _cross_attend(pr["qm_b"].astype(F32), mem_k.reshape(db, -1, wm), mem_v.reshape(db, -1, wm), T_PAD, F32)

    h = _merge(xp, o_a, o_r, o_m, pr["gates_f"], wts["wpa"], wts["wpb"], wts["wpc"], wts["wo"],
               norms["post_mix"], rows)
    s0e = jnp.repeat(state_conv[:, 0], T_PAD, axis=0)
    s1e = jnp.repeat(state_conv[:, 1], T_PAD, axis=0)
    y, u_all = _conv_ffn(h, s0e, s1e, norms["pre_ffn"], norms["post_ffn"], wts["wu"], wts["wg"], conv_w, conv_b,
                         wts["wd"], rows, T_PAD, rows)
    ext = jnp.concatenate([state_conv.astype(F32), u_all.reshape(db, T_PAD, f)[:, :t]], axis=1)
    conv_new = ext[:, t:]
    unpad = lambda a: a.reshape(db, T_PAD, -1)[:, :t]
    return (unpad(y), unpad(pr["ka_f"]), unpad(pr["va_f"]), unpad(pr["ki_f"]), ret_new, conv_new)


PROMPT_TILES = dict(proj=512, sel_q=512, att_q=512, att_k=1024, cross=512, merge=512, ffn=256)


def kernel(x_prompt, x_sample, cache_k, cache_v, cache_kidx, cache_mem_k, cache_mem_v, state_ret, state_conv,
           page_table, mem_prompt, norm_pre_mix, norm_post_mix, norm_pre_ffn, norm_post_ffn, norm_mem,
           w_in, w_mem_kv, w_proj_a, w_proj_b, w_proj_c, w_out, w_up, conv_w, conv_b, w_down):
    bp, s, d = x_prompt.shape
    db, t, _ = x_sample.shape
    depth = w_in.shape[0]
    assert bp == 1 and t <= T_PAD and CONV_W - 1 <= t
    tiles = {k: min(v, s) for k, v in PROMPT_TILES.items()}
    yp, ys = x_prompt[0], x_sample
    outs = [[] for _ in range(12)]
    for l in range(depth):
        wts = _layer_weights(l, w_in, w_proj_a, w_proj_b, w_proj_c, w_out, w_up, w_down)
        norms = dict(pre_mix=norm_pre_mix[l], post_mix=norm_post_mix[l], pre_ffn=norm_pre_ffn[l],
                     post_ffn=norm_post_ffn[l], mem=norm_mem[l])
        yp, kp, vp, kip, rp, cp, mk, mv = _prompt_layer(yp, mem_prompt[0], wts, norms, conv_w[l], conv_b[l],
                                                        w_mem_kv[l], tiles)
        ys, ks, vs, kis, rs, cs = _sample_layer(ys, wts, norms, conv_w[l], conv_b[l], cache_k[l], cache_v[l],
                                                cache_kidx[l], cache_mem_k[l], cache_mem_v[l], state_ret[l],
                                                state_conv[l], page_table)
        n_mem = mk.shape[0]
        vals = (kp.reshape(1, s, N_HEADS_A, HEAD_DIM_A), vp.reshape(1, s, N_HEADS_A, HEAD_DIM_A),
                kip.reshape(1, s, IDX_DIM), rp, cp[None],
                mk.reshape(1, n_mem, N_HEADS_M, HEAD_DIM_M), mv.reshape(1, n_mem, N_HEADS_M, HEAD_DIM_M),
                ks.reshape(db, t, N_HEADS_A, HEAD_DIM_A), vs.reshape(db, t, N_HEADS_A, HEAD_DIM_A),
                kis, rs, cs)
        for o, v in zip(outs, vals):
            o.append(v)
    stacked = [jnp.stack(o) for o in outs]
    return (yp[None], ys, *stacked)
```

```python
import functools
import math

import numpy as np
import jax
import jax.numpy as jnp
from jax import lax
from jax.experimental import pallas as pl
from jax.experimental.pallas import tpu as pltpu

F32 = jnp.float32
I32 = jnp.int32
MXU_DTYPE = jnp.bfloat16

N_HEADS_A, HEAD_DIM_A = 8, 64
IDX_HEADS, IDX_DIM = 4, 64
TOPK_MAX = 256
N_HEADS_R, DK_R, DV_R = 4, 128, 256
RET_CHUNK = 128
N_HEADS_M, HEAD_DIM_M = 4, 128
CONV_W = 3
ROPE_THETA = 10000.0
EPS = 1e-6

LANES = 128
SUBLANES = 8
BF16_ROWS = 16
VMEM_LIMIT = 56 * 1024 * 1024
NEG_BIG = -1e30
F32_LOWEST = float(np.finfo(np.float32).min)
T_PAD = 8


def _cparams(sem):
    return pltpu.CompilerParams(dimension_semantics=sem, vmem_limit_bytes=VMEM_LIMIT)


def _dot(a, b):
    return jnp.dot(a, b, preferred_element_type=F32)


def _dot_nt(a, b):
    return lax.dot_general(a, b, (((1,), (1,)), ((), ())), preferred_element_type=F32)


def _mx(a):
    return a.astype(MXU_DTYPE)


def _rms(x, g):
    return x * lax.rsqrt(jnp.mean(x * x, axis=-1, keepdims=True) + EPS) * g


def _sigmoid(x):
    return 1.0 / (1.0 + jnp.exp(-x))


def _rope_tables(pos, d):
    half = d // 2
    inv = 1.0 / (ROPE_THETA ** (jnp.arange(half, dtype=F32) * 2.0 / d))
    ang = pos.astype(F32)[:, None] * inv[None, :]
    cos, sin = jnp.cos(ang), jnp.sin(ang)
    reps = LANES // d
    cos_t = jnp.tile(jnp.concatenate([cos, cos], axis=1), (1, reps))
    sin_t = jnp.tile(jnp.concatenate([-sin, sin], axis=1), (1, reps))
    return cos_t, sin_t


def _rope(y, cos, sin, d):
    w = y.shape[1]
    half = d // 2
    reps = w // LANES
    c = jnp.concatenate([cos] * reps, axis=1) if reps > 1 else cos
    s = jnp.concatenate([sin] * reps, axis=1) if reps > 1 else sin
    lane = lax.broadcasted_iota(I32, y.shape, 1)
    first = (lane & (d - 1)) < half
    rot = jnp.where(first, pltpu.roll(y, w - half, 1), pltpu.roll(y, half, 1))
    return y * c + rot * s


def _proj_a_kernel(x_ref, g_ref, cos_ref, sin_ref, wa_ref, wv_ref,
                   qa_hm, qa_t, ka_f, ka_hm, qi_hm, ki_f, ki_b, va_f, va_t, wi_f):
    xn = _mx(_rms(x_ref[...], g_ref[...]))
    wa = N_HEADS_A * HEAD_DIM_A
    y = _rope(_dot(xn, wa_ref[...]), cos_ref[...], sin_ref[...], HEAD_DIM_A)
    q = y[:, :wa] * (HEAD_DIM_A ** -0.5)
    k = y[:, wa:2 * wa]
    ka_f[...] = k
    qt = q.T
    for h in range(N_HEADS_A):
        sl = slice(h * HEAD_DIM_A, (h + 1) * HEAD_DIM_A)
        qa_hm[h] = _mx(q[:, sl])
        qa_t[h] = _mx(qt[sl, :])
        ka_hm[h] = _mx(k[:, sl])
    qi = y[:, 2 * wa:2 * wa + IDX_HEADS * IDX_DIM]
    for h in range(IDX_HEADS):
        qi_hm[h] = _mx(qi[:, h * IDX_DIM:(h + 1) * IDX_DIM])
    ki = y[:, 2 * wa + IDX_HEADS * IDX_DIM:2 * wa + IDX_HEADS * IDX_DIM + IDX_DIM]
    ki_f[...] = ki
    ki_b[...] = _mx(ki)
    z = _dot(xn, wv_ref[...])
    v = z[:, :wa]
    va_f[...] = v
    vt = v.T
    for h in range(N_HEADS_A):
        va_t[h] = _mx(vt[h * HEAD_DIM_A:(h + 1) * HEAD_DIM_A, :])
    wi_f[...] = z[:, wa:wa + LANES]


def _proj_b_kernel(x_ref, g_ref, cos_ref, sin_ref, wr_ref, wvg_ref, qr_f, kr_f, vr_b, gr_f):
    xn = _mx(_rms(x_ref[...], g_ref[...]))
    wr = N_HEADS_R * DK_R
    y = _rope(_dot(xn, wr_ref[...]), cos_ref[...], sin_ref[...], DK_R)
    qr_f[...] = y[:, :wr]
    kr_f[...] = y[:, wr:] * (DK_R ** -0.5)
    z = _dot(xn, wvg_ref[...])
    wv = N_HEADS_R * DV_R
    vr_b[...] = _mx(z[:, :wv])
    gr_f[...] = z[:, wv:]


def _proj_c_kernel(x_ref, g_ref, wc_ref, qm_b, gates_f):
    xn = _mx(_rms(x_ref[...], g_ref[...]))
    z = _dot(xn, wc_ref[...])
    wm = N_HEADS_M * HEAD_DIM_M
    qm_b[...] = _mx(z[:, :wm])
    gates_f[...] = z[:, wm:]


def _row_spec(tm, w):
    return pl.BlockSpec((tm, w), lambda i: (i, 0))


def _full_spec(shape):
    nd = len(shape)
    return pl.BlockSpec(shape, lambda i: (0,) * nd)


def _hm_spec(nh, tm, d):
    return pl.BlockSpec((nh, tm, d), lambda i: (0, i, 0))


def _projections(x, gain, pos, wts, tm):
    rows, d = x.shape
    grid = (rows // tm,)
    wa = N_HEADS_A * HEAD_DIM_A
    cos64, sin64 = _rope_tables(pos, HEAD_DIM_A)
    cos128, sin128 = _rope_tables(pos, DK_R)
    g2 = gain.reshape(1, d)
    sds = jax.ShapeDtypeStruct
    t_spec = pl.BlockSpec((N_HEADS_A, HEAD_DIM_A, tm), lambda i: (0, 0, i))
    outs_a = pl.pallas_call(
        _proj_a_kernel,
        grid=grid,
        in_specs=[_row_spec(tm, d), _full_spec((1, d)), _row_spec(tm, LANES), _row_spec(tm, LANES),
                  _full_spec(wts["wa"].shape), _full_spec(wts["wv"].shape)],
        out_specs=[_hm_spec(N_HEADS_A, tm, HEAD_DIM_A), t_spec, _row_spec(tm, wa),
                   _hm_spec(N_HEADS_A, tm, HEAD_DIM_A),
                   _hm_spec(IDX_HEADS, tm, IDX_DIM), _row_spec(tm, IDX_DIM), _row_spec(tm, IDX_DIM),
                   _row_spec(tm, wa), t_spec, _row_spec(tm, LANES)],
        out_shape=[sds((N_HEADS_A, rows, HEAD_DIM_A), MXU_DTYPE), sds((N_HEADS_A, HEAD_DIM_A, rows), MXU_DTYPE),
                   sds((rows, wa), F32),
                   sds((N_HEADS_A, rows, HEAD_DIM_A), MXU_DTYPE), sds((IDX_HEADS, rows, IDX_DIM), MXU_DTYPE),
                   sds((rows, IDX_DIM), F32), sds((rows, IDX_DIM), MXU_DTYPE),
                   sds((rows, wa), F32), sds((N_HEADS_A, HEAD_DIM_A, rows), MXU_DTYPE), sds((rows, LANES), F32)],
        compiler_params=_cparams(("parallel",)),
        name="proj_a",
    )(x, g2, cos64, sin64, wts["wa"], wts["wv"])
    names_a = ("qa_hm", "qa_t", "ka_f", "ka_hm", "qi_hm", "ki_f", "ki_b", "va_f", "va_t", "wi_f")
    wr, wv = N_HEADS_R * DK_R, N_HEADS_R * DV_R
    outs_b = pl.pallas_call(
        _proj_b_kernel,
        grid=grid,
        in_specs=[_row_spec(tm, d), _full_spec((1, d)), _row_spec(tm, LANES), _row_spec(tm, LANES),
                  _full_spec(wts["wr"].shape), _full_spec(wts["wvg"].shape)],
        out_specs=[_row_spec(tm, wr), _row_spec(tm, wr), _row_spec(tm, wv), _row_spec(tm, wv)],
        out_shape=[sds((rows, wr), F32), sds((rows, wr), F32), sds((rows, wv), MXU_DTYPE), sds((rows, wv), F32)],
        compiler_params=_cparams(("parallel",)),
        name="proj_b",
    )(x, g2, cos128, sin128, wts["wr"], wts["wvg"])
    names_b = ("qr_f", "kr_f", "vr_b", "gr_f")
    wm = N_HEADS_M * HEAD_DIM_M
    outs_c = pl.pallas_call(
        _proj_c_kernel,
        grid=grid,
        in_specs=[_row_spec(tm, d), _full_spec((1, d)), _full_spec(wts["wc"].shape)],
        out_specs=[_row_spec(tm, wm), _row_spec(tm, 3 * d)],
        out_shape=[sds((rows, wm), MXU_DTYPE), sds((rows, 3 * d), F32)],
        compiler_params=_cparams(("parallel",)),
        name="proj_c",
    )(x, g2, wts["wc"])
    names_c = ("qm_b", "gates_f")
    out = dict(zip(names_a, outs_a))
    out.update(zip(names_b, outs_b))
    out.update(zip(names_c, outs_c))
    return out


def _prep_in_weights(w_in):
    d = w_in.shape[0]
    wa = N_HEADS_A * HEAD_DIM_A
    widths = (wa, wa, wa, IDX_HEADS * IDX_DIM, IDX_DIM, IDX_HEADS,
              N_HEADS_R * DK_R, N_HEADS_R * DK_R, N_HEADS_R * DV_R, N_HEADS_R * DV_R,
              N_HEADS_M * HEAD_DIM_M, 3 * d)
    offs = np.concatenate([[0], np.cumsum(widths)])
    seg = [w_in[:, int(offs[i]):int(offs[i + 1])] for i in range(len(widths))]
    q_a, k_a, v_a, q_i, k_i, w_i, q_r, k_r, v_r, g_r, q_m, gates = seg
    zpad = lambda n: jnp.zeros((d, n), w_in.dtype)
    return {
        "wa": _mx(jnp.concatenate([q_a, k_a, q_i, k_i, zpad(LANES - IDX_DIM)], axis=1)),
        "wv": _mx(jnp.concatenate([v_a, w_i, zpad(LANES - IDX_HEADS)], axis=1)),
        "wr": _mx(jnp.concatenate([q_r, k_r], axis=1)),
        "wvg": _mx(jnp.concatenate([v_r, g_r], axis=1)),
        "wc": _mx(jnp.concatenate([q_m, gates], axis=1)),
    }


SEL_ROWS = 128
SEL_WIDE = 512
SWEEP_UNROLL = 2
SEL_GROUPS = 2 * LANES
NO_LIMIT = 2 ** 30
MIN_NORMAL_KEY = 0x00800000
SEARCH_INTERP_STEPS = 8
SEARCH_CAP = SEARCH_INTERP_STEPS + 33


def _key_to_f32(key):
    bits = jnp.where(key >= 0, key, key ^ jnp.int32(0x7FFFFFFF))
    return pltpu.bitcast(bits, F32)


def _f32_to_key(f):
    bits = pltpu.bitcast(f, I32)
    return jnp.where(f == 0.0, 0, jnp.where(bits >= 0, bits, bits ^ jnp.int32(0x7FFFFFFF)))


def _sweep(segments, rs, init, fn):
    acc = init
    base = 0
    for ref, n_wide in segments:
        def body(c, a, ref=ref, base=base):
            off = c * SEL_WIDE
            for k in range(SEL_WIDE // LANES):
                x = ref[rs, pl.ds(pl.multiple_of(off + k * LANES, LANES), LANES)]
                a = fn(a, x, base + off + k * LANES, k)
            return a
        def multi(c, a, body=body):
            for u in range(SWEEP_UNROLL):
                a = body(SWEEP_UNROLL * c + u, a)
            return a
        full = n_wide // SWEEP_UNROLL
        acc = lax.fori_loop(0, full, multi, acc)
        acc = lax.fori_loop(SWEEP_UNROLL * full, n_wide, body, acc)
        base = base + n_wide * SEL_WIDE
    return acc


def _count(segments, rows, make_pred, wanted=None):
    outs = []
    for g in range(rows // SEL_ROWS):
        def group_count(_, g=g):
            rs = slice(g * SEL_ROWS, (g + 1) * SEL_ROWS)
            pred = make_pred(rs)
            acc = _sweep(segments, rs, jnp.zeros((SEL_ROWS, LANES), F32),
                         lambda a, x, idx0, k: a + jnp.where(pred(x, idx0), 1.0, 0.0))
            return jnp.sum(acc.T, axis=0, keepdims=True)
        if wanted is None:
            outs.append(group_count(0))
        else:
            outs.append(lax.cond(wanted[g] > 0, group_count, lambda _: jnp.zeros((1, SEL_ROWS), F32), 0))
    return jnp.concatenate(outs, axis=1) if len(outs) > 1 else outs[0]


def _col_to_row(col):
    return jnp.broadcast_to(col, (col.shape[0], LANES)).T[0:1]


def _row_to_lanes(vec):
    return jnp.broadcast_to(vec, (LANES, vec.shape[1])).T


def _search(count_fn, lo, hi, c_lo, c_hi, target, alive):
    def unfinished(lo, hi, c_lo):
        return jnp.logical_and(alive, jnp.logical_and(c_lo > target, lo + 1 < hi))

    def any_row(flag):
        return jnp.max(jnp.where(flag, 1, 0).astype(I32))

    def cond(carry):
        return jnp.logical_and(carry[0] < SEARCH_CAP, carry[1] > 0)

    def body(carry):
        it, _, lo, hi, c_lo, c_hi, w_lo, w_hi, last = carry
        act = unfinished(lo, hi, c_lo)
        bis = (lo >> 1) + (hi >> 1) + (lo & hi & 1)
        width = hi - lo
        a = (c_lo - target + 0.5) * w_lo
        b = (target - 0.5 - c_hi) * w_hi
        frac = a / jnp.maximum(a + b, 1e-6)
        step = (frac * width.astype(F32)).astype(I32)
        itp = lo + jnp.clip(step, 1, jnp.maximum(width - 1, 1))
        use_itp = jnp.logical_and((lo ^ hi) >= 0, (jnp.zeros_like(lo) + it) < SEARCH_INTERP_STEPS)
        v = jnp.where(use_itp, itp, bis)
        n_groups = lo.shape[1] // SEL_ROWS
        cnt = count_fn(v, [any_row(act[:, g * SEL_ROWS:(g + 1) * SEL_ROWS]) for g in range(n_groups)])
        up = jnp.logical_and(act, cnt >= target)
        dn = jnp.logical_and(act, cnt < target)
        lo = jnp.where(up, v, lo)
        c_lo = jnp.where(up, cnt, c_lo)
        hi = jnp.where(dn, v, hi)
        c_hi = jnp.where(dn, cnt, c_hi)
        w_hi = jnp.where(up, jnp.where(last == 1, w_hi * 0.5, 1.0), jnp.where(dn, 1.0, w_hi))
        w_lo = jnp.where(dn, jnp.where(last == -1, w_lo * 0.5, 1.0), jnp.where(up, 1.0, w_lo))
        last = jnp.where(up, 1, jnp.where(dn, -1, last))
        return it + 1, any_row(unfinished(lo, hi, c_lo)), lo, hi, c_lo, c_hi, w_lo, w_hi, last

    go = any_row(unfinished(lo, hi, c_lo))
    one = jnp.ones(lo.shape, F32)
    out = lax.while_loop(cond, body, (jnp.int32(0), go, lo, hi, c_lo, c_hi, one, one, jnp.zeros(lo.shape, I32)))
    return out[2], out[4], out[5]


def _select_threshold(segments, rows, topk, alive=None):
    assert topk <= SEL_GROUPS
    kf = jnp.full((1, rows), float(topk), F32)
    if alive is None:
        alive = jnp.full((1, rows), True)

    los, his = [], []
    for g in range(rows // SEL_ROWS):
        rs = slice(g * SEL_ROWS, (g + 1) * SEL_ROWS)
        ninf = jnp.full((SEL_ROWS, LANES), -jnp.inf, F32)
        ga, gb = _sweep(segments, rs, (ninf, ninf),
                        lambda a, x, idx0, k: ((jnp.maximum(a[0], x), a[1]) if k % 2 == 0
                                               else (a[0], jnp.maximum(a[1], x))))
        los.append(jnp.min(jnp.minimum(ga, gb), axis=1, keepdims=True))
        his.append(jnp.max(jnp.maximum(ga, gb), axis=1, keepdims=True))
    cat = lambda xs: _col_to_row(jnp.concatenate(xs, axis=0) if len(xs) > 1 else xs[0])
    lo = _f32_to_key(jnp.maximum(cat(los), F32_LOWEST))
    hi = _f32_to_key(cat(his)) + 1

    def count_ge_f(thr, wanted=None):
        thr_l = _row_to_lanes(thr)

        def make_pred(rs):
            t = thr_l[rs]
            return lambda x, idx0: x >= t
        return _count(segments, rows, make_pred, wanted)

    count_ge = lambda v, wanted=None: count_ge_f(_key_to_f32(v), wanted)
    c_lo = count_ge(lo)
    c_hi = jnp.zeros((1, rows), F32)
    def zero_probes(state):
        lo, hi, c_lo, c_hi = state
        for probe in (0, MIN_NORMAL_KEY):
            v = jnp.full((1, rows), probe, I32)
            cnt = count_ge_f(jnp.full((1, rows), np.int32(probe).view(np.float32), F32))
            inside = jnp.logical_and(lo < v, v < hi)
            up = jnp.logical_and(inside, cnt >= kf)
            dn = jnp.logical_and(inside, cnt < kf)
            lo, c_lo = jnp.where(up, v, lo), jnp.where(up, cnt, c_lo)
            hi, c_hi = jnp.where(dn, v, hi), jnp.where(dn, cnt, c_hi)
        return lo, hi, c_lo, c_hi

    straddles = jnp.logical_or(jnp.logical_and(lo < 0, hi > 0),
                               jnp.logical_and(lo < MIN_NORMAL_KEY, hi > MIN_NORMAL_KEY))
    lo, hi, c_lo, c_hi = lax.cond(jnp.max(jnp.where(straddles, 1, 0).astype(I32)) > 0,
                                  zero_probes, lambda s: s, (lo, hi, c_lo, c_hi))
    at_zero = jnp.logical_and(lo == 0, hi == MIN_NORMAL_KEY)
    lo, c_lo, c_hi = _search(count_ge, lo, hi, c_lo, c_hi, kf, jnp.logical_and(alive, jnp.logical_not(at_zero)))
    tau = _key_to_f32(lo)

    need = jnp.logical_and(alive, c_lo > kf)
    room = jnp.where(need, kf - c_hi, float(NO_LIMIT))
    return _row_to_lanes(tau), _row_to_lanes(room)


def _write_mask(segments_out, rows, tau, room):
    r_io = lax.broadcasted_iota(I32, (LANES, 2 * LANES), 0)
    c_io = lax.broadcasted_iota(I32, (LANES, 2 * LANES), 1)
    tri = jnp.where(jnp.logical_or(r_io <= c_io, c_io >= LANES), 1.0, 0.0).astype(MXU_DTYPE)
    for g in range(rows // SEL_ROWS):
        rs = slice(g * SEL_ROWS, (g + 1) * SEL_ROWS)
        t, rm = tau[rs], room[rs]
        seen = jnp.zeros((SEL_ROWS, LANES), F32)
        for src, dst, n_wide, n_total in segments_out:
            def body(c, seen, src=src, dst=dst):
                off = c * SEL_WIDE
                nsub = SEL_WIDE // LANES
                dss = [pl.ds(pl.multiple_of(off + k * LANES, LANES), LANES) for k in range(nsub)]
                xs = [src[rs, ds] for ds in dss]
                eqs = [jnp.where(x == t, 1.0, 0.0) for x in xs]
                run = jnp.dot(jnp.concatenate(eqs, axis=0).astype(MXU_DTYPE), tri, preferred_element_type=F32)
                for k in range(nsub):
                    run_k = run[k * SEL_ROWS:(k + 1) * SEL_ROWS]
                    before = seen + run_k[:, :LANES] - eqs[k]
                    keep_eq = jnp.where(before < rm, eqs[k], 0.0)
                    dst[rs, dss[k]] = jnp.where(xs[k] > t, 1.0, keep_eq).astype(dst.dtype)
                    seen = seen + run_k[:, LANES:]
                return seen

            pairs = n_wide // 2
            seen = lax.fori_loop(0, pairs, lambda c, sn, body=body: body(2 * c + 1, body(2 * c, sn)), seen)
            seen = lax.fori_loop(2 * pairs, n_wide, body, seen)

            def zbody(c, carry, dst=dst):
                off = pl.multiple_of(c * SEL_WIDE, SEL_WIDE)
                dst[rs, pl.ds(off, SEL_WIDE)] = jnp.zeros((SEL_ROWS, SEL_WIDE), dst.dtype)
                return carry

            lax.fori_loop(n_wide, n_total, zbody, 0)


def _index_scores(qi_ref, w, kb, transposed_keys=False):
    acc = None
    for h in range(IDX_HEADS):
        s = _dot(_mx(qi_ref[h]), kb) if transposed_keys else _dot_nt(_mx(qi_ref[h]), kb)
        t = w[:, h:h + 1] * jnp.maximum(s, 0.0)
        acc = t if acc is None else acc + t
    return acc


def _causal_pairs(nq, tq, tk):
    pairs = [(i, j) for i in range(nq) for j in range((i * tq + tq - 1) // tk + 1)]
    return jnp.asarray([p[0] for p in pairs], I32), jnp.asarray([p[1] for p in pairs], I32)


def _prompt_select_kernel(qb_ref, kb_ref, qi_ref, wi_ref, kidx_ref, mask_ref, i_scr, *, tq, tk, topk):
    p = pl.program_id(0)
    i, j = qb_ref[p], kb_ref[p]
    q_lo = i * tq
    n_wide = (q_lo + tq - 1) // tk + 1

    acc = _index_scores(qi_ref, wi_ref[...], kidx_ref[...])
    cols = pl.ds(pl.multiple_of(j * tk, tk), tk)
    below_diagonal = (j + 1) * tk <= q_lo + 1

    @pl.when(below_diagonal)
    def _():
        i_scr[:, cols] = acc

    @pl.when(jnp.logical_not(below_diagonal))
    def _():
        qpos = q_lo + lax.broadcasted_iota(I32, acc.shape, 0)
        kpos = j * tk + lax.broadcasted_iota(I32, acc.shape, 1)
        i_scr[:, cols] = jnp.where(kpos <= qpos, acc, -jnp.inf)

    @pl.when(j == n_wide - 1)
    def _():
        seg = [(i_scr, n_wide)]
        tau, room = _select_threshold(seg, tq, topk)
        _write_mask([(i_scr, mask_ref, n_wide, mask_ref.shape[1] // tk)], tq, tau, room)


def _prompt_select(qi_hm, wi_f, ki_b, topk, tq, tk):
    s = ki_b.shape[0]
    assert tk == SEL_WIDE and s % tk == 0 and s % tq == 0 and tq % SEL_ROWS == 0
    qb, kb = _causal_pairs(s // tq, tq, tk)
    grid_spec = pltpu.PrefetchScalarGridSpec(
        num_scalar_prefetch=2,
        grid=(qb.shape[0],),
        in_specs=[pl.BlockSpec((IDX_HEADS, tq, IDX_DIM), lambda p, qb, kb: (0, qb[p], 0)),
                  pl.BlockSpec((tq, LANES), lambda p, qb, kb: (qb[p], 0)),
                  pl.BlockSpec((tk, IDX_DIM), lambda p, qb, kb: (kb[p], 0))],
        out_specs=pl.BlockSpec((tq, s), lambda p, qb, kb: (qb[p], 0), pipeline_mode=pl.Buffered(1)),
        scratch_shapes=[pltpu.VMEM((tq, s), F32)],
    )
    return pl.pallas_call(
        functools.partial(_prompt_select_kernel, tq=tq, tk=tk, topk=topk),
        grid_spec=grid_spec,
        out_shape=jax.ShapeDtypeStruct((s, s), MXU_DTYPE),
        compiler_params=_cparams(("arbitrary",)),
        name="prompt_select",
    )(qb, kb, qi_hm, wi_f, ki_b)


def _spread(a, n):
    if n <= LANES:
        return a[:, :n]
    return jnp.concatenate([a] * (n // LANES), axis=1)


def _prompt_attend_kernel(qb_ref, kb_ref, qt_ref, k_ref, vt_ref, mask_ref, o_ref, m_scr, l_scr, acc_scr, *, tq, tk):
    p_id = pl.program_id(0)
    i, j = qb_ref[p_id], kb_ref[p_id]

    @pl.when(j == 0)
    def _():
        m_scr[...] = jnp.full(m_scr.shape, NEG_BIG, F32)
        l_scr[...] = jnp.zeros(l_scr.shape, F32)
        acc_scr[...] = jnp.zeros(acc_scr.shape, F32)

    bias = (1.0 - mask_ref[...].astype(F32).T) * NEG_BIG
    scores = [_dot(k_ref[h], qt_ref[h]) + bias for h in range(N_HEADS_A)]
    for h, s in enumerate(scores):
        m_prev = m_scr[h]
        m_new = jnp.maximum(m_prev, jnp.max(s, axis=0, keepdims=True))
        alpha = jnp.exp(m_prev - m_new)
        p = jnp.exp(s - m_new[0:1])
        l_scr[h] = alpha * l_scr[h] + jnp.sum(p, axis=0, keepdims=True)
        acc_scr[h] = alpha[0:1] * acc_scr[h] + _dot(vt_ref[h], _mx(p))
        m_scr[h] = m_new

    @pl.when(j == (i * tq + tq - 1) // tk)
    def _():
        ot = jnp.concatenate([acc_scr[h] / l_scr[h][0:1] for h in range(N_HEADS_A)], axis=0)
        o_ref[...] = ot.T.astype(o_ref.dtype)


def _prompt_attend(qa_t, ka_hm, va_t, mask, tq, tk):
    nh, dh, s = qa_t.shape
    qb, kb = _causal_pairs(s // tq, tq, tk)
    grid_spec = pltpu.PrefetchScalarGridSpec(
        num_scalar_prefetch=2,
        grid=(qb.shape[0],),
        in_specs=[pl.BlockSpec((nh, dh, tq), lambda p, qb, kb: (0, 0, qb[p])),
                  pl.BlockSpec((nh, tk, dh), lambda p, qb, kb: (0, kb[p], 0)),
                  pl.BlockSpec((nh, dh, tk), lambda p, qb, kb: (0, 0, kb[p])),
                  pl.BlockSpec((tq, tk), lambda p, qb, kb: (qb[p], kb[p]))],
        out_specs=pl.BlockSpec((tq, nh * dh), lambda p, qb, kb: (qb[p], 0)),
        scratch_shapes=[pltpu.VMEM((nh, SUBLANES, tq), F32), pltpu.VMEM((nh, SUBLANES, tq), F32),
                        pltpu.VMEM((nh, dh, tq), F32)],
    )
    return pl.pallas_call(
        functools.partial(_prompt_attend_kernel, tq=tq, tk=tk),
        grid_spec=grid_spec,
        out_shape=jax.ShapeDtypeStruct((s, nh * dh), MXU_DTYPE),
        compiler_params=_cparams(("arbitrary",)),
        name="prompt_attend",
    )(qb, kb, qa_t, ka_hm, va_t, mask)


PAGES_PER_STEP = 16
INDEX_PAGES_PER_STEP = 32


def _sample_index_kernel(pt_ref, qi_ref, wi_ref, *refs, page):
    del pt_ref
    pages, out_ref = refs[:-1], refs[-1]
    w = wi_ref[0]
    for p, kref in enumerate(pages):
        out_ref[0, :, p * page:(p + 1) * page] = _index_scores(qi_ref, w, _mx(kref[0]), transposed_keys=True)


def _sample_index(page_table, qi_hm, wi_f, cache_kidx_t):
    db, n_pages = page_table.shape
    _, idim, page = cache_kidx_t.shape
    pps = math.gcd(INDEX_PAGES_PER_STEP, n_pages)
    nsteps = n_pages // pps
    pt = page_table.reshape(-1).astype(I32)

    def kspec(p):
        return pl.BlockSpec((1, idim, page), lambda b, j, pt: (pt[b * n_pages + j * pps + p], 0, 0))

    grid_spec = pltpu.PrefetchScalarGridSpec(
        num_scalar_prefetch=1,
        grid=(db, nsteps),
        in_specs=[pl.BlockSpec((IDX_HEADS, T_PAD, idim), lambda b, j, pt: (0, b, 0)),
                  pl.BlockSpec((1, T_PAD, LANES), lambda b, j, pt: (b, 0, 0))]
                 + [kspec(p) for p in range(pps)],
        out_specs=pl.BlockSpec((1, T_PAD, pps * page), lambda b, j, pt: (b, 0, j)),
    )
    return pl.pallas_call(
        functools.partial(_sample_index_kernel, page=page),
        grid_spec=grid_spec,
        out_shape=jax.ShapeDtypeStruct((db, T_PAD, n_pages * page), F32),
        compiler_params=_cparams(("parallel", "arbitrary")),
        name="sample_index",
    )(pt, qi_hm, wi_f.reshape(db, T_PAD, LANES), *([cache_kidx_t] * pps))


def _sample_select_kernel(ipast_ref, qi_ref, wi_ref, kin_ref, mpast_ref, mnew_ref, inew_scr, *, t_real, topk):
    rows = ipast_ref.shape[0]
    acc = _index_scores(qi_ref, wi_ref[...], kin_ref[...])
    r = lax.broadcasted_iota(I32, acc.shape, 0)
    c = lax.broadcasted_iota(I32, acc.shape, 1)
    same = (r // T_PAD) == (c // T_PAD)
    tq, tc = r % T_PAD, c % T_PAD
    ok = jnp.logical_and(same, jnp.logical_and(tc <= tq, tc < t_real))
    inew_scr[...] = jnp.full(inew_scr.shape, -jnp.inf, F32)
    inew_scr[:, :rows] = jnp.where(ok, acc, -jnp.inf)
    n_past = ipast_ref.shape[1] // SEL_WIDE
    n_new = inew_scr.shape[1] // SEL_WIDE
    alive = lax.rem(lax.broadcasted_iota(I32, (1, rows), 1), T_PAD) < t_real
    tau, room = _select_threshold([(ipast_ref, n_past), (inew_scr, n_new)], rows, topk, alive)
    _write_mask([(ipast_ref, mpast_ref, n_past, n_past), (inew_scr, mnew_ref, n_new, n_new)], rows, tau, room)


def _sample_select(i_past, qi_hm, wi_f, ki_b, t_real, topk):
    rows, past = i_past.shape
    assert past % SEL_WIDE == 0 and rows % SEL_ROWS == 0
    wnew = -(-rows // SEL_WIDE) * SEL_WIDE
    return pl.pallas_call(
        functools.partial(_sample_select_kernel, t_real=t_real, topk=topk),
        out_shape=[jax.ShapeDtypeStruct((rows, past), F32), jax.ShapeDtypeStruct((rows, wnew), F32)],
        scratch_shapes=[pltpu.VMEM((rows, wnew), F32)],
        compiler_params=pltpu.CompilerParams(vmem_limit_bytes=VMEM_LIMIT),
        name="sample_select",
    )(i_past, qi_hm, wi_f, ki_b)


def _sample_attend_kernel(pt_ref, q_ref, mp_ref, mn_ref, kn_ref, vn_ref, *refs, page, pps):
    del pt_ref
    kpages, vpages = refs[:pps], refs[pps:2 * pps]
    o_ref, m_scr, l_scr, acc_scr = refs[2 * pps:]
    j = pl.program_id(1)
    nj = pl.num_programs(1)
    q = q_ref[0]
    nh = q.shape[0] // T_PAD

    def scores(m_t, kt):
        keep = jnp.concatenate([m_t] * nh, axis=0) > 0
        return jnp.where(keep, _dot(q, kt), NEG_BIG)

    def update(s_list, vt_list):
        m_prev = m_scr[...]
        m_blk = functools.reduce(jnp.maximum, [jnp.max(s, axis=1, keepdims=True) for s in s_list])
        m_new = jnp.maximum(m_prev, m_blk)
        alpha = jnp.exp(m_prev - m_new)
        l_new = alpha * l_scr[...]
        acc = _spread(alpha, acc_scr.shape[1]) * acc_scr[...]
        for s, vt in zip(s_list, vt_list):
            p = jnp.exp(s - _spread(m_new, s.shape[1]))
            l_new = l_new + jnp.sum(p, axis=1, keepdims=True)
            acc = acc + _dot_nt(_mx(p), vt)
        m_scr[...], l_scr[...], acc_scr[...] = m_new, l_new, acc

    @pl.when(j == 0)
    def _():
        m_scr[...] = jnp.full(m_scr.shape, NEG_BIG, F32)
        l_scr[...] = jnp.zeros(l_scr.shape, F32)
        acc_scr[...] = jnp.zeros(acc_scr.shape, F32)

    update([scores(mp_ref[0, :, p * page:(p + 1) * page], _mx(kpages[p][0])) for p in range(pps)],
           [_mx(vpages[p][0]) for p in range(pps)])

    @pl.when(j == nj - 1)
    def _():
        update([scores(mn_ref[0], kn_ref[0])], [vn_ref[0]])
        full = acc_scr[...] / _spread(l_scr[...], acc_scr.shape[1])
        lane = lax.broadcasted_iota(I32, (T_PAD, full.shape[1]), 1)
        out = jnp.zeros((T_PAD, full.shape[1]), F32)
        for h in range(nh):
            out = out + jnp.where((lane // HEAD_DIM_A) == h, full[h * T_PAD:(h + 1) * T_PAD], 0.0)
        o_ref[0] = out


def _sample_attend(page_table, q_bd, m_past, m_new, kt_new, vt_new, cache_kt, cache_vt):
    db, n_pages = page_table.shape
    _, hd, page = cache_kt.shape
    pps = math.gcd(PAGES_PER_STEP, n_pages)
    nsteps = n_pages // pps
    nnew = kt_new.shape[2]
    nq = q_bd.shape[1]
    pt = page_table.reshape(-1).astype(I32)

    def pspec(p):
        return pl.BlockSpec((1, hd, page), lambda b, j, pt: (pt[b * n_pages + j * pps + p], 0, 0))

    bspec = lambda shape: pl.BlockSpec((1,) + shape, lambda b, j, pt: (b, 0, 0))
    grid_spec = pltpu.PrefetchScalarGridSpec(
        num_scalar_prefetch=1,
        grid=(db, nsteps),
        in_specs=[bspec((nq, hd)),
                  pl.BlockSpec((1, T_PAD, pps * page), lambda b, j, pt: (b, 0, j)),
                  bspec((T_PAD, nnew)), bspec((hd, nnew)), bspec((hd, nnew))]
                 + [pspec(p) for p in range(pps)] * 2,
        out_specs=bspec((T_PAD, hd)),
        scratch_shapes=[pltpu.VMEM((nq, LANES), F32), pltpu.VMEM((nq, LANES), F32), pltpu.VMEM((nq, hd), F32)],
    )
    return pl.pallas_call(
        functools.partial(_sample_attend_kernel, page=page, pps=pps),
        grid_spec=grid_spec,
        out_shape=jax.ShapeDtypeStruct((db, T_PAD, hd), F32),
        compiler_params=_cparams(("parallel", "arbitrary")),
        name="sample_attend",
    )(pt, q_bd, m_past, m_new, kt_new, vt_new, *([cache_kt] * pps), *([cache_vt] * pps))


def _retention_tables(c_real, c_pad):
    h = np.arange(N_HEADS_R, dtype=np.float64)
    log_g = np.log1p(-np.exp2(-5.0 - h))
    i = np.arange(c_pad, dtype=np.float64)
    diff = i[:, None] - i[None, :]
    live = (diff >= 0) & (i[:, None] < c_real) & (i[None, :] < c_real)
    inner = np.where(live[None], np.exp(np.maximum(diff, 0.0)[None] * log_g[:, None, None]), 0.0)
    q_dec = np.exp((i + 1.0)[None, :] * log_g[:, None])
    k_dec = np.where(i[None, :] < c_real, np.exp((c_real - 1.0 - i)[None, :] * log_g[:, None]), 0.0)
    c_dec = np.exp(c_real * log_g)
    f = lambda a: jnp.asarray(a, F32)
    return f(inner), f(q_dec[:, :, None]), f(k_dec[:, :, None]), [float(v) for v in c_dec]


def _retention_kernel(q_ref, k_ref, v_ref, g_ref, s0_ref, inner_ref, qdec_ref, kdec_ref,
                      o_ref, s_out_ref, s_scr, *, c_dec):
    j = pl.program_id(1)
    nj = pl.num_programs(1)

    @pl.when(j == 0)
    def _():
        s_scr[...] = s0_ref[0]

    for h in range(N_HEADS_R):
        q = q_ref[:, h * DK_R:(h + 1) * DK_R]
        k = k_ref[:, h * DK_R:(h + 1) * DK_R]
        v = v_ref[:, h * DV_R:(h + 1) * DV_R]
        s_prev = s_scr[h]
        a = _dot_nt(_mx(q), _mx(k)) * inner_ref[h]
        o = _dot(_mx(a), v) + _dot(_mx(q), _mx(s_prev)) * qdec_ref[h]
        kd = k * kdec_ref[h]
        s_scr[h] = s_prev * c_dec[h] + _dot(_mx(kd.T), v)
        mu = jnp.mean(o, axis=-1, keepdims=True)
        var = jnp.mean(jnp.square(o - mu), axis=-1, keepdims=True)
        gn = (o - mu) * lax.rsqrt(var + EPS)
        g = g_ref[:, h * DV_R:(h + 1) * DV_R]
        o_ref[:, h * DV_R:(h + 1) * DV_R] = (gn * (g * _sigmoid(g))).astype(o_ref.dtype)

    @pl.when(j == nj - 1)
    def _():
        s_out_ref[0] = s_scr[...]


def _retention(qr, kr, vr, gr, s0, c_real):
    b = s0.shape[0]
    c = RET_CHUNK
    n = qr.shape[0] // (b * c)
    inner, qdec, kdec, c_dec = _retention_tables(c_real, c)
    wr, wv = N_HEADS_R * DK_R, N_HEADS_R * DV_R
    rmap = lambda bi, j: (bi * n + j, 0)
    full3 = lambda shape: pl.BlockSpec(shape, lambda bi, j: (0, 0, 0))
    return pl.pallas_call(
        functools.partial(_retention_kernel, c_dec=c_dec),
        grid=(b, n),
        in_specs=[pl.BlockSpec((c, wr), rmap), pl.BlockSpec((c, wr), rmap), pl.BlockSpec((c, wv), rmap),
                  pl.BlockSpec((c, wv), rmap),
                  pl.BlockSpec((1, N_HEADS_R, DK_R, DV_R), lambda bi, j: (bi, 0, 0, 0)),
                  full3(inner.shape), full3(qdec.shape), full3(kdec.shape)],
        out_specs=[pl.BlockSpec((c, wv), rmap),
                   pl.BlockSpec((1, N_HEADS_R, DK_R, DV_R), lambda bi, j: (bi, 0, 0, 0))],
        out_shape=[jax.ShapeDtypeStruct((b * n * c, wv), MXU_DTYPE),
                   jax.ShapeDtypeStruct((b, N_HEADS_R, DK_R, DV_R), F32)],
        scratch_shapes=[pltpu.VMEM((N_HEADS_R, DK_R, DV_R), F32)],
        compiler_params=_cparams(("parallel", "arbitrary")),
        name="retention",
    )(qr, kr, vr, gr, s0, inner, qdec, kdec)


def _cross_kernel(q_ref, mk_ref, mv_ref, o_ref):
    scale = HEAD_DIM_M ** -0.5
    for h in range(N_HEADS_M):
        sl = slice(h * HEAD_DIM_M, (h + 1) * HEAD_DIM_M)
        s = _dot_nt(_mx(q_ref[:, sl]), _mx(mk_ref[0, :, sl])) * scale
        p = jnp.exp(s - jnp.max(s, axis=1, keepdims=True))
        p = p / jnp.sum(p, axis=1, keepdims=True)
        o_ref[:, sl] = _dot(_mx(p), _mx(mv_ref[0, :, sl])).astype(o_ref.dtype)


def _cross_attend(qm, mk, mv, tm, out_dtype):
    b, n_mem, hd = mk.shape
    nt = qm.shape[0] // (b * tm)
    return pl.pallas_call(
        _cross_kernel,
        grid=(b, nt),
        in_specs=[pl.BlockSpec((tm, hd), lambda bi, i: (bi * nt + i, 0)),
                  pl.BlockSpec((1, n_mem, hd), lambda bi, i: (bi, 0, 0)),
                  pl.BlockSpec((1, n_mem, hd), lambda bi, i: (bi, 0, 0))],
        out_specs=pl.BlockSpec((tm, hd), lambda bi, i: (bi * nt + i, 0)),
        out_shape=jax.ShapeDtypeStruct(qm.shape, out_dtype),
        compiler_params=_cparams(("parallel", "parallel")),
        name="cross_attend",
    )(qm, mk, mv)


def _memkv_kernel(x_ref, g_ref, w_ref, o_ref):
    o_ref[...] = _dot(_mx(_rms(x_ref[...], g_ref[...])), w_ref[...])


def _memory_kv(mem, gain, w):
    rows, d = mem.shape
    return pl.pallas_call(
        _memkv_kernel,
        out_shape=jax.ShapeDtypeStruct((rows, w.shape[1]), F32),
        compiler_params=pltpu.CompilerParams(vmem_limit_bytes=VMEM_LIMIT),
        name="memory_kv",
    )(mem, gain.reshape(1, d), _mx(w))


def _merge_kernel(x_ref, oa_ref, or_ref, om_ref, gates_ref, wpa_ref, wpb_ref, wpc_ref, wo_ref, g_ref, h_ref):
    d = x_ref.shape[1]
    gt = gates_ref[...]
    mixed = (_sigmoid(gt[:, :d]) * _dot(_mx(oa_ref[...]), wpa_ref[...])
             + _sigmoid(gt[:, d:2 * d]) * _dot(_mx(or_ref[...]), wpb_ref[...])
             + _sigmoid(gt[:, 2 * d:]) * _dot(_mx(om_ref[...]), wpc_ref[...]))
    z = _dot(_mx(mixed), wo_ref[...])
    h_ref[...] = x_ref[...] + _rms(z, g_ref[...])


def _merge(x, oa, o_r, om, gates, wpa, wpb, wpc, wo, gain, tm):
    rows, d = x.shape
    return pl.pallas_call(
        _merge_kernel,
        grid=(rows // tm,),
        in_specs=[_row_spec(tm, d), _row_spec(tm, oa.shape[1]), _row_spec(tm, o_r.shape[1]),
                  _row_spec(tm, om.shape[1]), _row_spec(tm, 3 * d),
                  _full_spec(wpa.shape), _full_spec(wpb.shape), _full_spec(wpc.shape), _full_spec(wo.shape),
                  _full_spec((1, d))],
        out_specs=_row_spec(tm, d),
        out_shape=jax.ShapeDtypeStruct((rows, d), F32),
        compiler_params=_cparams(("parallel",)),
        name="merge",
    )(x, oa, o_r, om, gates, wpa, wpb, wpc, wo, gain.reshape(1, d))


HALO = BF16_ROWS


def _ffn_kernel(h_ref, halo_ref, s0_ref, s1_ref, g1_ref, g2_ref, wu_ref, wg_ref, cw_ref, cb_ref, wd_ref,
                y_ref, utail_ref, x_scr, u_scr, *, tm, seq, keep):
    i = pl.program_id(0)
    h = h_ref[...]
    hn = _rms(h, g1_ref[...])
    x_scr[HALO:, :] = _mx(hn)
    x_scr[:HALO, :] = _mx(_rms(halo_ref[...], g1_ref[...]))
    xc = x_scr[...]
    u_scr[...] = _dot(xc, wu_ref[...])
    gate = _dot(xc[HALO:], wg_ref[...])
    cur = u_scr[HALO:, :]
    prev1 = u_scr[HALO - 1:HALO - 1 + tm, :]
    prev2 = u_scr[HALO - 2:HALO - 2 + tm, :]
    seq_loc = min(seq, tm)
    t = lax.rem(lax.broadcasted_iota(I32, (tm, 1), 0), seq_loc)
    t = jnp.where(lax.rem(i * tm, seq) == 0, t, CONV_W)
    st0, st1 = s0_ref[...], s1_ref[...]
    if st0.shape[0] != tm:
        st0, st1 = st0[0:1], st1[0:1]
    prev1 = jnp.where(t == 0, st1, prev1)
    prev2 = jnp.where(t == 0, st0, jnp.where(t == 1, st1, prev2))
    c = cb_ref[...] + prev2 * cw_ref[0:1, :] + prev1 * cw_ref[1:2, :] + cur * cw_ref[2:3, :]
    act = jax.nn.gelu(c, approximate=True) * gate
    ff = _dot(_mx(act), wd_ref[...])
    y_ref[...] = h + _rms(ff, g2_ref[...])
    utail_ref[...] = u_scr[HALO + tm - keep:, :]


def _conv_ffn(h, s0e, s1e, g1, g2, wu, wg, cw, cb, wd, tm, seq, keep):
    rows, d = h.shape
    f = wu.shape[1]
    nt = rows // tm
    hb = tm // HALO
    sr = s0e.shape[0]
    return pl.pallas_call(
        functools.partial(_ffn_kernel, tm=tm, seq=seq, keep=keep),
        grid=(nt,),
        in_specs=[_row_spec(tm, d),
                  pl.BlockSpec((HALO, d), lambda i: (jnp.maximum(i * hb - 1, 0), 0)),
                  _full_spec((sr, f)), _full_spec((sr, f)),
                  _full_spec((1, d)), _full_spec((1, d)),
                  _full_spec(wu.shape), _full_spec(wg.shape), _full_spec(cw.shape), _full_spec((1, f)),
                  _full_spec(wd.shape)],
        out_specs=[_row_spec(tm, d), _row_spec(keep, f)],
        out_shape=[jax.ShapeDtypeStruct((rows, d), F32), jax.ShapeDtypeStruct((nt * keep, f), F32)],
        scratch_shapes=[pltpu.VMEM((tm + HALO, d), MXU_DTYPE), pltpu.VMEM((tm + HALO, f), F32)],
        compiler_params=_cparams(("parallel",)),
        name="conv_ffn",
    )(h, h, s0e, s1e, g1.reshape(1, d), g2.reshape(1, d), wu, wg, cw, cb.reshape(1, f), wd)


def _layer_weights(l, w_in, w_proj_a, w_proj_b, w_proj_c, w_out, w_up, w_down):
    wts = _prep_in_weights(w_in[l])
    f = w_down.shape[1]
    wts.update(wpa=_mx(w_proj_a[l]), wpb=_mx(w_proj_b[l]), wpc=_mx(w_proj_c[l]), wo=_mx(w_out[l]),
               wu=_mx(w_up[l][:, :f]), wg=_mx(w_up[l][:, f:]), wd=_mx(w_down[l]))
    return wts


def _prompt_layer(x, mem, wts, norms, conv_w, conv_b, w_mem_kv, tiles):
    s, d = x.shape
    f = wts["wd"].shape[0]
    pos = jnp.arange(s)
    pr = _projections(x, norms["pre_mix"], pos, wts, tiles["proj"])
    topk = min(TOPK_MAX, s // 4)
    mask = _prompt_select(pr["qi_hm"], pr["wi_f"], pr["ki_b"], topk, tiles["sel_q"], SEL_WIDE)
    o_a = _prompt_attend(pr["qa_t"], pr["ka_hm"], pr["va_t"], mask, tiles["att_q"], tiles["att_k"])
    s0 = jnp.zeros((1, N_HEADS_R, DK_R, DV_R), F32)
    o_r, ret_new = _retention(pr["qr_f"], pr["kr_f"], pr["vr_b"], pr["gr_f"], s0, RET_CHUNK)
    kv = _memory_kv(mem, norms["mem"], w_mem_kv)
    wm = N_HEADS_M * HEAD_DIM_M
    mk, mv = kv[:, :wm], kv[:, wm:]
    o_m = _cross_attend(pr["qm_b"], mk[None], mv[None], tiles["cross"], MXU_DTYPE)
    h = _merge(x, o_a, o_r, o_m, pr["gates_f"], wts["wpa"], wts["wpb"], wts["wpc"], wts["wo"],
               norms["post_mix"], tiles["merge"])
    zst = jnp.zeros((SUBLANES, f), F32)
    y, utail = _conv_ffn(h, zst, zst, norms["pre_ffn"], norms["post_ffn"], wts["wu"], wts["wg"], conv_w, conv_b,
                         wts["wd"], tiles["ffn"], s, SUBLANES)
    conv_new = utail[-(CONV_W - 1):]
    return y, pr["ka_f"], pr["va_f"], pr["ki_f"], ret_new, conv_new, mk, mv


def _sample_layer(x, wts, norms, conv_w, conv_b, cache_k, cache_v, cache_kidx, mem_k, mem_v,
                  state_ret, state_conv, page_table):
    db, t, d = x.shape
    f = wts["wd"].shape[0]
    n_pages = page_table.shape[1]
    page = cache_k.shape[1]
    past = n_pages * page
    rows = db * T_PAD
    xp = jnp.pad(x, ((0, 0), (0, T_PAD - t), (0, 0))).reshape(rows, d)
    pos = jnp.tile(past + jnp.arange(T_PAD), db)
    pr = _projections(xp, norms["pre_mix"], pos, wts, rows)
    hd = N_HEADS_A * HEAD_DIM_A

    topk = min(TOPK_MAX, (past + t) // 4)
    i_past = _sample_index(page_table, pr["qi_hm"].astype(F32), pr["wi_f"], cache_kidx.transpose(0, 2, 1)).reshape(rows, past)
    m_past, m_new = _sample_select(i_past, pr["qi_hm"], pr["wi_f"], pr["ki_b"], t, topk)
    own = m_new[:, :rows].reshape(db, T_PAD, db, T_PAD)[jnp.arange(db), :, jnp.arange(db), :]
    m_new_own = jnp.pad(own, ((0, 0), (0, 0), (0, LANES - T_PAD)))
    q_rows = pr["qa_hm"].reshape(N_HEADS_A, db, T_PAD, HEAD_DIM_A)
    eye = jnp.eye(N_HEADS_A, dtype=MXU_DTYPE)
    q_bd = jnp.einsum("hbtd,hg->bhtgd", q_rows, eye).reshape(db, N_HEADS_A * T_PAD, hd)
    new_t = lambda a: jnp.pad(_mx(a).reshape(db, T_PAD, hd).transpose(0, 2, 1), ((0, 0), (0, 0), (0, LANES - T_PAD)))
    paged_t = lambda c: c.transpose(0, 2, 3, 1).reshape(c.shape[0], hd, page)
    o_a = _sample_attend(page_table, q_bd, m_past.reshape(db, T_PAD, past), m_new_own,
                         new_t(pr["ka_f"]), new_t(pr["va_f"]), paged_t(cache_k), paged_t(cache_v)).reshape(rows, hd)

    padc = lambda a: jnp.pad(a.reshape(db, T_PAD, -1), ((0, 0), (0, RET_CHUNK - T_PAD), (0, 0))).reshape(db * RET_CHUNK, -1)
    o_r, ret_new = _retention(padc(pr["qr_f"]), padc(pr["kr_f"]), padc(pr["vr_b"]), padc(pr["gr_f"]), state_ret, t)
    o_r = o_r.reshape(db, RET_CHUNK, -1)[:, :T_PAD].reshape(rows, -1)

    wm = N_HEADS_M * HEAD_DIM_M
    o_m = _cross_attend(pr["qm_b"].astype(F32), mem_k.reshape(db, -1, wm), mem_v.reshape(db, -1, wm), T_PAD, F32)

    h = _merge(xp, o_a, o_r, o_m, pr["gates_f"], wts["wpa"], wts["wpb"], wts["wpc"], wts["wo"],
               norms["post_mix"], rows)
    s0e = jnp.repeat(state_conv[:, 0], T_PAD, axis=0)
    s1e = jnp.repeat(state_conv[:, 1], T_PAD, axis=0)
    y, u_all = _conv_ffn(h, s0e, s1e, norms["pre_ffn"], norms["post_ffn"], wts["wu"], wts["wg"], conv_w, conv_b,
                         wts["wd"], rows, T_PAD, rows)
    ext = jnp.concatenate([state_conv.astype(F32), u_all.reshape(db, T_PAD, f)[:, :t]], axis=1)
    conv_new = ext[:, t:]
    unpad = lambda a: a.reshape(db, T_PAD, -1)[:, :t]
    return (unpad(y), unpad(pr["ka_f"]), unpad(pr["va_f"]), unpad(pr["ki_f"]), ret_new, conv_new)


PROMPT_TILES = dict(proj=512, sel_q=512, att_q=512, att_k=1024, cross=512, merge=512, ffn=256)


def kernel(x_prompt, x_sample, cache_k, cache_v, cache_kidx, cache_mem_k, cache_mem_v, state_ret, state_conv,
           page_table, mem_prompt, norm_pre_mix, norm_post_mix, norm_pre_ffn, norm_post_ffn, norm_mem,
           w_in, w_mem_kv, w_proj_a, w_proj_b, w_proj_c, w_out, w_up, conv_w, conv_b, w_down):
    bp, s, d = x_prompt.shape
    db, t, _ = x_sample.shape
    depth = w_in.shape[0]
    assert bp == 1 and t <= T_PAD and CONV_W - 1 <= t
    tiles = {k: min(v, s) for k, v in PROMPT_TILES.items()}
    yp, ys = x_prompt[0], x_sample
    outs = [[] for _ in range(12)]
    for l in range(depth):
        wts = _layer_weights(l, w_in, w_proj_a, w_proj_b, w_proj_c, w_out, w_up, w_down)
        norms = dict(pre_mix=norm_pre_mix[l], post_mix=norm_post_mix[l], pre_ffn=norm_pre_ffn[l],
                     post_ffn=norm_post_ffn[l], mem=norm_mem[l])
        yp, kp, vp, kip, rp, cp, mk, mv = _prompt_layer(yp, mem_prompt[0], wts, norms, conv_w[l], conv_b[l],
                                                        w_mem_kv[l], tiles)
        ys, ks, vs, kis, rs, cs = _sample_layer(ys, wts, norms, conv_w[l], conv_b[l], cache_k[l], cache_v[l],
                                                cache_kidx[l], cache_mem_k[l], cache_mem_v[l], state_ret[l],
                                                state_conv[l], page_table)
        n_mem = mk.shape[0]
        vals = (kp.reshape(1, s, N_HEADS_A, HEAD_DIM_A), vp.reshape(1, s, N_HEADS_A, HEAD_DIM_A),
                kip.reshape(1, s, IDX_DIM), rp, cp[None],
                mk.reshape(1, n_mem, N_HEADS_M, HEAD_DIM_M), mv.reshape(1, n_mem, N_HEADS_M, HEAD_DIM_M),
                ks.reshape(db, t, N_HEADS_A, HEAD_DIM_A), vs.reshape(db, t, N_HEADS_A, HEAD_DIM_A),
                kis, rs, cs)
        for o, v in zip(outs, vals):
            o.append(v)
    stacked = [jnp.stack(o) for o in outs]
    return (yp[None], ys, *stacked)
```

```python
import functools
import math

import numpy as np
import jax
import jax.numpy as jnp
from jax import lax
from jax.experimental import pallas as pl
from jax.experimental.pallas import tpu as pltpu

F32 = jnp.float32
I32 = jnp.int32
MXU_DTYPE = jnp.bfloat16

N_HEADS_A, HEAD_DIM_A = 8, 64
IDX_HEADS, IDX_DIM = 4, 64
TOPK_MAX = 256
N_HEADS_R, DK_R, DV_R = 4, 128, 256
RET_CHUNK = 128
N_HEADS_M, HEAD_DIM_M = 4, 128
CONV_W = 3
ROPE_THETA = 10000.0
EPS = 1e-6

LANES = 128
SUBLANES = 8
BF16_ROWS = 16
VMEM_LIMIT = 56 * 1024 * 1024
NEG_BIG = -1e30
F32_LOWEST = float(np.finfo(np.float32).min)
T_PAD = 8


def _cparams(sem):
    return pltpu.CompilerParams(dimension_semantics=sem, vmem_limit_bytes=VMEM_LIMIT)


def _dot(a, b):
    return jnp.dot(a, b, preferred_element_type=F32)


def _dot_nt(a, b):
    return lax.dot_general(a, b, (((1,), (1,)), ((), ())), preferred_element_type=F32)


def _mx(a):
    return a.astype(MXU_DTYPE)


def _rms(x, g):
    return x * lax.rsqrt(jnp.mean(x * x, axis=-1, keepdims=True) + EPS) * g


def _sigmoid(x):
    return 1.0 / (1.0 + jnp.exp(-x))


def _rope_tables(pos, d):
    half = d // 2
    inv = 1.0 / (ROPE_THETA ** (jnp.arange(half, dtype=F32) * 2.0 / d))
    ang = pos.astype(F32)[:, None] * inv[None, :]
    cos, sin = jnp.cos(ang), jnp.sin(ang)
    reps = LANES // d
    cos_t = jnp.tile(jnp.concatenate([cos, cos], axis=1), (1, reps))
    sin_t = jnp.tile(jnp.concatenate([-sin, sin], axis=1), (1, reps))
    return cos_t, sin_t


def _rope(y, cos, sin, d):
    w = y.shape[1]
    half = d // 2
    reps = w // LANES
    c = jnp.concatenate([cos] * reps, axis=1) if reps > 1 else cos
    s = jnp.concatenate([sin] * reps, axis=1) if reps > 1 else sin
    lane = lax.broadcasted_iota(I32, y.shape, 1)
    first = (lane & (d - 1)) < half
    rot = jnp.where(first, pltpu.roll(y, w - half, 1), pltpu.roll(y, half, 1))
    return y * c + rot * s


def _proj_a_kernel(x_ref, g_ref, cos_ref, sin_ref, wa_ref, wv_ref,
                   qa_hm, qa_t, ka_f, ka_hm, qi_hm, ki_f, ki_b, va_f, va_t, wi_f):
    xn = _mx(_rms(x_ref[...], g_ref[...]))
    wa = N_HEADS_A * HEAD_DIM_A
    y = _rope(_dot(xn, wa_ref[...]), cos_ref[...], sin_ref[...], HEAD_DIM_A)
    q = y[:, :wa] * (HEAD_DIM_A ** -0.5)
    k = y[:, wa:2 * wa]
    ka_f[...] = k
    qt = q.T
    for h in range(N_HEADS_A):
        sl = slice(h * HEAD_DIM_A, (h + 1) * HEAD_DIM_A)
        qa_hm[h] = _mx(q[:, sl])
        qa_t[h] = _mx(qt[sl, :])
        ka_hm[h] = _mx(k[:, sl])
    qi = y[:, 2 * wa:2 * wa + IDX_HEADS * IDX_DIM]
    for h in range(IDX_HEADS):
        qi_hm[h] = _mx(qi[:, h * IDX_DIM:(h + 1) * IDX_DIM])
    ki = y[:, 2 * wa + IDX_HEADS * IDX_DIM:2 * wa + IDX_HEADS * IDX_DIM + IDX_DIM]
    ki_f[...] = ki
    ki_b[...] = _mx(ki)
    z = _dot(xn, wv_ref[...])
    v = z[:, :wa]
    va_f[...] = v
    vt = v.T
    for h in range(N_HEADS_A):
        va_t[h] = _mx(vt[h * HEAD_DIM_A:(h + 1) * HEAD_DIM_A, :])
    wi_f[...] = z[:, wa:wa + LANES]


def _proj_b_kernel(x_ref, g_ref, cos_ref, sin_ref, wr_ref, wvg_ref, qr_f, kr_f, vr_b, gr_f):
    xn = _mx(_rms(x_ref[...], g_ref[...]))
    wr = N_HEADS_R * DK_R
    y = _rope(_dot(xn, wr_ref[...]), cos_ref[...], sin_ref[...], DK_R)
    qr_f[...] = y[:, :wr]
    kr_f[...] = y[:, wr:] * (DK_R ** -0.5)
    z = _dot(xn, wvg_ref[...])
    wv = N_HEADS_R * DV_R
    vr_b[...] = _mx(z[:, :wv])
    gr_f[...] = z[:, wv:]


def _proj_c_kernel(x_ref, g_ref, wc_ref, qm_b, gates_f):
    xn = _mx(_rms(x_ref[...], g_ref[...]))
    z = _dot(xn, wc_ref[...])
    wm = N_HEADS_M * HEAD_DIM_M
    qm_b[...] = _mx(z[:, :wm])
    gates_f[...] = z[:, wm:]


def _row_spec(tm, w):
    return pl.BlockSpec((tm, w), lambda i: (i, 0))


def _full_spec(shape):
    nd = len(shape)
    return pl.BlockSpec(shape, lambda i: (0,) * nd)


def _hm_spec(nh, tm, d):
    return pl.BlockSpec((nh, tm, d), lambda i: (0, i, 0))


def _projections(x, gain, pos, wts, tm):
    rows, d = x.shape
    grid = (rows // tm,)
    wa = N_HEADS_A * HEAD_DIM_A
    cos64, sin64 = _rope_tables(pos, HEAD_DIM_A)
    cos128, sin128 = _rope_tables(pos, DK_R)
    g2 = gain.reshape(1, d)
    sds = jax.ShapeDtypeStruct
    t_spec = pl.BlockSpec((N_HEADS_A, HEAD_DIM_A, tm), lambda i: (0, 0, i))
    outs_a = pl.pallas_call(
        _proj_a_kernel,
        grid=grid,
        in_specs=[_row_spec(tm, d), _full_spec((1, d)), _row_spec(tm, LANES), _row_spec(tm, LANES),
                  _full_spec(wts["wa"].shape), _full_spec(wts["wv"].shape)],
        out_specs=[_hm_spec(N_HEADS_A, tm, HEAD_DIM_A), t_spec, _row_spec(tm, wa),
                   _hm_spec(N_HEADS_A, tm, HEAD_DIM_A),
                   _hm_spec(IDX_HEADS, tm, IDX_DIM), _row_spec(tm, IDX_DIM), _row_spec(tm, IDX_DIM),
                   _row_spec(tm, wa), t_spec, _row_spec(tm, LANES)],
        out_shape=[sds((N_HEADS_A, rows, HEAD_DIM_A), MXU_DTYPE), sds((N_HEADS_A, HEAD_DIM_A, rows), MXU_DTYPE),
                   sds((rows, wa), F32),
                   sds((N_HEADS_A, rows, HEAD_DIM_A), MXU_DTYPE), sds((IDX_HEADS, rows, IDX_DIM), MXU_DTYPE),
                   sds((rows, IDX_DIM), F32), sds((rows, IDX_DIM), MXU_DTYPE),
                   sds((rows, wa), F32), sds((N_HEADS_A, HEAD_DIM_A, rows), MXU_DTYPE), sds((rows, LANES), F32)],
        compiler_params=_cparams(("parallel",)),
        name="proj_a",
    )(x, g2, cos64, sin64, wts["wa"], wts["wv"])
    names_a = ("qa_hm", "qa_t", "ka_f", "ka_hm", "qi_hm", "ki_f", "ki_b", "va_f", "va_t", "wi_f")
    wr, wv = N_HEADS_R * DK_R, N_HEADS_R * DV_R
    outs_b = pl.pallas_call(
        _proj_b_kernel,
        grid=grid,
        in_specs=[_row_spec(tm, d), _full_spec((1, d)), _row_spec(tm, LANES), _row_spec(tm, LANES),
                  _full_spec(wts["wr"].shape), _full_spec(wts["wvg"].shape)],
        out_specs=[_row_spec(tm, wr), _row_spec(tm, wr), _row_spec(tm, wv), _row_spec(tm, wv)],
        out_shape=[sds((rows, wr), F32), sds((rows, wr), F32), sds((rows, wv), MXU_DTYPE), sds((rows, wv), F32)],
        compiler_params=_cparams(("parallel",)),
        name="proj_b",
    )(x, g2, cos128, sin128, wts["wr"], wts["wvg"])
    names_b = ("qr_f", "kr_f", "vr_b", "gr_f")
    wm = N_HEADS_M * HEAD_DIM_M
    outs_c = pl.pallas_call(
        _proj_c_kernel,
        grid=grid,
        in_specs=[_row_spec(tm, d), _full_spec((1, d)), _full_spec(wts["wc"].shape)],
        out_specs=[_row_spec(tm, wm), _row_spec(tm, 3 * d)],
        out_shape=[sds((rows, wm), MXU_DTYPE), sds((rows, 3 * d), F32)],
        compiler_params=_cparams(("parallel",)),
        name="proj_c",
    )(x, g2, wts["wc"])
    names_c = ("qm_b", "gates_f")
    out = dict(zip(names_a, outs_a))
    out.update(zip(names_b, outs_b))
    out.update(zip(names_c, outs_c))
    return out


def _prep_in_weights(w_in):
    d = w_in.shape[0]
    wa = N_HEADS_A * HEAD_DIM_A
    widths = (wa, wa, wa, IDX_HEADS * IDX_DIM, IDX_DIM, IDX_HEADS,
              N_HEADS_R * DK_R, N_HEADS_R * DK_R, N_HEADS_R * DV_R, N_HEADS_R * DV_R,
              N_HEADS_M * HEAD_DIM_M, 3 * d)
    offs = np.concatenate([[0], np.cumsum(widths)])
    seg = [w_in[:, int(offs[i]):int(offs[i + 1])] for i in range(len(widths))]
    q_a, k_a, v_a, q_i, k_i, w_i, q_r, k_r, v_r, g_r, q_m, gates = seg
    zpad = lambda n: jnp.zeros((d, n), w_in.dtype)
    return {
        "wa": _mx(jnp.concatenate([q_a, k_a, q_i, k_i, zpad(LANES - IDX_DIM)], axis=1)),
        "wv": _mx(jnp.concatenate([v_a, w_i, zpad(LANES - IDX_HEADS)], axis=1)),
        "wr": _mx(jnp.concatenate([q_r, k_r], axis=1)),
        "wvg": _mx(jnp.concatenate([v_r, g_r], axis=1)),
        "wc": _mx(jnp.concatenate([q_m, gates], axis=1)),
    }


SEL_ROWS = 128
SEL_WIDE = 512
SWEEP_UNROLL = 2
SEL_GROUPS = 2 * LANES
NO_LIMIT = 2 ** 30
MIN_NORMAL_KEY = 0x00800000
SEARCH_INTERP_STEPS = 8
SEARCH_CAP = SEARCH_INTERP_STEPS + 33


def _key_to_f32(key):
    bits = jnp.where(key >= 0, key, key ^ jnp.int32(0x7FFFFFFF))
    return pltpu.bitcast(bits, F32)


def _f32_to_key(f):
    bits = pltpu.bitcast(f, I32)
    return jnp.where(f == 0.0, 0, jnp.where(bits >= 0, bits, bits ^ jnp.int32(0x7FFFFFFF)))


def _sweep(segments, rs, init, fn):
    acc = init
    base = 0
    for ref, n_wide in segments:
        def body(c, a, ref=ref, base=base):
            off = c * SEL_WIDE
            for k in range(SEL_WIDE // LANES):
                x = ref[rs, pl.ds(pl.multiple_of(off + k * LANES, LANES), LANES)]
                a = fn(a, x, base + off + k * LANES, k)
            return a
        def multi(c, a, body=body):
            for u in range(SWEEP_UNROLL):
                a = body(SWEEP_UNROLL * c + u, a)
            return a
        full = n_wide // SWEEP_UNROLL
        acc = lax.fori_loop(0, full, multi, acc)
        acc = lax.fori_loop(SWEEP_UNROLL * full, n_wide, body, acc)
        base = base + n_wide * SEL_WIDE
    return acc


def _count(segments, rows, make_pred, wanted=None):
    outs = []
    for g in range(rows // SEL_ROWS):
        def group_count(_, g=g):
            rs = slice(g * SEL_ROWS, (g + 1) * SEL_ROWS)
            pred = make_pred(rs)
            acc = _sweep(segments, rs, jnp.zeros((SEL_ROWS, LANES), F32),
                         lambda a, x, idx0, k: a + jnp.where(pred(x, idx0), 1.0, 0.0))
            return jnp.sum(acc.T, axis=0, keepdims=True)
        if wanted is None:
            outs.append(group_count(0))
        else:
            outs.append(lax.cond(wanted[g] > 0, group_count, lambda _: jnp.zeros((1, SEL_ROWS), F32), 0))
    return jnp.concatenate(outs, axis=1) if len(outs) > 1 else outs[0]


def _col_to_row(col):
    return jnp.broadcast_to(col, (col.shape[0], LANES)).T[0:1]


def _row_to_lanes(vec):
    return jnp.broadcast_to(vec, (LANES, vec.shape[1])).T


def _search(count_fn, lo, hi, c_lo, c_hi, target, alive):
    def unfinished(lo, hi, c_lo):
        return jnp.logical_and(alive, jnp.logical_and(c_lo > target, lo + 1 < hi))

    def any_row(flag):
        return jnp.max(jnp.where(flag, 1, 0).astype(I32))

    def cond(carry):
        return jnp.logical_and(carry[0] < SEARCH_CAP, carry[1] > 0)

    def body(carry):
        it, _, lo, hi, c_lo, c_hi, w_lo, w_hi, last = carry
        act = unfinished(lo, hi, c_lo)
        bis = (lo >> 1) + (hi >> 1) + (lo & hi & 1)
        width = hi - lo
        a = (c_lo - target + 0.5) * w_lo
        b = (target - 0.5 - c_hi) * w_hi
        frac = a / jnp.maximum(a + b, 1e-6)
        step = (frac * width.astype(F32)).astype(I32)
        itp = lo + jnp.clip(step, 1, jnp.maximum(width - 1, 1))
        use_itp = jnp.logical_and((lo ^ hi) >= 0, (jnp.zeros_like(lo) + it) < SEARCH_INTERP_STEPS)
        v = jnp.where(use_itp, itp, bis)
        n_groups = lo.shape[1] // SEL_ROWS
        cnt = count_fn(v, [any_row(act[:, g * SEL_ROWS:(g + 1) * SEL_ROWS]) for g in range(n_groups)])
        up = jnp.logical_and(act, cnt >= target)
        dn = jnp.logical_and(act, cnt < target)
        lo = jnp.where(up, v, lo)
        c_lo = jnp.where(up, cnt, c_lo)
        hi = jnp.where(dn, v, hi)
        c_hi = jnp.where(dn, cnt, c_hi)
        w_hi = jnp.where(up, jnp.where(last == 1, w_hi * 0.5, 1.0), jnp.where(dn, 1.0, w_hi))
        w_lo = jnp.where(dn, jnp.where(last == -1, w_lo * 0.5, 1.0), jnp.where(up, 1.0, w_lo))
        last = jnp.where(up, 1, jnp.where(dn, -1, last))
        return it + 1, any_row(unfinished(lo, hi, c_lo)), lo, hi, c_lo, c_hi, w_lo, w_hi, last

    go = any_row(unfinished(lo, hi, c_lo))
    one = jnp.ones(lo.shape, F32)
    out = lax.while_loop(cond, body, (jnp.int32(0), go, lo, hi, c_lo, c_hi, one, one, jnp.zeros(lo.shape, I32)))
    return out[2], out[4], out[5]


def _select_threshold(segments, rows, topk, alive=None):
    assert topk <= SEL_GROUPS
    kf = jnp.full((1, rows), float(topk), F32)
    if alive is None:
        alive = jnp.full((1, rows), True)

    los, his = [], []
    for g in range(rows // SEL_ROWS):
        rs = slice(g * SEL_ROWS, (g + 1) * SEL_ROWS)
        ninf = jnp.full((SEL_ROWS, LANES), -jnp.inf, F32)
        ga, gb = _sweep(segments, rs, (ninf, ninf),
                        lambda a, x, idx0, k: ((jnp.maximum(a[0], x), a[1]) if k % 2 == 0
                                               else (a[0], jnp.maximum(a[1], x))))
        los.append(jnp.min(jnp.minimum(ga, gb), axis=1, keepdims=True))
        his.append(jnp.max(jnp.maximum(ga, gb), axis=1, keepdims=True))
    cat = lambda xs: _col_to_row(jnp.concatenate(xs, axis=0) if len(xs) > 1 else xs[0])
    lo = _f32_to_key(jnp.maximum(cat(los), F32_LOWEST))
    hi = _f32_to_key(cat(his)) + 1

    def count_ge_f(thr, wanted=None):
        thr_l = _row_to_lanes(thr)

        def make_pred(rs):
            t = thr_l[rs]
            return lambda x, idx0: x >= t
        return _count(segments, rows, make_pred, wanted)

    count_ge = lambda v, wanted=None: count_ge_f(_key_to_f32(v), wanted)
    c_lo = count_ge(lo)
    c_hi = jnp.zeros((1, rows), F32)
    def zero_probes(state):
        lo, hi, c_lo, c_hi = state
        for probe in (0, MIN_NORMAL_KEY):
            v = jnp.full((1, rows), probe, I32)
            cnt = count_ge_f(jnp.full((1, rows), np.int32(probe).view(np.float32), F32))
            inside = jnp.logical_and(lo < v, v < hi)
            up = jnp.logical_and(inside, cnt >= kf)
            dn = jnp.logical_and(inside, cnt < kf)
            lo, c_lo = jnp.where(up, v, lo), jnp.where(up, cnt, c_lo)
            hi, c_hi = jnp.where(dn, v, hi), jnp.where(dn, cnt, c_hi)
        return lo, hi, c_lo, c_hi

    straddles = jnp.logical_or(jnp.logical_and(lo < 0, hi > 0),
                               jnp.logical_and(lo < MIN_NORMAL_KEY, hi > MIN_NORMAL_KEY))
    lo, hi, c_lo, c_hi = lax.cond(jnp.max(jnp.where(straddles, 1, 0).astype(I32)) > 0,
                                  zero_probes, lambda s: s, (lo, hi, c_lo, c_hi))
    at_zero = jnp.logical_and(lo == 0, hi == MIN_NORMAL_KEY)
    lo, c_lo, c_hi = _search(count_ge, lo, hi, c_lo, c_hi, kf, jnp.logical_and(alive, jnp.logical_not(at_zero)))
    tau = _key_to_f32(lo)

    need = jnp.logical_and(alive, c_lo > kf)
    room = jnp.where(need, kf - c_hi, float(NO_LIMIT))
    return _row_to_lanes(tau), _row_to_lanes(room)


def _write_mask(segments_out, rows, tau, room):
    r_io = lax.broadcasted_iota(I32, (LANES, 2 * LANES), 0)
    c_io = lax.broadcasted_iota(I32, (LANES, 2 * LANES), 1)
    tri = jnp.where(jnp.logical_or(r_io <= c_io, c_io >= LANES), 1.0, 0.0).astype(MXU_DTYPE)
    for g in range(rows // SEL_ROWS):
        rs = slice(g * SEL_ROWS, (g + 1) * SEL_ROWS)
        t, rm = tau[rs], room[rs]
        seen = jnp.zeros((SEL_ROWS, LANES), F32)
        for src, dst, n_wide, n_total in segments_out:
            def body(c, seen, src=src, dst=dst):
                off = c * SEL_WIDE
                nsub = SEL_WIDE // LANES
                dss = [pl.ds(pl.multiple_of(off + k * LANES, LANES), LANES) for k in range(nsub)]
                xs = [src[rs, ds] for ds in dss]
                eqs = [jnp.where(x == t, 1.0, 0.0) for x in xs]
                run = jnp.dot(jnp.concatenate(eqs, axis=0).astype(MXU_DTYPE), tri, preferred_element_type=F32)
                for k in range(nsub):
                    run_k = run[k * SEL_ROWS:(k + 1) * SEL_ROWS]
                    before = seen + run_k[:, :LANES] - eqs[k]
                    keep_eq = jnp.where(before < rm, eqs[k], 0.0)
                    dst[rs, dss[k]] = jnp.where(xs[k] > t, 1.0, keep_eq).astype(dst.dtype)
                    seen = seen + run_k[:, LANES:]
                return seen

            pairs = n_wide // 2
            seen = lax.fori_loop(0, pairs, lambda c, sn, body=body: body(2 * c + 1, body(2 * c, sn)), seen)
            seen = lax.fori_loop(2 * pairs, n_wide, body, seen)

            def zbody(c, carry, dst=dst):
                off = pl.multiple_of(c * SEL_WIDE, SEL_WIDE)
                dst[rs, pl.ds(off, SEL_WIDE)] = jnp.zeros((SEL_ROWS, SEL_WIDE), dst.dtype)
                return carry

            lax.fori_loop(n_wide, n_total, zbody, 0)


def _index_scores(qi_ref, w, kb, transposed_keys=False):
    acc = None
    for h in range(IDX_HEADS):
        s = _dot(_mx(qi_ref[h]), kb) if transposed_keys else _dot_nt(_mx(qi_ref[h]), kb)
        t = w[:, h:h + 1] * jnp.maximum(s, 0.0)
        acc = t if acc is None else acc + t
    return acc


def _causal_pairs(nq, tq, tk):
    pairs = [(i, j) for i in range(nq) for j in range((i * tq + tq - 1) // tk + 1)]
    return jnp.asarray([p[0] for p in pairs], I32), jnp.asarray([p[1] for p in pairs], I32)


def _prompt_select_kernel(qb_ref, kb_ref, qi_ref, wi_ref, kidx_ref, mask_ref, i_scr, *, tq, tk, topk):
    p = pl.program_id(0)
    i, j = qb_ref[p], kb_ref[p]
    q_lo = i * tq
    n_wide = (q_lo + tq - 1) // tk + 1

    acc = _index_scores(qi_ref, wi_ref[...], kidx_ref[...])
    cols = pl.ds(pl.multiple_of(j * tk, tk), tk)
    below_diagonal = (j + 1) * tk <= q_lo + 1

    @pl.when(below_diagonal)
    def _():
        i_scr[:, cols] = acc

    @pl.when(jnp.logical_not(below_diagonal))
    def _():
        qpos = q_lo + lax.broadcasted_iota(I32, acc.shape, 0)
        kpos = j * tk + lax.broadcasted_iota(I32, acc.shape, 1)
        i_scr[:, cols] = jnp.where(kpos <= qpos, acc, -jnp.inf)

    @pl.when(j == n_wide - 1)
    def _():
        seg = [(i_scr, n_wide)]
        tau, room = _select_threshold(seg, tq, topk)
        _write_mask([(i_scr, mask_ref, n_wide, mask_ref.shape[1] // tk)], tq, tau, room)


def _prompt_select(qi_hm, wi_f, ki_b, topk, tq, tk):
    s = ki_b.shape[0]
    assert tk == SEL_WIDE and s % tk == 0 and s % tq == 0 and tq % SEL_ROWS == 0
    qb, kb = _causal_pairs(s // tq, tq, tk)
    grid_spec = pltpu.PrefetchScalarGridSpec(
        num_scalar_prefetch=2,
        grid=(qb.shape[0],),
        in_specs=[pl.BlockSpec((IDX_HEADS, tq, IDX_DIM), lambda p, qb, kb: (0, qb[p], 0)),
                  pl.BlockSpec((tq, LANES), lambda p, qb, kb: (qb[p], 0)),
                  pl.BlockSpec((tk, IDX_DIM), lambda p, qb, kb: (kb[p], 0))],
        out_specs=pl.BlockSpec((tq, s), lambda p, qb, kb: (qb[p], 0), pipeline_mode=pl.Buffered(1)),
        scratch_shapes=[pltpu.VMEM((tq, s), F32)],
    )
    return pl.pallas_call(
        functools.partial(_prompt_select_kernel, tq=tq, tk=tk, topk=topk),
        grid_spec=grid_spec,
        out_shape=jax.ShapeDtypeStruct((s, s), MXU_DTYPE),
        compiler_params=_cparams(("arbitrary",)),
        name="prompt_select",
    )(qb, kb, qi_hm, wi_f, ki_b)


def _spread(a, n):
    if n <= LANES:
        return a[:, :n]
    return jnp.concatenate([a] * (n // LANES), axis=1)


def _prompt_attend_kernel(qb_ref, kb_ref, qt_ref, k_ref, vt_ref, mask_ref, o_ref, m_scr, l_scr, acc_scr, *, tq, tk):
    p_id = pl.program_id(0)
    i, j = qb_ref[p_id], kb_ref[p_id]

    @pl.when(j == 0)
    def _():
        m_scr[...] = jnp.full(m_scr.shape, NEG_BIG, F32)
        l_scr[...] = jnp.zeros(l_scr.shape, F32)
        acc_scr[...] = jnp.zeros(acc_scr.shape, F32)

    bias = (1.0 - mask_ref[...].astype(F32).T) * NEG_BIG
    scores = [_dot(k_ref[h], qt_ref[h]) + bias for h in range(N_HEADS_A)]
    for h, s in enumerate(scores):
        m_prev = m_scr[h]
        m_new = jnp.maximum(m_prev, jnp.max(s, axis=0, keepdims=True))
        alpha = jnp.exp(m_prev - m_new)
        p = jnp.exp(s - m_new[0:1])
        l_scr[h] = alpha * l_scr[h] + jnp.sum(p, axis=0, keepdims=True)
        acc_scr[h] = alpha[0:1] * acc_scr[h] + _dot(vt_ref[h], _mx(p))
        m_scr[h] = m_new

    @pl.when(j == (i * tq + tq - 1) // tk)
    def _():
        ot = jnp.concatenate([acc_scr[h] / l_scr[h][0:1] for h in range(N_HEADS_A)], axis=0)
        o_ref[...] = ot.T.astype(o_ref.dtype)


def _prompt_attend(qa_t, ka_hm, va_t, mask, tq, tk):
    nh, dh, s = qa_t.shape
    qb, kb = _causal_pairs(s // tq, tq, tk)
    grid_spec = pltpu.PrefetchScalarGridSpec(
        num_scalar_prefetch=2,
        grid=(qb.shape[0],),
        in_specs=[pl.BlockSpec((nh, dh, tq), lambda p, qb, kb: (0, 0, qb[p])),
                  pl.BlockSpec((nh, tk, dh), lambda p, qb, kb: (0, kb[p], 0)),
                  pl.BlockSpec((nh, dh, tk), lambda p, qb, kb: (0, 0, kb[p])),
                  pl.BlockSpec((tq, tk), lambda p, qb, kb: (qb[p], kb[p]))],
        out_specs=pl.BlockSpec((tq, nh * dh), lambda p, qb, kb: (qb[p], 0)),
        scratch_shapes=[pltpu.VMEM((nh, SUBLANES, tq), F32), pltpu.VMEM((nh, SUBLANES, tq), F32),
                        pltpu.VMEM((nh, dh, tq), F32)],
    )
    return pl.pallas_call(
        functools.partial(_prompt_attend_kernel, tq=tq, tk=tk),
        grid_spec=grid_spec,
        out_shape=jax.ShapeDtypeStruct((s, nh * dh), MXU_DTYPE),
        compiler_params=_cparams(("arbitrary",)),
        name="prompt_attend",
    )(qb, kb, qa_t, ka_hm, va_t, mask)


PAGES_PER_STEP = 32
INDEX_PAGES_PER_STEP = 32


def _sample_index_kernel(pt_ref, qi_ref, wi_ref, *refs, page):
    del pt_ref
    pages, out_ref = refs[:-1], refs[-1]
    w = wi_ref[0]
    for p, kref in enumerate(pages):
        out_ref[0, :, p * page:(p + 1) * page] = _index_scores(qi_ref, w, _mx(kref[0]), transposed_keys=True)


def _sample_index(page_table, qi_hm, wi_f, cache_kidx_t):
    db, n_pages = page_table.shape
    _, idim, page = cache_kidx_t.shape
    pps = math.gcd(INDEX_PAGES_PER_STEP, n_pages)
    nsteps = n_pages // pps
    pt = page_table.reshape(-1).astype(I32)

    def kspec(p):
        return pl.BlockSpec((1, idim, page), lambda b, j, pt: (pt[b * n_pages + j * pps + p], 0, 0))

    grid_spec = pltpu.PrefetchScalarGridSpec(
        num_scalar_prefetch=1,
        grid=(db, nsteps),
        in_specs=[pl.BlockSpec((IDX_HEADS, T_PAD, idim), lambda b, j, pt: (0, b, 0)),
                  pl.BlockSpec((1, T_PAD, LANES), lambda b, j, pt: (b, 0, 0))]
                 + [kspec(p) for p in range(pps)],
        out_specs=pl.BlockSpec((1, T_PAD, pps * page), lambda b, j, pt: (b, 0, j)),
    )
    return pl.pallas_call(
        functools.partial(_sample_index_kernel, page=page),
        grid_spec=grid_spec,
        out_shape=jax.ShapeDtypeStruct((db, T_PAD, n_pages * page), F32),
        compiler_params=_cparams(("parallel", "arbitrary")),
        name="sample_index",
    )(pt, qi_hm, wi_f.reshape(db, T_PAD, LANES), *([cache_kidx_t] * pps))


def _sample_select_kernel(ipast_ref, qi_ref, wi_ref, kin_ref, mpast_ref, mnew_ref, inew_scr, *, t_real, topk):
    rows = ipast_ref.shape[0]
    acc = _index_scores(qi_ref, wi_ref[...], kin_ref[...])
    r = lax.broadcasted_iota(I32, acc.shape, 0)
    c = lax.broadcasted_iota(I32, acc.shape, 1)
    same = (r // T_PAD) == (c // T_PAD)
    tq, tc = r % T_PAD, c % T_PAD
    ok = jnp.logical_and(same, jnp.logical_and(tc <= tq, tc < t_real))
    inew_scr[...] = jnp.full(inew_scr.shape, -jnp.inf, F32)
    inew_scr[:, :rows] = jnp.where(ok, acc, -jnp.inf)
    n_past = ipast_ref.shape[1] // SEL_WIDE
    n_new = inew_scr.shape[1] // SEL_WIDE
    alive = lax.rem(lax.broadcasted_iota(I32, (1, rows), 1), T_PAD) < t_real
    tau, room = _select_threshold([(ipast_ref, n_past), (inew_scr, n_new)], rows, topk, alive)
    _write_mask([(ipast_ref, mpast_ref, n_past, n_past), (inew_scr, mnew_ref, n_new, n_new)], rows, tau, room)


def _sample_select(i_past, qi_hm, wi_f, ki_b, t_real, topk):
    rows, past = i_past.shape
    assert past % SEL_WIDE == 0 and rows % SEL_ROWS == 0
    wnew = -(-rows // SEL_WIDE) * SEL_WIDE
    return pl.pallas_call(
        functools.partial(_sample_select_kernel, t_real=t_real, topk=topk),
        out_shape=[jax.ShapeDtypeStruct((rows, past), F32), jax.ShapeDtypeStruct((rows, wnew), F32)],
        scratch_shapes=[pltpu.VMEM((rows, wnew), F32)],
        compiler_params=pltpu.CompilerParams(vmem_limit_bytes=VMEM_LIMIT),
        name="sample_select",
    )(i_past, qi_hm, wi_f, ki_b)


def _sample_attend_kernel(pt_ref, q_ref, mp_ref, mn_ref, kn_ref, vn_ref, *refs, page, pps):
    del pt_ref
    kpages, vpages = refs[:pps], refs[pps:2 * pps]
    o_ref, m_scr, l_scr, acc_scr = refs[2 * pps:]
    j = pl.program_id(1)
    nj = pl.num_programs(1)
    q = q_ref[0]
    nh = q.shape[0] // T_PAD

    def scores(m_t, kt):
        keep = jnp.concatenate([m_t] * nh, axis=0) > 0
        return jnp.where(keep, _dot(q, kt), NEG_BIG)

    def update(s_list, vt_list):
        m_prev = m_scr[...]
        m_blk = functools.reduce(jnp.maximum, [jnp.max(s, axis=1, keepdims=True) for s in s_list])
        m_new = jnp.maximum(m_prev, m_blk)
        alpha = jnp.exp(m_prev - m_new)
        l_new = alpha * l_scr[...]
        acc = _spread(alpha, acc_scr.shape[1]) * acc_scr[...]
        for s, vt in zip(s_list, vt_list):
            p = jnp.exp(s - _spread(m_new, s.shape[1]))
            l_new = l_new + jnp.sum(p, axis=1, keepdims=True)
            acc = acc + _dot_nt(_mx(p), vt)
        m_scr[...], l_scr[...], acc_scr[...] = m_new, l_new, acc

    @pl.when(j == 0)
    def _():
        m_scr[...] = jnp.full(m_scr.shape, NEG_BIG, F32)
        l_scr[...] = jnp.zeros(l_scr.shape, F32)
        acc_scr[...] = jnp.zeros(acc_scr.shape, F32)

    update([scores(mp_ref[0, :, p * page:(p + 1) * page], _mx(kpages[p][0])) for p in range(pps)],
           [_mx(vpages[p][0]) for p in range(pps)])

    @pl.when(j == nj - 1)
    def _():
        update([scores(mn_ref[0], kn_ref[0])], [vn_ref[0]])
        full = acc_scr[...] / _spread(l_scr[...], acc_scr.shape[1])
        lane = lax.broadcasted_iota(I32, (T_PAD, full.shape[1]), 1)
        out = jnp.zeros((T_PAD, full.shape[1]), F32)
        for h in range(nh):
            out = out + jnp.where((lane // HEAD_DIM_A) == h, full[h * T_PAD:(h + 1) * T_PAD], 0.0)
        o_ref[0] = out


def _sample_attend(page_table, q_bd, m_past, m_new, kt_new, vt_new, cache_kt, cache_vt):
    db, n_pages = page_table.shape
    _, hd, page = cache_kt.shape
    pps = math.gcd(PAGES_PER_STEP, n_pages)
    nsteps = n_pages // pps
    nnew = kt_new.shape[2]
    nq = q_bd.shape[1]
    pt = page_table.reshape(-1).astype(I32)

    def pspec(p):
        return pl.BlockSpec((1, hd, page), lambda b, j, pt: (pt[b * n_pages + j * pps + p], 0, 0))

    bspec = lambda shape: pl.BlockSpec((1,) + shape, lambda b, j, pt: (b, 0, 0))
    grid_spec = pltpu.PrefetchScalarGridSpec(
        num_scalar_prefetch=1,
        grid=(db, nsteps),
        in_specs=[bspec((nq, hd)),
                  pl.BlockSpec((1, T_PAD, pps * page), lambda b, j, pt: (b, 0, j)),
                  bspec((T_PAD, nnew)), bspec((hd, nnew)), bspec((hd, nnew))]
                 + [pspec(p) for p in range(pps)] * 2,
        out_specs=bspec((T_PAD, hd)),
        scratch_shapes=[pltpu.VMEM((nq, LANES), F32), pltpu.VMEM((nq, LANES), F32), pltpu.VMEM((nq, hd), F32)],
    )
    return pl.pallas_call(
        functools.partial(_sample_attend_kernel, page=page, pps=pps),
        grid_spec=grid_spec,
        out_shape=jax.ShapeDtypeStruct((db, T_PAD, hd), F32),
        compiler_params=_cparams(("parallel", "arbitrary")),
        name="sample_attend",
    )(pt, q_bd, m_past, m_new, kt_new, vt_new, *([cache_kt] * pps), *([cache_vt] * pps))


def _retention_tables(c_real, c_pad):
    h = np.arange(N_HEADS_R, dtype=np.float64)
    log_g = np.log1p(-np.exp2(-5.0 - h))
    i = np.arange(c_pad, dtype=np.float64)
    diff = i[:, None] - i[None, :]
    live = (diff >= 0) & (i[:, None] < c_real) & (i[None, :] < c_real)
    inner = np.where(live[None], np.exp(np.maximum(diff, 0.0)[None] * log_g[:, None, None]), 0.0)
    q_dec = np.exp((i + 1.0)[None, :] * log_g[:, None])
    k_dec = np.where(i[None, :] < c_real, np.exp((c_real - 1.0 - i)[None, :] * log_g[:, None]), 0.0)
    c_dec = np.exp(c_real * log_g)
    f = lambda a: jnp.asarray(a, F32)
    return f(inner), f(q_dec[:, :, None]), f(k_dec[:, :, None]), [float(v) for v in c_dec]


def _retention_kernel(q_ref, k_ref, v_ref, g_ref, s0_ref, inner_ref, qdec_ref, kdec_ref,
                      o_ref, s_out_ref, s_scr, *, c_dec):
    j = pl.program_id(1)
    nj = pl.num_programs(1)

    @pl.when(j == 0)
    def _():
        s_scr[...] = s0_ref[0]

    for h in range(N_HEADS_R):
        q = q_ref[:, h * DK_R:(h + 1) * DK_R]
        k = k_ref[:, h * DK_R:(h + 1) * DK_R]
        v = v_ref[:, h * DV_R:(h + 1) * DV_R]
        s_prev = s_scr[h]
        a = _dot_nt(_mx(q), _mx(k)) * inner_ref[h]
        o = _dot(_mx(a), v) + _dot(_mx(q), _mx(s_prev)) * qdec_ref[h]
        kd = k * kdec_ref[h]
        s_scr[h] = s_prev * c_dec[h] + _dot(_mx(kd.T), v)
        mu = jnp.mean(o, axis=-1, keepdims=True)
        var = jnp.mean(jnp.square(o - mu), axis=-1, keepdims=True)
        gn = (o - mu) * lax.rsqrt(var + EPS)
        g = g_ref[:, h * DV_R:(h + 1) * DV_R]
        o_ref[:, h * DV_R:(h + 1) * DV_R] = (gn * (g * _sigmoid(g))).astype(o_ref.dtype)

    @pl.when(j == nj - 1)
    def _():
        s_out_ref[0] = s_scr[...]


def _retention(qr, kr, vr, gr, s0, c_real):
    b = s0.shape[0]
    c = RET_CHUNK
    n = qr.shape[0] // (b * c)
    inner, qdec, kdec, c_dec = _retention_tables(c_real, c)
    wr, wv = N_HEADS_R * DK_R, N_HEADS_R * DV_R
    rmap = lambda bi, j: (bi * n + j, 0)
    full3 = lambda shape: pl.BlockSpec(shape, lambda bi, j: (0, 0, 0))
    return pl.pallas_call(
        functools.partial(_retention_kernel, c_dec=c_dec),
        grid=(b, n),
        in_specs=[pl.BlockSpec((c, wr), rmap), pl.BlockSpec((c, wr), rmap), pl.BlockSpec((c, wv), rmap),
                  pl.BlockSpec((c, wv), rmap),
                  pl.BlockSpec((1, N_HEADS_R, DK_R, DV_R), lambda bi, j: (bi, 0, 0, 0)),
                  full3(inner.shape), full3(qdec.shape), full3(kdec.shape)],
        out_specs=[pl.BlockSpec((c, wv), rmap),
                   pl.BlockSpec((1, N_HEADS_R, DK_R, DV_R), lambda bi, j: (bi, 0, 0, 0))],
        out_shape=[jax.ShapeDtypeStruct((b * n * c, wv), MXU_DTYPE),
                   jax.ShapeDtypeStruct((b, N_HEADS_R, DK_R, DV_R), F32)],
        scratch_shapes=[pltpu.VMEM((N_HEADS_R, DK_R, DV_R), F32)],
        compiler_params=_cparams(("parallel", "arbitrary")),
        name="retention",
    )(qr, kr, vr, gr, s0, inner, qdec, kdec)


def _cross_kernel(q_ref, mk_ref, mv_ref, o_ref):
    scale = HEAD_DIM_M ** -0.5
    for h in range(N_HEADS_M):
        sl = slice(h * HEAD_DIM_M, (h + 1) * HEAD_DIM_M)
        s = _dot_nt(_mx(q_ref[:, sl]), _mx(mk_ref[0, :, sl])) * scale
        p = jnp.exp(s - jnp.max(s, axis=1, keepdims=True))
        p = p / jnp.sum(p, axis=1, keepdims=True)
        o_ref[:, sl] = _dot(_mx(p), _mx(mv_ref[0, :, sl])).astype(o_ref.dtype)


def _cross_attend(qm, mk, mv, tm, out_dtype):
    b, n_mem, hd = mk.shape
    nt = qm.shape[0] // (b * tm)
    return pl.pallas_call(
        _cross_kernel,
        grid=(b, nt),
        in_specs=[pl.BlockSpec((tm, hd), lambda bi, i: (bi * nt + i, 0)),
                  pl.BlockSpec((1, n_mem, hd), lambda bi, i: (bi, 0, 0)),
                  pl.BlockSpec((1, n_mem, hd), lambda bi, i: (bi, 0, 0))],
        out_specs=pl.BlockSpec((tm, hd), lambda bi, i: (bi * nt + i, 0)),
        out_shape=jax.ShapeDtypeStruct(qm.shape, out_dtype),
        compiler_params=_cparams(("parallel", "parallel")),
        name="cross_attend",
    )(qm, mk, mv)


def _memkv_kernel(x_ref, g_ref, w_ref, o_ref):
    o_ref[...] = _dot(_mx(_rms(x_ref[...], g_ref[...])), w_ref[...])


def _memory_kv(mem, gain, w):
    rows, d = mem.shape
    return pl.pallas_call(
        _memkv_kernel,
        out_shape=jax.ShapeDtypeStruct((rows, w.shape[1]), F32),
        compiler_params=pltpu.CompilerParams(vmem_limit_bytes=VMEM_LIMIT),
        name="memory_kv",
    )(mem, gain.reshape(1, d), _mx(w))


def _merge_kernel(x_ref, oa_ref, or_ref, om_ref, gates_ref, wpa_ref, wpb_ref, wpc_ref, wo_ref, g_ref, h_ref):
    d = x_ref.shape[1]
    gt = gates_ref[...]
    mixed = (_sigmoid(gt[:, :d]) * _dot(_mx(oa_ref[...]), wpa_ref[...])
             + _sigmoid(gt[:, d:2 * d]) * _dot(_mx(or_ref[...]), wpb_ref[...])
             + _sigmoid(gt[:, 2 * d:]) * _dot(_mx(om_ref[...]), wpc_ref[...]))
    z = _dot(_mx(mixed), wo_ref[...])
    h_ref[...] = x_ref[...] + _rms(z, g_ref[...])


def _merge(x, oa, o_r, om, gates, wpa, wpb, wpc, wo, gain, tm):
    rows, d = x.shape
    return pl.pallas_call(
        _merge_kernel,
        grid=(rows // tm,),
        in_specs=[_row_spec(tm, d), _row_spec(tm, oa.shape[1]), _row_spec(tm, o_r.shape[1]),
                  _row_spec(tm, om.shape[1]), _row_spec(tm, 3 * d),
                  _full_spec(wpa.shape), _full_spec(wpb.shape), _full_spec(wpc.shape), _full_spec(wo.shape),
                  _full_spec((1, d))],
        out_specs=_row_spec(tm, d),
        out_shape=jax.ShapeDtypeStruct((rows, d), F32),
        compiler_params=_cparams(("parallel",)),
        name="merge",
    )(x, oa, o_r, om, gates, wpa, wpb, wpc, wo, gain.reshape(1, d))


HALO = BF16_ROWS


def _ffn_kernel(h_ref, halo_ref, s0_ref, s1_ref, g1_ref, g2_ref, wu_ref, wg_ref, cw_ref, cb_ref, wd_ref,
                y_ref, utail_ref, x_scr, u_scr, *, tm, seq, keep):
    i = pl.program_id(0)
    h = h_ref[...]
    hn = _rms(h, g1_ref[...])
    x_scr[HALO:, :] = _mx(hn)
    x_scr[:HALO, :] = _mx(_rms(halo_ref[...], g1_ref[...]))
    xc = x_scr[...]
    u_scr[...] = _dot(xc, wu_ref[...])
    gate = _dot(xc[HALO:], wg_ref[...])
    cur = u_scr[HALO:, :]
    prev1 = u_scr[HALO - 1:HALO - 1 + tm, :]
    prev2 = u_scr[HALO - 2:HALO - 2 + tm, :]
    seq_loc = min(seq, tm)
    t = lax.rem(lax.broadcasted_iota(I32, (tm, 1), 0), seq_loc)
    t = jnp.where(lax.rem(i * tm, seq) == 0, t, CONV_W)
    st0, st1 = s0_ref[...], s1_ref[...]
    if st0.shape[0] != tm:
        st0, st1 = st0[0:1], st1[0:1]
    prev1 = jnp.where(t == 0, st1, prev1)
    prev2 = jnp.where(t == 0, st0, jnp.where(t == 1, st1, prev2))
    c = cb_ref[...] + prev2 * cw_ref[0:1, :] + prev1 * cw_ref[1:2, :] + cur * cw_ref[2:3, :]
    act = jax.nn.gelu(c, approximate=True) * gate
    ff = _dot(_mx(act), wd_ref[...])
    y_ref[...] = h + _rms(ff, g2_ref[...])
    utail_ref[...] = u_scr[HALO + tm - keep:, :]


def _conv_ffn(h, s0e, s1e, g1, g2, wu, wg, cw, cb, wd, tm, seq, keep):
    rows, d = h.shape
    f = wu.shape[1]
    nt = rows // tm
    hb = tm // HALO
    sr = s0e.shape[0]
    return pl.pallas_call(
        functools.partial(_ffn_kernel, tm=tm, seq=seq, keep=keep),
        grid=(nt,),
        in_specs=[_row_spec(tm, d),
                  pl.BlockSpec((HALO, d), lambda i: (jnp.maximum(i * hb - 1, 0), 0)),
                  _full_spec((sr, f)), _full_spec((sr, f)),
                  _full_spec((1, d)), _full_spec((1, d)),
                  _full_spec(wu.shape), _full_spec(wg.shape), _full_spec(cw.shape), _full_spec((1, f)),
                  _full_spec(wd.shape)],
        out_specs=[_row_spec(tm, d), _row_spec(keep, f)],
        out_shape=[jax.ShapeDtypeStruct((rows, d), F32), jax.ShapeDtypeStruct((nt * keep, f), F32)],
        scratch_shapes=[pltpu.VMEM((tm + HALO, d), MXU_DTYPE), pltpu.VMEM((tm + HALO, f), F32)],
        compiler_params=_cparams(("parallel",)),
        name="conv_ffn",
    )(h, h, s0e, s1e, g1.reshape(1, d), g2.reshape(1, d), wu, wg, cw, cb.reshape(1, f), wd)


def _layer_weights(l, w_in, w_proj_a, w_proj_b, w_proj_c, w_out, w_up, w_down):
    wts = _prep_in_weights(w_in[l])
    f = w_down.shape[1]
    wts.update(wpa=_mx(w_proj_a[l]), wpb=_mx(w_proj_b[l]), wpc=_mx(w_proj_c[l]), wo=_mx(w_out[l]),
               wu=_mx(w_up[l][:, :f]), wg=_mx(w_up[l][:, f:]), wd=_mx(w_down[l]))
    return wts


def _prompt_layer(x, mem, wts, norms, conv_w, conv_b, w_mem_kv, tiles):
    s, d = x.shape
    f = wts["wd"].shape[0]
    pos = jnp.arange(s)
    pr = _projections(x, norms["pre_mix"], pos, wts, tiles["proj"])
    topk = min(TOPK_MAX, s // 4)
    mask = _prompt_select(pr["qi_hm"], pr["wi_f"], pr["ki_b"], topk, tiles["sel_q"], SEL_WIDE)
    o_a = _prompt_attend(pr["qa_t"], pr["ka_hm"], pr["va_t"], mask, tiles["att_q"], tiles["att_k"])
    s0 = jnp.zeros((1, N_HEADS_R, DK_R, DV_R), F32)
    o_r, ret_new = _retention(pr["qr_f"], pr["kr_f"], pr["vr_b"], pr["gr_f"], s0, RET_CHUNK)
    kv = _memory_kv(mem, norms["mem"], w_mem_kv)
    wm = N_HEADS_M * HEAD_DIM_M
    mk, mv = kv[:, :wm], kv[:, wm:]
    o_m = _cross_attend(pr["qm_b"], mk[None], mv[None], tiles["cross"], MXU_DTYPE)
    h = _merge(x, o_a, o_r, o_m, pr["gates_f"], wts["wpa"], wts["wpb"], wts["wpc"], wts["wo"],
               norms["post_mix"], tiles["merge"])
    zst = jnp.zeros((SUBLANES, f), F32)
    y, utail = _conv_ffn(h, zst, zst, norms["pre_ffn"], norms["post_ffn"], wts["wu"], wts["wg"], conv_w, conv_b,
                         wts["wd"], tiles["ffn"], s, SUBLANES)
    conv_new = utail[-(CONV_W - 1):]
    return y, pr["ka_f"], pr["va_f"], pr["ki_f"], ret_new, conv_new, mk, mv


def _sample_layer(x, wts, norms, conv_w, conv_b, cache_k, cache_v, cache_kidx, mem_k, mem_v,
                  state_ret, state_conv, page_table):
    db, t, d = x.shape
    f = wts["wd"].shape[0]
    n_pages = page_table.shape[1]
    page = cache_k.shape[1]
    past = n_pages * page
    rows = db * T_PAD
    xp = jnp.pad(x, ((0, 0), (0, T_PAD - t), (0, 0))).reshape(rows, d)
    pos = jnp.tile(past + jnp.arange(T_PAD), db)
    pr = _projections(xp, norms["pre_mix"], pos, wts, rows)
    hd = N_HEADS_A * HEAD_DIM_A

    topk = min(TOPK_MAX, (past + t) // 4)
    i_past = _sample_index(page_table, pr["qi_hm"].astype(F32), pr["wi_f"], cache_kidx.transpose(0, 2, 1)).reshape(rows, past)
    m_past, m_new = _sample_select(i_past, pr["qi_hm"], pr["wi_f"], pr["ki_b"], t, topk)
    own = m_new[:, :rows].reshape(db, T_PAD, db, T_PAD)[jnp.arange(db), :, jnp.arange(db), :]
    m_new_own = jnp.pad(own, ((0, 0), (0, 0), (0, LANES - T_PAD)))
    q_rows = pr["qa_hm"].reshape(N_HEADS_A, db, T_PAD, HEAD_DIM_A)
    eye = jnp.eye(N_HEADS_A, dtype=MXU_DTYPE)
    q_bd = jnp.einsum("hbtd,hg->bhtgd", q_rows, eye).reshape(db, N_HEADS_A * T_PAD, hd)
    new_t = lambda a: jnp.pad(_mx(a).reshape(db, T_PAD, hd).transpose(0, 2, 1), ((0, 0), (0, 0), (0, LANES - T_PAD)))
    paged_t = lambda c: c.transpose(0, 2, 3, 1).reshape(c.shape[0], hd, page)
    o_a = _sample_attend(page_table, q_bd, m_past.reshape(db, T_PAD, past), m_new_own,
                         new_t(pr["ka_f"]), new_t(pr["va_f"]), paged_t(cache_k), paged_t(cache_v)).reshape(rows, hd)

    padc = lambda a: jnp.pad(a.reshape(db, T_PAD, -1), ((0, 0), (0, RET_CHUNK - T_PAD), (0, 0))).reshape(db * RET_CHUNK, -1)
    o_r, ret_new = _retention(padc(pr["qr_f"]), padc(pr["kr_f"]), padc(pr["vr_b"]), padc(pr["gr_f"]), state_ret, t)
    o_r = o_r.reshape(db, RET_CHUNK, -1)[:, :T_PAD].reshape(rows, -1)

    wm = N_HEADS_M * HEAD_DIM_M
    o_m = _cross_attend(pr["qm_b"].astype(F32), mem_k.reshape(db, -1, wm), mem_v.reshape(db, -1, wm), T_PAD, F32)

    h = _merge(xp, o_a, o_r, o_m, pr["gates_f"], wts["wpa"], wts["wpb"], wts["wpc"], wts["wo"],
               norms["post_mix"], rows)
    s0e = jnp.repeat(state_conv[:, 0], T_PAD, axis=0)
    s1e = jnp.repeat(state_conv[:, 1], T_PAD, axis=0)
    y, u_all = _conv_ffn(h, s0e, s1e, norms["pre_ffn"], norms["post_ffn"], wts["wu"], wts["wg"], conv_w, conv_b,
                         wts["wd"], rows, T_PAD, rows)
    ext = jnp.concatenate([state_conv.astype(F32), u_all.reshape(db, T_PAD, f)[:, :t]], axis=1)
    conv_new = ext[:, t:]
    unpad = lambda a: a.reshape(db, T_PAD, -1)[:, :t]
    return (unpad(y), unpad(pr["ka_f"]), unpad(pr["va_f"]), unpad(pr["ki_f"]), ret_new, conv_new)


PROMPT_TILES = dict(proj=512, sel_q=512, att_q=512, att_k=1024, cross=512, merge=512, ffn=256)


def kernel(x_prompt, x_sample, cache_k, cache_v, cache_kidx, cache_mem_k, cache_mem_v, state_ret, state_conv,
           page_table, mem_prompt, norm_pre_mix, norm_post_mix, norm_pre_ffn, norm_post_ffn, norm_mem,
           w_in, w_mem_kv, w_proj_a, w_proj_b, w_proj_c, w_out, w_up, conv_w, conv_b, w_down):
    bp, s, d = x_prompt.shape
    db, t, _ = x_sample.shape
    depth = w_in.shape[0]
    assert bp == 1 and t <= T_PAD and CONV_W - 1 <= t
    tiles = {k: min(v, s) for k, v in PROMPT_TILES.items()}
    yp, ys = x_prompt[0], x_sample
    outs = [[] for _ in range(12)]
    for l in range(depth):
        wts = _layer_weights(l, w_in, w_proj_a, w_proj_b, w_proj_c, w_out, w_up, w_down)
        norms = dict(pre_mix=norm_pre_mix[l], post_mix=norm_post_mix[l], pre_ffn=norm_pre_ffn[l],
                     post_ffn=norm_post_ffn[l], mem=norm_mem[l])
        yp, kp, vp, kip, rp, cp, mk, mv = _prompt_layer(yp, mem_prompt[0], wts, norms, conv_w[l], conv_b[l],
                                                        w_mem_kv[l], tiles)
        ys, ks, vs, kis, rs, cs = _sample_layer(ys, wts, norms, conv_w[l], conv_b[l], cache_k[l], cache_v[l],
                                                cache_kidx[l], cache_mem_k[l], cache_mem_v[l], state_ret[l],
                                                state_conv[l], page_table)
        n_mem = mk.shape[0]
        vals = (kp.reshape(1, s, N_HEADS_A, HEAD_DIM_A), vp.reshape(1, s, N_HEADS_A, HEAD_DIM_A),
                kip.reshape(1, s, IDX_DIM), rp, cp[None],
                mk.reshape(1, n_mem, N_HEADS_M, HEAD_DIM_M), mv.reshape(1, n_mem, N_HEADS_M, HEAD_DIM_M),
                ks.reshape(db, t, N_HEADS_A, HEAD_DIM_A), vs.reshape(db, t, N_HEADS_A, HEAD_DIM_A),
                kis, rs, cs)
        for o, v in zip(outs, vals):
            o.append(v)
    stacked = [jnp.stack(o) for o in outs]
    return (yp[None], ys, *stacked)
```

```python
import functools
import math

import numpy as np
import jax
import jax.numpy as jnp
from jax import lax
from jax.experimental import pallas as pl
from jax.experimental.pallas import tpu as pltpu

F32 = jnp.float32
I32 = jnp.int32
MXU_DTYPE = jnp.bfloat16

N_HEADS_A, HEAD_DIM_A = 8, 64
IDX_HEADS, IDX_DIM = 4, 64
TOPK_MAX = 256
N_HEADS_R, DK_R, DV_R = 4, 128, 256
RET_CHUNK = 128
N_HEADS_M, HEAD_DIM_M = 4, 128
CONV_W = 3
ROPE_THETA = 10000.0
EPS = 1e-6

LANES = 128
SUBLANES = 8
BF16_ROWS = 16
VMEM_LIMIT = 56 * 1024 * 1024
NEG_BIG = -1e30
F32_LOWEST = float(np.finfo(np.float32).min)
T_PAD = 8


def _cparams(sem):
    return pltpu.CompilerParams(dimension_semantics=sem, vmem_limit_bytes=VMEM_LIMIT)


def _dot(a, b):
    return jnp.dot(a, b, preferred_element_type=F32)


def _dot_nt(a, b):
    return lax.dot_general(a, b, (((1,), (1,)), ((), ())), preferred_element_type=F32)


def _mx(a):
    return a.astype(MXU_DTYPE)


def _rms(x, g):
    return x * lax.rsqrt(jnp.mean(x * x, axis=-1, keepdims=True) + EPS) * g


def _sigmoid(x):
    return 1.0 / (1.0 + jnp.exp(-x))


def _rope_tables(pos, d):
    half = d // 2
    inv = 1.0 / (ROPE_THETA ** (jnp.arange(half, dtype=F32) * 2.0 / d))
    ang = pos.astype(F32)[:, None] * inv[None, :]
    cos, sin = jnp.cos(ang), jnp.sin(ang)
    reps = LANES // d
    cos_t = jnp.tile(jnp.concatenate([cos, cos], axis=1), (1, reps))
    sin_t = jnp.tile(jnp.concatenate([-sin, sin], axis=1), (1, reps))
    return cos_t, sin_t


def _rope(y, cos, sin, d):
    w = y.shape[1]
    half = d // 2
    reps = w // LANES
    c = jnp.concatenate([cos] * reps, axis=1) if reps > 1 else cos
    s = jnp.concatenate([sin] * reps, axis=1) if reps > 1 else sin
    lane = lax.broadcasted_iota(I32, y.shape, 1)
    first = (lane & (d - 1)) < half
    rot = jnp.where(first, pltpu.roll(y, w - half, 1), pltpu.roll(y, half, 1))
    return y * c + rot * s


def _proj_a_kernel(x_ref, g_ref, cos_ref, sin_ref, wa_ref, wv_ref,
                   qa_hm, qa_t, ka_f, ka_hm, qi_hm, ki_f, ki_b, va_f, va_t, wi_f):
    xn = _mx(_rms(x_ref[...], g_ref[...]))
    wa = N_HEADS_A * HEAD_DIM_A
    y = _rope(_dot(xn, wa_ref[...]), cos_ref[...], sin_ref[...], HEAD_DIM_A)
    q = y[:, :wa] * (HEAD_DIM_A ** -0.5)
    k = y[:, wa:2 * wa]
    ka_f[...] = k
    qt = q.T
    for h in range(N_HEADS_A):
        sl = slice(h * HEAD_DIM_A, (h + 1) * HEAD_DIM_A)
        qa_hm[h] = _mx(q[:, sl])
        qa_t[h] = _mx(qt[sl, :])
        ka_hm[h] = _mx(k[:, sl])
    qi = y[:, 2 * wa:2 * wa + IDX_HEADS * IDX_DIM]
    for h in range(IDX_HEADS):
        qi_hm[h] = _mx(qi[:, h * IDX_DIM:(h + 1) * IDX_DIM])
    ki = y[:, 2 * wa + IDX_HEADS * IDX_DIM:2 * wa + IDX_HEADS * IDX_DIM + IDX_DIM]
    ki_f[...] = ki
    ki_b[...] = _mx(ki)
    z = _dot(xn, wv_ref[...])
    v = z[:, :wa]
    va_f[...] = v
    vt = v.T
    for h in range(N_HEADS_A):
        va_t[h] = _mx(vt[h * HEAD_DIM_A:(h + 1) * HEAD_DIM_A, :])
    wi_f[...] = z[:, wa:wa + LANES]


def _proj_b_kernel(x_ref, g_ref, cos_ref, sin_ref, wr_ref, wvg_ref, qr_f, kr_f, vr_b, gr_f):
    xn = _mx(_rms(x_ref[...], g_ref[...]))
    wr = N_HEADS_R * DK_R
    y = _rope(_dot(xn, wr_ref[...]), cos_ref[...], sin_ref[...], DK_R)
    qr_f[...] = y[:, :wr]
    kr_f[...] = y[:, wr:] * (DK_R ** -0.5)
    z = _dot(xn, wvg_ref[...])
    wv = N_HEADS_R * DV_R
    vr_b[...] = _mx(z[:, :wv])
    gr_f[...] = z[:, wv:]


def _proj_c_kernel(x_ref, g_ref, wc_ref, qm_b, gates_f):
    xn = _mx(_rms(x_ref[...], g_ref[...]))
    z = _dot(xn, wc_ref[...])
    wm = N_HEADS_M * HEAD_DIM_M
    qm_b[...] = _mx(z[:, :wm])
    gates_f[...] = z[:, wm:]


def _row_spec(tm, w):
    return pl.BlockSpec((tm, w), lambda i: (i, 0))


def _full_spec(shape):
    nd = len(shape)
    return pl.BlockSpec(shape, lambda i: (0,) * nd)


def _hm_spec(nh, tm, d):
    return pl.BlockSpec((nh, tm, d), lambda i: (0, i, 0))


def _projections(x, gain, pos, wts, tm):
    rows, d = x.shape
    grid = (rows // tm,)
    wa = N_HEADS_A * HEAD_DIM_A
    cos64, sin64 = _rope_tables(pos, HEAD_DIM_A)
    cos128, sin128 = _rope_tables(pos, DK_R)
    g2 = gain.reshape(1, d)
    sds = jax.ShapeDtypeStruct
    t_spec = pl.BlockSpec((N_HEADS_A, HEAD_DIM_A, tm), lambda i: (0, 0, i))
    outs_a = pl.pallas_call(
        _proj_a_kernel,
        grid=grid,
        in_specs=[_row_spec(tm, d), _full_spec((1, d)), _row_spec(tm, LANES), _row_spec(tm, LANES),
                  _full_spec(wts["wa"].shape), _full_spec(wts["wv"].shape)],
        out_specs=[_hm_spec(N_HEADS_A, tm, HEAD_DIM_A), t_spec, _row_spec(tm, wa),
                   _hm_spec(N_HEADS_A, tm, HEAD_DIM_A),
                   _hm_spec(IDX_HEADS, tm, IDX_DIM), _row_spec(tm, IDX_DIM), _row_spec(tm, IDX_DIM),
                   _row_spec(tm, wa), t_spec, _row_spec(tm, LANES)],
        out_shape=[sds((N_HEADS_A, rows, HEAD_DIM_A), MXU_DTYPE), sds((N_HEADS_A, HEAD_DIM_A, rows), MXU_DTYPE),
                   sds((rows, wa), F32),
                   sds((N_HEADS_A, rows, HEAD_DIM_A), MXU_DTYPE), sds((IDX_HEADS, rows, IDX_DIM), MXU_DTYPE),
                   sds((rows, IDX_DIM), F32), sds((rows, IDX_DIM), MXU_DTYPE),
                   sds((rows, wa), F32), sds((N_HEADS_A, HEAD_DIM_A, rows), MXU_DTYPE), sds((rows, LANES), F32)],
        compiler_params=_cparams(("parallel",)),
        name="proj_a",
    )(x, g2, cos64, sin64, wts["wa"], wts["wv"])
    names_a = ("qa_hm", "qa_t", "ka_f", "ka_hm", "qi_hm", "ki_f", "ki_b", "va_f", "va_t", "wi_f")
    wr, wv = N_HEADS_R * DK_R, N_HEADS_R * DV_R
    outs_b = pl.pallas_call(
        _proj_b_kernel,
        grid=grid,
        in_specs=[_row_spec(tm, d), _full_spec((1, d)), _row_spec(tm, LANES), _row_spec(tm, LANES),
                  _full_spec(wts["wr"].shape), _full_spec(wts["wvg"].shape)],
        out_specs=[_row_spec(tm, wr), _row_spec(tm, wr), _row_spec(tm, wv), _row_spec(tm, wv)],
        out_shape=[sds((rows, wr), F32), sds((rows, wr), F32), sds((rows, wv), MXU_DTYPE), sds((rows, wv), F32)],
        compiler_params=_cparams(("parallel",)),
        name="proj_b",
    )(x, g2, cos128, sin128, wts["wr"], wts["wvg"])
    names_b = ("qr_f", "kr_f", "vr_b", "gr_f")
    wm = N_HEADS_M * HEAD_DIM_M
    outs_c = pl.pallas_call(
        _proj_c_kernel,
        grid=grid,
        in_specs=[_row_spec(tm, d), _full_spec((1, d)), _full_spec(wts["wc"].shape)],
        out_specs=[_row_spec(tm, wm), _row_spec(tm, 3 * d)],
        out_shape=[sds((rows, wm), MXU_DTYPE), sds((rows, 3 * d), F32)],
        compiler_params=_cparams(("parallel",)),
        name="proj_c",
    )(x, g2, wts["wc"])
    names_c = ("qm_b", "gates_f")
    out = dict(zip(names_a, outs_a))
    out.update(zip(names_b, outs_b))
    out.update(zip(names_c, outs_c))
    return out


def _prep_in_weights(w_in):
    d = w_in.shape[0]
    wa = N_HEADS_A * HEAD_DIM_A
    widths = (wa, wa, wa, IDX_HEADS * IDX_DIM, IDX_DIM, IDX_HEADS,
              N_HEADS_R * DK_R, N_HEADS_R * DK_R, N_HEADS_R * DV_R, N_HEADS_R * DV_R,
              N_HEADS_M * HEAD_DIM_M, 3 * d)
    offs = np.concatenate([[0], np.cumsum(widths)])
    seg = [w_in[:, int(offs[i]):int(offs[i + 1])] for i in range(len(widths))]
    q_a, k_a, v_a, q_i, k_i, w_i, q_r, k_r, v_r, g_r, q_m, gates = seg
    zpad = lambda n: jnp.zeros((d, n), w_in.dtype)
    return {
        "wa": _mx(jnp.concatenate([q_a, k_a, q_i, k_i, zpad(LANES - IDX_DIM)], axis=1)),
        "wv": _mx(jnp.concatenate([v_a, w_i, zpad(LANES - IDX_HEADS)], axis=1)),
        "wr": _mx(jnp.concatenate([q_r, k_r], axis=1)),
        "wvg": _mx(jnp.concatenate([v_r, g_r], axis=1)),
        "wc": _mx(jnp.concatenate([q_m, gates], axis=1)),
    }


SEL_ROWS = 128
SEL_WIDE = 512
SWEEP_UNROLL = 2
SEL_GROUPS = 2 * LANES
NO_LIMIT = 2 ** 30
MIN_NORMAL_KEY = 0x00800000
SEARCH_INTERP_STEPS = 8
SEARCH_CAP = SEARCH_INTERP_STEPS + 33


def _key_to_f32(key):
    bits = jnp.where(key >= 0, key, key ^ jnp.int32(0x7FFFFFFF))
    return pltpu.bitcast(bits, F32)


def _f32_to_key(f):
    bits = pltpu.bitcast(f, I32)
    return jnp.where(f == 0.0, 0, jnp.where(bits >= 0, bits, bits ^ jnp.int32(0x7FFFFFFF)))


def _sweep(segments, rs, init, fn):
    acc = init
    base = 0
    for ref, n_wide in segments:
        def body(c, a, ref=ref, base=base):
            off = c * SEL_WIDE
            for k in range(SEL_WIDE // LANES):
                x = ref[rs, pl.ds(pl.multiple_of(off + k * LANES, LANES), LANES)]
                a = fn(a, x, base + off + k * LANES, k)
            return a
        def multi(c, a, body=body):
            for u in range(SWEEP_UNROLL):
                a = body(SWEEP_UNROLL * c + u, a)
            return a
        full = n_wide // SWEEP_UNROLL
        acc = lax.fori_loop(0, full, multi, acc)
        acc = lax.fori_loop(SWEEP_UNROLL * full, n_wide, body, acc)
        base = base + n_wide * SEL_WIDE
    return acc


def _count(segments, rows, make_pred, wanted=None):
    outs = []
    for g in range(rows // SEL_ROWS):
        def group_count(_, g=g):
            rs = slice(g * SEL_ROWS, (g + 1) * SEL_ROWS)
            pred = make_pred(rs)
            acc = _sweep(segments, rs, jnp.zeros((SEL_ROWS, LANES), F32),
                         lambda a, x, idx0, k: a + jnp.where(pred(x, idx0), 1.0, 0.0))
            return jnp.sum(acc.T, axis=0, keepdims=True)
        if wanted is None:
            outs.append(group_count(0))
        else:
            outs.append(lax.cond(wanted[g] > 0, group_count, lambda _: jnp.zeros((1, SEL_ROWS), F32), 0))
    return jnp.concatenate(outs, axis=1) if len(outs) > 1 else outs[0]


def _col_to_row(col):
    return jnp.broadcast_to(col, (col.shape[0], LANES)).T[0:1]


def _row_to_lanes(vec):
    return jnp.broadcast_to(vec, (LANES, vec.shape[1])).T


def _search(count_fn, lo, hi, c_lo, c_hi, target, alive):
    def unfinished(lo, hi, c_lo):
        return jnp.logical_and(alive, jnp.logical_and(c_lo > target, lo + 1 < hi))

    def any_row(flag):
        return jnp.max(jnp.where(flag, 1, 0).astype(I32))

    def cond(carry):
        return jnp.logical_and(carry[0] < SEARCH_CAP, carry[1] > 0)

    def body(carry):
        it, _, lo, hi, c_lo, c_hi, w_lo, w_hi, last = carry
        act = unfinished(lo, hi, c_lo)
        bis = (lo >> 1) + (hi >> 1) + (lo & hi & 1)
        width = hi - lo
        a = (c_lo - target + 0.5) * w_lo
        b = (target - 0.5 - c_hi) * w_hi
        frac = a / jnp.maximum(a + b, 1e-6)
        step = (frac * width.astype(F32)).astype(I32)
        itp = lo + jnp.clip(step, 1, jnp.maximum(width - 1, 1))
        use_itp = jnp.logical_and((lo ^ hi) >= 0, (jnp.zeros_like(lo) + it) < SEARCH_INTERP_STEPS)
        v = jnp.where(use_itp, itp, bis)
        n_groups = lo.shape[1] // SEL_ROWS
        cnt = count_fn(v, [any_row(act[:, g * SEL_ROWS:(g + 1) * SEL_ROWS]) for g in range(n_groups)])
        up = jnp.logical_and(act, cnt >= target)
        dn = jnp.logical_and(act, cnt < target)
        lo = jnp.where(up, v, lo)
        c_lo = jnp.where(up, cnt, c_lo)
        hi = jnp.where(dn, v, hi)
        c_hi = jnp.where(dn, cnt, c_hi)
        w_hi = jnp.where(up, jnp.where(last == 1, w_hi * 0.5, 1.0), jnp.where(dn, 1.0, w_hi))
        w_lo = jnp.where(dn, jnp.where(last == -1, w_lo * 0.5, 1.0), jnp.where(up, 1.0, w_lo))
        last = jnp.where(up, 1, jnp.where(dn, -1, last))
        return it + 1, any_row(unfinished(lo, hi, c_lo)), lo, hi, c_lo, c_hi, w_lo, w_hi, last

    go = any_row(unfinished(lo, hi, c_lo))
    one = jnp.ones(lo.shape, F32)
    out = lax.while_loop(cond, body, (jnp.int32(0), go, lo, hi, c_lo, c_hi, one, one, jnp.zeros(lo.shape, I32)))
    return out[2], out[4], out[5]


def _select_threshold(segments, rows, topk, alive=None):
    assert topk <= SEL_GROUPS
    kf = jnp.full((1, rows), float(topk), F32)
    if alive is None:
        alive = jnp.full((1, rows), True)

    los, his = [], []
    for g in range(rows // SEL_ROWS):
        rs = slice(g * SEL_ROWS, (g + 1) * SEL_ROWS)
        ninf = jnp.full((SEL_ROWS, LANES), -jnp.inf, F32)
        ga, gb = _sweep(segments, rs, (ninf, ninf),
                        lambda a, x, idx0, k: ((jnp.maximum(a[0], x), a[1]) if k % 2 == 0
                                               else (a[0], jnp.maximum(a[1], x))))
        los.append(jnp.min(jnp.minimum(ga, gb), axis=1, keepdims=True))
        his.append(jnp.max(jnp.maximum(ga, gb), axis=1, keepdims=True))
    cat = lambda xs: _col_to_row(jnp.concatenate(xs, axis=0) if len(xs) > 1 else xs[0])
    lo = _f32_to_key(jnp.maximum(cat(los), F32_LOWEST))
    hi = _f32_to_key(cat(his)) + 1

    def count_ge_f(thr, wanted=None):
        thr_l = _row_to_lanes(thr)

        def make_pred(rs):
            t = thr_l[rs]
            return lambda x, idx0: x >= t
        return _count(segments, rows, make_pred, wanted)

    count_ge = lambda v, wanted=None: count_ge_f(_key_to_f32(v), wanted)
    c_lo = count_ge(lo)
    c_hi = jnp.zeros((1, rows), F32)
    def zero_probes(state):
        lo, hi, c_lo, c_hi = state
        for probe in (0, MIN_NORMAL_KEY):
            v = jnp.full((1, rows), probe, I32)
            cnt = count_ge_f(jnp.full((1, rows), np.int32(probe).view(np.float32), F32))
            inside = jnp.logical_and(lo < v, v < hi)
            up = jnp.logical_and(inside, cnt >= kf)
            dn = jnp.logical_and(inside, cnt < kf)
            lo, c_lo = jnp.where(up, v, lo), jnp.where(up, cnt, c_lo)
            hi, c_hi = jnp.where(dn, v, hi), jnp.where(dn, cnt, c_hi)
        return lo, hi, c_lo, c_hi

    straddles = jnp.logical_or(jnp.logical_and(lo < 0, hi > 0),
                               jnp.logical_and(lo < MIN_NORMAL_KEY, hi > MIN_NORMAL_KEY))
    lo, hi, c_lo, c_hi = lax.cond(jnp.max(jnp.where(straddles, 1, 0).astype(I32)) > 0,
                                  zero_probes, lambda s: s, (lo, hi, c_lo, c_hi))
    at_zero = jnp.logical_and(lo == 0, hi == MIN_NORMAL_KEY)
    lo, c_lo, c_hi = _search(count_ge, lo, hi, c_lo, c_hi, kf, jnp.logical_and(alive, jnp.logical_not(at_zero)))
    tau = _key_to_f32(lo)

    need = jnp.logical_and(alive, c_lo > kf)
    room = jnp.where(need, kf - c_hi, float(NO_LIMIT))
    return _row_to_lanes(tau), _row_to_lanes(room)


def _write_mask(segments_out, rows, tau, room):
    r_io = lax.broadcasted_iota(I32, (LANES, 2 * LANES), 0)
    c_io = lax.broadcasted_iota(I32, (LANES, 2 * LANES), 1)
    tri = jnp.where(jnp.logical_or(r_io <= c_io, c_io >= LANES), 1.0, 0.0).astype(MXU_DTYPE)
    for g in range(rows // SEL_ROWS):
        rs = slice(g * SEL_ROWS, (g + 1) * SEL_ROWS)
        t, rm = tau[rs], room[rs]
        seen = jnp.zeros((SEL_ROWS, LANES), F32)
        for src, dst, n_wide, n_total in segments_out:
            def body(c, seen, src=src, dst=dst):
                off = c * SEL_WIDE
                nsub = SEL_WIDE // LANES
                dss = [pl.ds(pl.multiple_of(off + k * LANES, LANES), LANES) for k in range(nsub)]
                xs = [src[rs, ds] for ds in dss]
                eqs = [jnp.where(x == t, 1.0, 0.0) for x in xs]
                run = jnp.dot(jnp.concatenate(eqs, axis=0).astype(MXU_DTYPE), tri, preferred_element_type=F32)
                for k in range(nsub):
                    run_k = run[k * SEL_ROWS:(k + 1) * SEL_ROWS]
                    before = seen + run_k[:, :LANES] - eqs[k]
                    keep_eq = jnp.where(before < rm, eqs[k], 0.0)
                    dst[rs, dss[k]] = jnp.where(xs[k] > t, 1.0, keep_eq).astype(dst.dtype)
                    seen = seen + run_k[:, LANES:]
                return seen

            pairs = n_wide // 2
            seen = lax.fori_loop(0, pairs, lambda c, sn, body=body: body(2 * c + 1, body(2 * c, sn)), seen)
            seen = lax.fori_loop(2 * pairs, n_wide, body, seen)

            def zbody(c, carry, dst=dst):
                off = pl.multiple_of(c * SEL_WIDE, SEL_WIDE)
                dst[rs, pl.ds(off, SEL_WIDE)] = jnp.zeros((SEL_ROWS, SEL_WIDE), dst.dtype)
                return carry

            lax.fori_loop(n_wide, n_total, zbody, 0)


def _index_scores(qi_ref, w, kb, transposed_keys=False):
    acc = None
    for h in range(IDX_HEADS):
        s = _dot(_mx(qi_ref[h]), kb) if transposed_keys else _dot_nt(_mx(qi_ref[h]), kb)
        t = w[:, h:h + 1] * jnp.maximum(s, 0.0)
        acc = t if acc is None else acc + t
    return acc


def _causal_pairs(nq, tq, tk):
    pairs = [(i, j) for i in range(nq) for j in range((i * tq + tq - 1) // tk + 1)]
    return jnp.asarray([p[0] for p in pairs], I32), jnp.asarray([p[1] for p in pairs], I32)


def _prompt_select_kernel(qb_ref, kb_ref, qi_ref, wi_ref, kidx_ref, mask_ref, i_scr, *, tq, tk, topk):
    p = pl.program_id(0)
    i, j = qb_ref[p], kb_ref[p]
    q_lo = i * tq
    n_wide = (q_lo + tq - 1) // tk + 1

    acc = _index_scores(qi_ref, wi_ref[...], kidx_ref[...])
    cols = pl.ds(pl.multiple_of(j * tk, tk), tk)
    below_diagonal = (j + 1) * tk <= q_lo + 1

    @pl.when(below_diagonal)
    def _():
        i_scr[:, cols] = acc

    @pl.when(jnp.logical_not(below_diagonal))
    def _():
        qpos = q_lo + lax.broadcasted_iota(I32, acc.shape, 0)
        kpos = j * tk + lax.broadcasted_iota(I32, acc.shape, 1)
        i_scr[:, cols] = jnp.where(kpos <= qpos, acc, -jnp.inf)

    @pl.when(j == n_wide - 1)
    def _():
        seg = [(i_scr, n_wide)]
        tau, room = _select_threshold(seg, tq, topk)
        _write_mask([(i_scr, mask_ref, n_wide, mask_ref.shape[1] // tk)], tq, tau, room)


def _prompt_select(qi_hm, wi_f, ki_b, topk, tq, tk):
    s = ki_b.shape[0]
    assert tk == SEL_WIDE and s % tk == 0 and s % tq == 0 and tq % SEL_ROWS == 0
    qb, kb = _causal_pairs(s // tq, tq, tk)
    grid_spec = pltpu.PrefetchScalarGridSpec(
        num_scalar_prefetch=2,
        grid=(qb.shape[0],),
        in_specs=[pl.BlockSpec((IDX_HEADS, tq, IDX_DIM), lambda p, qb, kb: (0, qb[p], 0)),
                  pl.BlockSpec((tq, LANES), lambda p, qb, kb: (qb[p], 0)),
                  pl.BlockSpec((tk, IDX_DIM), lambda p, qb, kb: (kb[p], 0))],
        out_specs=pl.BlockSpec((tq, s), lambda p, qb, kb: (qb[p], 0), pipeline_mode=pl.Buffered(1)),
        scratch_shapes=[pltpu.VMEM((tq, s), F32)],
    )
    return pl.pallas_call(
        functools.partial(_prompt_select_kernel, tq=tq, tk=tk, topk=topk),
        grid_spec=grid_spec,
        out_shape=jax.ShapeDtypeStruct((s, s), MXU_DTYPE),
        compiler_params=_cparams(("arbitrary",)),
        name="prompt_select",
    )(qb, kb, qi_hm, wi_f, ki_b)


def _spread(a, n):
    if n <= LANES:
        return a[:, :n]
    return jnp.concatenate([a] * (n // LANES), axis=1)


def _prompt_attend_kernel(qb_ref, kb_ref, qt_ref, k_ref, vt_ref, mask_ref, o_ref, m_scr, l_scr, acc_scr, *, tq, tk):
    p_id = pl.program_id(0)
    i, j = qb_ref[p_id], kb_ref[p_id]

    @pl.when(j == 0)
    def _():
        m_scr[...] = jnp.full(m_scr.shape, NEG_BIG, F32)
        l_scr[...] = jnp.zeros(l_scr.shape, F32)
        acc_scr[...] = jnp.zeros(acc_scr.shape, F32)

    bias = (1.0 - mask_ref[...].astype(F32).T) * NEG_BIG
    scores = [_dot(k_ref[h], qt_ref[h]) + bias for h in range(N_HEADS_A)]
    for h, s in enumerate(scores):
        m_prev = m_scr[h]
        m_new = jnp.maximum(m_prev, jnp.max(s, axis=0, keepdims=True))
        alpha = jnp.exp(m_prev - m_new)
        p = jnp.exp(s - m_new[0:1])
        l_scr[h] = alpha * l_scr[h] + jnp.sum(p, axis=0, keepdims=True)
        acc_scr[h] = alpha[0:1] * acc_scr[h] + _dot(vt_ref[h], _mx(p))
        m_scr[h] = m_new

    @pl.when(j == (i * tq + tq - 1) // tk)
    def _():
        ot = jnp.concatenate([acc_scr[h] / l_scr[h][0:1] for h in range(N_HEADS_A)], axis=0)
        o_ref[...] = ot.T.astype(o_ref.dtype)


def _prompt_attend(qa_t, ka_hm, va_t, mask, tq, tk):
    nh, dh, s = qa_t.shape
    qb, kb = _causal_pairs(s // tq, tq, tk)
    grid_spec = pltpu.PrefetchScalarGridSpec(
        num_scalar_prefetch=2,
        grid=(qb.shape[0],),
        in_specs=[pl.BlockSpec((nh, dh, tq), lambda p, qb, kb: (0, 0, qb[p])),
                  pl.BlockSpec((nh, tk, dh), lambda p, qb, kb: (0, kb[p], 0)),
                  pl.BlockSpec((nh, dh, tk), lambda p, qb, kb: (0, 0, kb[p])),
                  pl.BlockSpec((tq, tk), lambda p, qb, kb: (qb[p], kb[p]))],
        out_specs=pl.BlockSpec((tq, nh * dh), lambda p, qb, kb: (qb[p], 0)),
        scratch_shapes=[pltpu.VMEM((nh, SUBLANES, tq), F32), pltpu.VMEM((nh, SUBLANES, tq), F32),
                        pltpu.VMEM((nh, dh, tq), F32)],
    )
    return pl.pallas_call(
        functools.partial(_prompt_attend_kernel, tq=tq, tk=tk),
        grid_spec=grid_spec,
        out_shape=jax.ShapeDtypeStruct((s, nh * dh), MXU_DTYPE),
        compiler_params=_cparams(("arbitrary",)),
        name="prompt_attend",
    )(qb, kb, qa_t, ka_hm, va_t, mask)


PAGES_PER_STEP = 32
INDEX_PAGES_PER_STEP = 32


def _sample_index_kernel(pt_ref, qi_ref, wi_ref, *refs, page):
    del pt_ref
    pages, out_ref = refs[:-1], refs[-1]
    w = wi_ref[0]
    q_all = _mx(jnp.concatenate([qi_ref[h] for h in range(IDX_HEADS)], axis=0))
    w_col = jnp.concatenate([w[:, h:h + 1] for h in range(IDX_HEADS)], axis=0)
    for p, kref in enumerate(pages):
        r = w_col * jnp.maximum(_dot(q_all, _mx(kref[0])), 0.0)
        acc = r[0:T_PAD]
        for h in range(1, IDX_HEADS):
            acc = acc + r[h * T_PAD:(h + 1) * T_PAD]
        out_ref[0, :, p * page:(p + 1) * page] = acc


def _sample_index(page_table, qi_hm, wi_f, cache_kidx_t):
    db, n_pages = page_table.shape
    _, idim, page = cache_kidx_t.shape
    pps = math.gcd(INDEX_PAGES_PER_STEP, n_pages)
    nsteps = n_pages // pps
    pt = page_table.reshape(-1).astype(I32)

    def kspec(p):
        return pl.BlockSpec((1, idim, page), lambda b, j, pt: (pt[b * n_pages + j * pps + p], 0, 0))

    grid_spec = pltpu.PrefetchScalarGridSpec(
        num_scalar_prefetch=1,
        grid=(db, nsteps),
        in_specs=[pl.BlockSpec((IDX_HEADS, T_PAD, idim), lambda b, j, pt: (0, b, 0)),
                  pl.BlockSpec((1, T_PAD, LANES), lambda b, j, pt: (b, 0, 0))]
                 + [kspec(p) for p in range(pps)],
        out_specs=pl.BlockSpec((1, T_PAD, pps * page), lambda b, j, pt: (b, 0, j)),
    )
    return pl.pallas_call(
        functools.partial(_sample_index_kernel, page=page),
        grid_spec=grid_spec,
        out_shape=jax.ShapeDtypeStruct((db, T_PAD, n_pages * page), F32),
        compiler_params=_cparams(("parallel", "arbitrary")),
        name="sample_index",
    )(pt, qi_hm, wi_f.reshape(db, T_PAD, LANES), *([cache_kidx_t] * pps))


def _sample_select_kernel(ipast_ref, qi_ref, wi_ref, kin_ref, mpast_ref, mnew_ref, inew_scr, *, t_real, topk):
    rows = ipast_ref.shape[0]
    acc = _index_scores(qi_ref, wi_ref[...], kin_ref[...])
    r = lax.broadcasted_iota(I32, acc.shape, 0)
    c = lax.broadcasted_iota(I32, acc.shape, 1)
    same = (r // T_PAD) == (c // T_PAD)
    tq, tc = r % T_PAD, c % T_PAD
    ok = jnp.logical_and(same, jnp.logical_and(tc <= tq, tc < t_real))
    inew_scr[...] = jnp.full(inew_scr.shape, -jnp.inf, F32)
    inew_scr[:, :rows] = jnp.where(ok, acc, -jnp.inf)
    n_past = ipast_ref.shape[1] // SEL_WIDE
    n_new = inew_scr.shape[1] // SEL_WIDE
    alive = lax.rem(lax.broadcasted_iota(I32, (1, rows), 1), T_PAD) < t_real
    tau, room = _select_threshold([(ipast_ref, n_past), (inew_scr, n_new)], rows, topk, alive)
    _write_mask([(ipast_ref, mpast_ref, n_past, n_past), (inew_scr, mnew_ref, n_new, n_new)], rows, tau, room)


def _sample_select(i_past, qi_hm, wi_f, ki_b, t_real, topk):
    rows, past = i_past.shape
    assert past % SEL_WIDE == 0 and rows % SEL_ROWS == 0
    wnew = -(-rows // SEL_WIDE) * SEL_WIDE
    return pl.pallas_call(
        functools.partial(_sample_select_kernel, t_real=t_real, topk=topk),
        out_shape=[jax.ShapeDtypeStruct((rows, past), F32), jax.ShapeDtypeStruct((rows, wnew), F32)],
        scratch_shapes=[pltpu.VMEM((rows, wnew), F32)],
        compiler_params=pltpu.CompilerParams(vmem_limit_bytes=VMEM_LIMIT),
        name="sample_select",
    )(i_past, qi_hm, wi_f, ki_b)


def _sample_attend_kernel(pt_ref, q_ref, mp_ref, mn_ref, kn_ref, vn_ref, *refs, page, pps):
    del pt_ref
    kpages, vpages = refs[:pps], refs[pps:2 * pps]
    o_ref, m_scr, l_scr, acc_scr = refs[2 * pps:]
    j = pl.program_id(1)
    nj = pl.num_programs(1)
    q = q_ref[0]
    nh = q.shape[0] // T_PAD

    def scores(m_t, kt):
        keep = jnp.concatenate([m_t] * nh, axis=0) > 0
        return jnp.where(keep, _dot(q, kt), NEG_BIG)

    def update(s_list, vt_list):
        m_prev = m_scr[...]
        m_blk = functools.reduce(jnp.maximum, [jnp.max(s, axis=1, keepdims=True) for s in s_list])
        m_new = jnp.maximum(m_prev, m_blk)
        alpha = jnp.exp(m_prev - m_new)
        l_new = alpha * l_scr[...]
        acc = _spread(alpha, acc_scr.shape[1]) * acc_scr[...]
        for s, vt in zip(s_list, vt_list):
            p = jnp.exp(s - _spread(m_new, s.shape[1]))
            l_new = l_new + jnp.sum(p, axis=1, keepdims=True)
            acc = acc + _dot_nt(_mx(p), vt)
        m_scr[...], l_scr[...], acc_scr[...] = m_new, l_new, acc

    @pl.when(j == 0)
    def _():
        m_scr[...] = jnp.full(m_scr.shape, NEG_BIG, F32)
        l_scr[...] = jnp.zeros(l_scr.shape, F32)
        acc_scr[...] = jnp.zeros(acc_scr.shape, F32)

    update([scores(mp_ref[0, :, p * page:(p + 1) * page], _mx(kpages[p][0])) for p in range(pps)],
           [_mx(vpages[p][0]) for p in range(pps)])

    @pl.when(j == nj - 1)
    def _():
        update([scores(mn_ref[0], kn_ref[0])], [vn_ref[0]])
        full = acc_scr[...] / _spread(l_scr[...], acc_scr.shape[1])
        lane = lax.broadcasted_iota(I32, (T_PAD, full.shape[1]), 1)
        out = jnp.zeros((T_PAD, full.shape[1]), F32)
        for h in range(nh):
            out = out + jnp.where((lane // HEAD_DIM_A) == h, full[h * T_PAD:(h + 1) * T_PAD], 0.0)
        o_ref[0] = out


def _sample_attend(page_table, q_bd, m_past, m_new, kt_new, vt_new, cache_kt, cache_vt):
    db, n_pages = page_table.shape
    _, hd, page = cache_kt.shape
    pps = math.gcd(PAGES_PER_STEP, n_pages)
    nsteps = n_pages // pps
    nnew = kt_new.shape[2]
    nq = q_bd.shape[1]
    pt = page_table.reshape(-1).astype(I32)

    def pspec(p):
        return pl.BlockSpec((1, hd, page), lambda b, j, pt: (pt[b * n_pages + j * pps + p], 0, 0))

    bspec = lambda shape: pl.BlockSpec((1,) + shape, lambda b, j, pt: (b, 0, 0))
    grid_spec = pltpu.PrefetchScalarGridSpec(
        num_scalar_prefetch=1,
        grid=(db, nsteps),
        in_specs=[bspec((nq, hd)),
                  pl.BlockSpec((1, T_PAD, pps * page), lambda b, j, pt: (b, 0, j)),
                  bspec((T_PAD, nnew)), bspec((hd, nnew)), bspec((hd, nnew))]
                 + [pspec(p) for p in range(pps)] * 2,
        out_specs=bspec((T_PAD, hd)),
        scratch_shapes=[pltpu.VMEM((nq, LANES), F32), pltpu.VMEM((nq, LANES), F32), pltpu.VMEM((nq, hd), F32)],
    )
    return pl.pallas_call(
        functools.partial(_sample_attend_kernel, page=page, pps=pps),
        grid_spec=grid_spec,
        out_shape=jax.ShapeDtypeStruct((db, T_PAD, hd), F32),
        compiler_params=_cparams(("parallel", "arbitrary")),
        name="sample_attend",
    )(pt, q_bd, m_past, m_new, kt_new, vt_new, *([cache_kt] * pps), *([cache_vt] * pps))


def _retention_tables(c_real, c_pad):
    h = np.arange(N_HEADS_R, dtype=np.float64)
    log_g = np.log1p(-np.exp2(-5.0 - h))
    i = np.arange(c_pad, dtype=np.float64)
    diff = i[:, None] - i[None, :]
    live = (diff >= 0) & (i[:, None] < c_real) & (i[None, :] < c_real)
    inner = np.where(live[None], np.exp(np.maximum(diff, 0.0)[None] * log_g[:, None, None]), 0.0)
    q_dec = np.exp((i + 1.0)[None, :] * log_g[:, None])
    k_dec = np.where(i[None, :] < c_real, np.exp((c_real - 1.0 - i)[None, :] * log_g[:, None]), 0.0)
    c_dec = np.exp(c_real * log_g)
    f = lambda a: jnp.asarray(a, F32)
    return f(inner), f(q_dec[:, :, None]), f(k_dec[:, :, None]), [float(v) for v in c_dec]


def _retention_kernel(q_ref, k_ref, v_ref, g_ref, s0_ref, inner_ref, qdec_ref, kdec_ref,
                      o_ref, s_out_ref, s_scr, *, c_dec):
    j = pl.program_id(1)
    nj = pl.num_programs(1)

    @pl.when(j == 0)
    def _():
        s_scr[...] = s0_ref[0]

    for h in range(N_HEADS_R):
        q = q_ref[:, h * DK_R:(h + 1) * DK_R]
        k = k_ref[:, h * DK_R:(h + 1) * DK_R]
        v = v_ref[:, h * DV_R:(h + 1) * DV_R]
        s_prev = s_scr[h]
        a = _dot_nt(_mx(q), _mx(k)) * inner_ref[h]
        o = _dot(_mx(a), v) + _dot(_mx(q), _mx(s_prev)) * qdec_ref[h]
        kd = k * kdec_ref[h]
        s_scr[h] = s_prev * c_dec[h] + _dot(_mx(kd.T), v)
        mu = jnp.mean(o, axis=-1, keepdims=True)
        var = jnp.mean(jnp.square(o - mu), axis=-1, keepdims=True)
        gn = (o - mu) * lax.rsqrt(var + EPS)
        g = g_ref[:, h * DV_R:(h + 1) * DV_R]
        o_ref[:, h * DV_R:(h + 1) * DV_R] = (gn * (g * _sigmoid(g))).astype(o_ref.dtype)

    @pl.when(j == nj - 1)
    def _():
        s_out_ref[0] = s_scr[...]


def _retention(qr, kr, vr, gr, s0, c_real):
    b = s0.shape[0]
    c = RET_CHUNK
    n = qr.shape[0] // (b * c)
    inner, qdec, kdec, c_dec = _retention_tables(c_real, c)
    wr, wv = N_HEADS_R * DK_R, N_HEADS_R * DV_R
    rmap = lambda bi, j: (bi * n + j, 0)
    full3 = lambda shape: pl.BlockSpec(shape, lambda bi, j: (0, 0, 0))
    return pl.pallas_call(
        functools.partial(_retention_kernel, c_dec=c_dec),
        grid=(b, n),
        in_specs=[pl.BlockSpec((c, wr), rmap), pl.BlockSpec((c, wr), rmap), pl.BlockSpec((c, wv), rmap),
                  pl.BlockSpec((c, wv), rmap),
                  pl.BlockSpec((1, N_HEADS_R, DK_R, DV_R), lambda bi, j: (bi, 0, 0, 0)),
                  full3(inner.shape), full3(qdec.shape), full3(kdec.shape)],
        out_specs=[pl.BlockSpec((c, wv), rmap),
                   pl.BlockSpec((1, N_HEADS_R, DK_R, DV_R), lambda bi, j: (bi, 0, 0, 0))],
        out_shape=[jax.ShapeDtypeStruct((b * n * c, wv), MXU_DTYPE),
                   jax.ShapeDtypeStruct((b, N_HEADS_R, DK_R, DV_R), F32)],
        scratch_shapes=[pltpu.VMEM((N_HEADS_R, DK_R, DV_R), F32)],
        compiler_params=_cparams(("parallel", "arbitrary")),
        name="retention",
    )(qr, kr, vr, gr, s0, inner, qdec, kdec)


def _cross_kernel(q_ref, mk_ref, mv_ref, o_ref):
    scale = HEAD_DIM_M ** -0.5
    for h in range(N_HEADS_M):
        sl = slice(h * HEAD_DIM_M, (h + 1) * HEAD_DIM_M)
        s = _dot_nt(_mx(q_ref[:, sl]), _mx(mk_ref[0, :, sl])) * scale
        p = jnp.exp(s - jnp.max(s, axis=1, keepdims=True))
        p = p / jnp.sum(p, axis=1, keepdims=True)
        o_ref[:, sl] = _dot(_mx(p), _mx(mv_ref[0, :, sl])).astype(o_ref.dtype)


def _cross_attend(qm, mk, mv, tm, out_dtype):
    b, n_mem, hd = mk.shape
    nt = qm.shape[0] // (b * tm)
    return pl.pallas_call(
        _cross_kernel,
        grid=(b, nt),
        in_specs=[pl.BlockSpec((tm, hd), lambda bi, i: (bi * nt + i, 0)),
                  pl.BlockSpec((1, n_mem, hd), lambda bi, i: (bi, 0, 0)),
                  pl.BlockSpec((1, n_mem, hd), lambda bi, i: (bi, 0, 0))],
        out_specs=pl.BlockSpec((tm, hd), lambda bi, i: (bi * nt + i, 0)),
        out_shape=jax.ShapeDtypeStruct(qm.shape, out_dtype),
        compiler_params=_cparams(("parallel", "parallel")),
        name="cross_attend",
    )(qm, mk, mv)


def _memkv_kernel(x_ref, g_ref, w_ref, o_ref):
    o_ref[...] = _dot(_mx(_rms(x_ref[...], g_ref[...])), w_ref[...])


def _memory_kv(mem, gain, w):
    rows, d = mem.shape
    return pl.pallas_call(
        _memkv_kernel,
        out_shape=jax.ShapeDtypeStruct((rows, w.shape[1]), F32),
        compiler_params=pltpu.CompilerParams(vmem_limit_bytes=VMEM_LIMIT),
        name="memory_kv",
    )(mem, gain.reshape(1, d), _mx(w))


def _merge_kernel(x_ref, oa_ref, or_ref, om_ref, gates_ref, wpa_ref, wpb_ref, wpc_ref, wo_ref, g_ref, h_ref):
    d = x_ref.shape[1]
    gt = gates_ref[...]
    mixed = (_sigmoid(gt[:, :d]) * _dot(_mx(oa_ref[...]), wpa_ref[...])
             + _sigmoid(gt[:, d:2 * d]) * _dot(_mx(or_ref[...]), wpb_ref[...])
             + _sigmoid(gt[:, 2 * d:]) * _dot(_mx(om_ref[...]), wpc_ref[...]))
    z = _dot(_mx(mixed), wo_ref[...])
    h_ref[...] = x_ref[...] + _rms(z, g_ref[...])


def _merge(x, oa, o_r, om, gates, wpa, wpb, wpc, wo, gain, tm):
    rows, d = x.shape
    return pl.pallas_call(
        _merge_kernel,
        grid=(rows // tm,),
        in_specs=[_row_spec(tm, d), _row_spec(tm, oa.shape[1]), _row_spec(tm, o_r.shape[1]),
                  _row_spec(tm, om.shape[1]), _row_spec(tm, 3 * d),
                  _full_spec(wpa.shape), _full_spec(wpb.shape), _full_spec(wpc.shape), _full_spec(wo.shape),
                  _full_spec((1, d))],
        out_specs=_row_spec(tm, d),
        out_shape=jax.ShapeDtypeStruct((rows, d), F32),
        compiler_params=_cparams(("parallel",)),
        name="merge",
    )(x, oa, o_r, om, gates, wpa, wpb, wpc, wo, gain.reshape(1, d))


HALO = BF16_ROWS


def _ffn_kernel(h_ref, halo_ref, s0_ref, s1_ref, g1_ref, g2_ref, wu_ref, wg_ref, cw_ref, cb_ref, wd_ref,
                y_ref, utail_ref, x_scr, u_scr, *, tm, seq, keep):
    i = pl.program_id(0)
    h = h_ref[...]
    hn = _rms(h, g1_ref[...])
    x_scr[HALO:, :] = _mx(hn)
    x_scr[:HALO, :] = _mx(_rms(halo_ref[...], g1_ref[...]))
    xc = x_scr[...]
    u_scr[...] = _dot(xc, wu_ref[...])
    gate = _dot(xc[HALO:], wg_ref[...])
    cur = u_scr[HALO:, :]
    prev1 = u_scr[HALO - 1:HALO - 1 + tm, :]
    prev2 = u_scr[HALO - 2:HALO - 2 + tm, :]
    seq_loc = min(seq, tm)
    t = lax.rem(lax.broadcasted_iota(I32, (tm, 1), 0), seq_loc)
    t = jnp.where(lax.rem(i * tm, seq) == 0, t, CONV_W)
    st0, st1 = s0_ref[...], s1_ref[...]
    if st0.shape[0] != tm:
        st0, st1 = st0[0:1], st1[0:1]
    prev1 = jnp.where(t == 0, st1, prev1)
    prev2 = jnp.where(t == 0, st0, jnp.where(t == 1, st1, prev2))
    c = cb_ref[...] + prev2 * cw_ref[0:1, :] + prev1 * cw_ref[1:2, :] + cur * cw_ref[2:3, :]
    act = jax.nn.gelu(c, approximate=True) * gate
    ff = _dot(_mx(act), wd_ref[...])
    y_ref[...] = h + _rms(ff, g2_ref[...])
    utail_ref[...] = u_scr[HALO + tm - keep:, :]


def _conv_ffn(h, s0e, s1e, g1, g2, wu, wg, cw, cb, wd, tm, seq, keep):
    rows, d = h.shape
    f = wu.shape[1]
    nt = rows // tm
    hb = tm // HALO
    sr = s0e.shape[0]
    return pl.pallas_call(
        functools.partial(_ffn_kernel, tm=tm, seq=seq, keep=keep),
        grid=(nt,),
        in_specs=[_row_spec(tm, d),
                  pl.BlockSpec((HALO, d), lambda i: (jnp.maximum(i * hb - 1, 0), 0)),
                  _full_spec((sr, f)), _full_spec((sr, f)),
                  _full_spec((1, d)), _full_spec((1, d)),
                  _full_spec(wu.shape), _full_spec(wg.shape), _full_spec(cw.shape), _full_spec((1, f)),
                  _full_spec(wd.shape)],
        out_specs=[_row_spec(tm, d), _row_spec(keep, f)],
        out_shape=[jax.ShapeDtypeStruct((rows, d), F32), jax.ShapeDtypeStruct((nt * keep, f), F32)],
        scratch_shapes=[pltpu.VMEM((tm + HALO, d), MXU_DTYPE), pltpu.VMEM((tm + HALO, f), F32)],
        compiler_params=_cparams(("parallel",)),
        name="conv_ffn",
    )(h, h, s0e, s1e, g1.reshape(1, d), g2.reshape(1, d), wu, wg, cw, cb.reshape(1, f), wd)


def _layer_weights(l, w_in, w_proj_a, w_proj_b, w_proj_c, w_out, w_up, w_down):
    wts = _prep_in_weights(w_in[l])
    f = w_down.shape[1]
    wts.update(wpa=_mx(w_proj_a[l]), wpb=_mx(w_proj_b[l]), wpc=_mx(w_proj_c[l]), wo=_mx(w_out[l]),
               wu=_mx(w_up[l][:, :f]), wg=_mx(w_up[l][:, f:]), wd=_mx(w_down[l]))
    return wts


def _prompt_layer(x, mem, wts, norms, conv_w, conv_b, w_mem_kv, tiles):
    s, d = x.shape
    f = wts["wd"].shape[0]
    pos = jnp.arange(s)
    pr = _projections(x, norms["pre_mix"], pos, wts, tiles["proj"])
    topk = min(TOPK_MAX, s // 4)
    mask = _prompt_select(pr["qi_hm"], pr["wi_f"], pr["ki_b"], topk, tiles["sel_q"], SEL_WIDE)
    o_a = _prompt_attend(pr["qa_t"], pr["ka_hm"], pr["va_t"], mask, tiles["att_q"], tiles["att_k"])
    s0 = jnp.zeros((1, N_HEADS_R, DK_R, DV_R), F32)
    o_r, ret_new = _retention(pr["qr_f"], pr["kr_f"], pr["vr_b"], pr["gr_f"], s0, RET_CHUNK)
    kv = _memory_kv(mem, norms["mem"], w_mem_kv)
    wm = N_HEADS_M * HEAD_DIM_M
    mk, mv = kv[:, :wm], kv[:, wm:]
    o_m = _cross_attend(pr["qm_b"], mk[None], mv[None], tiles["cross"], MXU_DTYPE)
    h = _merge(x, o_a, o_r, o_m, pr["gates_f"], wts["wpa"], wts["wpb"], wts["wpc"], wts["wo"],
               norms["post_mix"], tiles["merge"])
    zst = jnp.zeros((SUBLANES, f), F32)
    y, utail = _conv_ffn(h, zst, zst, norms["pre_ffn"], norms["post_ffn"], wts["wu"], wts["wg"], conv_w, conv_b,
                         wts["wd"], tiles["ffn"], s, SUBLANES)
    conv_new = utail[-(CONV_W - 1):]
    return y, pr["ka_f"], pr["va_f"], pr["ki_f"], ret_new, conv_new, mk, mv


def _sample_layer(x, wts, norms, conv_w, conv_b, cache_k, cache_v, cache_kidx, mem_k, mem_v,
                  state_ret, state_conv, page_table):
    db, t, d = x.shape
    f = wts["wd"].shape[0]
    n_pages = page_table.shape[1]
    page = cache_k.shape[1]
    past = n_pages * page
    rows = db * T_PAD
    xp = jnp.pad(x, ((0, 0), (0, T_PAD - t), (0, 0))).reshape(rows, d)
    pos = jnp.tile(past + jnp.arange(T_PAD), db)
    pr = _projections(xp, norms["pre_mix"], pos, wts, rows)
    hd = N_HEADS_A * HEAD_DIM_A

    topk = min(TOPK_MAX, (past + t) // 4)
    i_past = _sample_index(page_table, pr["qi_hm"].astype(F32), pr["wi_f"], cache_kidx.transpose(0, 2, 1)).reshape(rows, past)
    m_past, m_new = _sample_select(i_past, pr["qi_hm"], pr["wi_f"], pr["ki_b"], t, topk)
    own = m_new[:, :rows].reshape(db, T_PAD, db, T_PAD)[jnp.arange(db), :, jnp.arange(db), :]
    m_new_own = jnp.pad(own, ((0, 0), (0, 0), (0, LANES - T_PAD)))
    q_rows = pr["qa_hm"].reshape(N_HEADS_A, db, T_PAD, HEAD_DIM_A)
    eye = jnp.eye(N_HEADS_A, dtype=MXU_DTYPE)
    q_bd = jnp.einsum("hbtd,hg->bhtgd", q_rows, eye).reshape(db, N_HEADS_A * T_PAD, hd)
    new_t = lambda a: jnp.pad(_mx(a).reshape(db, T_PAD, hd).transpose(0, 2, 1), ((0, 0), (0, 0), (0, LANES - T_PAD)))
    paged_t = lambda c: c.transpose(0, 2, 3, 1).reshape(c.shape[0], hd, page)
    o_a = _sample_attend(page_table, q_bd, m_past.reshape(db, T_PAD, past), m_new_own,
                         new_t(pr["ka_f"]), new_t(pr["va_f"]), paged_t(cache_k), paged_t(cache_v)).reshape(rows, hd)

    padc = lambda a: jnp.pad(a.reshape(db, T_PAD, -1), ((0, 0), (0, RET_CHUNK - T_PAD), (0, 0))).reshape(db * RET_CHUNK, -1)
    o_r, ret_new = _retention(padc(pr["qr_f"]), padc(pr["kr_f"]), padc(pr["vr_b"]), padc(pr["gr_f"]), state_ret, t)
    o_r = o_r.reshape(db, RET_CHUNK, -1)[:, :T_PAD].reshape(rows, -1)

    wm = N_HEADS_M * HEAD_DIM_M
    o_m = _cross_attend(pr["qm_b"].astype(F32), mem_k.reshape(db, -1, wm), mem_v.reshape(db, -1, wm), T_PAD, F32)

    h = _merge(xp, o_a, o_r, o_m, pr["gates_f"], wts["wpa"], wts["wpb"], wts["wpc"], wts["wo"],
               norms["post_mix"], rows)
    s0e = jnp.repeat(state_conv[:, 0], T_PAD, axis=0)
    s1e = jnp.repeat(state_conv[:, 1], T_PAD, axis=0)
    y, u_all = _conv_ffn(h, s0e, s1e, norms["pre_ffn"], norms["post_ffn"], wts["wu"], wts["wg"], conv_w, conv_b,
                         wts["wd"], rows, T_PAD, rows)
    ext = jnp.concatenate([state_conv.astype(F32), u_all.reshape(db, T_PAD, f)[:, :t]], axis=1)
    conv_new = ext[:, t:]
    unpad = lambda a: a.reshape(db, T_PAD, -1)[:, :t]
    return (unpad(y), unpad(pr["ka_f"]), unpad(pr["va_f"]), unpad(pr["ki_f"]), ret_new, conv_new)


PROMPT_TILES = dict(proj=512, sel_q=512, att_q=512, att_k=1024, cross=512, merge=512, ffn=256)


def kernel(x_prompt, x_sample, cache_k, cache_v, cache_kidx, cache_mem_k, cache_mem_v, state_ret, state_conv,
           page_table, mem_prompt, norm_pre_mix, norm_post_mix, norm_pre_ffn, norm_post_ffn, norm_mem,
           w_in, w_mem_kv, w_proj_a, w_proj_b, w_proj_c, w_out, w_up, conv_w, conv_b, w_down):
    bp, s, d = x_prompt.shape
    db, t, _ = x_sample.shape
    depth = w_in.shape[0]
    assert bp == 1 and t <= T_PAD and CONV_W - 1 <= t
    tiles = {k: min(v, s) for k, v in PROMPT_TILES.items()}
    yp, ys = x_prompt[0], x_sample
    outs = [[] for _ in range(12)]
    for l in range(depth):
        wts = _layer_weights(l, w_in, w_proj_a, w_proj_b, w_proj_c, w_out, w_up, w_down)
        norms = dict(pre_mix=norm_pre_mix[l], post_mix=norm_post_mix[l], pre_ffn=norm_pre_ffn[l],
                     post_ffn=norm_post_ffn[l], mem=norm_mem[l])
        yp, kp, vp, kip, rp, cp, mk, mv = _prompt_layer(yp, mem_prompt[0], wts, norms, conv_w[l], conv_b[l],
                                                        w_mem_kv[l], tiles)
        ys, ks, vs, kis, rs, cs = _sample_layer(ys, wts, norms, conv_w[l], conv_b[l], cache_k[l], cache_v[l],
                                                cache_kidx[l], cache_mem_k[l], cache_mem_v[l], state_ret[l],
                                                state_conv[l], page_table)
        n_mem = mk.shape[0]
        vals = (kp.reshape(1, s, N_HEADS_A, HEAD_DIM_A), vp.reshape(1, s, N_HEADS_A, HEAD_DIM_A),
                kip.reshape(1, s, IDX_DIM), rp, cp[None],
                mk.reshape(1, n_mem, N_HEADS_M, HEAD_DIM_M), mv.reshape(1, n_mem, N_HEADS_M, HEAD_DIM_M),
                ks.reshape(db, t, N_HEADS_A, HEAD_DIM_A), vs.reshape(db, t, N_HEADS_A, HEAD_DIM_A),
                kis, rs, cs)
        for o, v in zip(outs, vals):
            o.append(v)
    stacked = [jnp.stack(o) for o in outs]
    return (yp[None], ys, *stacked)
```

```python
import functools
import math

import numpy as np
import jax
import jax.numpy as jnp
from jax import lax
from jax.experimental import pallas as pl
from jax.experimental.pallas import tpu as pltpu

F32 = jnp.float32
I32 = jnp.int32
MXU_DTYPE = jnp.bfloat16

N_HEADS_A, HEAD_DIM_A = 8, 64
IDX_HEADS, IDX_DIM = 4, 64
TOPK_MAX = 256
N_HEADS_R, DK_R, DV_R = 4, 128, 256
RET_CHUNK = 128
N_HEADS_M, HEAD_DIM_M = 4, 128
CONV_W = 3
ROPE_THETA = 10000.0
EPS = 1e-6

LANES = 128
SUBLANES = 8
BF16_ROWS = 16
VMEM_LIMIT = 56 * 1024 * 1024
NEG_BIG = -1e30
F32_LOWEST = float(np.finfo(np.float32).min)
T_PAD = 8


def _cparams(sem):
    return pltpu.CompilerParams(dimension_semantics=sem, vmem_limit_bytes=VMEM_LIMIT)


def _dot(a, b):
    return jnp.dot(a, b, preferred_element_type=F32)


def _dot_nt(a, b):
    return lax.dot_general(a, b, (((1,), (1,)), ((), ())), preferred_element_type=F32)


def _mx(a):
    return a.astype(MXU_DTYPE)


def _rms(x, g):
    return x * lax.rsqrt(jnp.mean(x * x, axis=-1, keepdims=True) + EPS) * g


def _sigmoid(x):
    return 1.0 / (1.0 + jnp.exp(-x))


def _rope_tables(pos, d):
    half = d // 2
    inv = 1.0 / (ROPE_THETA ** (jnp.arange(half, dtype=F32) * 2.0 / d))
    ang = pos.astype(F32)[:, None] * inv[None, :]
    cos, sin = jnp.cos(ang), jnp.sin(ang)
    reps = LANES // d
    cos_t = jnp.tile(jnp.concatenate([cos, cos], axis=1), (1, reps))
    sin_t = jnp.tile(jnp.concatenate([-sin, sin], axis=1), (1, reps))
    return cos_t, sin_t


def _rope(y, cos, sin, d):
    w = y.shape[1]
    half = d // 2
    reps = w // LANES
    c = jnp.concatenate([cos] * reps, axis=1) if reps > 1 else cos
    s = jnp.concatenate([sin] * reps, axis=1) if reps > 1 else sin
    lane = lax.broadcasted_iota(I32, y.shape, 1)
    first = (lane & (d - 1)) < half
    rot = jnp.where(first, pltpu.roll(y, w - half, 1), pltpu.roll(y, half, 1))
    return y * c + rot * s


def _proj_a_kernel(x_ref, g_ref, cos_ref, sin_ref, wa_ref, wv_ref,
                   qa_hm, qa_t, ka_f, ka_hm, qi_hm, ki_f, ki_b, va_f, va_t, wi_f):
    xn = _mx(_rms(x_ref[...], g_ref[...]))
    wa = N_HEADS_A * HEAD_DIM_A
    y = _rope(_dot(xn, wa_ref[...]), cos_ref[...], sin_ref[...], HEAD_DIM_A)
    q = y[:, :wa] * (HEAD_DIM_A ** -0.5)
    k = y[:, wa:2 * wa]
    ka_f[...] = k
    qt = q.T
    for h in range(N_HEADS_A):
        sl = slice(h * HEAD_DIM_A, (h + 1) * HEAD_DIM_A)
        qa_hm[h] = _mx(q[:, sl])
        qa_t[h] = _mx(qt[sl, :])
        ka_hm[h] = _mx(k[:, sl])
    qi = y[:, 2 * wa:2 * wa + IDX_HEADS * IDX_DIM]
    for h in range(IDX_HEADS):
        qi_hm[h] = _mx(qi[:, h * IDX_DIM:(h + 1) * IDX_DIM])
    ki = y[:, 2 * wa + IDX_HEADS * IDX_DIM:2 * wa + IDX_HEADS * IDX_DIM + IDX_DIM]
    ki_f[...] = ki
    ki_b[...] = _mx(ki)
    z = _dot(xn, wv_ref[...])
    v = z[:, :wa]
    va_f[...] = v
    vt = v.T
    for h in range(N_HEADS_A):
        va_t[h] = _mx(vt[h * HEAD_DIM_A:(h + 1) * HEAD_DIM_A, :])
    wi_f[...] = z[:, wa:wa + LANES]


def _proj_b_kernel(x_ref, g_ref, cos_ref, sin_ref, wr_ref, wvg_ref, qr_f, kr_f, vr_b, gr_f):
    xn = _mx(_rms(x_ref[...], g_ref[...]))
    wr = N_HEADS_R * DK_R
    y = _rope(_dot(xn, wr_ref[...]), cos_ref[...], sin_ref[...], DK_R)
    qr_f[...] = y[:, :wr]
    kr_f[...] = y[:, wr:] * (DK_R ** -0.5)
    z = _dot(xn, wvg_ref[...])
    wv = N_HEADS_R * DV_R
    vr_b[...] = _mx(z[:, :wv])
    gr_f[...] = z[:, wv:]


def _proj_c_kernel(x_ref, g_ref, wc_ref, qm_b, gates_f):
    xn = _mx(_rms(x_ref[...], g_ref[...]))
    z = _dot(xn, wc_ref[...])
    wm = N_HEADS_M * HEAD_DIM_M
    qm_b[...] = _mx(z[:, :wm])
    gates_f[...] = z[:, wm:]


def _row_spec(tm, w):
    return pl.BlockSpec((tm, w), lambda i: (i, 0))


def _full_spec(shape):
    nd = len(shape)
    return pl.BlockSpec(shape, lambda i: (0,) * nd)


def _hm_spec(nh, tm, d):
    return pl.BlockSpec((nh, tm, d), lambda i: (0, i, 0))


def _projections(x, gain, pos, wts, tm):
    rows, d = x.shape
    grid = (rows // tm,)
    wa = N_HEADS_A * HEAD_DIM_A
    cos64, sin64 = _rope_tables(pos, HEAD_DIM_A)
    cos128, sin128 = _rope_tables(pos, DK_R)
    g2 = gain.reshape(1, d)
    sds = jax.ShapeDtypeStruct
    t_spec = pl.BlockSpec((N_HEADS_A, HEAD_DIM_A, tm), lambda i: (0, 0, i))
    outs_a = pl.pallas_call(
        _proj_a_kernel,
        grid=grid,
        in_specs=[_row_spec(tm, d), _full_spec((1, d)), _row_spec(tm, LANES), _row_spec(tm, LANES),
                  _full_spec(wts["wa"].shape), _full_spec(wts["wv"].shape)],
        out_specs=[_hm_spec(N_HEADS_A, tm, HEAD_DIM_A), t_spec, _row_spec(tm, wa),
                   _hm_spec(N_HEADS_A, tm, HEAD_DIM_A),
                   _hm_spec(IDX_HEADS, tm, IDX_DIM), _row_spec(tm, IDX_DIM), _row_spec(tm, IDX_DIM),
                   _row_spec(tm, wa), t_spec, _row_spec(tm, LANES)],
        out_shape=[sds((N_HEADS_A, rows, HEAD_DIM_A), MXU_DTYPE), sds((N_HEADS_A, HEAD_DIM_A, rows), MXU_DTYPE),
                   sds((rows, wa), F32),
                   sds((N_HEADS_A, rows, HEAD_DIM_A), MXU_DTYPE), sds((IDX_HEADS, rows, IDX_DIM), MXU_DTYPE),
                   sds((rows, IDX_DIM), F32), sds((rows, IDX_DIM), MXU_DTYPE),
                   sds((rows, wa), F32), sds((N_HEADS_A, HEAD_DIM_A, rows), MXU_DTYPE), sds((rows, LANES), F32)],
        compiler_params=_cparams(("parallel",)),
        name="proj_a",
    )(x, g2, cos64, sin64, wts["wa"], wts["wv"])
    names_a = ("qa_hm", "qa_t", "ka_f", "ka_hm", "qi_hm", "ki_f", "ki_b", "va_f", "va_t", "wi_f")
    wr, wv = N_HEADS_R * DK_R, N_HEADS_R * DV_R
    outs_b = pl.pallas_call(
        _proj_b_kernel,
        grid=grid,
        in_specs=[_row_spec(tm, d), _full_spec((1, d)), _row_spec(tm, LANES), _row_spec(tm, LANES),
                  _full_spec(wts["wr"].shape), _full_spec(wts["wvg"].shape)],
        out_specs=[_row_spec(tm, wr), _row_spec(tm, wr), _row_spec(tm, wv), _row_spec(tm, wv)],
        out_shape=[sds((rows, wr), F32), sds((rows, wr), F32), sds((rows, wv), MXU_DTYPE), sds((rows, wv), F32)],
        compiler_params=_cparams(("parallel",)),
        name="proj_b",
    )(x, g2, cos128, sin128, wts["wr"], wts["wvg"])
    names_b = ("qr_f", "kr_f", "vr_b", "gr_f")
    wm = N_HEADS_M * HEAD_DIM_M
    outs_c = pl.pallas_call(
        _proj_c_kernel,
        grid=grid,
        in_specs=[_row_spec(tm, d), _full_spec((1, d)), _full_spec(wts["wc"].shape)],
        out_specs=[_row_spec(tm, wm), _row_spec(tm, 3 * d)],
        out_shape=[sds((rows, wm), MXU_DTYPE), sds((rows, 3 * d), F32)],
        compiler_params=_cparams(("parallel",)),
        name="proj_c",
    )(x, g2, wts["wc"])
    names_c = ("qm_b", "gates_f")
    out = dict(zip(names_a, outs_a))
    out.update(zip(names_b, outs_b))
    out.update(zip(names_c, outs_c))
    return out


def _prep_in_weights(w_in):
    d = w_in.shape[0]
    wa = N_HEADS_A * HEAD_DIM_A
    widths = (wa, wa, wa, IDX_HEADS * IDX_DIM, IDX_DIM, IDX_HEADS,
              N_HEADS_R * DK_R, N_HEADS_R * DK_R, N_HEADS_R * DV_R, N_HEADS_R * DV_R,
              N_HEADS_M * HEAD_DIM_M, 3 * d)
    offs = np.concatenate([[0], np.cumsum(widths)])
    seg = [w_in[:, int(offs[i]):int(offs[i + 1])] for i in range(len(widths))]
    q_a, k_a, v_a, q_i, k_i, w_i, q_r, k_r, v_r, g_r, q_m, gates = seg
    zpad = lambda n: jnp.zeros((d, n), w_in.dtype)
    return {
        "wa": _mx(jnp.concatenate([q_a, k_a, q_i, k_i, zpad(LANES - IDX_DIM)], axis=1)),
        "wv": _mx(jnp.concatenate([v_a, w_i, zpad(LANES - IDX_HEADS)], axis=1)),
        "wr": _mx(jnp.concatenate([q_r, k_r], axis=1)),
        "wvg": _mx(jnp.concatenate([v_r, g_r], axis=1)),
        "wc": _mx(jnp.concatenate([q_m, gates], axis=1)),
    }


SEL_ROWS = 128
SEL_WIDE = 512
SWEEP_UNROLL = 2
SEL_GROUPS = 2 * LANES
NO_LIMIT = 2 ** 30
MIN_NORMAL_KEY = 0x00800000
SEARCH_INTERP_STEPS = 8
SEARCH_CAP = SEARCH_INTERP_STEPS + 33


def _key_to_f32(key):
    bits = jnp.where(key >= 0, key, key ^ jnp.int32(0x7FFFFFFF))
    return pltpu.bitcast(bits, F32)


def _f32_to_key(f):
    bits = pltpu.bitcast(f, I32)
    return jnp.where(f == 0.0, 0, jnp.where(bits >= 0, bits, bits ^ jnp.int32(0x7FFFFFFF)))


def _sweep(segments, rs, init, fn):
    acc = init
    base = 0
    for ref, n_wide in segments:
        def body(c, a, ref=ref, base=base):
            off = c * SEL_WIDE
            for k in range(SEL_WIDE // LANES):
                x = ref[rs, pl.ds(pl.multiple_of(off + k * LANES, LANES), LANES)]
                a = fn(a, x, base + off + k * LANES, k)
            return a
        def multi(c, a, body=body):
            for u in range(SWEEP_UNROLL):
                a = body(SWEEP_UNROLL * c + u, a)
            return a
        full = n_wide // SWEEP_UNROLL
        acc = lax.fori_loop(0, full, multi, acc)
        acc = lax.fori_loop(SWEEP_UNROLL * full, n_wide, body, acc)
        base = base + n_wide * SEL_WIDE
    return acc


def _count(segments, rows, make_pred, wanted=None):
    outs = []
    for g in range(rows // SEL_ROWS):
        def group_count(_, g=g):
            rs = slice(g * SEL_ROWS, (g + 1) * SEL_ROWS)
            pred = make_pred(rs)
            acc = _sweep(segments, rs, jnp.zeros((SEL_ROWS, LANES), F32),
                         lambda a, x, idx0, k: a + jnp.where(pred(x, idx0), 1.0, 0.0))
            return jnp.sum(acc.T, axis=0, keepdims=True)
        if wanted is None:
            outs.append(group_count(0))
        else:
            outs.append(lax.cond(wanted[g] > 0, group_count, lambda _: jnp.zeros((1, SEL_ROWS), F32), 0))
    return jnp.concatenate(outs, axis=1) if len(outs) > 1 else outs[0]


def _col_to_row(col):
    return jnp.broadcast_to(col, (col.shape[0], LANES)).T[0:1]


def _row_to_lanes(vec):
    return jnp.broadcast_to(vec, (LANES, vec.shape[1])).T


def _search(count_fn, lo, hi, c_lo, c_hi, target, alive):
    def unfinished(lo, hi, c_lo):
        return jnp.logical_and(alive, jnp.logical_and(c_lo > target, lo + 1 < hi))

    def any_row(flag):
        return jnp.max(jnp.where(flag, 1, 0).astype(I32))

    def cond(carry):
        return jnp.logical_and(carry[0] < SEARCH_CAP, carry[1] > 0)

    def body(carry):
        it, _, lo, hi, c_lo, c_hi, w_lo, w_hi, last = carry
        act = unfinished(lo, hi, c_lo)
        bis = (lo >> 1) + (hi >> 1) + (lo & hi & 1)
        width = hi - lo
        a = (c_lo - target + 0.5) * w_lo
        b = (target - 0.5 - c_hi) * w_hi
        frac = a / jnp.maximum(a + b, 1e-6)
        step = (frac * width.astype(F32)).astype(I32)
        itp = lo + jnp.clip(step, 1, jnp.maximum(width - 1, 1))
        use_itp = jnp.logical_and((lo ^ hi) >= 0, (jnp.zeros_like(lo) + it) < SEARCH_INTERP_STEPS)
        v = jnp.where(use_itp, itp, bis)
        n_groups = lo.shape[1] // SEL_ROWS
        cnt = count_fn(v, [any_row(act[:, g * SEL_ROWS:(g + 1) * SEL_ROWS]) for g in range(n_groups)])
        up = jnp.logical_and(act, cnt >= target)
        dn = jnp.logical_and(act, cnt < target)
        lo = jnp.where(up, v, lo)
        c_lo = jnp.where(up, cnt, c_lo)
        hi = jnp.where(dn, v, hi)
        c_hi = jnp.where(dn, cnt, c_hi)
        w_hi = jnp.where(up, jnp.where(last == 1, w_hi * 0.5, 1.0), jnp.where(dn, 1.0, w_hi))
        w_lo = jnp.where(dn, jnp.where(last == -1, w_lo * 0.5, 1.0), jnp.where(up, 1.0, w_lo))
        last = jnp.where(up, 1, jnp.where(dn, -1, last))
        return it + 1, any_row(unfinished(lo, hi, c_lo)), lo, hi, c_lo, c_hi, w_lo, w_hi, last

    go = any_row(unfinished(lo, hi, c_lo))
    one = jnp.ones(lo.shape, F32)
    out = lax.while_loop(cond, body, (jnp.int32(0), go, lo, hi, c_lo, c_hi, one, one, jnp.zeros(lo.shape, I32)))
    return out[2], out[4], out[5]


def _select_threshold(segments, rows, topk, alive=None):
    assert topk <= SEL_GROUPS
    kf = jnp.full((1, rows), float(topk), F32)
    if alive is None:
        alive = jnp.full((1, rows), True)

    los, his = [], []
    for g in range(rows // SEL_ROWS):
        rs = slice(g * SEL_ROWS, (g + 1) * SEL_ROWS)
        ninf = jnp.full((SEL_ROWS, LANES), -jnp.inf, F32)
        ga, gb = _sweep(segments, rs, (ninf, ninf),
                        lambda a, x, idx0, k: ((jnp.maximum(a[0], x), a[1]) if k % 2 == 0
                                               else (a[0], jnp.maximum(a[1], x))))
        los.append(jnp.min(jnp.minimum(ga, gb), axis=1, keepdims=True))
        his.append(jnp.max(jnp.maximum(ga, gb), axis=1, keepdims=True))
    cat = lambda xs: _col_to_row(jnp.concatenate(xs, axis=0) if len(xs) > 1 else xs[0])
    lo = _f32_to_key(jnp.maximum(cat(los), F32_LOWEST))
    hi = _f32_to_key(cat(his)) + 1

    def count_ge_f(thr, wanted=None):
        thr_l = _row_to_lanes(thr)

        def make_pred(rs):
            t = thr_l[rs]
            return lambda x, idx0: x >= t
        return _count(segments, rows, make_pred, wanted)

    count_ge = lambda v, wanted=None: count_ge_f(_key_to_f32(v), wanted)
    c_lo = count_ge(lo)
    c_hi = jnp.zeros((1, rows), F32)
    def zero_probes(state):
        lo, hi, c_lo, c_hi = state
        for probe in (0, MIN_NORMAL_KEY):
            v = jnp.full((1, rows), probe, I32)
            cnt = count_ge_f(jnp.full((1, rows), np.int32(probe).view(np.float32), F32))
            inside = jnp.logical_and(lo < v, v < hi)
            up = jnp.logical_and(inside, cnt >= kf)
            dn = jnp.logical_and(inside, cnt < kf)
            lo, c_lo = jnp.where(up, v, lo), jnp.where(up, cnt, c_lo)
            hi, c_hi = jnp.where(dn, v, hi), jnp.where(dn, cnt, c_hi)
        return lo, hi, c_lo, c_hi

    straddles = jnp.logical_or(jnp.logical_and(lo < 0, hi > 0),
                               jnp.logical_and(lo < MIN_NORMAL_KEY, hi > MIN_NORMAL_KEY))
    lo, hi, c_lo, c_hi = lax.cond(jnp.max(jnp.where(straddles, 1, 0).astype(I32)) > 0,
                                  zero_probes, lambda s: s, (lo, hi, c_lo, c_hi))
    at_zero = jnp.logical_and(lo == 0, hi == MIN_NORMAL_KEY)
    lo, c_lo, c_hi = _search(count_ge, lo, hi, c_lo, c_hi, kf, jnp.logical_and(alive, jnp.logical_not(at_zero)))
    tau = _key_to_f32(lo)

    need = jnp.logical_and(alive, c_lo > kf)
    room = jnp.where(need, kf - c_hi, float(NO_LIMIT))
    return _row_to_lanes(tau), _row_to_lanes(room)


def _write_mask(segments_out, rows, tau, room):
    r_io = lax.broadcasted_iota(I32, (LANES, 2 * LANES), 0)
    c_io = lax.broadcasted_iota(I32, (LANES, 2 * LANES), 1)
    tri = jnp.where(jnp.logical_or(r_io <= c_io, c_io >= LANES), 1.0, 0.0).astype(MXU_DTYPE)
    for g in range(rows // SEL_ROWS):
        rs = slice(g * SEL_ROWS, (g + 1) * SEL_ROWS)
        t, rm = tau[rs], room[rs]
        seen = jnp.zeros((SEL_ROWS, LANES), F32)
        for src, dst, n_wide, n_total in segments_out:
            def body(c, seen, src=src, dst=dst):
                off = c * SEL_WIDE
                nsub = SEL_WIDE // LANES
                dss = [pl.ds(pl.multiple_of(off + k * LANES, LANES), LANES) for k in range(nsub)]
                xs = [src[rs, ds] for ds in dss]
                eqs = [jnp.where(x == t, 1.0, 0.0) for x in xs]
                run = jnp.dot(jnp.concatenate(eqs, axis=0).astype(MXU_DTYPE), tri, preferred_element_type=F32)
                for k in range(nsub):
                    run_k = run[k * SEL_ROWS:(k + 1) * SEL_ROWS]
                    before = seen + run_k[:, :LANES] - eqs[k]
                    keep_eq = jnp.where(before < rm, eqs[k], 0.0)
                    dst[rs, dss[k]] = jnp.where(xs[k] > t, 1.0, keep_eq).astype(dst.dtype)
                    seen = seen + run_k[:, LANES:]
                return seen

            pairs = n_wide // 2
            seen = lax.fori_loop(0, pairs, lambda c, sn, body=body: body(2 * c + 1, body(2 * c, sn)), seen)
            seen = lax.fori_loop(2 * pairs, n_wide, body, seen)

            def zbody(c, carry, dst=dst):
                off = pl.multiple_of(c * SEL_WIDE, SEL_WIDE)
                dst[rs, pl.ds(off, SEL_WIDE)] = jnp.zeros((SEL_ROWS, SEL_WIDE), dst.dtype)
                return carry

            lax.fori_loop(n_wide, n_total, zbody, 0)


def _index_scores(qi_ref, w, kb, transposed_keys=False):
    acc = None
    for h in range(IDX_HEADS):
        s = _dot(_mx(qi_ref[h]), kb) if transposed_keys else _dot_nt(_mx(qi_ref[h]), kb)
        t = w[:, h:h + 1] * jnp.maximum(s, 0.0)
        acc = t if acc is None else acc + t
    return acc


def _causal_pairs(nq, tq, tk):
    pairs = [(i, j) for i in range(nq) for j in range((i * tq + tq - 1) // tk + 1)]
    return jnp.asarray([p[0] for p in pairs], I32), jnp.asarray([p[1] for p in pairs], I32)


def _prompt_select_kernel(qb_ref, kb_ref, qi_ref, wi_ref, kidx_ref, mask_ref, i_scr, *, tq, tk, topk):
    p = pl.program_id(0)
    i, j = qb_ref[p], kb_ref[p]
    q_lo = i * tq
    n_wide = (q_lo + tq - 1) // tk + 1

    acc = _index_scores(qi_ref, wi_ref[...], kidx_ref[...])
    cols = pl.ds(pl.multiple_of(j * tk, tk), tk)
    below_diagonal = (j + 1) * tk <= q_lo + 1

    @pl.when(below_diagonal)
    def _():
        i_scr[:, cols] = acc

    @pl.when(jnp.logical_not(below_diagonal))
    def _():
        qpos = q_lo + lax.broadcasted_iota(I32, acc.shape, 0)
        kpos = j * tk + lax.broadcasted_iota(I32, acc.shape, 1)
        i_scr[:, cols] = jnp.where(kpos <= qpos, acc, -jnp.inf)

    @pl.when(j == n_wide - 1)
    def _():
        seg = [(i_scr, n_wide)]
        tau, room = _select_threshold(seg, tq, topk)
        _write_mask([(i_scr, mask_ref, n_wide, mask_ref.shape[1] // tk)], tq, tau, room)


def _prompt_select(qi_hm, wi_f, ki_b, topk, tq, tk):
    s = ki_b.shape[0]
    assert tk == SEL_WIDE and s % tk == 0 and s % tq == 0 and tq % SEL_ROWS == 0
    qb, kb = _causal_pairs(s // tq, tq, tk)
    grid_spec = pltpu.PrefetchScalarGridSpec(
        num_scalar_prefetch=2,
        grid=(qb.shape[0],),
        in_specs=[pl.BlockSpec((IDX_HEADS, tq, IDX_DIM), lambda p, qb, kb: (0, qb[p], 0)),
                  pl.BlockSpec((tq, LANES), lambda p, qb, kb: (qb[p], 0)),
                  pl.BlockSpec((tk, IDX_DIM), lambda p, qb, kb: (kb[p], 0))],
        out_specs=pl.BlockSpec((tq, s), lambda p, qb, kb: (qb[p], 0), pipeline_mode=pl.Buffered(1)),
        scratch_shapes=[pltpu.VMEM((tq, s), F32)],
    )
    return pl.pallas_call(
        functools.partial(_prompt_select_kernel, tq=tq, tk=tk, topk=topk),
        grid_spec=grid_spec,
        out_shape=jax.ShapeDtypeStruct((s, s), MXU_DTYPE),
        compiler_params=_cparams(("arbitrary",)),
        name="prompt_select",
    )(qb, kb, qi_hm, wi_f, ki_b)


def _spread(a, n):
    if n <= LANES:
        return a[:, :n]
    return jnp.concatenate([a] * (n // LANES), axis=1)


def _prompt_attend_kernel(qb_ref, kb_ref, qt_ref, k_ref, vt_ref, mask_ref, o_ref, m_scr, l_scr, acc_scr, *, tq, tk):
    p_id = pl.program_id(0)
    i, j = qb_ref[p_id], kb_ref[p_id]

    @pl.when(j == 0)
    def _():
        m_scr[...] = jnp.full(m_scr.shape, NEG_BIG, F32)
        l_scr[...] = jnp.zeros(l_scr.shape, F32)
        acc_scr[...] = jnp.zeros(acc_scr.shape, F32)

    bias = (1.0 - mask_ref[...].astype(F32).T) * NEG_BIG
    scores = [_dot(k_ref[h], qt_ref[h]) + bias for h in range(N_HEADS_A)]
    for h, s in enumerate(scores):
        m_prev = m_scr[h]
        m_new = jnp.maximum(m_prev, jnp.max(s, axis=0, keepdims=True))
        alpha = jnp.exp(m_prev - m_new)
        p = jnp.exp(s - m_new[0:1])
        l_scr[h] = alpha * l_scr[h] + jnp.sum(p, axis=0, keepdims=True)
        acc_scr[h] = alpha[0:1] * acc_scr[h] + _dot(vt_ref[h], _mx(p))
        m_scr[h] = m_new

    @pl.when(j == (i * tq + tq - 1) // tk)
    def _():
        ot = jnp.concatenate([acc_scr[h] / l_scr[h][0:1] for h in range(N_HEADS_A)], axis=0)
        o_ref[...] = ot.T.astype(o_ref.dtype)


def _prompt_attend(qa_t, ka_hm, va_t, mask, tq, tk):
    nh, dh, s = qa_t.shape
    qb, kb = _causal_pairs(s // tq, tq, tk)
    grid_spec = pltpu.PrefetchScalarGridSpec(
        num_scalar_prefetch=2,
        grid=(qb.shape[0],),
        in_specs=[pl.BlockSpec((nh, dh, tq), lambda p, qb, kb: (0, 0, qb[p])),
                  pl.BlockSpec((nh, tk, dh), lambda p, qb, kb: (0, kb[p], 0)),
                  pl.BlockSpec((nh, dh, tk), lambda p, qb, kb: (0, 0, kb[p])),
                  pl.BlockSpec((tq, tk), lambda p, qb, kb: (qb[p], kb[p]))],
        out_specs=pl.BlockSpec((tq, nh * dh), lambda p, qb, kb: (qb[p], 0)),
        scratch_shapes=[pltpu.VMEM((nh, SUBLANES, tq), F32), pltpu.VMEM((nh, SUBLANES, tq), F32),
                        pltpu.VMEM((nh, dh, tq), F32)],
    )
    return pl.pallas_call(
        functools.partial(_prompt_attend_kernel, tq=tq, tk=tk),
        grid_spec=grid_spec,
        out_shape=jax.ShapeDtypeStruct((s, nh * dh), MXU_DTYPE),
        compiler_params=_cparams(("arbitrary",)),
        name="prompt_attend",
    )(qb, kb, qa_t, ka_hm, va_t, mask)


PAGES_PER_STEP = 32
INDEX_PAGES_PER_STEP = 32


def _sample_index_kernel(pt_ref, qi_ref, wi_ref, *refs, page):
    del pt_ref
    pages, out_ref = refs[:-1], refs[-1]
    w = wi_ref[0]
    q_all = _mx(jnp.concatenate([qi_ref[h] for h in range(IDX_HEADS)], axis=0))
    w_col = jnp.concatenate([w[:, h:h + 1] for h in range(IDX_HEADS)], axis=0)
    for p, kref in enumerate(pages):
        r = w_col * jnp.maximum(_dot(q_all, _mx(kref[0])), 0.0)
        acc = r[0:T_PAD]
        for h in range(1, IDX_HEADS):
            acc = acc + r[h * T_PAD:(h + 1) * T_PAD]
        out_ref[0, :, p * page:(p + 1) * page] = acc


def _sample_index(page_table, qi_hm, wi_f, cache_kidx_t):
    db, n_pages = page_table.shape
    _, idim, page = cache_kidx_t.shape
    pps = math.gcd(INDEX_PAGES_PER_STEP, n_pages)
    nsteps = n_pages // pps
    pt = page_table.reshape(-1).astype(I32)

    def kspec(p):
        return pl.BlockSpec((1, idim, page), lambda b, j, pt: (pt[b * n_pages + j * pps + p], 0, 0))

    grid_spec = pltpu.PrefetchScalarGridSpec(
        num_scalar_prefetch=1,
        grid=(db, nsteps),
        in_specs=[pl.BlockSpec((IDX_HEADS, T_PAD, idim), lambda b, j, pt: (0, b, 0)),
                  pl.BlockSpec((1, T_PAD, LANES), lambda b, j, pt: (b, 0, 0))]
                 + [kspec(p) for p in range(pps)],
        out_specs=pl.BlockSpec((1, T_PAD, pps * page), lambda b, j, pt: (b, 0, j)),
    )
    return pl.pallas_call(
        functools.partial(_sample_index_kernel, page=page),
        grid_spec=grid_spec,
        out_shape=jax.ShapeDtypeStruct((db, T_PAD, n_pages * page), F32),
        compiler_params=_cparams(("parallel", "arbitrary")),
        name="sample_index",
    )(pt, qi_hm, wi_f.reshape(db, T_PAD, LANES), *([cache_kidx_t] * pps))


def _sample_select_kernel(ipast_ref, qi_ref, wi_ref, kin_ref, mpast_ref, mnew_ref, inew_scr, *, t_real, topk):
    rows = ipast_ref.shape[0]
    acc = _index_scores(qi_ref, wi_ref[...], kin_ref[...])
    r = lax.broadcasted_iota(I32, acc.shape, 0)
    c = lax.broadcasted_iota(I32, acc.shape, 1)
    same = (r // T_PAD) == (c // T_PAD)
    tq, tc = r % T_PAD, c % T_PAD
    ok = jnp.logical_and(same, jnp.logical_and(tc <= tq, tc < t_real))
    inew_scr[...] = jnp.full(inew_scr.shape, -jnp.inf, F32)
    inew_scr[:, :rows] = jnp.where(ok, acc, -jnp.inf)
    n_past = ipast_ref.shape[1] // SEL_WIDE
    n_new = inew_scr.shape[1] // SEL_WIDE
    alive = lax.rem(lax.broadcasted_iota(I32, (1, rows), 1), T_PAD) < t_real
    tau, room = _select_threshold([(ipast_ref, n_past), (inew_scr, n_new)], rows, topk, alive)
    _write_mask([(ipast_ref, mpast_ref, n_past, n_past), (inew_scr, mnew_ref, n_new, n_new)], rows, tau, room)


def _sample_select(i_past, qi_hm, wi_f, ki_b, t_real, topk):
    rows, past = i_past.shape
    assert past % SEL_WIDE == 0 and rows % SEL_ROWS == 0
    wnew = -(-rows // SEL_WIDE) * SEL_WIDE
    return pl.pallas_call(
        functools.partial(_sample_select_kernel, t_real=t_real, topk=topk),
        out_shape=[jax.ShapeDtypeStruct((rows, past), F32), jax.ShapeDtypeStruct((rows, wnew), F32)],
        scratch_shapes=[pltpu.VMEM((rows, wnew), F32)],
        compiler_params=pltpu.CompilerParams(vmem_limit_bytes=VMEM_LIMIT),
        name="sample_select",
    )(i_past, qi_hm, wi_f, ki_b)


def _sample_attend_kernel(pt_ref, q_ref, mp_ref, mn_ref, kn_ref, vn_ref, *refs, page, pps):
    del pt_ref
    kpages, vpages = refs[:pps], refs[pps:2 * pps]
    o_ref, m_scr, l_scr, acc_scr = refs[2 * pps:]
    j = pl.program_id(1)
    nj = pl.num_programs(1)
    q = q_ref[0]
    nh = q.shape[0] // T_PAD

    def scores(m_t, kt):
        keep = jnp.concatenate([m_t] * nh, axis=0) > 0
        return jnp.where(keep, _dot(q, kt), NEG_BIG)

    def update(s_list, vt_list):
        m_prev = m_scr[...]
        m_blk = functools.reduce(jnp.maximum, [jnp.max(s, axis=1, keepdims=True) for s in s_list])
        m_new = jnp.maximum(m_prev, m_blk)
        alpha = jnp.exp(m_prev - m_new)
        l_new = alpha * l_scr[...]
        acc = _spread(alpha, acc_scr.shape[1]) * acc_scr[...]
        for s, vt in zip(s_list, vt_list):
            p = jnp.exp(s - _spread(m_new, s.shape[1]))
            l_new = l_new + jnp.sum(p, axis=1, keepdims=True)
            acc = acc + _dot_nt(_mx(p), vt)
        m_scr[...], l_scr[...], acc_scr[...] = m_new, l_new, acc

    @pl.when(j == 0)
    def _():
        m_scr[...] = jnp.full(m_scr.shape, NEG_BIG, F32)
        l_scr[...] = jnp.zeros(l_scr.shape, F32)
        acc_scr[...] = jnp.zeros(acc_scr.shape, F32)

    update([scores(mp_ref[0, :, p * page:(p + 1) * page], _mx(kpages[p][0])) for p in range(pps)],
           [_mx(vpages[p][0]) for p in range(pps)])

    @pl.when(j == nj - 1)
    def _():
        update([scores(mn_ref[0], kn_ref[0])], [vn_ref[0]])
        full = acc_scr[...] / _spread(l_scr[...], acc_scr.shape[1])
        lane = lax.broadcasted_iota(I32, (T_PAD, full.shape[1]), 1)
        out = jnp.zeros((T_PAD, full.shape[1]), F32)
        for h in range(nh):
            out = out + jnp.where((lane // HEAD_DIM_A) == h, full[h * T_PAD:(h + 1) * T_PAD], 0.0)
        o_ref[0] = out


def _sample_attend(page_table, q_bd, m_past, m_new, kt_new, vt_new, cache_kt, cache_vt):
    db, n_pages = page_table.shape
    _, hd, page = cache_kt.shape
    pps = math.gcd(PAGES_PER_STEP, n_pages)
    nsteps = n_pages // pps
    nnew = kt_new.shape[2]
    nq = q_bd.shape[1]
    pt = page_table.reshape(-1).astype(I32)

    def pspec(p):
        return pl.BlockSpec((1, hd, page), lambda b, j, pt: (pt[b * n_pages + j * pps + p], 0, 0))

    bspec = lambda shape: pl.BlockSpec((1,) + shape, lambda b, j, pt: (b, 0, 0))
    grid_spec = pltpu.PrefetchScalarGridSpec(
        num_scalar_prefetch=1,
        grid=(db, nsteps),
        in_specs=[bspec((nq, hd)),
                  pl.BlockSpec((1, T_PAD, pps * page), lambda b, j, pt: (b, 0, j)),
                  bspec((T_PAD, nnew)), bspec((hd, nnew)), bspec((hd, nnew))]
                 + [pspec(p) for p in range(pps)] * 2,
        out_specs=bspec((T_PAD, hd)),
        scratch_shapes=[pltpu.VMEM((nq, LANES), F32), pltpu.VMEM((nq, LANES), F32), pltpu.VMEM((nq, hd), F32)],
    )
    return pl.pallas_call(
        functools.partial(_sample_attend_kernel, page=page, pps=pps),
        grid_spec=grid_spec,
        out_shape=jax.ShapeDtypeStruct((db, T_PAD, hd), F32),
        compiler_params=_cparams(("parallel", "arbitrary")),
        name="sample_attend",
    )(pt, q_bd, m_past, m_new, kt_new, vt_new, *([cache_kt] * pps), *([cache_vt] * pps))


def _retention_tables(c_real, c_pad):
    h = np.arange(N_HEADS_R, dtype=np.float64)
    log_g = np.log1p(-np.exp2(-5.0 - h))
    i = np.arange(c_pad, dtype=np.float64)
    diff = i[:, None] - i[None, :]
    live = (diff >= 0) & (i[:, None] < c_real) & (i[None, :] < c_real)
    inner = np.where(live[None], np.exp(np.maximum(diff, 0.0)[None] * log_g[:, None, None]), 0.0)
    q_dec = np.exp((i + 1.0)[None, :] * log_g[:, None])
    k_dec = np.where(i[None, :] < c_real, np.exp((c_real - 1.0 - i)[None, :] * log_g[:, None]), 0.0)
    c_dec = np.exp(c_real * log_g)
    f = lambda a: jnp.asarray(a, F32)
    return f(inner), f(q_dec[:, :, None]), f(k_dec[:, :, None]), [float(v) for v in c_dec]


def _retention_kernel(q_ref, k_ref, v_ref, g_ref, s0_ref, inner_ref, qdec_ref, kdec_ref,
                      o_ref, s_out_ref, s_scr, *, c_dec):
    j = pl.program_id(1)
    nj = pl.num_programs(1)

    @pl.when(j == 0)
    def _():
        s_scr[...] = s0_ref[0]

    heads = range(N_HEADS_R)
    qs = [_mx(q_ref[:, h * DK_R:(h + 1) * DK_R]) for h in heads]
    ks = [k_ref[:, h * DK_R:(h + 1) * DK_R] for h in heads]
    vs = [v_ref[:, h * DV_R:(h + 1) * DV_R] for h in heads]
    s_prevs = [s_scr[h] for h in heads]
    inner_scores = [_dot_nt(qs[h], _mx(ks[h])) for h in heads]
    cross = [_dot(qs[h], _mx(s_prevs[h])) for h in heads]
    updates = [_dot(_mx((ks[h] * kdec_ref[h]).T), vs[h]) for h in heads]
    for h in heads:
        v = vs[h]
        o = _dot(_mx(inner_scores[h] * inner_ref[h]), v) + cross[h] * qdec_ref[h]
        s_scr[h] = s_prevs[h] * c_dec[h] + updates[h]
        mu = jnp.mean(o, axis=-1, keepdims=True)
        var = jnp.mean(jnp.square(o - mu), axis=-1, keepdims=True)
        gn = (o - mu) * lax.rsqrt(var + EPS)
        g = g_ref[:, h * DV_R:(h + 1) * DV_R]
        o_ref[:, h * DV_R:(h + 1) * DV_R] = (gn * (g * _sigmoid(g))).astype(o_ref.dtype)

    @pl.when(j == nj - 1)
    def _():
        s_out_ref[0] = s_scr[...]


def _retention(qr, kr, vr, gr, s0, c_real):
    b = s0.shape[0]
    c = RET_CHUNK
    n = qr.shape[0] // (b * c)
    inner, qdec, kdec, c_dec = _retention_tables(c_real, c)
    wr, wv = N_HEADS_R * DK_R, N_HEADS_R * DV_R
    rmap = lambda bi, j: (bi * n + j, 0)
    full3 = lambda shape: pl.BlockSpec(shape, lambda bi, j: (0, 0, 0))
    return pl.pallas_call(
        functools.partial(_retention_kernel, c_dec=c_dec),
        grid=(b, n),
        in_specs=[pl.BlockSpec((c, wr), rmap), pl.BlockSpec((c, wr), rmap), pl.BlockSpec((c, wv), rmap),
                  pl.BlockSpec((c, wv), rmap),
                  pl.BlockSpec((1, N_HEADS_R, DK_R, DV_R), lambda bi, j: (bi, 0, 0, 0)),
                  full3(inner.shape), full3(qdec.shape), full3(kdec.shape)],
        out_specs=[pl.BlockSpec((c, wv), rmap),
                   pl.BlockSpec((1, N_HEADS_R, DK_R, DV_R), lambda bi, j: (bi, 0, 0, 0))],
        out_shape=[jax.ShapeDtypeStruct((b * n * c, wv), MXU_DTYPE),
                   jax.ShapeDtypeStruct((b, N_HEADS_R, DK_R, DV_R), F32)],
        scratch_shapes=[pltpu.VMEM((N_HEADS_R, DK_R, DV_R), F32)],
        compiler_params=_cparams(("parallel", "arbitrary")),
        name="retention",
    )(qr, kr, vr, gr, s0, inner, qdec, kdec)


def _cross_kernel(q_ref, mk_ref, mv_ref, o_ref):
    scale = HEAD_DIM_M ** -0.5
    for h in range(N_HEADS_M):
        sl = slice(h * HEAD_DIM_M, (h + 1) * HEAD_DIM_M)
        s = _dot_nt(_mx(q_ref[:, sl]), _mx(mk_ref[0, :, sl])) * scale
        p = jnp.exp(s - jnp.max(s, axis=1, keepdims=True))
        p = p / jnp.sum(p, axis=1, keepdims=True)
        o_ref[:, sl] = _dot(_mx(p), _mx(mv_ref[0, :, sl])).astype(o_ref.dtype)


def _cross_attend(qm, mk, mv, tm, out_dtype):
    b, n_mem, hd = mk.shape
    nt = qm.shape[0] // (b * tm)
    return pl.pallas_call(
        _cross_kernel,
        grid=(b, nt),
        in_specs=[pl.BlockSpec((tm, hd), lambda bi, i: (bi * nt + i, 0)),
                  pl.BlockSpec((1, n_mem, hd), lambda bi, i: (bi, 0, 0)),
                  pl.BlockSpec((1, n_mem, hd), lambda bi, i: (bi, 0, 0))],
        out_specs=pl.BlockSpec((tm, hd), lambda bi, i: (bi * nt + i, 0)),
        out_shape=jax.ShapeDtypeStruct(qm.shape, out_dtype),
        compiler_params=_cparams(("parallel", "parallel")),
        name="cross_attend",
    )(qm, mk, mv)


def _memkv_kernel(x_ref, g_ref, w_ref, o_ref):
    o_ref[...] = _dot(_mx(_rms(x_ref[...], g_ref[...])), w_ref[...])


def _memory_kv(mem, gain, w):
    rows, d = mem.shape
    return pl.pallas_call(
        _memkv_kernel,
        out_shape=jax.ShapeDtypeStruct((rows, w.shape[1]), F32),
        compiler_params=pltpu.CompilerParams(vmem_limit_bytes=VMEM_LIMIT),
        name="memory_kv",
    )(mem, gain.reshape(1, d), _mx(w))


def _merge_kernel(x_ref, oa_ref, or_ref, om_ref, gates_ref, wpa_ref, wpb_ref, wpc_ref, wo_ref, g_ref, h_ref):
    d = x_ref.shape[1]
    gt = gates_ref[...]
    mixed = (_sigmoid(gt[:, :d]) * _dot(_mx(oa_ref[...]), wpa_ref[...])
             + _sigmoid(gt[:, d:2 * d]) * _dot(_mx(or_ref[...]), wpb_ref[...])
             + _sigmoid(gt[:, 2 * d:]) * _dot(_mx(om_ref[...]), wpc_ref[...]))
    z = _dot(_mx(mixed), wo_ref[...])
    h_ref[...] = x_ref[...] + _rms(z, g_ref[...])


def _merge(x, oa, o_r, om, gates, wpa, wpb, wpc, wo, gain, tm):
    rows, d = x.shape
    return pl.pallas_call(
        _merge_kernel,
        grid=(rows // tm,),
        in_specs=[_row_spec(tm, d), _row_spec(tm, oa.shape[1]), _row_spec(tm, o_r.shape[1]),
                  _row_spec(tm, om.shape[1]), _row_spec(tm, 3 * d),
                  _full_spec(wpa.shape), _full_spec(wpb.shape), _full_spec(wpc.shape), _full_spec(wo.shape),
                  _full_spec((1, d))],
        out_specs=_row_spec(tm, d),
        out_shape=jax.ShapeDtypeStruct((rows, d), F32),
        compiler_params=_cparams(("parallel",)),
        name="merge",
    )(x, oa, o_r, om, gates, wpa, wpb, wpc, wo, gain.reshape(1, d))


HALO = BF16_ROWS


def _ffn_kernel(h_ref, halo_ref, s0_ref, s1_ref, g1_ref, g2_ref, wu_ref, wg_ref, cw_ref, cb_ref, wd_ref,
                y_ref, utail_ref, x_scr, u_scr, *, tm, seq, keep):
    i = pl.program_id(0)
    h = h_ref[...]
    hn = _rms(h, g1_ref[...])
    x_scr[HALO:, :] = _mx(hn)
    x_scr[:HALO, :] = _mx(_rms(halo_ref[...], g1_ref[...]))
    xc = x_scr[...]
    u_scr[...] = _dot(xc, wu_ref[...])
    gate = _dot(xc[HALO:], wg_ref[...])
    cur = u_scr[HALO:, :]
    prev1 = u_scr[HALO - 1:HALO - 1 + tm, :]
    prev2 = u_scr[HALO - 2:HALO - 2 + tm, :]
    seq_loc = min(seq, tm)
    t = lax.rem(lax.broadcasted_iota(I32, (tm, 1), 0), seq_loc)
    t = jnp.where(lax.rem(i * tm, seq) == 0, t, CONV_W)
    st0, st1 = s0_ref[...], s1_ref[...]
    if st0.shape[0] != tm:
        st0, st1 = st0[0:1], st1[0:1]
    prev1 = jnp.where(t == 0, st1, prev1)
    prev2 = jnp.where(t == 0, st0, jnp.where(t == 1, st1, prev2))
    c = cb_ref[...] + prev2 * cw_ref[0:1, :] + prev1 * cw_ref[1:2, :] + cur * cw_ref[2:3, :]
    act = jax.nn.gelu(c, approximate=True) * gate
    ff = _dot(_mx(act), wd_ref[...])
    y_ref[...] = h + _rms(ff, g2_ref[...])
    utail_ref[...] = u_scr[HALO + tm - keep:, :]


def _conv_ffn(h, s0e, s1e, g1, g2, wu, wg, cw, cb, wd, tm, seq, keep):
    rows, d = h.shape
    f = wu.shape[1]
    nt = rows // tm
    hb = tm // HALO
    sr = s0e.shape[0]
    return pl.pallas_call(
        functools.partial(_ffn_kernel, tm=tm, seq=seq, keep=keep),
        grid=(nt,),
        in_specs=[_row_spec(tm, d),
                  pl.BlockSpec((HALO, d), lambda i: (jnp.maximum(i * hb - 1, 0), 0)),
                  _full_spec((sr, f)), _full_spec((sr, f)),
                  _full_spec((1, d)), _full_spec((1, d)),
                  _full_spec(wu.shape), _full_spec(wg.shape), _full_spec(cw.shape), _full_spec((1, f)),
                  _full_spec(wd.shape)],
        out_specs=[_row_spec(tm, d), _row_spec(keep, f)],
        out_shape=[jax.ShapeDtypeStruct((rows, d), F32), jax.ShapeDtypeStruct((nt * keep, f), F32)],
        scratch_shapes=[pltpu.VMEM((tm + HALO, d), MXU_DTYPE), pltpu.VMEM((tm + HALO, f), F32)],
        compiler_params=_cparams(("parallel",)),
        name="conv_ffn",
    )(h, h, s0e, s1e, g1.reshape(1, d), g2.reshape(1, d), wu, wg, cw, cb.reshape(1, f), wd)


def _layer_weights(l, w_in, w_proj_a, w_proj_b, w_proj_c, w_out, w_up, w_down):
    wts = _prep_in_weights(w_in[l])
    f = w_down.shape[1]
    wts.update(wpa=_mx(w_proj_a[l]), wpb=_mx(w_proj_b[l]), wpc=_mx(w_proj_c[l]), wo=_mx(w_out[l]),
               wu=_mx(w_up[l][:, :f]), wg=_mx(w_up[l][:, f:]), wd=_mx(w_down[l]))
    return wts


def _prompt_layer(x, mem, wts, norms, conv_w, conv_b, w_mem_kv, tiles):
    s, d = x.shape
    f = wts["wd"].shape[0]
    pos = jnp.arange(s)
    pr = _projections(x, norms["pre_mix"], pos, wts, tiles["proj"])
    topk = min(TOPK_MAX, s // 4)
    mask = _prompt_select(pr["qi_hm"], pr["wi_f"], pr["ki_b"], topk, tiles["sel_q"], SEL_WIDE)
    o_a = _prompt_attend(pr["qa_t"], pr["ka_hm"], pr["va_t"], mask, tiles["att_q"], tiles["att_k"])
    s0 = jnp.zeros((1, N_HEADS_R, DK_R, DV_R), F32)
    o_r, ret_new = _retention(pr["qr_f"], pr["kr_f"], pr["vr_b"], pr["gr_f"], s0, RET_CHUNK)
    kv = _memory_kv(mem, norms["mem"], w_mem_kv)
    wm = N_HEADS_M * HEAD_DIM_M
    mk, mv = kv[:, :wm], kv[:, wm:]
    o_m = _cross_attend(pr["qm_b"], mk[None], mv[None], tiles["cross"], MXU_DTYPE)
    h = _merge(x, o_a, o_r, o_m, pr["gates_f"], wts["wpa"], wts["wpb"], wts["wpc"], wts["wo"],
               norms["post_mix"], tiles["merge"])
    zst = jnp.zeros((SUBLANES, f), F32)
    y, utail = _conv_ffn(h, zst, zst, norms["pre_ffn"], norms["post_ffn"], wts["wu"], wts["wg"], conv_w, conv_b,
                         wts["wd"], tiles["ffn"], s, SUBLANES)
    conv_new = utail[-(CONV_W - 1):]
    return y, pr["ka_f"], pr["va_f"], pr["ki_f"], ret_new, conv_new, mk, mv


def _sample_layer(x, wts, norms, conv_w, conv_b, cache_k, cache_v, cache_kidx, mem_k, mem_v,
                  state_ret, state_conv, page_table):
    db, t, d = x.shape
    f = wts["wd"].shape[0]
    n_pages = page_table.shape[1]
    page = cache_k.shape[1]
    past = n_pages * page
    rows = db * T_PAD
    xp = jnp.pad(x, ((0, 0), (0, T_PAD - t), (0, 0))).reshape(rows, d)
    pos = jnp.tile(past + jnp.arange(T_PAD), db)
    pr = _projections(xp, norms["pre_mix"], pos, wts, rows)
    hd = N_HEADS_A * HEAD_DIM_A

    topk = min(TOPK_MAX, (past + t) // 4)
    i_past = _sample_index(page_table, pr["qi_hm"].astype(F32), pr["wi_f"], cache_kidx.transpose(0, 2, 1)).reshape(rows, past)
    m_past, m_new = _sample_select(i_past, pr["qi_hm"], pr["wi_f"], pr["ki_b"], t, topk)
    own = m_new[:, :rows].reshape(db, T_PAD, db, T_PAD)[jnp.arange(db), :, jnp.arange(db), :]
    m_new_own = jnp.pad(own, ((0, 0), (0, 0), (0, LANES - T_PAD)))
    q_rows = pr["qa_hm"].reshape(N_HEADS_A, db, T_PAD, HEAD_DIM_A)
    eye = jnp.eye(N_HEADS_A, dtype=MXU_DTYPE)
    q_bd = jnp.einsum("hbtd,hg->bhtgd", q_rows, eye).reshape(db, N_HEADS_A * T_PAD, hd)
    new_t = lambda a: jnp.pad(_mx(a).reshape(db, T_PAD, hd).transpose(0, 2, 1), ((0, 0), (0, 0), (0, LANES - T_PAD)))
    paged_t = lambda c: c.transpose(0, 2, 3, 1).reshape(c.shape[0], hd, page)
    o_a = _sample_attend(page_table, q_bd, m_past.reshape(db, T_PAD, past), m_new_own,
                         new_t(pr["ka_f"]), new_t(pr["va_f"]), paged_t(cache_k), paged_t(cache_v)).reshape(rows, hd)

    padc = lambda a: jnp.pad(a.reshape(db, T_PAD, -1), ((0, 0), (0, RET_CHUNK - T_PAD), (0, 0))).reshape(db * RET_CHUNK, -1)
    o_r, ret_new = _retention(padc(pr["qr_f"]), padc(pr["kr_f"]), padc(pr["vr_b"]), padc(pr["gr_f"]), state_ret, t)
    o_r = o_r.reshape(db, RET_CHUNK, -1)[:, :T_PAD].reshape(rows, -1)

    wm = N_HEADS_M * HEAD_DIM_M
    o_m = _cross_attend(pr["qm_b"].astype(F32), mem_k.reshape(db, -1, wm), mem_v.reshape(db, -1, wm), T_PAD, F32)

    h = _merge(xp, o_a, o_r, o_m, pr["gates_f"], wts["wpa"], wts["wpb"], wts["wpc"], wts["wo"],
               norms["post_mix"], rows)
    s0e = jnp.repeat(state_conv[:, 0], T_PAD, axis=0)
    s1e = jnp.repeat(state_conv[:, 1], T_PAD, axis=0)
    y, u_all = _conv_ffn(h, s0e, s1e, norms["pre_ffn"], norms["post_ffn"], wts["wu"], wts["wg"], conv_w, conv_b,
                         wts["wd"], rows, T_PAD, rows)
    ext = jnp.concatenate([state_conv.astype(F32), u_all.reshape(db, T_PAD, f)[:, :t]], axis=1)
    conv_new = ext[:, t:]
    unpad = lambda a: a.reshape(db, T_PAD, -1)[:, :t]
    return (unpad(y), unpad(pr["ka_f"]), unpad(pr["va_f"]), unpad(pr["ki_f"]), ret_new, conv_new)


PROMPT_TILES = dict(proj=512, sel_q=512, att_q=512, att_k=1024, cross=512, merge=512, ffn=256)


def kernel(x_prompt, x_sample, cache_k, cache_v, cache_kidx, cache_mem_k, cache_mem_v, state_ret, state_conv,
           page_table, mem_prompt, norm_pre_mix, norm_post_mix, norm_pre_ffn, norm_post_ffn, norm_mem,
           w_in, w_mem_kv, w_proj_a, w_proj_b, w_proj_c, w_out, w_up, conv_w, conv_b, w_down):
    bp, s, d = x_prompt.shape
    db, t, _ = x_sample.shape
    depth = w_in.shape[0]
    assert bp == 1 and t <= T_PAD and CONV_W - 1 <= t
    tiles = {k: min(v, s) for k, v in PROMPT_TILES.items()}
    yp, ys = x_prompt[0], x_sample
    outs = [[] for _ in range(12)]
    for l in range(depth):
        wts = _layer_weights(l, w_in, w_proj_a, w_proj_b, w_proj_c, w_out, w_up, w_down)
        norms = dict(pre_mix=norm_pre_mix[l], post_mix=norm_post_mix[l], pre_ffn=norm_pre_ffn[l],
                     post_ffn=norm_post_ffn[l], mem=norm_mem[l])
        yp, kp, vp, kip, rp, cp, mk, mv = _prompt_layer(yp, mem_prompt[0], wts, norms, conv_w[l], conv_b[l],
                                                        w_mem_kv[l], tiles)
        ys, ks, vs, kis, rs, cs = _sample_layer(ys, wts, norms, conv_w[l], conv_b[l], cache_k[l], cache_v[l],
                                                cache_kidx[l], cache_mem_k[l], cache_mem_v[l], state_ret[l],
                                                state_conv[l], page_table)
        n_mem = mk.shape[0]
        vals = (kp.reshape(1, s, N_HEADS_A, HEAD_DIM_A), vp.reshape(1, s, N_HEADS_A, HEAD_DIM_A),
                kip.reshape(1, s, IDX_DIM), rp, cp[None],
                mk.reshape(1, n_mem, N_HEADS_M, HEAD_DIM_M), mv.reshape(1, n_mem, N_HEADS_M, HEAD_DIM_M),
                ks.reshape(db, t, N_HEADS_A, HEAD_DIM_A), vs.reshape(db, t, N_HEADS_A, HEAD_DIM_A),
                kis, rs, cs)
        for o, v in zip(outs, vals):
            o.append(v)
    stacked = [jnp.stack(o) for o in outs]
    return (yp[None], ys, *stacked)
```
